```python
import math
import jax, jax.numpy as jnp
from jax import lax
import numpy as np

D_MODEL = 2048
BATCH = 1
SEQ = 8192
DEPTH = 2

N_EVEN = (DEPTH + 1) // 2
N_ODD = DEPTH // 2
D_HY = D_MODEL // 2
D_SC = D_MODEL // 2
HY_STREAMS = 3
SC_STREAMS = 3
D_IN = HY_STREAMS * D_HY + SC_STREAMS * D_SC
SHORT_W = 3
FILTER_BANDS = 16
EMB_DIM = 1 + 2 * FILTER_BANDS
FILTER_HIDDEN = 64
DECAY_TARGET = 1e-2
FAST_DECAY_PCT = 0.3
SLOW_DECAY_PCT = 1.5
POOL_WINDOWS = (2, 4, 8, 16)
N_POOL_GROUPS = len(POOL_WINDOWS)
D_POOL_G = D_MODEL // N_POOL_GROUPS
D_FF = 4 * D_MODEL
NORM_EPS = 1e-6

kernel_name = "hybrid_hyena_shortconv_pool_encoder"


def rmsnorm(u, g):
    u32 = u.astype(jnp.float32)
    r = lax.rsqrt(jnp.mean(u32 * u32, axis=-1, keepdims=True) + NORM_EPS)
    return (u32 * r * g.astype(jnp.float32)).astype(u.dtype)


def dwconv3(u, w, b=None):
    up = jnp.pad(u, ((0, 0), (1, 1), (0, 0)))
    y = up[:, :-2] * w[0] + up[:, 1:-1] * w[1] + up[:, 2:] * w[2]
    return y if b is None else y + b


def hyena_filters(L, w1, b1, w2, b2, w3, b3, freq):
    f32 = jnp.float32
    t = jnp.linspace(0.0, 1.0, L, dtype=f32)[:, None]
    w_pos = 2.0 * math.pi * jnp.arange(L, dtype=f32)[:, None] / L
    bands = jnp.linspace(1e-4, FILTER_BANDS - 1, FILTER_BANDS, dtype=f32)[None, :]
    ang = w_pos * bands
    emb = jnp.concatenate([t, jnp.cos(ang), -jnp.sin(ang)], axis=-1)
    fr = freq.astype(f32)
    h = jnp.sin(fr * (emb @ w1.astype(f32) + b1.astype(f32)))
    h = jnp.sin(fr * (h @ w2.astype(f32) + b2.astype(f32)))
    h = h @ w3.astype(f32) + b3.astype(f32)
    max_decay = math.log(DECAY_TARGET) / FAST_DECAY_PCT
    min_decay = math.log(DECAY_TARGET) / SLOW_DECAY_PCT
    deltas = jnp.linspace(min_decay, max_decay, D_HY, dtype=f32)[None, :]
    window = jnp.exp(-t * jnp.abs(deltas))
    return h[:, :D_HY] * window, h[:, D_HY:] * window


def two_sided_fftconv(v, h_fwd, h_bwd, bias):
    L = v.shape[1]
    v32 = v.astype(jnp.float32)
    k = jnp.concatenate([h_fwd, jnp.zeros((1, D_HY), jnp.float32), h_bwd[:0:-1]], axis=0)
    V = jnp.fft.rfft(v32, n=2 * L, axis=1)
    K = jnp.fft.rfft(k, n=2 * L, axis=0)
    y = jnp.fft.irfft(V * K[None], n=2 * L, axis=1)[:, :L]
    return (y + v32 * bias.astype(jnp.float32)).astype(v.dtype)


def hyena_shortconv_mixer(h, w_in, hy_short_w, hy_short_b, f_w1, f_b1, f_w2, f_b2, f_w3, f_b3,
                          freq, hy_bias, sc_conv_w, w_out):
    L = h.shape[1]
    z = h @ w_in
    hy = dwconv3(z[..., :HY_STREAMS * D_HY], hy_short_w, hy_short_b)
    x0, x1, v = jnp.split(hy, HY_STREAMS, axis=-1)
    h_fwd, h_bwd = hyena_filters(L, f_w1, f_b1, f_w2, f_b2, f_w3, f_b3, freq)
    y_a = x0 * two_sided_fftconv(v * x1, h_fwd, h_bwd, hy_bias)
    gb, gc, xv = jnp.split(z[..., HY_STREAMS * D_HY:], SC_STREAMS, axis=-1)
    y_b = gb * dwconv3(gc * xv, sc_conv_w)
    return jnp.concatenate([y_a, y_b], axis=-1) @ w_out


def centred_mean(u, r):
    L = u.shape[1]
    cs = jnp.pad(lax.cumsum(u.astype(jnp.float32), axis=1), ((0, 0), (1, 0), (0, 0)))
    t = jnp.arange(L)
    lo = jnp.maximum(t - r, 0)
    hi = jnp.minimum(t + r + 1, L)
    s = cs[:, hi] - cs[:, lo]
    cnt = (hi - lo).astype(jnp.float32)[None, :, None]
    return (s / cnt).astype(u.dtype)


def pool_mixer(h, pool_w, pool_scale):
    outs = []
    for g, w in enumerate(POOL_WINDOWS):
        u = h[..., g * D_POOL_G:(g + 1) * D_POOL_G]
        d = centred_mean(u, w // 2) - u
        outs.append(d @ pool_w[g])
    return jnp.concatenate(outs, axis=-1) * pool_scale


def setup_inputs(seed: int = 0) -> dict:
    key = jax.random.key(seed)
    ks = jax.random.split(key, 20)
    nrm = lambda k, shape, s: jax.random.normal(k, shape, jnp.float32) * s
    return {
        "x": nrm(ks[0], (BATCH, SEQ, D_MODEL), 1.0),
        "norm_g": 1.0 + nrm(ks[1], (DEPTH, 4, D_MODEL), 0.05),
        "mix_w_in": nrm(ks[2], (N_EVEN, D_MODEL, D_IN), D_MODEL ** -0.5),
        "hy_short_w": nrm(ks[3], (N_EVEN, SHORT_W, HY_STREAMS * D_HY), SHORT_W ** -0.5),
        "hy_short_b": nrm(ks[4], (N_EVEN, HY_STREAMS * D_HY), 0.02),
        "hy_filt_w1": nrm(ks[5], (N_EVEN, EMB_DIM, FILTER_HIDDEN), EMB_DIM ** -0.5),
        "hy_filt_b1": nrm(ks[6], (N_EVEN, FILTER_HIDDEN), 0.02),
        "hy_filt_w2": nrm(ks[7], (N_EVEN, FILTER_HIDDEN, FILTER_HIDDEN), FILTER_HIDDEN ** -0.5),
        "hy_filt_b2": nrm(ks[8], (N_EVEN, FILTER_HIDDEN), 0.02),
        "hy_filt_w3": nrm(ks[9], (N_EVEN, FILTER_HIDDEN, 2 * D_HY), FILTER_HIDDEN ** -0.5),
        "hy_filt_b3": nrm(ks[10], (N_EVEN, 2 * D_HY), 0.02),
        "hy_freq": 1.0 + nrm(ks[11], (N_EVEN, FILTER_HIDDEN), 0.05),
        "hy_bias": nrm(ks[12], (N_EVEN, D_HY), 0.5),
        "sc_conv_w": nrm(ks[13], (N_EVEN, SHORT_W, D_SC), SHORT_W ** -0.5),
        "mix_w_out": nrm(ks[14], (N_EVEN, D_HY + D_SC, D_MODEL), (D_HY + D_SC) ** -0.5),
        "pool_w": nrm(ks[15], (N_ODD, N_POOL_GROUPS, D_POOL_G, D_POOL_G), D_POOL_G ** -0.5),
        "pool_scale": 1.0 + nrm(ks[16], (N_ODD, D_MODEL), 0.1),
        "mlp_w1": nrm(ks[17], (DEPTH, D_MODEL, D_FF), D_MODEL ** -0.5),
        "mlp_w2": nrm(ks[18], (DEPTH, D_FF, D_MODEL), D_FF ** -0.5),
    }


def reference(x, norm_g, mix_w_in, hy_short_w, hy_short_b, hy_filt_w1, hy_filt_b1, hy_filt_w2,
              hy_filt_b2, hy_filt_w3, hy_filt_b3, hy_freq, hy_bias, sc_conv_w, mix_w_out,
              pool_w, pool_scale, mlp_w1, mlp_w2):
    for i in range(DEPTH):
        g = norm_g[i]
        j = i // 2
        h = rmsnorm(x, g[0])
        if i % 2 == 0:
            m = hyena_shortconv_mixer(h, mix_w_in[j], hy_short_w[j], hy_short_b[j],
                                      hy_filt_w1[j], hy_filt_b1[j], hy_filt_w2[j], hy_filt_b2[j],
                                      hy_filt_w3[j], hy_filt_b3[j], hy_freq[j], hy_bias[j],
                                      sc_conv_w[j], mix_w_out[j])
        else:
            m = pool_mixer(h, pool_w[j], pool_scale[j])
        x = x + rmsnorm(m, g[1])
        h = rmsnorm(x, g[2])
        f = jnp.square(jax.nn.relu(h @ mlp_w1[i])) @ mlp_w2[i]
        x = x + rmsnorm(f, g[3])
    return x
```

```python
import functools
import math

import numpy as np
import jax
import jax.numpy as jnp
from jax import lax
from jax.experimental import pallas as pl
from jax.experimental.pallas import tpu as pltpu

F32 = jnp.float32
BF16 = jnp.bfloat16

D_MODEL = 2048
SEQ = 8192
D_HY = D_MODEL // 2
D_SC = D_MODEL // 2
D_IN = 3 * D_HY + 3 * D_SC
FILTER_BANDS = 16
EMB_DIM = 1 + 2 * FILTER_BANDS
EMB_PAD = 40
FILTER_HIDDEN = 64
DECAY_TARGET = 1e-2
FAST_DECAY_PCT = 0.3
SLOW_DECAY_PCT = 1.5
POOL_WINDOWS = (2, 4, 8, 16)
D_POOL_G = D_MODEL // len(POOL_WINDOWS)
D_FF = 4 * D_MODEL
NORM_EPS = 1e-6

FFT_N = 2 * SEQ
FFT_N2 = 256
FFT_N1 = FFT_N // FFT_N2
FFT_N1_LIVE = SEQ // FFT_N2
CH_GROUP = 8

VMEM_LIMIT = 56 * 1024 * 1024


def _cparams(n_axes):
    return pltpu.CompilerParams(
        dimension_semantics=("arbitrary",) * n_axes, vmem_limit_bytes=VMEM_LIMIT)


def _rms_rows(x, g):
    r = lax.rsqrt(jnp.mean(x * x, axis=-1, keepdims=True) + NORM_EPS)
    return x * r * g


IN_TM = 1024
IN_TN = 1024


def _in_proj_kernel(x_ref, g_ref, wt_ref, zt_ref, h_ref):
    @pl.when(pl.program_id(1) == 0)
    def _():
        h_ref[...] = _rms_rows(x_ref[...], g_ref[...]).astype(BF16)

    zt_ref[...] = lax.dot_general(
        wt_ref[...], h_ref[...], (((1,), (1,)), ((), ())), preferred_element_type=F32)


def _in_proj(x2d, g, wt):
    return pl.pallas_call(
        _in_proj_kernel,
        grid=(SEQ // IN_TM, D_IN // IN_TN),
        in_specs=[
            pl.BlockSpec((IN_TM, D_MODEL), lambda i, j: (i, 0)),
            pl.BlockSpec((1, D_MODEL), lambda i, j: (0, 0)),
            pl.BlockSpec((IN_TN, D_MODEL), lambda i, j: (j, 0)),
        ],
        out_specs=pl.BlockSpec((IN_TN, IN_TM), lambda i, j: (j, i)),
        out_shape=jax.ShapeDtypeStruct((D_IN, SEQ), F32),
        scratch_shapes=[pltpu.VMEM((IN_TM, D_MODEL), BF16)],
        compiler_params=_cparams(2),
        name="in_proj",
    )(x2d, g, wt)


GATE_CB = 32


def _conv3_lanes(z, w, lane):
    zm = jnp.where(lane == 0, 0.0, pltpu.roll(z, 1, 1))
    zp = jnp.where(lane == SEQ - 1, 0.0, pltpu.roll(z, SEQ - 1, 1))
    return zm * w[:, 0:1] + z * w[:, 1:2] + zp * w[:, 2:3]


def _gate_kernel(zx0_ref, zx1_ref, zv_ref, zgb_ref, zgc_ref, zxv_ref,
                 px0_ref, px1_ref, pv_ref, psc_ref, x0_ref, u_ref, yb_ref):
    lane = lax.broadcasted_iota(jnp.int32, (GATE_CB, SEQ), 1)
    px0, px1, pv = px0_ref[...], px1_ref[...], pv_ref[...]
    x0 = _conv3_lanes(zx0_ref[...], px0, lane) + px0[:, 3:4]
    x1 = _conv3_lanes(zx1_ref[...], px1, lane) + px1[:, 3:4]
    v = _conv3_lanes(zv_ref[...], pv, lane) + pv[:, 3:4]
    x0_ref[...] = x0.astype(BF16)
    u_ref[...] = (v * x1).astype(BF16)
    yb = zgb_ref[...] * _conv3_lanes(zgc_ref[...] * zxv_ref[...], psc_ref[...], lane)
    yb_ref[...] = yb.astype(BF16)


def _gate(zt, hy_par, sc_par):
    nb = D_HY // GATE_CB
    zspec = lambda s: pl.BlockSpec((GATE_CB, SEQ), lambda j, s=s: (s * nb + j, 0))
    pspec = lambda s: pl.BlockSpec((GATE_CB, 4), lambda j, s=s: (s * nb + j, 0))
    out = jax.ShapeDtypeStruct((D_HY, SEQ), BF16)
    ospec = pl.BlockSpec((GATE_CB, SEQ), lambda j: (j, 0))
    return pl.pallas_call(
        _gate_kernel,
        grid=(nb,),
        in_specs=[zspec(s) for s in range(6)] + [pspec(0), pspec(1), pspec(2),
                                                 pl.BlockSpec((GATE_CB, 4), lambda j: (j, 0))],
        out_specs=[ospec, ospec, ospec],
        out_shape=[out, out, out],
        compiler_params=_cparams(1),
        name="gate",
    )(zt, zt, zt, zt, zt, zt, hy_par, hy_par, hy_par, sc_par)


FILT_TB = 1024


def _filter_kernel(emb_ref, t_ref, w1_ref, b1_ref, w2_ref, b2_ref, fr_ref, w3_ref, b3_ref,
                   ad_ref, hf_ref, hb_ref):
    hi = lax.Precision.HIGHEST
    fr = fr_ref[...]
    a1 = jnp.dot(w1_ref[...], emb_ref[...], precision=hi, preferred_element_type=F32) + b1_ref[...]
    h1 = jnp.sin(fr * a1)
    a2 = jnp.dot(w2_ref[...], h1, precision=hi, preferred_element_type=F32) + b2_ref[...]
    h2 = jnp.sin(fr * a2)
    o = jnp.dot(w3_ref[...], h2.astype(BF16), preferred_element_type=F32) + b3_ref[...]
    window = jnp.exp(-(ad_ref[...] * t_ref[...]))
    hf_ref[...] = (o[:D_HY] * window).astype(BF16)
    pos = lax.broadcasted_iota(jnp.int32, (D_HY, FILT_TB), 1) + pl.program_id(0) * FILT_TB
    hb_ref[...] = jnp.where(pos == 0, 0.0, o[D_HY:] * window).astype(BF16)


def _filters(emb_t, t_row, w1t, b1, w2t, b2, fr, w3t, b3, absdelta):
    full = lambda a: pl.BlockSpec(a.shape, lambda i: (0,) * a.ndim)
    out = jax.ShapeDtypeStruct((D_HY, SEQ), BF16)
    return pl.pallas_call(
        _filter_kernel,
        grid=(SEQ // FILT_TB,),
        in_specs=[pl.BlockSpec((EMB_PAD, FILT_TB), lambda i: (0, i)),
                  pl.BlockSpec((1, FILT_TB), lambda i: (0, i)),
                  full(w1t), full(b1), full(w2t), full(b2), full(fr), full(w3t), full(b3),
                  full(absdelta)],
        out_specs=[pl.BlockSpec((D_HY, FILT_TB), lambda i: (0, i))] * 2,
        out_shape=[out, out],
        compiler_params=_cparams(1),
        name="filters",
    )(emb_t, t_row, w1t, b1, w2t, b2, fr, w3t, b3, absdelta)


HY_CB = 32
GROUP_ROWS = CH_GROUP * FFT_N1_LIVE
HALF_ROWS = CH_GROUP * FFT_N1


def _dft_tables():
    n1 = np.arange(FFT_N1_LIVE, dtype=np.float64)
    k1 = np.arange(FFT_N1, dtype=np.float64)
    th = 2.0 * np.pi * np.outer(k1, n1) / FFT_N1
    f1 = np.concatenate([np.cos(th), -np.sin(th)], axis=0)
    bd1 = np.kron(np.eye(CH_GROUP), f1)
    g1 = np.concatenate([np.cos(th).T, -np.sin(th).T], axis=1) / FFT_N
    bd1inv = np.kron(np.eye(CH_GROUP), g1)
    n2 = np.arange(FFT_N2, dtype=np.float64)
    ph = 2.0 * np.pi * np.outer(k1, n2) / FFT_N
    twr, twi = np.cos(ph), -np.sin(ph)
    ps = 2.0 * np.pi * np.outer(n2, n2) / FFT_N2
    cr, ci = np.cos(ps), -np.sin(ps)
    w3 = np.block([[cr, ci], [-ci, cr]])
    w3inv = np.block([[cr, -ci], [ci, cr]])
    f32 = lambda a: jnp.asarray(a, F32)
    return (f32(bd1).astype(BF16), f32(bd1inv).astype(BF16), f32(twr), f32(twi),
            f32(w3).astype(BF16), f32(w3inv).astype(BF16))


def _hyena_kernel(u_ref, hf_ref, hb_ref, x0_ref, bias_ref, bd1_ref, bd1inv_ref, twr_ref, twi_ref,
                  w3_ref, w3inv_ref, ya_ref):
    twr, twi = twr_ref[...], twi_ref[...]

    def forward(xq):
        a = jnp.dot(bd1_ref[...], xq, preferred_element_type=F32)
        a4 = a.reshape(CH_GROUP, 2, FFT_N1, FFT_N2)
        ar, ai = a4[:, 0], a4[:, 1]
        br = (ar * twr - ai * twi).reshape(HALF_ROWS, FFT_N2)
        bi = (ar * twi + ai * twr).reshape(HALF_ROWS, FFT_N2)
        bcat = jnp.concatenate([br, bi], axis=1).astype(BF16)
        return jnp.dot(bcat, w3_ref[...], preferred_element_type=F32)

    def group(gi, carry):
        r0 = pl.multiple_of(gi * GROUP_ROWS, GROUP_ROWS)
        rows = pl.ds(r0, GROUP_ROWS)
        u = u_ref[rows, :]
        xs = forward(u)
        fs = forward(hf_ref[rows, :])
        bs = forward(hb_ref[rows, :])
        xr, xi = xs[:, :FFT_N2], xs[:, FFT_N2:]
        kr = fs[:, :FFT_N2] + bs[:, :FFT_N2]
        ki = fs[:, FFT_N2:] - bs[:, FFT_N2:]
        ycat = jnp.concatenate([xr * kr - xi * ki, xr * ki + xi * kr], axis=1).astype(BF16)
        ap = jnp.dot(ycat, w3inv_ref[...], preferred_element_type=F32)
        apr = ap[:, :FFT_N2].reshape(CH_GROUP, 1, FFT_N1, FFT_N2)
        api = ap[:, FFT_N2:].reshape(CH_GROUP, 1, FFT_N1, FFT_N2)
        bpr = apr * twr + api * twi
        bpi = api * twr - apr * twi
        b2 = jnp.concatenate([bpr, bpi], axis=1).reshape(CH_GROUP * 2 * FFT_N1, FFT_N2)
        y = jnp.dot(bd1inv_ref[...], b2.astype(BF16), preferred_element_type=F32)
        y3 = y.reshape(CH_GROUP, FFT_N1_LIVE, FFT_N2)
        u3 = u.astype(F32).reshape(CH_GROUP, FFT_N1_LIVE, FFT_N2)
        x03 = x0_ref[rows, :].astype(F32).reshape(CH_GROUP, FFT_N1_LIVE, FFT_N2)
        bias = bias_ref[pl.ds(pl.multiple_of(gi * CH_GROUP, CH_GROUP), CH_GROUP)]
        ya = x03 * (y3 + bias * u3)
        ya_ref[rows, :] = ya.reshape(GROUP_ROWS, FFT_N2).astype(BF16)
        return carry

    lax.fori_loop(0, HY_CB // CH_GROUP, group, 0)


def _hyena(u_q, hf_q, hb_q, x0_q, bias3, tables):
    bd1, bd1inv, twr, twi, w3, w3inv = tables
    rows = HY_CB * FFT_N1_LIVE
    qspec = pl.BlockSpec((rows, FFT_N2), lambda j: (j, 0))
    full = lambda a: pl.BlockSpec(a.shape, lambda j: (0,) * a.ndim)
    return pl.pallas_call(
        _hyena_kernel,
        grid=(D_HY // HY_CB,),
        in_specs=[qspec, qspec, qspec, qspec,
                  pl.BlockSpec((HY_CB, 1, FFT_N2), lambda j: (j, 0, 0)),
                  full(bd1), full(bd1inv), full(twr), full(twi), full(w3), full(w3inv)],
        out_specs=qspec,
        out_shape=jax.ShapeDtypeStruct((D_HY * FFT_N1_LIVE, FFT_N2), BF16),
        compiler_params=_cparams(1),
        name="hyena_fftconv",
    )(u_q, hf_q, hb_q, x0_q, bias3, bd1, bd1inv, twr, twi, w3, w3inv)


OUT_TM = 512


def _out_proj_kernel(ya_ref, yb_ref, wa_ref, wb_ref, x_ref, g_ref, o_ref):
    tn = (((0,), (0,)), ((), ()))
    m = lax.dot_general(ya_ref[...], wa_ref[...], tn, preferred_element_type=F32)
    m = m + lax.dot_general(yb_ref[...], wb_ref[...], tn, preferred_element_type=F32)
    o_ref[...] = x_ref[...] + _rms_rows(m, g_ref[...])


def _out_proj(ya_t, yb_t, wa, wb, x2d, g):
    return pl.pallas_call(
        _out_proj_kernel,
        grid=(SEQ // OUT_TM,),
        in_specs=[
            pl.BlockSpec((D_HY, OUT_TM), lambda i: (0, i)),
            pl.BlockSpec((D_SC, OUT_TM), lambda i: (0, i)),
            pl.BlockSpec((D_HY, D_MODEL), lambda i: (0, 0)),
            pl.BlockSpec((D_SC, D_MODEL), lambda i: (0, 0)),
            pl.BlockSpec((OUT_TM, D_MODEL), lambda i: (i, 0)),
            pl.BlockSpec((1, D_MODEL), lambda i: (0, 0)),
        ],
        out_specs=pl.BlockSpec((OUT_TM, D_MODEL), lambda i: (i, 0)),
        out_shape=jax.ShapeDtypeStruct((SEQ, D_MODEL), F32),
        compiler_params=_cparams(1),
        name="out_proj",
    )(ya_t, yb_t, wa, wb, x2d, g)


MLP_TM = 512
MLP_TK = 1024


def _mlp_kernel(x_ref, gin_ref, gout_ref, w1_ref, w2_ref, o_ref, h_ref, acc_ref):
    k = pl.program_id(1)

    @pl.when(k == 0)
    def _():
        h_ref[...] = _rms_rows(x_ref[...], gin_ref[...]).astype(BF16)

    a = jnp.dot(h_ref[...], w1_ref[...], preferred_element_type=F32)
    a = jnp.square(jnp.maximum(a, 0.0)).astype(BF16)
    part = jnp.dot(a, w2_ref[...], preferred_element_type=F32)

    @pl.when(k == 0)
    def _():
        acc_ref[...] = part

    @pl.when(k > 0)
    def _():
        acc_ref[...] += part

    @pl.when(k == pl.num_programs(1) - 1)
    def _():
        o_ref[...] = x_ref[...] + _rms_rows(acc_ref[...], gout_ref[...])


def _mlp(x2d, gin, gout, w1, w2):
    return pl.pallas_call(
        _mlp_kernel,
        grid=(SEQ // MLP_TM, D_FF // MLP_TK),
        in_specs=[
            pl.BlockSpec((MLP_TM, D_MODEL), lambda i, k: (i, 0)),
            pl.BlockSpec((1, D_MODEL), lambda i, k: (0, 0)),
            pl.BlockSpec((1, D_MODEL), lambda i, k: (0, 0)),
            pl.BlockSpec((D_MODEL, MLP_TK), lambda i, k: (0, k)),
            pl.BlockSpec((MLP_TK, D_MODEL), lambda i, k: (k, 0)),
        ],
        out_specs=pl.BlockSpec((MLP_TM, D_MODEL), lambda i, k: (i, 0)),
        out_shape=jax.ShapeDtypeStruct((SEQ, D_MODEL), F32),
        scratch_shapes=[pltpu.VMEM((MLP_TM, D_MODEL), BF16), pltpu.VMEM((MLP_TM, D_MODEL), F32)],
        compiler_params=_cparams(2),
        name="mlp",
    )(x2d, gin, gout, w1, w2)


POOL_TM = 512
HALO = 8


def _pool_kernel(xp_ref, xm_ref, xn_ref, gin_ref, gout_ref, pw_ref, ps_ref, o_ref, hs_ref, m_ref):
    i = pl.program_id(0)
    gin = gin_ref[...]
    hs_ref[0:HALO, :] = jnp.where(i > 0, _rms_rows(xp_ref[...], gin), 0.0)
    hs_ref[HALO:HALO + POOL_TM, :] = _rms_rows(xm_ref[...], gin)
    hs_ref[HALO + POOL_TM:, :] = jnp.where(i < pl.num_programs(0) - 1, _rms_rows(xn_ref[...], gin), 0.0)
    row = lax.broadcasted_iota(jnp.int32, (POOL_TM, D_POOL_G), 0) + i * POOL_TM
    for gi, w in enumerate(POOL_WINDOWS):
        r = w // 2
        cols = slice(gi * D_POOL_G, (gi + 1) * D_POOL_G)
        s = hs_ref[HALO - r:HALO - r + POOL_TM, cols]
        for j in range(-r + 1, r + 1):
            s = s + hs_ref[HALO + j:HALO + j + POOL_TM, cols]
        cnt = jnp.minimum(row + r + 1, SEQ) - jnp.maximum(row - r, 0)
        d = s / cnt.astype(F32) - hs_ref[HALO:HALO + POOL_TM, cols]
        mg = jnp.dot(d.astype(BF16), pw_ref[gi], preferred_element_type=F32)
        m_ref[:, cols] = mg * ps_ref[:, cols]
    o_ref[...] = xm_ref[...] + _rms_rows(m_ref[...], gout_ref[...])


def _pool(x2d, gin, gout, pw, ps):
    nb8 = POOL_TM // HALO
    last8 = SEQ // HALO - 1
    return pl.pallas_call(
        _pool_kernel,
        grid=(SEQ // POOL_TM,),
        in_specs=[
            pl.BlockSpec((HALO, D_MODEL), lambda i: (jnp.maximum(i * nb8 - 1, 0), 0)),
            pl.BlockSpec((POOL_TM, D_MODEL), lambda i: (i, 0)),
            pl.BlockSpec((HALO, D_MODEL), lambda i: (jnp.minimum((i + 1) * nb8, last8), 0)),
            pl.BlockSpec((1, D_MODEL), lambda i: (0, 0)),
            pl.BlockSpec((1, D_MODEL), lambda i: (0, 0)),
            pl.BlockSpec((len(POOL_WINDOWS), D_POOL_G, D_POOL_G), lambda i: (0, 0, 0)),
            pl.BlockSpec((1, D_MODEL), lambda i: (0, 0)),
        ],
        out_specs=pl.BlockSpec((POOL_TM, D_MODEL), lambda i: (i, 0)),
        out_shape=jax.ShapeDtypeStruct((SEQ, D_MODEL), F32),
        scratch_shapes=[pltpu.VMEM((POOL_TM + 2 * HALO, D_MODEL), F32),
                        pltpu.VMEM((POOL_TM, D_MODEL), F32)],
        compiler_params=_cparams(1),
        name="pool_mixer",
    )(x2d, x2d, x2d, gin, gout, pw, ps)


def _filter_constants():
    L = SEQ
    t = np.linspace(0.0, 1.0, L)
    w_pos = 2.0 * np.pi * np.arange(L) / L
    bands = np.linspace(1e-4, FILTER_BANDS - 1, FILTER_BANDS)
    ang = w_pos[:, None] * bands[None, :]
    emb = np.concatenate([t[:, None], np.cos(ang), -np.sin(ang)], axis=-1)
    emb_t = np.zeros((EMB_PAD, L))
    emb_t[:EMB_DIM] = emb.T
    max_decay = math.log(DECAY_TARGET) / FAST_DECAY_PCT
    min_decay = math.log(DECAY_TARGET) / SLOW_DECAY_PCT
    absdelta = np.abs(np.linspace(min_decay, max_decay, D_HY))[:, None]
    return (jnp.asarray(emb_t, F32), jnp.asarray(t[None, :], F32), jnp.asarray(absdelta, F32))


def _mixer_layer0(x2d, g, w_in, hy_short_w, hy_short_b, f_w1, f_b1, f_w2, f_b2, f_w3, f_b3,
                  freq, hy_bias, sc_conv_w, w_out):
    wt = w_in.T.astype(BF16)
    zt = _in_proj(x2d, g[0][None, :], wt)

    hy_par = jnp.concatenate([hy_short_w.T, hy_short_b[:, None]], axis=1)
    sc_par = jnp.concatenate([sc_conv_w.T, jnp.zeros((D_SC, 1), F32)], axis=1)
    x0_t, u_t, yb_t = _gate(zt, hy_par, sc_par)

    emb_t, t_row, absdelta = _filter_constants()
    w1t = jnp.zeros((FILTER_HIDDEN, EMB_PAD), F32).at[:, :EMB_DIM].set(f_w1.T)
    col = lambda v: v[:, None].astype(F32)
    hf_t, hb_t = _filters(emb_t, t_row, w1t, col(f_b1), f_w2.T, col(f_b2), col(freq),
                          f_w3.T.astype(BF16), col(f_b3), absdelta)

    to_q = lambda a: a.reshape(D_HY * FFT_N1_LIVE, FFT_N2)
    bias3 = jnp.broadcast_to(hy_bias[:, None, None].astype(F32), (D_HY, 1, FFT_N2))
    ya_q = _hyena(to_q(u_t), to_q(hf_t), to_q(hb_t), to_q(x0_t), bias3, _dft_tables())
    ya_t = ya_q.reshape(D_HY, SEQ)

    wo = w_out.astype(BF16)
    return _out_proj(ya_t, yb_t, wo[:D_HY], wo[D_HY:], x2d, g[1][None, :])


def kernel(x, norm_g, mix_w_in, hy_short_w, hy_short_b, hy_filt_w1, hy_filt_b1, hy_filt_w2,
           hy_filt_b2, hy_filt_w3, hy_filt_b3, hy_freq, hy_bias, sc_conv_w, mix_w_out,
           pool_w, pool_scale, mlp_w1, mlp_w2):
    x2d = x.reshape(SEQ, D_MODEL)
    depth = norm_g.shape[0]
    for i in range(depth):
        g = norm_g[i]
        j = i // 2
        if i % 2 == 0:
            x2d = _mixer_layer0(x2d, g, mix_w_in[j], hy_short_w[j], hy_short_b[j], hy_filt_w1[j],
                                hy_filt_b1[j], hy_filt_w2[j], hy_filt_b2[j], hy_filt_w3[j],
                                hy_filt_b3[j], hy_freq[j], hy_bias[j], sc_conv_w[j], mix_w_out[j])
        else:
            x2d = _pool(x2d, g[0][None, :], g[1][None, :], pool_w[j].astype(BF16),
                        pool_scale[j][None, :])
        x2d = _mlp(x2d, g[2][None, :], g[3][None, :], mlp_w1[i].astype(BF16), mlp_w2[i].astype(BF16))
    return x2d.reshape(x.shape)
```

```python
import functools
import math

import numpy as np
import jax
import jax.numpy as jnp
from jax import lax
from jax.experimental import pallas as pl
from jax.experimental.pallas import tpu as pltpu

F32 = jnp.float32
BF16 = jnp.bfloat16

D_MODEL = 2048
SEQ = 8192
D_HY = D_MODEL // 2
D_SC = D_MODEL // 2
D_IN = 3 * D_HY + 3 * D_SC
FILTER_BANDS = 16
EMB_DIM = 1 + 2 * FILTER_BANDS
EMB_PAD = 40
FILTER_HIDDEN = 64
DECAY_TARGET = 1e-2
FAST_DECAY_PCT = 0.3
SLOW_DECAY_PCT = 1.5
POOL_WINDOWS = (2, 4, 8, 16)
D_POOL_G = D_MODEL // len(POOL_WINDOWS)
D_FF = 4 * D_MODEL
NORM_EPS = 1e-6

FFT_N = 2 * SEQ
FFT_N2 = 256
FFT_N1 = FFT_N // FFT_N2
FFT_N1_LIVE = SEQ // FFT_N2
CH_GROUP = 8
Q8_SHAPE = (D_MODEL // 2 // CH_GROUP, FFT_N1_LIVE, CH_GROUP, FFT_N2)

VMEM_LIMIT = 56 * 1024 * 1024


def _cparams(n_axes):
    return pltpu.CompilerParams(
        dimension_semantics=("arbitrary",) * n_axes, vmem_limit_bytes=VMEM_LIMIT)


def _rms_rows(x, g):
    r = lax.rsqrt(jnp.mean(x * x, axis=-1, keepdims=True) + NORM_EPS)
    return x * r * g


IN_TM = 1024
IN_TN = 1024


def _in_proj_kernel(x_ref, g_ref, wt_ref, zt_ref, h_ref):
    @pl.when(pl.program_id(1) == 0)
    def _():
        h_ref[...] = _rms_rows(x_ref[...], g_ref[...]).astype(BF16)

    zt_ref[...] = lax.dot_general(
        wt_ref[...], h_ref[...], (((1,), (1,)), ((), ())), preferred_element_type=F32)


def _in_proj(x2d, g, wt):
    return pl.pallas_call(
        _in_proj_kernel,
        grid=(SEQ // IN_TM, D_IN // IN_TN),
        in_specs=[
            pl.BlockSpec((IN_TM, D_MODEL), lambda i, j: (i, 0)),
            pl.BlockSpec((1, D_MODEL), lambda i, j: (0, 0)),
            pl.BlockSpec((IN_TN, D_MODEL), lambda i, j: (j, 0)),
        ],
        out_specs=pl.BlockSpec((IN_TN, IN_TM), lambda i, j: (j, i)),
        out_shape=jax.ShapeDtypeStruct((D_IN, SEQ), F32),
        scratch_shapes=[pltpu.VMEM((IN_TM, D_MODEL), BF16)],
        compiler_params=_cparams(2),
        name="in_proj",
    )(x2d, g, wt)


GATE_CB = 32


def _conv3_lanes(z, w, lane):
    zm = jnp.where(lane == 0, 0.0, pltpu.roll(z, 1, 1))
    zp = jnp.where(lane == SEQ - 1, 0.0, pltpu.roll(z, SEQ - 1, 1))
    return zm * w[:, 0:1] + z * w[:, 1:2] + zp * w[:, 2:3]


def _store_q(ref, val):
    for cg in range(val.shape[0] // CH_GROUP):
        for n1 in range(val.shape[1] // FFT_N2):
            ref[cg, n1] = val[cg * CH_GROUP:(cg + 1) * CH_GROUP, n1 * FFT_N2:(n1 + 1) * FFT_N2]


def _gate_kernel(zx0_ref, zx1_ref, zv_ref, zgb_ref, zgc_ref, zxv_ref,
                 px0_ref, px1_ref, pv_ref, psc_ref, x0_ref, u_ref, yb_ref):
    lane = lax.broadcasted_iota(jnp.int32, (GATE_CB, SEQ), 1)
    px0, px1, pv = px0_ref[...], px1_ref[...], pv_ref[...]
    x0 = _conv3_lanes(zx0_ref[...], px0, lane) + px0[:, 3:4]
    x1 = _conv3_lanes(zx1_ref[...], px1, lane) + px1[:, 3:4]
    v = _conv3_lanes(zv_ref[...], pv, lane) + pv[:, 3:4]
    _store_q(x0_ref, x0)
    _store_q(u_ref, v * x1)
    yb = zgb_ref[...] * _conv3_lanes(zgc_ref[...] * zxv_ref[...], psc_ref[...], lane)
    yb_ref[...] = yb.astype(BF16)


def _gate(zt, hy_par, sc_par):
    nb = D_HY // GATE_CB
    zspec = lambda s: pl.BlockSpec((GATE_CB, SEQ), lambda j, s=s: (s * nb + j, 0))
    pspec = lambda s: pl.BlockSpec((GATE_CB, 4), lambda j, s=s: (s * nb + j, 0))
    out = jax.ShapeDtypeStruct((D_HY, SEQ), BF16)
    ospec = pl.BlockSpec((GATE_CB, SEQ), lambda j: (j, 0))
    qout = jax.ShapeDtypeStruct(Q8_SHAPE, F32)
    qspec = pl.BlockSpec((GATE_CB // CH_GROUP,) + Q8_SHAPE[1:], lambda j: (j, 0, 0, 0))
    return pl.pallas_call(
        _gate_kernel,
        grid=(nb,),
        in_specs=[zspec(s) for s in range(6)] + [pspec(0), pspec(1), pspec(2),
                                                 pl.BlockSpec((GATE_CB, 4), lambda j: (j, 0))],
        out_specs=[qspec, qspec, ospec],
        out_shape=[qout, qout, out],
        compiler_params=_cparams(1),
        name="gate",
    )(zt, zt, zt, zt, zt, zt, hy_par, hy_par, hy_par, sc_par)


FILT_TB = 1024


def _filter_kernel(emb_ref, t_ref, w1_ref, b1_ref, w2_ref, b2_ref, fr_ref, w3_ref, b3_ref,
                   ad_ref, hf_ref, hb_ref, hfs_ref, hbs_ref):
    hi = lax.Precision.HIGHEST
    fr = fr_ref[...]
    a1 = jnp.dot(w1_ref[...], emb_ref[...], precision=hi, preferred_element_type=F32) + b1_ref[...]
    h1 = jnp.sin(fr * a1)
    a2 = jnp.dot(w2_ref[...], h1, precision=hi, preferred_element_type=F32) + b2_ref[...]
    h2 = jnp.sin(fr * a2)
    o = jnp.dot(w3_ref[...], h2.astype(BF16), preferred_element_type=F32) + b3_ref[...]
    window = jnp.exp(-(ad_ref[...] * t_ref[...]))
    hfs_ref[...] = o[:D_HY] * window
    pos = lax.broadcasted_iota(jnp.int32, (D_HY, FILT_TB), 1) + pl.program_id(0) * FILT_TB
    hbs_ref[...] = jnp.where(pos == 0, 0.0, o[D_HY:] * window)

    def to_q(cg, carry):
        rows = pl.ds(pl.multiple_of(cg * CH_GROUP, CH_GROUP), CH_GROUP)
        for n1 in range(FILT_TB // FFT_N2):
            lanes = slice(n1 * FFT_N2, (n1 + 1) * FFT_N2)
            hf_ref[cg, n1] = hfs_ref[rows, lanes]
            hb_ref[cg, n1] = hbs_ref[rows, lanes]
        return carry

    lax.fori_loop(0, D_HY // CH_GROUP, to_q, 0)


def _filters(emb_t, t_row, w1t, b1, w2t, b2, fr, w3t, b3, absdelta):
    full = lambda a: pl.BlockSpec(a.shape, lambda i: (0,) * a.ndim)
    out = jax.ShapeDtypeStruct(Q8_SHAPE, F32)
    qspec = pl.BlockSpec((Q8_SHAPE[0], FILT_TB // FFT_N2, CH_GROUP, FFT_N2), lambda i: (0, i, 0, 0))
    return pl.pallas_call(
        _filter_kernel,
        grid=(SEQ // FILT_TB,),
        in_specs=[pl.BlockSpec((EMB_PAD, FILT_TB), lambda i: (0, i)),
                  pl.BlockSpec((1, FILT_TB), lambda i: (0, i)),
                  full(w1t), full(b1), full(w2t), full(b2), full(fr), full(w3t), full(b3),
                  full(absdelta)],
        out_specs=[qspec, qspec],
        out_shape=[out, out],
        scratch_shapes=[pltpu.VMEM((D_HY, FILT_TB), F32)] * 2,
        compiler_params=_cparams(1),
        name="filters",
    )(emb_t, t_row, w1t, b1, w2t, b2, fr, w3t, b3, absdelta)


HY_CB = 32
GROUP_ROWS = CH_GROUP * FFT_N1_LIVE
HALF_ROWS = CH_GROUP * FFT_N1


def _dft_tables():
    n1 = np.arange(FFT_N1_LIVE, dtype=np.float64)
    k1 = np.arange(FFT_N1, dtype=np.float64)
    th = 2.0 * np.pi * np.outer(k1, n1) / FFT_N1
    f1 = np.concatenate([np.cos(th), -np.sin(th)], axis=0)
    g1 = np.concatenate([np.cos(th).T, -np.sin(th).T], axis=1) / FFT_N
    bd1 = np.zeros((CH_GROUP, 2 * FFT_N1, FFT_N1_LIVE, CH_GROUP))
    bd1inv = np.zeros((FFT_N1_LIVE, CH_GROUP, CH_GROUP, 2 * FFT_N1))
    for c in range(CH_GROUP):
        bd1[c, :, :, c] = f1
        bd1inv[:, c, c, :] = g1
    bd1 = bd1.reshape(CH_GROUP * 2 * FFT_N1, GROUP_ROWS)
    bd1inv = bd1inv.reshape(GROUP_ROWS, CH_GROUP * 2 * FFT_N1)
    n2 = np.arange(FFT_N2, dtype=np.float64)
    ph = 2.0 * np.pi * np.outer(k1, n2) / FFT_N
    twr, twi = np.cos(ph), -np.sin(ph)
    ps = 2.0 * np.pi * np.outer(n2, n2) / FFT_N2
    cr, ci = np.cos(ps), -np.sin(ps)
    w3 = np.block([[cr, ci], [-ci, cr]])
    w3inv = np.block([[cr, -ci], [ci, cr]])
    f32 = lambda a: jnp.asarray(a, F32)
    return (f32(bd1).astype(BF16), f32(bd1inv).astype(BF16), f32(twr), f32(twi),
            f32(w3).astype(BF16), f32(w3inv).astype(BF16))


def _hyena_kernel(u_ref, hf_ref, hb_ref, x0_ref, bias_ref, bd1_ref, bd1inv_ref, twr_ref, twi_ref,
                  w3_ref, w3inv_ref, ya_ref):
    twr, twi = twr_ref[...], twi_ref[...]

    def forward(xq):
        xq = xq.reshape(GROUP_ROWS, FFT_N2).astype(BF16)
        a = jnp.dot(bd1_ref[...], xq, preferred_element_type=F32)
        a4 = a.reshape(CH_GROUP, 2, FFT_N1, FFT_N2)
        ar, ai = a4[:, 0], a4[:, 1]
        br = (ar * twr - ai * twi).reshape(HALF_ROWS, FFT_N2)
        bi = (ar * twi + ai * twr).reshape(HALF_ROWS, FFT_N2)
        bcat = jnp.concatenate([br, bi], axis=1).astype(BF16)
        return jnp.dot(bcat, w3_ref[...], preferred_element_type=F32)

    def group(gi, carry):
        u = u_ref[gi]
        xs = forward(u)
        fs = forward(hf_ref[gi])
        bs = forward(hb_ref[gi])
        xr, xi = xs[:, :FFT_N2], xs[:, FFT_N2:]
        kr = fs[:, :FFT_N2] + bs[:, :FFT_N2]
        ki = fs[:, FFT_N2:] - bs[:, FFT_N2:]
        ycat = jnp.concatenate([xr * kr - xi * ki, xr * ki + xi * kr], axis=1).astype(BF16)
        ap = jnp.dot(ycat, w3inv_ref[...], preferred_element_type=F32)
        apr = ap[:, :FFT_N2].reshape(CH_GROUP, 1, FFT_N1, FFT_N2)
        api = ap[:, FFT_N2:].reshape(CH_GROUP, 1, FFT_N1, FFT_N2)
        bpr = apr * twr + api * twi
        bpi = api * twr - apr * twi
        b2 = jnp.concatenate([bpr, bpi], axis=1).reshape(CH_GROUP * 2 * FFT_N1, FFT_N2)
        y = jnp.dot(bd1inv_ref[...], b2.astype(BF16), preferred_element_type=F32)
        y3 = y.reshape(FFT_N1_LIVE, CH_GROUP, FFT_N2)
        bias = bias_ref[pl.ds(pl.multiple_of(gi * CH_GROUP, CH_GROUP), CH_GROUP), :]
        ya_ref[gi] = x0_ref[gi] * (y3 + bias[None] * u)
        return carry

    lax.fori_loop(0, HY_CB // CH_GROUP, group, 0)


def _hyena(u_q, hf_q, hb_q, x0_q, bias2, tables):
    bd1, bd1inv, twr, twi, w3, w3inv = tables
    qspec = pl.BlockSpec((HY_CB // CH_GROUP,) + Q8_SHAPE[1:], lambda j: (j, 0, 0, 0))
    full = lambda a: pl.BlockSpec(a.shape, lambda j: (0,) * a.ndim)
    return pl.pallas_call(
        _hyena_kernel,
        grid=(D_HY // HY_CB,),
        in_specs=[qspec, qspec, qspec, qspec,
                  pl.BlockSpec((HY_CB, FFT_N2), lambda j: (j, 0)),
                  full(bd1), full(bd1inv), full(twr), full(twi), full(w3), full(w3inv)],
        out_specs=qspec,
        out_shape=jax.ShapeDtypeStruct(Q8_SHAPE, F32),
        compiler_params=_cparams(1),
        name="hyena_fftconv",
    )(u_q, hf_q, hb_q, x0_q, bias2, bd1, bd1inv, twr, twi, w3, w3inv)


OUT_TM = 512


def _out_proj_kernel(ya_ref, yb_ref, wa_ref, wb_ref, x_ref, g_ref, o_ref):
    tn = (((0,), (0,)), ((), ()))
    ya = jnp.concatenate([ya_ref[:, a].reshape(D_HY, FFT_N2) for a in range(OUT_TM // FFT_N2)],
                         axis=1).astype(BF16)
    m = lax.dot_general(ya, wa_ref[...], tn, preferred_element_type=F32)
    m = m + lax.dot_general(yb_ref[...], wb_ref[...], tn, preferred_element_type=F32)
    o_ref[...] = x_ref[...] + _rms_rows(m, g_ref[...])


def _out_proj(ya_t, yb_t, wa, wb, x2d, g):
    return pl.pallas_call(
        _out_proj_kernel,
        grid=(SEQ // OUT_TM,),
        in_specs=[
            pl.BlockSpec((Q8_SHAPE[0], OUT_TM // FFT_N2, CH_GROUP, FFT_N2), lambda i: (0, i, 0, 0)),
            pl.BlockSpec((D_SC, OUT_TM), lambda i: (0, i)),
            pl.BlockSpec((D_HY, D_MODEL), lambda i: (0, 0)),
            pl.BlockSpec((D_SC, D_MODEL), lambda i: (0, 0)),
            pl.BlockSpec((OUT_TM, D_MODEL), lambda i: (i, 0)),
            pl.BlockSpec((1, D_MODEL), lambda i: (0, 0)),
        ],
        out_specs=pl.BlockSpec((OUT_TM, D_MODEL), lambda i: (i, 0)),
        out_shape=jax.ShapeDtypeStruct((SEQ, D_MODEL), F32),
        compiler_params=_cparams(1),
        name="out_proj",
    )(ya_t, yb_t, wa, wb, x2d, g)


MLP_TM = 1024
MLP_TK = 512
MLP_TN = 512


def _mlp_kernel(x_ref, gin_ref, gout_ref, w1_ref, w2_ref, o_ref, h_ref):
    k = pl.program_id(1)

    @pl.when(k == 0)
    def _():
        h_ref[...] = _rms_rows(x_ref[...], gin_ref[...]).astype(BF16)
        o_ref[...] = jnp.zeros_like(o_ref)

    a = jnp.dot(h_ref[...], w1_ref[...], preferred_element_type=F32)
    a = jnp.square(jnp.maximum(a, 0.0)).astype(BF16)
    for n in range(D_MODEL // MLP_TN):
        cols = slice(n * MLP_TN, (n + 1) * MLP_TN)
        o_ref[:, cols] += jnp.dot(a, w2_ref[:, cols], preferred_element_type=F32)

    @pl.when(k == pl.num_programs(1) - 1)
    def _():
        o_ref[...] = x_ref[...] + _rms_rows(o_ref[...], gout_ref[...])


def _mlp(x2d, gin, gout, w1, w2):
    return pl.pallas_call(
        _mlp_kernel,
        grid=(SEQ // MLP_TM, D_FF // MLP_TK),
        in_specs=[
            pl.BlockSpec((MLP_TM, D_MODEL), lambda i, k: (i, 0)),
            pl.BlockSpec((1, D_MODEL), lambda i, k: (0, 0)),
            pl.BlockSpec((1, D_MODEL), lambda i, k: (0, 0)),
            pl.BlockSpec((D_MODEL, MLP_TK), lambda i, k: (0, k)),
            pl.BlockSpec((MLP_TK, D_MODEL), lambda i, k: (k, 0)),
        ],
        out_specs=pl.BlockSpec((MLP_TM, D_MODEL), lambda i, k: (i, 0)),
        out_shape=jax.ShapeDtypeStruct((SEQ, D_MODEL), F32),
        scratch_shapes=[pltpu.VMEM((MLP_TM, D_MODEL), BF16)],
        compiler_params=_cparams(2),
        name="mlp",
    )(x2d, gin, gout, w1, w2)


POOL_TM = 512
HALO = 8


def _pool_kernel(xp_ref, xm_ref, xn_ref, gin_ref, gout_ref, pw_ref, ps_ref, o_ref, hs_ref, m_ref):
    i = pl.program_id(0)
    gin = gin_ref[...]
    hs_ref[0:HALO, :] = jnp.where(i > 0, _rms_rows(xp_ref[...], gin), 0.0)
    hs_ref[HALO:HALO + POOL_TM, :] = _rms_rows(xm_ref[...], gin)
    hs_ref[HALO + POOL_TM:, :] = jnp.where(i < pl.num_programs(0) - 1, _rms_rows(xn_ref[...], gin), 0.0)
    row = lax.broadcasted_iota(jnp.int32, (POOL_TM, D_POOL_G), 0) + i * POOL_TM
    for gi, w in enumerate(POOL_WINDOWS):
        r = w // 2
        cols = slice(gi * D_POOL_G, (gi + 1) * D_POOL_G)
        s = hs_ref[HALO - r:HALO - r + POOL_TM, cols]
        for j in range(-r + 1, r + 1):
            s = s + hs_ref[HALO + j:HALO + j + POOL_TM, cols]
        cnt = jnp.minimum(row + r + 1, SEQ) - jnp.maximum(row - r, 0)
        d = s / cnt.astype(F32) - hs_ref[HALO:HALO + POOL_TM, cols]
        mg = jnp.dot(d.astype(BF16), pw_ref[gi], preferred_element_type=F32)
        m_ref[:, cols] = mg * ps_ref[:, cols]
    o_ref[...] = xm_ref[...] + _rms_rows(m_ref[...], gout_ref[...])


def _pool(x2d, gin, gout, pw, ps):
    nb8 = POOL_TM // HALO
    last8 = SEQ // HALO - 1
    return pl.pallas_call(
        _pool_kernel,
        grid=(SEQ // POOL_TM,),
        in_specs=[
            pl.BlockSpec((HALO, D_MODEL), lambda i: (jnp.maximum(i * nb8 - 1, 0), 0)),
            pl.BlockSpec((POOL_TM, D_MODEL), lambda i: (i, 0)),
            pl.BlockSpec((HALO, D_MODEL), lambda i: (jnp.minimum((i + 1) * nb8, last8), 0)),
            pl.BlockSpec((1, D_MODEL), lambda i: (0, 0)),
            pl.BlockSpec((1, D_MODEL), lambda i: (0, 0)),
            pl.BlockSpec((len(POOL_WINDOWS), D_POOL_G, D_POOL_G), lambda i: (0, 0, 0)),
            pl.BlockSpec((1, D_MODEL), lambda i: (0, 0)),
        ],
        out_specs=pl.BlockSpec((POOL_TM, D_MODEL), lambda i: (i, 0)),
        out_shape=jax.ShapeDtypeStruct((SEQ, D_MODEL), F32),
        scratch_shapes=[pltpu.VMEM((POOL_TM + 2 * HALO, D_MODEL), F32),
                        pltpu.VMEM((POOL_TM, D_MODEL), F32)],
        compiler_params=_cparams(1),
        name="pool_mixer",
    )(x2d, x2d, x2d, gin, gout, pw, ps)


def _filter_constants():
    L = SEQ
    t = np.linspace(0.0, 1.0, L)
    w_pos = 2.0 * np.pi * np.arange(L) / L
    bands = np.linspace(1e-4, FILTER_BANDS - 1, FILTER_BANDS)
    ang = w_pos[:, None] * bands[None, :]
    emb = np.concatenate([t[:, None], np.cos(ang), -np.sin(ang)], axis=-1)
    emb_t = np.zeros((EMB_PAD, L))
    emb_t[:EMB_DIM] = emb.T
    max_decay = math.log(DECAY_TARGET) / FAST_DECAY_PCT
    min_decay = math.log(DECAY_TARGET) / SLOW_DECAY_PCT
    absdelta = np.abs(np.linspace(min_decay, max_decay, D_HY))[:, None]
    return (jnp.asarray(emb_t, F32), jnp.asarray(t[None, :], F32), jnp.asarray(absdelta, F32))


def _mixer_layer0(x2d, g, w_in, hy_short_w, hy_short_b, f_w1, f_b1, f_w2, f_b2, f_w3, f_b3,
                  freq, hy_bias, sc_conv_w, w_out):
    wt = w_in.T.astype(BF16)
    zt = _in_proj(x2d, g[0][None, :], wt)

    hy_par = jnp.concatenate([hy_short_w.T, hy_short_b[:, None]], axis=1)
    sc_par = jnp.concatenate([sc_conv_w.T, jnp.zeros((D_SC, 1), F32)], axis=1)
    x0_q, u_q, yb_t = _gate(zt, hy_par, sc_par)

    emb_t, t_row, absdelta = _filter_constants()
    w1t = jnp.zeros((FILTER_HIDDEN, EMB_PAD), F32).at[:, :EMB_DIM].set(f_w1.T)
    col = lambda v: v[:, None].astype(F32)
    hf_q, hb_q = _filters(emb_t, t_row, w1t, col(f_b1), f_w2.T, col(f_b2), col(freq),
                          f_w3.T.astype(BF16), col(f_b3), absdelta)

    bias2 = jnp.broadcast_to(hy_bias[:, None].astype(F32), (D_HY, FFT_N2))
    ya_q = _hyena(u_q, hf_q, hb_q, x0_q, bias2, _dft_tables())

    wo = w_out.astype(BF16)
    return _out_proj(ya_q, yb_t, wo[:D_HY], wo[D_HY:], x2d, g[1][None, :])


def kernel(x, norm_g, mix_w_in, hy_short_w, hy_short_b, hy_filt_w1, hy_filt_b1, hy_filt_w2,
           hy_filt_b2, hy_filt_w3, hy_filt_b3, hy_freq, hy_bias, sc_conv_w, mix_w_out,
           pool_w, pool_scale, mlp_w1, mlp_w2):
    x2d = x.reshape(SEQ, D_MODEL)
    depth = norm_g.shape[0]
    for i in range(depth):
        g = norm_g[i]
        j = i // 2
        if i % 2 == 0:
            x2d = _mixer_layer0(x2d, g, mix_w_in[j], hy_short_w[j], hy_short_b[j], hy_filt_w1[j],
                                hy_filt_b1[j], hy_filt_w2[j], hy_filt_b2[j], hy_filt_w3[j],
                                hy_filt_b3[j], hy_freq[j], hy_bias[j], sc_conv_w[j], mix_w_out[j])
        else:
            x2d = _pool(x2d, g[0][None, :], g[1][None, :], pool_w[j].astype(BF16),
                        pool_scale[j][None, :])
        x2d = _mlp(x2d, g[2][None, :], g[3][None, :], mlp_w1[i].astype(BF16), mlp_w2[i].astype(BF16))
    return x2d.reshape(x.shape)
```

```python
import functools
import math

import numpy as np
import jax
import jax.numpy as jnp
from jax import lax
from jax.experimental import pallas as pl
from jax.experimental.pallas import tpu as pltpu

F32 = jnp.float32
BF16 = jnp.bfloat16

D_MODEL = 2048
SEQ = 8192
D_HY = D_MODEL // 2
D_SC = D_MODEL // 2
D_IN = 3 * D_HY + 3 * D_SC
FILTER_BANDS = 16
EMB_DIM = 1 + 2 * FILTER_BANDS
EMB_PAD = 40
FILTER_HIDDEN = 64
DECAY_TARGET = 1e-2
FAST_DECAY_PCT = 0.3
SLOW_DECAY_PCT = 1.5
POOL_WINDOWS = (2, 4, 8, 16)
D_POOL_G = D_MODEL // len(POOL_WINDOWS)
D_FF = 4 * D_MODEL
NORM_EPS = 1e-6

FFT_N = 2 * SEQ
FFT_N2 = 256
FFT_N1 = FFT_N // FFT_N2
FFT_N1_LIVE = SEQ // FFT_N2
CH_GROUP = 8
Q8_SHAPE = (D_MODEL // 2 // CH_GROUP, FFT_N1_LIVE, CH_GROUP, FFT_N2)

VMEM_LIMIT = 56 * 1024 * 1024


def _cparams(n_axes):
    return pltpu.CompilerParams(
        dimension_semantics=("arbitrary",) * n_axes, vmem_limit_bytes=VMEM_LIMIT)


def _rms_rows(x, g):
    r = lax.rsqrt(jnp.mean(x * x, axis=-1, keepdims=True) + NORM_EPS)
    return x * r * g


IN_TM = 1024
IN_TN = 1024


def _in_proj_kernel(x_ref, g_ref, wt_ref, zt_ref, h_ref):
    @pl.when(pl.program_id(1) == 0)
    def _():
        h_ref[...] = _rms_rows(x_ref[...], g_ref[...]).astype(BF16)

    zt_ref[...] = lax.dot_general(
        wt_ref[...], h_ref[...], (((1,), (1,)), ((), ())), preferred_element_type=F32)


def _in_proj(x2d, g, wt):
    return pl.pallas_call(
        _in_proj_kernel,
        grid=(SEQ // IN_TM, D_IN // IN_TN),
        in_specs=[
            pl.BlockSpec((IN_TM, D_MODEL), lambda i, j: (i, 0)),
            pl.BlockSpec((1, D_MODEL), lambda i, j: (0, 0)),
            pl.BlockSpec((IN_TN, D_MODEL), lambda i, j: (j, 0)),
        ],
        out_specs=pl.BlockSpec((IN_TN, IN_TM), lambda i, j: (j, i)),
        out_shape=jax.ShapeDtypeStruct((D_IN, SEQ), F32),
        scratch_shapes=[pltpu.VMEM((IN_TM, D_MODEL), BF16)],
        compiler_params=_cparams(2),
        name="in_proj",
    )(x2d, g, wt)


GATE_CB = 32


def _conv3_lanes(z, w, lane):
    zm = jnp.where(lane == 0, 0.0, pltpu.roll(z, 1, 1))
    zp = jnp.where(lane == SEQ - 1, 0.0, pltpu.roll(z, SEQ - 1, 1))
    return zm * w[:, 0:1] + z * w[:, 1:2] + zp * w[:, 2:3]


def _store_q(ref, val):
    for cg in range(val.shape[0] // CH_GROUP):
        for n1 in range(val.shape[1] // FFT_N2):
            ref[cg, n1] = val[cg * CH_GROUP:(cg + 1) * CH_GROUP, n1 * FFT_N2:(n1 + 1) * FFT_N2]


def _gate_kernel(zx0_ref, zx1_ref, zv_ref, zgb_ref, zgc_ref, zxv_ref,
                 px0_ref, px1_ref, pv_ref, psc_ref, x0_ref, u_ref, yb_ref):
    lane = lax.broadcasted_iota(jnp.int32, (GATE_CB, SEQ), 1)
    px0, px1, pv = px0_ref[...], px1_ref[...], pv_ref[...]
    x0 = _conv3_lanes(zx0_ref[...], px0, lane) + px0[:, 3:4]
    x1 = _conv3_lanes(zx1_ref[...], px1, lane) + px1[:, 3:4]
    v = _conv3_lanes(zv_ref[...], pv, lane) + pv[:, 3:4]
    _store_q(x0_ref, x0)
    _store_q(u_ref, v * x1)
    yb = zgb_ref[...] * _conv3_lanes(zgc_ref[...] * zxv_ref[...], psc_ref[...], lane)
    yb_ref[...] = yb.astype(BF16)


def _gate(zt, hy_par, sc_par):
    nb = D_HY // GATE_CB
    zspec = lambda s: pl.BlockSpec((GATE_CB, SEQ), lambda j, s=s: (s * nb + j, 0))
    pspec = lambda s: pl.BlockSpec((GATE_CB, 4), lambda j, s=s: (s * nb + j, 0))
    out = jax.ShapeDtypeStruct((D_HY, SEQ), BF16)
    ospec = pl.BlockSpec((GATE_CB, SEQ), lambda j: (j, 0))
    qout = jax.ShapeDtypeStruct(Q8_SHAPE, F32)
    qspec = pl.BlockSpec((GATE_CB // CH_GROUP,) + Q8_SHAPE[1:], lambda j: (j, 0, 0, 0))
    return pl.pallas_call(
        _gate_kernel,
        grid=(nb,),
        in_specs=[zspec(s) for s in range(6)] + [pspec(0), pspec(1), pspec(2),
                                                 pl.BlockSpec((GATE_CB, 4), lambda j: (j, 0))],
        out_specs=[qspec, qspec, ospec],
        out_shape=[qout, qout, out],
        compiler_params=_cparams(1),
        name="gate",
    )(zt, zt, zt, zt, zt, zt, hy_par, hy_par, hy_par, sc_par)


FILT_TB = 1024


def _filter_kernel(emb_ref, t_ref, w1_ref, b1_ref, w2_ref, b2_ref, fr_ref, w3_ref, b3_ref,
                   ad_ref, hf_ref, hb_ref, hfs_ref, hbs_ref):
    hi = lax.Precision.HIGHEST
    fr = fr_ref[...]
    a1 = jnp.dot(w1_ref[...], emb_ref[...], precision=hi, preferred_element_type=F32) + b1_ref[...]
    h1 = jnp.sin(fr * a1)
    a2 = jnp.dot(w2_ref[...], h1, precision=hi, preferred_element_type=F32) + b2_ref[...]
    h2 = jnp.sin(fr * a2)
    o = jnp.dot(w3_ref[...], h2.astype(BF16), preferred_element_type=F32) + b3_ref[...]
    window = jnp.exp(-(ad_ref[...] * t_ref[...]))
    hfs_ref[...] = o[:D_HY] * window
    pos = lax.broadcasted_iota(jnp.int32, (D_HY, FILT_TB), 1) + pl.program_id(0) * FILT_TB
    hbs_ref[...] = jnp.where(pos == 0, 0.0, o[D_HY:] * window)

    def to_q(cg, carry):
        rows = pl.ds(pl.multiple_of(cg * CH_GROUP, CH_GROUP), CH_GROUP)
        for n1 in range(FILT_TB // FFT_N2):
            lanes = slice(n1 * FFT_N2, (n1 + 1) * FFT_N2)
            hf_ref[cg, n1] = hfs_ref[rows, lanes]
            hb_ref[cg, n1] = hbs_ref[rows, lanes]
        return carry

    lax.fori_loop(0, D_HY // CH_GROUP, to_q, 0)


def _filters(emb_t, t_row, w1t, b1, w2t, b2, fr, w3t, b3, absdelta):
    full = lambda a: pl.BlockSpec(a.shape, lambda i: (0,) * a.ndim)
    out = jax.ShapeDtypeStruct(Q8_SHAPE, F32)
    qspec = pl.BlockSpec((Q8_SHAPE[0], FILT_TB // FFT_N2, CH_GROUP, FFT_N2), lambda i: (0, i, 0, 0))
    return pl.pallas_call(
        _filter_kernel,
        grid=(SEQ // FILT_TB,),
        in_specs=[pl.BlockSpec((EMB_PAD, FILT_TB), lambda i: (0, i)),
                  pl.BlockSpec((1, FILT_TB), lambda i: (0, i)),
                  full(w1t), full(b1), full(w2t), full(b2), full(fr), full(w3t), full(b3),
                  full(absdelta)],
        out_specs=[qspec, qspec],
        out_shape=[out, out],
        scratch_shapes=[pltpu.VMEM((D_HY, FILT_TB), F32)] * 2,
        compiler_params=_cparams(1),
        name="filters",
    )(emb_t, t_row, w1t, b1, w2t, b2, fr, w3t, b3, absdelta)


HY_CB = 32
HY_UNROLL = 4
GROUP_ROWS = CH_GROUP * FFT_N1_LIVE
N_K1 = FFT_N1 // 2 + 1
RE_ROWS = N_K1 * CH_GROUP
IM_ROWS = (N_K1 - 2) * CH_GROUP


def _dft_tables():
    n1 = np.arange(FFT_N1_LIVE, dtype=np.float64)
    k1 = np.arange(N_K1, dtype=np.float64)
    th = 2.0 * np.pi * np.outer(k1, n1) / FFT_N1
    eye = np.eye(CH_GROUP)
    bd1 = np.concatenate([np.einsum("kn,cd->kcnd", np.cos(th), eye).reshape(RE_ROWS, GROUP_ROWS),
                          np.einsum("kn,cd->kcnd", -np.sin(th)[1:-1], eye).reshape(IM_ROWS, GROUP_ROWS)])
    wgt = np.where((k1 == 0) | (k1 == FFT_N1 // 2), 1.0, 2.0)[:, None] / FFT_N
    bd1inv = np.concatenate(
        [np.einsum("kn,cd->nckd", wgt * np.cos(th), eye).reshape(GROUP_ROWS, RE_ROWS),
         np.einsum("kn,cd->nckd", (-wgt * np.sin(th))[1:-1], eye).reshape(GROUP_ROWS, IM_ROWS)], axis=1)
    n2 = np.arange(FFT_N2, dtype=np.float64)
    ph = 2.0 * np.pi * np.outer(k1, n2) / FFT_N
    twr = np.repeat(np.cos(ph), CH_GROUP, axis=0)
    twi = np.repeat(-np.sin(ph), CH_GROUP, axis=0)
    ps = 2.0 * np.pi * np.outer(n2, n2) / FFT_N2
    cr, ci = np.cos(ps), -np.sin(ps)
    w3 = np.block([[cr, ci], [-ci, cr]])
    w3inv = np.block([[cr, -ci], [ci, cr]])
    f32 = lambda a: jnp.asarray(a, F32)
    return (f32(bd1).astype(BF16), f32(bd1inv).astype(BF16), f32(twr), f32(twi),
            f32(w3).astype(BF16), f32(w3inv).astype(BF16))


def _hyena_kernel(u_ref, hf_ref, hb_ref, x0_ref, bias_ref, bd1_ref, bd1inv_ref, twr_ref, twi_ref,
                  w3_ref, w3inv_ref, ya_ref):
    twr, twi = twr_ref[...], twi_ref[...]
    zrow = jnp.zeros((CH_GROUP, FFT_N2), F32)

    def forward(xq):
        xq = xq.reshape(GROUP_ROWS, FFT_N2).astype(BF16)
        a = jnp.dot(bd1_ref[...], xq, preferred_element_type=F32)
        ar = a[:RE_ROWS]
        ai = jnp.concatenate([zrow, a[RE_ROWS:], zrow], axis=0)
        br = ar * twr - ai * twi
        bi = ar * twi + ai * twr
        bcat = jnp.concatenate([br, bi], axis=1).astype(BF16)
        return jnp.dot(bcat, w3_ref[...], preferred_element_type=F32)

    def group(gi, carry):
        u = u_ref[gi]
        xs = forward(u)
        fs = forward(hf_ref[gi])
        bs = forward(hb_ref[gi])
        xr, xi = xs[:, :FFT_N2], xs[:, FFT_N2:]
        kr = fs[:, :FFT_N2] + bs[:, :FFT_N2]
        ki = fs[:, FFT_N2:] - bs[:, FFT_N2:]
        ycat = jnp.concatenate([xr * kr - xi * ki, xr * ki + xi * kr], axis=1).astype(BF16)
        ap = jnp.dot(ycat, w3inv_ref[...], preferred_element_type=F32)
        apr, api = ap[:, :FFT_N2], ap[:, FFT_N2:]
        bpr = apr * twr + api * twi
        bpi = api * twr - apr * twi
        b2 = jnp.concatenate([bpr, bpi[CH_GROUP:RE_ROWS - CH_GROUP]], axis=0).astype(BF16)
        y = jnp.dot(bd1inv_ref[...], b2, preferred_element_type=F32)
        y3 = y.reshape(FFT_N1_LIVE, CH_GROUP, FFT_N2)
        bias = bias_ref[pl.ds(pl.multiple_of(gi * CH_GROUP, CH_GROUP), CH_GROUP), :]
        ya_ref[gi] = x0_ref[gi] * (y3 + bias[None] * u)
        return carry

    lax.fori_loop(0, HY_CB // CH_GROUP, group, 0, unroll=HY_UNROLL)


def _hyena(u_q, hf_q, hb_q, x0_q, bias2, tables):
    bd1, bd1inv, twr, twi, w3, w3inv = tables
    qspec = pl.BlockSpec((HY_CB // CH_GROUP,) + Q8_SHAPE[1:], lambda j: (j, 0, 0, 0))
    full = lambda a: pl.BlockSpec(a.shape, lambda j: (0,) * a.ndim)
    return pl.pallas_call(
        _hyena_kernel,
        grid=(D_HY // HY_CB,),
        in_specs=[qspec, qspec, qspec, qspec,
                  pl.BlockSpec((HY_CB, FFT_N2), lambda j: (j, 0)),
                  full(bd1), full(bd1inv), full(twr), full(twi), full(w3), full(w3inv)],
        out_specs=qspec,
        out_shape=jax.ShapeDtypeStruct(Q8_SHAPE, F32),
        compiler_params=_cparams(1),
        name="hyena_fftconv",
    )(u_q, hf_q, hb_q, x0_q, bias2, bd1, bd1inv, twr, twi, w3, w3inv)


OUT_TM = 512


def _out_proj_kernel(ya_ref, yb_ref, wa_ref, wb_ref, x_ref, g_ref, o_ref):
    tn = (((0,), (0,)), ((), ()))
    ya = jnp.concatenate([ya_ref[:, a].reshape(D_HY, FFT_N2) for a in range(OUT_TM // FFT_N2)],
                         axis=1).astype(BF16)
    m = lax.dot_general(ya, wa_ref[...], tn, preferred_element_type=F32)
    m = m + lax.dot_general(yb_ref[...], wb_ref[...], tn, preferred_element_type=F32)
    o_ref[...] = x_ref[...] + _rms_rows(m, g_ref[...])


def _out_proj(ya_t, yb_t, wa, wb, x2d, g):
    return pl.pallas_call(
        _out_proj_kernel,
        grid=(SEQ // OUT_TM,),
        in_specs=[
            pl.BlockSpec((Q8_SHAPE[0], OUT_TM // FFT_N2, CH_GROUP, FFT_N2), lambda i: (0, i, 0, 0)),
            pl.BlockSpec((D_SC, OUT_TM), lambda i: (0, i)),
            pl.BlockSpec((D_HY, D_MODEL), lambda i: (0, 0)),
            pl.BlockSpec((D_SC, D_MODEL), lambda i: (0, 0)),
            pl.BlockSpec((OUT_TM, D_MODEL), lambda i: (i, 0)),
            pl.BlockSpec((1, D_MODEL), lambda i: (0, 0)),
        ],
        out_specs=pl.BlockSpec((OUT_TM, D_MODEL), lambda i: (i, 0)),
        out_shape=jax.ShapeDtypeStruct((SEQ, D_MODEL), F32),
        compiler_params=_cparams(1),
        name="out_proj",
    )(ya_t, yb_t, wa, wb, x2d, g)


MLP_TM = 1024
MLP_TK = 512
MLP_TN = 512


def _mlp_kernel(x_ref, gin_ref, gout_ref, w1_ref, w2_ref, o_ref, h_ref):
    k = pl.program_id(1)

    @pl.when(k == 0)
    def _():
        h_ref[...] = _rms_rows(x_ref[...], gin_ref[...]).astype(BF16)
        o_ref[...] = jnp.zeros_like(o_ref)

    a = jnp.dot(h_ref[...], w1_ref[...], preferred_element_type=F32)
    a = jnp.square(jnp.maximum(a, 0.0)).astype(BF16)
    for n in range(D_MODEL // MLP_TN):
        cols = slice(n * MLP_TN, (n + 1) * MLP_TN)
        o_ref[:, cols] += jnp.dot(a, w2_ref[:, cols], preferred_element_type=F32)

    @pl.when(k == pl.num_programs(1) - 1)
    def _():
        o_ref[...] = x_ref[...] + _rms_rows(o_ref[...], gout_ref[...])


def _mlp(x2d, gin, gout, w1, w2, layer):
    return pl.pallas_call(
        _mlp_kernel,
        grid=(SEQ // MLP_TM, D_FF // MLP_TK),
        in_specs=[
            pl.BlockSpec((MLP_TM, D_MODEL), lambda i, k: (i, 0)),
            pl.BlockSpec((1, D_MODEL), lambda i, k: (0, 0)),
            pl.BlockSpec((1, D_MODEL), lambda i, k: (0, 0)),
            pl.BlockSpec((None, D_MODEL, MLP_TK), lambda i, k: (layer, 0, k)),
            pl.BlockSpec((None, MLP_TK, D_MODEL), lambda i, k: (layer, k, 0)),
        ],
        out_specs=pl.BlockSpec((MLP_TM, D_MODEL), lambda i, k: (i, 0)),
        out_shape=jax.ShapeDtypeStruct((SEQ, D_MODEL), F32),
        scratch_shapes=[pltpu.VMEM((MLP_TM, D_MODEL), BF16)],
        compiler_params=_cparams(2),
        name="mlp",
    )(x2d, gin, gout, w1, w2)


CAST_BLOCK_BYTES = 8 * 1024 * 1024


def _cast_kernel(w_ref, o_ref):
    o_ref[...] = w_ref[...].astype(BF16)


def _cast_bf16(w):
    nl, rows, cols = w.shape
    br = CAST_BLOCK_BYTES // (cols * 4)
    spec = pl.BlockSpec((1, br, cols), lambda l, r: (l, r, 0))
    return pl.pallas_call(
        _cast_kernel,
        grid=(nl, rows // br),
        in_specs=[spec],
        out_specs=spec,
        out_shape=jax.ShapeDtypeStruct(w.shape, BF16),
        compiler_params=_cparams(2),
        name="cast_bf16",
    )(w)


POOL_TM = 512
HALO = 8


def _pool_kernel(xp_ref, xm_ref, xn_ref, gin_ref, gout_ref, pw_ref, ps_ref, o_ref, hs_ref, m_ref):
    i = pl.program_id(0)
    gin = gin_ref[...]
    hs_ref[0:HALO, :] = jnp.where(i > 0, _rms_rows(xp_ref[...], gin), 0.0)
    hs_ref[HALO:HALO + POOL_TM, :] = _rms_rows(xm_ref[...], gin)
    hs_ref[HALO + POOL_TM:, :] = jnp.where(i < pl.num_programs(0) - 1, _rms_rows(xn_ref[...], gin), 0.0)
    row = lax.broadcasted_iota(jnp.int32, (POOL_TM, D_POOL_G), 0) + i * POOL_TM
    for gi, w in enumerate(POOL_WINDOWS):
        r = w // 2
        cols = slice(gi * D_POOL_G, (gi + 1) * D_POOL_G)
        s = hs_ref[HALO - r:HALO - r + POOL_TM, cols]
        for j in range(-r + 1, r + 1):
            s = s + hs_ref[HALO + j:HALO + j + POOL_TM, cols]
        cnt = jnp.minimum(row + r + 1, SEQ) - jnp.maximum(row - r, 0)
        d = s / cnt.astype(F32) - hs_ref[HALO:HALO + POOL_TM, cols]
        mg = jnp.dot(d.astype(BF16), pw_ref[gi], preferred_element_type=F32)
        m_ref[:, cols] = mg * ps_ref[:, cols]
    o_ref[...] = xm_ref[...] + _rms_rows(m_ref[...], gout_ref[...])


def _pool(x2d, gin, gout, pw, ps):
    nb8 = POOL_TM // HALO
    last8 = SEQ // HALO - 1
    return pl.pallas_call(
        _pool_kernel,
        grid=(SEQ // POOL_TM,),
        in_specs=[
            pl.BlockSpec((HALO, D_MODEL), lambda i: (jnp.maximum(i * nb8 - 1, 0), 0)),
            pl.BlockSpec((POOL_TM, D_MODEL), lambda i: (i, 0)),
            pl.BlockSpec((HALO, D_MODEL), lambda i: (jnp.minimum((i + 1) * nb8, last8), 0)),
            pl.BlockSpec((1, D_MODEL), lambda i: (0, 0)),
            pl.BlockSpec((1, D_MODEL), lambda i: (0, 0)),
            pl.BlockSpec((len(POOL_WINDOWS), D_POOL_G, D_POOL_G), lambda i: (0, 0, 0)),
            pl.BlockSpec((1, D_MODEL), lambda i: (0, 0)),
        ],
        out_specs=pl.BlockSpec((POOL_TM, D_MODEL), lambda i: (i, 0)),
        out_shape=jax.ShapeDtypeStruct((SEQ, D_MODEL), F32),
        scratch_shapes=[pltpu.VMEM((POOL_TM + 2 * HALO, D_MODEL), F32),
                        pltpu.VMEM((POOL_TM, D_MODEL), F32)],
        compiler_params=_cparams(1),
        name="pool_mixer",
    )(x2d, x2d, x2d, gin, gout, pw, ps)


def _filter_constants():
    L = SEQ
    t = np.linspace(0.0, 1.0, L)
    w_pos = 2.0 * np.pi * np.arange(L) / L
    bands = np.linspace(1e-4, FILTER_BANDS - 1, FILTER_BANDS)
    ang = w_pos[:, None] * bands[None, :]
    emb = np.concatenate([t[:, None], np.cos(ang), -np.sin(ang)], axis=-1)
    emb_t = np.zeros((EMB_PAD, L))
    emb_t[:EMB_DIM] = emb.T
    max_decay = math.log(DECAY_TARGET) / FAST_DECAY_PCT
    min_decay = math.log(DECAY_TARGET) / SLOW_DECAY_PCT
    absdelta = np.abs(np.linspace(min_decay, max_decay, D_HY))[:, None]
    return (jnp.asarray(emb_t, F32), jnp.asarray(t[None, :], F32), jnp.asarray(absdelta, F32))


def _mixer_layer0(x2d, g, w_in, hy_short_w, hy_short_b, f_w1, f_b1, f_w2, f_b2, f_w3, f_b3,
                  freq, hy_bias, sc_conv_w, w_out):
    wt = w_in.T.astype(BF16)
    zt = _in_proj(x2d, g[0][None, :], wt)

    hy_par = jnp.concatenate([hy_short_w.T, hy_short_b[:, None]], axis=1)
    sc_par = jnp.concatenate([sc_conv_w.T, jnp.zeros((D_SC, 1), F32)], axis=1)
    x0_q, u_q, yb_t = _gate(zt, hy_par, sc_par)

    emb_t, t_row, absdelta = _filter_constants()
    w1t = jnp.zeros((FILTER_HIDDEN, EMB_PAD), F32).at[:, :EMB_DIM].set(f_w1.T)
    col = lambda v: v[:, None].astype(F32)
    hf_q, hb_q = _filters(emb_t, t_row, w1t, col(f_b1), f_w2.T, col(f_b2), col(freq),
                          f_w3.T.astype(BF16), col(f_b3), absdelta)

    bias2 = jnp.broadcast_to(hy_bias[:, None].astype(F32), (D_HY, FFT_N2))
    ya_q = _hyena(u_q, hf_q, hb_q, x0_q, bias2, _dft_tables())

    wo = w_out.astype(BF16)
    return _out_proj(ya_q, yb_t, wo[:D_HY], wo[D_HY:], x2d, g[1][None, :])


def kernel(x, norm_g, mix_w_in, hy_short_w, hy_short_b, hy_filt_w1, hy_filt_b1, hy_filt_w2,
           hy_filt_b2, hy_filt_w3, hy_filt_b3, hy_freq, hy_bias, sc_conv_w, mix_w_out,
           pool_w, pool_scale, mlp_w1, mlp_w2):
    x2d = x.reshape(SEQ, D_MODEL)
    depth = norm_g.shape[0]
    mlp_w1 = _cast_bf16(mlp_w1)
    mlp_w2 = _cast_bf16(mlp_w2)
    for i in range(depth):
        g = norm_g[i]
        j = i // 2
        if i % 2 == 0:
            x2d = _mixer_layer0(x2d, g, mix_w_in[j], hy_short_w[j], hy_short_b[j], hy_filt_w1[j],
                                hy_filt_b1[j], hy_filt_w2[j], hy_filt_b2[j], hy_filt_w3[j],
                                hy_filt_b3[j], hy_freq[j], hy_bias[j], sc_conv_w[j], mix_w_out[j])
        else:
            x2d = _pool(x2d, g[0][None, :], g[1][None, :], pool_w[j].astype(BF16),
                        pool_scale[j][None, :])
        x2d = _mlp(x2d, g[2][None, :], g[3][None, :], mlp_w1, mlp_w2, i)
    return x2d.reshape(x.shape)
```

```python
import math

import numpy as np
import jax
import jax.numpy as jnp
from jax import lax
from jax.experimental import pallas as pl
from jax.experimental.pallas import tpu as pltpu

F32 = jnp.float32
BF16 = jnp.bfloat16

D_MODEL = 2048
SEQ = 8192
D_HY = D_MODEL // 2
D_SC = D_MODEL // 2
D_IN = 3 * D_HY + 3 * D_SC
FILTER_BANDS = 16
EMB_DIM = 1 + 2 * FILTER_BANDS
EMB_PAD = 40
FILTER_HIDDEN = 64
DECAY_TARGET = 1e-2
FAST_DECAY_PCT = 0.3
SLOW_DECAY_PCT = 1.5
POOL_WINDOWS = (2, 4, 8, 16)
D_POOL_G = D_MODEL // len(POOL_WINDOWS)
D_FF = 4 * D_MODEL
NORM_EPS = 1e-6
LANES = 128

FFT_N = 2 * SEQ
FFT_N2 = 256
FFT_N1 = FFT_N // FFT_N2
FFT_N1_LIVE = SEQ // FFT_N2
CH_GROUP = 8
Q8_SHAPE = (D_MODEL // 2 // CH_GROUP, FFT_N1_LIVE, CH_GROUP, FFT_N2)

VMEM_LIMIT = 56 * 1024 * 1024


def _cparams(n_axes):
    return pltpu.CompilerParams(
        dimension_semantics=("arbitrary",) * n_axes, vmem_limit_bytes=VMEM_LIMIT)


def _rms_rows(x, g):
    r = lax.rsqrt(jnp.mean(x * x, axis=-1, keepdims=True) + NORM_EPS)
    return x * r * g


IN_TM = 1024
IN_TN = 768


def _in_proj_kernel(x_ref, g_ref, wt_ref, w1_ref, w2_ref, zt_ref, w1b_ref, w2b_ref, h_ref):
    @pl.when(pl.program_id(1) == 0)
    def _():
        h_ref[...] = _rms_rows(x_ref[...], g_ref[...]).astype(BF16)

    zt_ref[...] = lax.dot_general(
        wt_ref[...], h_ref[...], (((1,), (1,)), ((), ())), preferred_element_type=F32)
    w1b_ref[...] = w1_ref[...].astype(BF16)
    w2b_ref[...] = w2_ref[...].astype(BF16)


def _in_proj(x2d, g, wt, mlp_w1, mlp_w2, layer):
    ni, nj = SEQ // IN_TM, D_IN // IN_TN
    w1_blk = (D_MODEL // ni, D_FF // nj)
    w2_blk = (D_FF // ni, D_MODEL // nj)
    return pl.pallas_call(
        _in_proj_kernel,
        grid=(ni, nj),
        in_specs=[
            pl.BlockSpec((IN_TM, D_MODEL), lambda i, j: (i, 0)),
            pl.BlockSpec((1, D_MODEL), lambda i, j: (0, 0)),
            pl.BlockSpec((IN_TN, D_MODEL), lambda i, j: (j, 0)),
            pl.BlockSpec((None,) + w1_blk, lambda i, j: (layer, i, j)),
            pl.BlockSpec((None,) + w2_blk, lambda i, j: (layer, i, j)),
        ],
        out_specs=[pl.BlockSpec((IN_TN, IN_TM), lambda i, j: (j, i)),
                   pl.BlockSpec(w1_blk, lambda i, j: (i, j)),
                   pl.BlockSpec(w2_blk, lambda i, j: (i, j))],
        out_shape=[jax.ShapeDtypeStruct((D_IN, SEQ), F32),
                   jax.ShapeDtypeStruct((D_MODEL, D_FF), BF16),
                   jax.ShapeDtypeStruct((D_FF, D_MODEL), BF16)],
        scratch_shapes=[pltpu.VMEM((IN_TM, D_MODEL), BF16)],
        compiler_params=_cparams(2),
        name="in_proj",
    )(x2d, g, wt, mlp_w1, mlp_w2)


GATE_CB = 32
GATE_ROWS = 16
GATE_CHUNK = 1024


def _conv3_chunk(zwin, w, off, first, last):
    width = zwin.shape[1]
    zm = pltpu.roll(zwin, 1, 1)[:, off:off + GATE_CHUNK]
    zp = pltpu.roll(zwin, width - 1, 1)[:, off:off + GATE_CHUNK]
    z = zwin[:, off:off + GATE_CHUNK]
    lane = lax.broadcasted_iota(jnp.int32, z.shape, 1)
    if first:
        zm = jnp.where(lane == 0, 0.0, zm)
    if last:
        zp = jnp.where(lane == GATE_CHUNK - 1, 0.0, zp)
    return zm * w[:, 0:1] + z * w[:, 1:2] + zp * w[:, 2:3]


def _gate_kernel(zx0_ref, zx1_ref, zv_ref, zgb_ref, zgc_ref, zxv_ref,
                 px0_ref, px1_ref, pv_ref, psc_ref, x0_ref, u_ref, yb_ref):
    n_chunks = SEQ // GATE_CHUNK
    groups = GATE_ROWS // CH_GROUP
    slabs = GATE_CHUNK // FFT_N2

    def row_tile(rt, carry):
        rows = pl.ds(pl.multiple_of(rt * GATE_ROWS, GATE_ROWS), GATE_ROWS)
        px0, px1, pv, psc = px0_ref[rows, :], px1_ref[rows, :], pv_ref[rows, :], psc_ref[rows, :]
        for c in range(n_chunks):
            lo = max(c * GATE_CHUNK - LANES, 0)
            hi = min((c + 1) * GATE_CHUNK + LANES, SEQ)
            edge = (c * GATE_CHUNK - lo, c == 0, c == n_chunks - 1)
            cur = slice(c * GATE_CHUNK, (c + 1) * GATE_CHUNK)
            x0 = _conv3_chunk(zx0_ref[rows, lo:hi], px0, *edge) + px0[:, 3:4]
            x1 = _conv3_chunk(zx1_ref[rows, lo:hi], px1, *edge) + px1[:, 3:4]
            v = _conv3_chunk(zv_ref[rows, lo:hi], pv, *edge) + pv[:, 3:4]
            u = v * x1
            for g in range(groups):
                for a in range(slabs):
                    sl = (slice(g * CH_GROUP, (g + 1) * CH_GROUP), slice(a * FFT_N2, (a + 1) * FFT_N2))
                    x0_ref[rt * groups + g, c * slabs + a] = x0[sl]
                    u_ref[rt * groups + g, c * slabs + a] = u[sl]
            p = zgc_ref[rows, lo:hi] * zxv_ref[rows, lo:hi]
            yb_ref[rows, cur] = (zgb_ref[rows, cur] * _conv3_chunk(p, psc, *edge)).astype(BF16)
        return carry

    lax.fori_loop(0, GATE_CB // GATE_ROWS, row_tile, 0)


def _gate(zt, hy_par, sc_par):
    nb = D_HY // GATE_CB
    zspec = lambda s: pl.BlockSpec((GATE_CB, SEQ), lambda j, s=s: (s * nb + j, 0))
    pspec = lambda s: pl.BlockSpec((GATE_CB, 4), lambda j, s=s: (s * nb + j, 0))
    out = jax.ShapeDtypeStruct((D_HY, SEQ), BF16)
    ospec = pl.BlockSpec((GATE_CB, SEQ), lambda j: (j, 0))
    qout = jax.ShapeDtypeStruct(Q8_SHAPE, F32)
    qspec = pl.BlockSpec((GATE_CB // CH_GROUP,) + Q8_SHAPE[1:], lambda j: (j, 0, 0, 0))
    return pl.pallas_call(
        _gate_kernel,
        grid=(nb,),
        in_specs=[zspec(s) for s in range(6)] + [pspec(0), pspec(1), pspec(2),
                                                 pl.BlockSpec((GATE_CB, 4), lambda j: (j, 0))],
        out_specs=[qspec, qspec, ospec],
        out_shape=[qout, qout, out],
        compiler_params=_cparams(1),
        name="gate",
    )(zt, zt, zt, zt, zt, zt, hy_par, hy_par, hy_par, sc_par)


FILT_TB = 1024


def _filter_kernel(emb_ref, t_ref, w1_ref, b1_ref, w2_ref, b2_ref, fr_ref, w3_ref, b3_ref,
                   ad_ref, hf_ref, hb_ref, hfs_ref, hbs_ref):
    hi = lax.Precision.HIGHEST
    fr = fr_ref[...]
    a1 = jnp.dot(w1_ref[...], emb_ref[...], precision=hi, preferred_element_type=F32) + b1_ref[...]
    h1 = jnp.sin(fr * a1)
    a2 = jnp.dot(w2_ref[...], h1, precision=hi, preferred_element_type=F32) + b2_ref[...]
    h2 = jnp.sin(fr * a2)
    o = jnp.dot(w3_ref[...], h2.astype(BF16), preferred_element_type=F32) + b3_ref[...]
    window = jnp.exp(-(ad_ref[...] * t_ref[...]))
    hfs_ref[...] = o[:D_HY] * window
    pos = lax.broadcasted_iota(jnp.int32, (D_HY, FILT_TB), 1) + pl.program_id(0) * FILT_TB
    hbs_ref[...] = jnp.where(pos == 0, 0.0, o[D_HY:] * window)

    def to_q(cg, carry):
        rows = pl.ds(pl.multiple_of(cg * CH_GROUP, CH_GROUP), CH_GROUP)
        for n1 in range(FILT_TB // FFT_N2):
            lanes = slice(n1 * FFT_N2, (n1 + 1) * FFT_N2)
            hf_ref[cg, n1] = hfs_ref[rows, lanes]
            hb_ref[cg, n1] = hbs_ref[rows, lanes]
        return carry

    lax.fori_loop(0, D_HY // CH_GROUP, to_q, 0)


def _filters(emb_t, t_row, w1t, b1, w2t, b2, fr, w3t, b3, absdelta):
    full = lambda a: pl.BlockSpec(a.shape, lambda i: (0,) * a.ndim)
    out = jax.ShapeDtypeStruct(Q8_SHAPE, F32)
    qspec = pl.BlockSpec((Q8_SHAPE[0], FILT_TB // FFT_N2, CH_GROUP, FFT_N2), lambda i: (0, i, 0, 0))
    return pl.pallas_call(
        _filter_kernel,
        grid=(SEQ // FILT_TB,),
        in_specs=[pl.BlockSpec((EMB_PAD, FILT_TB), lambda i: (0, i)),
                  pl.BlockSpec((1, FILT_TB), lambda i: (0, i)),
                  full(w1t), full(b1), full(w2t), full(b2), full(fr), full(w3t), full(b3),
                  full(absdelta)],
        out_specs=[qspec, qspec],
        out_shape=[out, out],
        scratch_shapes=[pltpu.VMEM((D_HY, FILT_TB), F32)] * 2,
        compiler_params=_cparams(1),
        name="filters",
    )(emb_t, t_row, w1t, b1, w2t, b2, fr, w3t, b3, absdelta)


HY_CB = 32
HY_UNROLL = 4
GROUP_ROWS = CH_GROUP * FFT_N1_LIVE
N_K1 = FFT_N1 // 2 + 1
RE_ROWS = N_K1 * CH_GROUP
IM_ROWS = (N_K1 - 2) * CH_GROUP


def _dft_tables():
    n1 = np.arange(FFT_N1_LIVE, dtype=np.float64)
    k1 = np.arange(N_K1, dtype=np.float64)
    th = 2.0 * np.pi * np.outer(k1, n1) / FFT_N1
    eye = np.eye(CH_GROUP)
    bd1 = np.concatenate([np.einsum("kn,cd->kcnd", np.cos(th), eye).reshape(RE_ROWS, GROUP_ROWS),
                          np.einsum("kn,cd->kcnd", -np.sin(th)[1:-1], eye).reshape(IM_ROWS, GROUP_ROWS)])
    wgt = np.where((k1 == 0) | (k1 == FFT_N1 // 2), 1.0, 2.0)[:, None] / FFT_N
    bd1inv = np.concatenate(
        [np.einsum("kn,cd->nckd", wgt * np.cos(th), eye).reshape(GROUP_ROWS, RE_ROWS),
         np.einsum("kn,cd->nckd", (-wgt * np.sin(th))[1:-1], eye).reshape(GROUP_ROWS, IM_ROWS)], axis=1)
    n2 = np.arange(FFT_N2, dtype=np.float64)
    ph = 2.0 * np.pi * np.outer(k1, n2) / FFT_N
    twr = np.repeat(np.cos(ph), CH_GROUP, axis=0)
    twi = np.repeat(-np.sin(ph), CH_GROUP, axis=0)
    ps = 2.0 * np.pi * np.outer(n2, n2) / FFT_N2
    cr, ci = np.cos(ps), -np.sin(ps)
    w3 = np.block([[cr, ci], [-ci, cr]])
    w3inv = np.block([[cr, -ci], [ci, cr]])
    f32 = lambda a: jnp.asarray(a, F32)
    return (f32(bd1).astype(BF16), f32(bd1inv).astype(BF16), f32(twr), f32(twi),
            f32(w3).astype(BF16), f32(w3inv).astype(BF16))


def _hyena_kernel(u_ref, hf_ref, hb_ref, x0_ref, bias_ref, bd1_ref, bd1inv_ref, twr_ref, twi_ref,
                  w3_ref, w3inv_ref, w1_ref, w2_ref, ya_ref, w1b_ref, w2b_ref):
    w1b_ref[...] = w1_ref[...].astype(BF16)
    w2b_ref[...] = w2_ref[...].astype(BF16)
    twr, twi = twr_ref[...], twi_ref[...]
    zrow = jnp.zeros((CH_GROUP, FFT_N2), F32)

    def forward(xq):
        xq = xq.reshape(GROUP_ROWS, FFT_N2).astype(BF16)
        a = jnp.dot(bd1_ref[...], xq, preferred_element_type=F32)
        ar = a[:RE_ROWS]
        ai = jnp.concatenate([zrow, a[RE_ROWS:], zrow], axis=0)
        br = ar * twr - ai * twi
        bi = ar * twi + ai * twr
        bcat = jnp.concatenate([br, bi], axis=1).astype(BF16)
        return jnp.dot(bcat, w3_ref[...], preferred_element_type=F32)

    def group(gi, carry):
        u = u_ref[gi]
        xs = forward(u)
        fs = forward(hf_ref[gi])
        bs = forward(hb_ref[gi])
        xr, xi = xs[:, :FFT_N2], xs[:, FFT_N2:]
        kr = fs[:, :FFT_N2] + bs[:, :FFT_N2]
        ki = fs[:, FFT_N2:] - bs[:, FFT_N2:]
        ycat = jnp.concatenate([xr * kr - xi * ki, xr * ki + xi * kr], axis=1).astype(BF16)
        ap = jnp.dot(ycat, w3inv_ref[...], preferred_element_type=F32)
        apr, api = ap[:, :FFT_N2], ap[:, FFT_N2:]
        bpr = apr * twr + api * twi
        bpi = api * twr - apr * twi
        b2 = jnp.concatenate([bpr, bpi[CH_GROUP:RE_ROWS - CH_GROUP]], axis=0).astype(BF16)
        y = jnp.dot(bd1inv_ref[...], b2, preferred_element_type=F32)
        y3 = y.reshape(FFT_N1_LIVE, CH_GROUP, FFT_N2)
        bias = bias_ref[pl.ds(pl.multiple_of(gi * CH_GROUP, CH_GROUP), CH_GROUP), :]
        ya_ref[gi] = x0_ref[gi] * (y3 + bias[None] * u)
        return carry

    lax.fori_loop(0, HY_CB // CH_GROUP, group, 0, unroll=HY_UNROLL)


def _hyena(u_q, hf_q, hb_q, x0_q, bias2, tables, mlp_w1, mlp_w2, layer):
    bd1, bd1inv, twr, twi, w3, w3inv = tables
    steps = D_HY // HY_CB
    qspec = pl.BlockSpec((HY_CB // CH_GROUP,) + Q8_SHAPE[1:], lambda j: (j, 0, 0, 0))
    full = lambda a: pl.BlockSpec(a.shape, lambda j: (0,) * a.ndim)
    w1_blk = (D_MODEL // steps, D_FF)
    w2_blk = (D_FF // steps, D_MODEL)
    return pl.pallas_call(
        _hyena_kernel,
        grid=(steps,),
        in_specs=[qspec, qspec, qspec, qspec,
                  pl.BlockSpec((HY_CB, FFT_N2), lambda j: (j, 0)),
                  full(bd1), full(bd1inv), full(twr), full(twi), full(w3), full(w3inv),
                  pl.BlockSpec((None,) + w1_blk, lambda j: (layer, j, 0)),
                  pl.BlockSpec((None,) + w2_blk, lambda j: (layer, j, 0))],
        out_specs=[qspec, pl.BlockSpec(w1_blk, lambda j: (j, 0)), pl.BlockSpec(w2_blk, lambda j: (j, 0))],
        out_shape=[jax.ShapeDtypeStruct(Q8_SHAPE, F32),
                   jax.ShapeDtypeStruct((D_MODEL, D_FF), BF16),
                   jax.ShapeDtypeStruct((D_FF, D_MODEL), BF16)],
        compiler_params=_cparams(1),
        name="hyena_fftconv",
    )(u_q, hf_q, hb_q, x0_q, bias2, bd1, bd1inv, twr, twi, w3, w3inv, mlp_w1, mlp_w2)


OUT_TM = 512


def _out_proj_kernel(ya_ref, yb_ref, wa_ref, wb_ref, x_ref, g_ref, o_ref):
    tn = (((0,), (0,)), ((), ()))
    ya = jnp.concatenate([ya_ref[:, a].reshape(D_HY, FFT_N2) for a in range(OUT_TM // FFT_N2)],
                         axis=1).astype(BF16)
    m = lax.dot_general(ya, wa_ref[...], tn, preferred_element_type=F32)
    m = m + lax.dot_general(yb_ref[...], wb_ref[...], tn, preferred_element_type=F32)
    o_ref[...] = x_ref[...] + _rms_rows(m, g_ref[...])


def _out_proj(ya_q, yb_t, wa, wb, x2d, g):
    return pl.pallas_call(
        _out_proj_kernel,
        grid=(SEQ // OUT_TM,),
        in_specs=[
            pl.BlockSpec((Q8_SHAPE[0], OUT_TM // FFT_N2, CH_GROUP, FFT_N2), lambda i: (0, i, 0, 0)),
            pl.BlockSpec((D_SC, OUT_TM), lambda i: (0, i)),
            pl.BlockSpec((D_HY, D_MODEL), lambda i: (0, 0)),
            pl.BlockSpec((D_SC, D_MODEL), lambda i: (0, 0)),
            pl.BlockSpec((OUT_TM, D_MODEL), lambda i: (i, 0)),
            pl.BlockSpec((1, D_MODEL), lambda i: (0, 0)),
        ],
        out_specs=pl.BlockSpec((OUT_TM, D_MODEL), lambda i: (i, 0)),
        out_shape=jax.ShapeDtypeStruct((SEQ, D_MODEL), F32),
        compiler_params=_cparams(1),
        name="out_proj",
    )(ya_q, yb_t, wa, wb, x2d, g)


MLP_TM = 1024
MLP_TK = 512
MLP_TN = 512


def _mlp_kernel(x_ref, gin_ref, gout_ref, w1_ref, w2_ref, o_ref, h_ref):
    k = pl.program_id(1)

    @pl.when(k == 0)
    def _():
        h_ref[...] = _rms_rows(x_ref[...], gin_ref[...]).astype(BF16)
        o_ref[...] = jnp.zeros_like(o_ref)

    a = jnp.dot(h_ref[...], w1_ref[...], preferred_element_type=F32)
    a = jnp.square(jnp.maximum(a, 0.0)).astype(BF16)
    for n in range(D_MODEL // MLP_TN):
        cols = slice(n * MLP_TN, (n + 1) * MLP_TN)
        o_ref[:, cols] += jnp.dot(a, w2_ref[:, cols], preferred_element_type=F32)

    @pl.when(k == pl.num_programs(1) - 1)
    def _():
        o_ref[...] = x_ref[...] + _rms_rows(o_ref[...], gout_ref[...])


def _mlp(x2d, gin, gout, w1, w2):
    return pl.pallas_call(
        _mlp_kernel,
        grid=(SEQ // MLP_TM, D_FF // MLP_TK),
        in_specs=[
            pl.BlockSpec((MLP_TM, D_MODEL), lambda i, k: (i, 0)),
            pl.BlockSpec((1, D_MODEL), lambda i, k: (0, 0)),
            pl.BlockSpec((1, D_MODEL), lambda i, k: (0, 0)),
            pl.BlockSpec((D_MODEL, MLP_TK), lambda i, k: (0, k)),
            pl.BlockSpec((MLP_TK, D_MODEL), lambda i, k: (k, 0)),
        ],
        out_specs=pl.BlockSpec((MLP_TM, D_MODEL), lambda i, k: (i, 0)),
        out_shape=jax.ShapeDtypeStruct((SEQ, D_MODEL), F32),
        scratch_shapes=[pltpu.VMEM((MLP_TM, D_MODEL), BF16)],
        compiler_params=_cparams(2),
        name="mlp",
    )(x2d, gin, gout, w1, w2)


POOL_TM = 512
HALO = 8


def _pool_kernel(xp_ref, xm_ref, xn_ref, gin_ref, gout_ref, pw_ref, ps_ref, o_ref, hs_ref, m_ref):
    i = pl.program_id(0)
    gin = gin_ref[...]
    hs_ref[0:HALO, :] = jnp.where(i > 0, _rms_rows(xp_ref[...], gin), 0.0)
    hs_ref[HALO:HALO + POOL_TM, :] = _rms_rows(xm_ref[...], gin)
    hs_ref[HALO + POOL_TM:, :] = jnp.where(i < pl.num_programs(0) - 1, _rms_rows(xn_ref[...], gin), 0.0)
    row = lax.broadcasted_iota(jnp.int32, (POOL_TM, D_POOL_G), 0) + i * POOL_TM
    for gi, w in enumerate(POOL_WINDOWS):
        r = w // 2
        cols = slice(gi * D_POOL_G, (gi + 1) * D_POOL_G)
        s = hs_ref[HALO - r:HALO - r + POOL_TM, cols]
        for j in range(-r + 1, r + 1):
            s = s + hs_ref[HALO + j:HALO + j + POOL_TM, cols]
        cnt = jnp.minimum(row + r + 1, SEQ) - jnp.maximum(row - r, 0)
        d = s / cnt.astype(F32) - hs_ref[HALO:HALO + POOL_TM, cols]
        mg = jnp.dot(d.astype(BF16), pw_ref[gi], preferred_element_type=F32)
        m_ref[:, cols] = mg * ps_ref[:, cols]
    o_ref[...] = xm_ref[...] + _rms_rows(m_ref[...], gout_ref[...])


def _pool(x2d, gin, gout, pw, ps):
    nb8 = POOL_TM // HALO
    last8 = SEQ // HALO - 1
    return pl.pallas_call(
        _pool_kernel,
        grid=(SEQ // POOL_TM,),
        in_specs=[
            pl.BlockSpec((HALO, D_MODEL), lambda i: (jnp.maximum(i * nb8 - 1, 0), 0)),
            pl.BlockSpec((POOL_TM, D_MODEL), lambda i: (i, 0)),
            pl.BlockSpec((HALO, D_MODEL), lambda i: (jnp.minimum((i + 1) * nb8, last8), 0)),
            pl.BlockSpec((1, D_MODEL), lambda i: (0, 0)),
            pl.BlockSpec((1, D_MODEL), lambda i: (0, 0)),
            pl.BlockSpec((len(POOL_WINDOWS), D_POOL_G, D_POOL_G), lambda i: (0, 0, 0)),
            pl.BlockSpec((1, D_MODEL), lambda i: (0, 0)),
        ],
        out_specs=pl.BlockSpec((POOL_TM, D_MODEL), lambda i: (i, 0)),
        out_shape=jax.ShapeDtypeStruct((SEQ, D_MODEL), F32),
        scratch_shapes=[pltpu.VMEM((POOL_TM + 2 * HALO, D_MODEL), F32),
                        pltpu.VMEM((POOL_TM, D_MODEL), F32)],
        compiler_params=_cparams(1),
        name="pool_mixer",
    )(x2d, x2d, x2d, gin, gout, pw, ps)


def _filter_constants():
    L = SEQ
    t = np.linspace(0.0, 1.0, L)
    w_pos = 2.0 * np.pi * np.arange(L) / L
    bands = np.linspace(1e-4, FILTER_BANDS - 1, FILTER_BANDS)
    ang = w_pos[:, None] * bands[None, :]
    emb = np.concatenate([t[:, None], np.cos(ang), -np.sin(ang)], axis=-1)
    emb_t = np.zeros((EMB_PAD, L))
    emb_t[:EMB_DIM] = emb.T
    max_decay = math.log(DECAY_TARGET) / FAST_DECAY_PCT
    min_decay = math.log(DECAY_TARGET) / SLOW_DECAY_PCT
    absdelta = np.abs(np.linspace(min_decay, max_decay, D_HY))[:, None]
    return (jnp.asarray(emb_t, F32), jnp.asarray(t[None, :], F32), jnp.asarray(absdelta, F32))


def _mixer_layer0(x2d, g, w_in, hy_short_w, hy_short_b, f_w1, f_b1, f_w2, f_b2, f_w3, f_b3,
                  freq, hy_bias, sc_conv_w, w_out, mlp_w1, mlp_w2, layer):
    wt = w_in.T.astype(BF16)
    zt, w1_next, w2_next = _in_proj(x2d, g[0][None, :], wt, mlp_w1, mlp_w2, layer + 1)

    hy_par = jnp.concatenate([hy_short_w.T, hy_short_b[:, None]], axis=1)
    sc_par = jnp.concatenate([sc_conv_w.T, jnp.zeros((D_SC, 1), F32)], axis=1)
    x0_q, u_q, yb_t = _gate(zt, hy_par, sc_par)

    emb_t, t_row, absdelta = _filter_constants()
    w1t = jnp.zeros((FILTER_HIDDEN, EMB_PAD), F32).at[:, :EMB_DIM].set(f_w1.T)
    col = lambda v: v[:, None].astype(F32)
    hf_q, hb_q = _filters(emb_t, t_row, w1t, col(f_b1), f_w2.T, col(f_b2), col(freq),
                          f_w3.T.astype(BF16), col(f_b3), absdelta)

    bias2 = jnp.broadcast_to(hy_bias[:, None].astype(F32), (D_HY, FFT_N2))
    ya_q, w1_this, w2_this = _hyena(u_q, hf_q, hb_q, x0_q, bias2, _dft_tables(), mlp_w1, mlp_w2, layer)

    wo = w_out.astype(BF16)
    x2d = _out_proj(ya_q, yb_t, wo[:D_HY], wo[D_HY:], x2d, g[1][None, :])
    return x2d, ((w1_this, w2_this), (w1_next, w2_next))


def kernel(x, norm_g, mix_w_in, hy_short_w, hy_short_b, hy_filt_w1, hy_filt_b1, hy_filt_w2,
           hy_filt_b2, hy_filt_w3, hy_filt_b3, hy_freq, hy_bias, sc_conv_w, mix_w_out,
           pool_w, pool_scale, mlp_w1, mlp_w2):
    x2d = x.reshape(SEQ, D_MODEL)
    depth = norm_g.shape[0]
    assert depth % 2 == 0, "each even layer's mixer narrows the MLP weights of itself and the next layer"
    mlp_bf16 = {}
    for i in range(depth):
        g = norm_g[i]
        j = i // 2
        if i % 2 == 0:
            x2d, (mlp_bf16[i], mlp_bf16[i + 1]) = _mixer_layer0(
                x2d, g, mix_w_in[j], hy_short_w[j], hy_short_b[j], hy_filt_w1[j], hy_filt_b1[j],
                hy_filt_w2[j], hy_filt_b2[j], hy_filt_w3[j], hy_filt_b3[j], hy_freq[j], hy_bias[j],
                sc_conv_w[j], mix_w_out[j], mlp_w1, mlp_w2, i)
        else:
            x2d = _pool(x2d, g[0][None, :], g[1][None, :], pool_w[j].astype(BF16),
                        pool_scale[j][None, :])
        x2d = _mlp(x2d, g[2][None, :], g[3][None, :], *mlp_bf16[i])
    return x2d.reshape(x.shape)
```

```python
import math

import numpy as np
import jax
import jax.numpy as jnp
from jax import lax
from jax.experimental import pallas as pl
from jax.experimental.pallas import tpu as pltpu

F32 = jnp.float32
BF16 = jnp.bfloat16

D_MODEL = 2048
SEQ = 8192
D_HY = D_MODEL // 2
D_SC = D_MODEL // 2
D_IN = 3 * D_HY + 3 * D_SC
FILTER_BANDS = 16
EMB_DIM = 1 + 2 * FILTER_BANDS
EMB_PAD = 40
FILTER_HIDDEN = 64
DECAY_TARGET = 1e-2
FAST_DECAY_PCT = 0.3
SLOW_DECAY_PCT = 1.5
POOL_WINDOWS = (2, 4, 8, 16)
D_POOL_G = D_MODEL // len(POOL_WINDOWS)
D_FF = 4 * D_MODEL
NORM_EPS = 1e-6
LANES = 128

FFT_N = 2 * SEQ
FFT_N2 = 256
FFT_N1 = FFT_N // FFT_N2
FFT_N1_LIVE = SEQ // FFT_N2
CH_GROUP = 8
Q8_SHAPE = (D_MODEL // 2 // CH_GROUP, FFT_N1_LIVE, CH_GROUP, FFT_N2)

VMEM_LIMIT = 56 * 1024 * 1024


def _cparams(n_axes):
    return pltpu.CompilerParams(
        dimension_semantics=("arbitrary",) * n_axes, vmem_limit_bytes=VMEM_LIMIT)


def _rms_rows(x, g):
    r = lax.rsqrt(jnp.mean(x * x, axis=-1, keepdims=True) + NORM_EPS)
    return x * r * g


IN_TM = 1024
IN_TN = 1024


def _in_proj_kernel(x_ref, g_ref, wt_ref, zt_ref, h_ref):
    @pl.when(pl.program_id(1) == 0)
    def _():
        h_ref[...] = _rms_rows(x_ref[...], g_ref[...]).astype(BF16)

    zt_ref[...] = lax.dot_general(
        wt_ref[...], h_ref[...], (((1,), (1,)), ((), ())), preferred_element_type=F32)


def _in_proj(x2d, g, wt):
    return pl.pallas_call(
        _in_proj_kernel,
        grid=(SEQ // IN_TM, D_IN // IN_TN),
        in_specs=[
            pl.BlockSpec((IN_TM, D_MODEL), lambda i, j: (i, 0)),
            pl.BlockSpec((1, D_MODEL), lambda i, j: (0, 0)),
            pl.BlockSpec((IN_TN, D_MODEL), lambda i, j: (j, 0)),
        ],
        out_specs=pl.BlockSpec((IN_TN, IN_TM), lambda i, j: (j, i)),
        out_shape=jax.ShapeDtypeStruct((D_IN, SEQ), F32),
        scratch_shapes=[pltpu.VMEM((IN_TM, D_MODEL), BF16)],
        compiler_params=_cparams(2),
        name="in_proj",
    )(x2d, g, wt)


GATE_CB = 32
GATE_ROWS = 16
GATE_CHUNK = 1024


def _conv3_chunk(zwin, w, off, first, last):
    width = zwin.shape[1]
    zm = pltpu.roll(zwin, 1, 1)[:, off:off + GATE_CHUNK]
    zp = pltpu.roll(zwin, width - 1, 1)[:, off:off + GATE_CHUNK]
    z = zwin[:, off:off + GATE_CHUNK]
    lane = lax.broadcasted_iota(jnp.int32, z.shape, 1)
    if first:
        zm = jnp.where(lane == 0, 0.0, zm)
    if last:
        zp = jnp.where(lane == GATE_CHUNK - 1, 0.0, zp)
    return zm * w[:, 0:1] + z * w[:, 1:2] + zp * w[:, 2:3]


def _gate_kernel(zx0_ref, zx1_ref, zv_ref, zgb_ref, zgc_ref, zxv_ref,
                 px0_ref, px1_ref, pv_ref, psc_ref, x0_ref, u_ref, yb_ref):
    n_chunks = SEQ // GATE_CHUNK
    groups = GATE_ROWS // CH_GROUP
    slabs = GATE_CHUNK // FFT_N2

    def row_tile(rt, carry):
        rows = pl.ds(pl.multiple_of(rt * GATE_ROWS, GATE_ROWS), GATE_ROWS)
        px0, px1, pv, psc = px0_ref[rows, :], px1_ref[rows, :], pv_ref[rows, :], psc_ref[rows, :]
        for c in range(n_chunks):
            lo = max(c * GATE_CHUNK - LANES, 0)
            hi = min((c + 1) * GATE_CHUNK + LANES, SEQ)
            edge = (c * GATE_CHUNK - lo, c == 0, c == n_chunks - 1)
            cur = slice(c * GATE_CHUNK, (c + 1) * GATE_CHUNK)
            x0 = _conv3_chunk(zx0_ref[rows, lo:hi], px0, *edge) + px0[:, 3:4]
            x1 = _conv3_chunk(zx1_ref[rows, lo:hi], px1, *edge) + px1[:, 3:4]
            v = _conv3_chunk(zv_ref[rows, lo:hi], pv, *edge) + pv[:, 3:4]
            u = v * x1
            for g in range(groups):
                for a in range(slabs):
                    sl = (slice(g * CH_GROUP, (g + 1) * CH_GROUP), slice(a * FFT_N2, (a + 1) * FFT_N2))
                    x0_ref[rt * groups + g, c * slabs + a] = x0[sl]
                    u_ref[rt * groups + g, c * slabs + a] = u[sl]
            p = zgc_ref[rows, lo:hi] * zxv_ref[rows, lo:hi]
            yb_ref[rows, cur] = (zgb_ref[rows, cur] * _conv3_chunk(p, psc, *edge)).astype(BF16)
        return carry

    lax.fori_loop(0, GATE_CB // GATE_ROWS, row_tile, 0)


def _gate(zt, hy_par, sc_par):
    nb = D_HY // GATE_CB
    zspec = lambda s: pl.BlockSpec((GATE_CB, SEQ), lambda j, s=s: (s * nb + j, 0))
    pspec = lambda s: pl.BlockSpec((GATE_CB, 4), lambda j, s=s: (s * nb + j, 0))
    out = jax.ShapeDtypeStruct((D_HY, SEQ), BF16)
    ospec = pl.BlockSpec((GATE_CB, SEQ), lambda j: (j, 0))
    qout = jax.ShapeDtypeStruct(Q8_SHAPE, F32)
    qspec = pl.BlockSpec((GATE_CB // CH_GROUP,) + Q8_SHAPE[1:], lambda j: (j, 0, 0, 0))
    return pl.pallas_call(
        _gate_kernel,
        grid=(nb,),
        in_specs=[zspec(s) for s in range(6)] + [pspec(0), pspec(1), pspec(2),
                                                 pl.BlockSpec((GATE_CB, 4), lambda j: (j, 0))],
        out_specs=[qspec, qspec, ospec],
        out_shape=[qout, qout, out],
        compiler_params=_cparams(1),
        name="gate",
    )(zt, zt, zt, zt, zt, zt, hy_par, hy_par, hy_par, sc_par)


FILT_TB = 1024


def _filter_kernel(emb_ref, t_ref, w1_ref, b1_ref, w2_ref, b2_ref, fr_ref, w3_ref, b3_ref,
                   ad_ref, hf_ref, hb_ref, hfs_ref, hbs_ref):
    hi = lax.Precision.HIGHEST
    fr = fr_ref[...]
    a1 = jnp.dot(w1_ref[...], emb_ref[...], precision=hi, preferred_element_type=F32) + b1_ref[...]
    h1 = jnp.sin(fr * a1)
    a2 = jnp.dot(w2_ref[...], h1, precision=hi, preferred_element_type=F32) + b2_ref[...]
    h2 = jnp.sin(fr * a2)
    o = jnp.dot(w3_ref[...], h2.astype(BF16), preferred_element_type=F32) + b3_ref[...]
    window = jnp.exp(-(ad_ref[...] * t_ref[...]))
    hfs_ref[...] = o[:D_HY] * window
    pos = lax.broadcasted_iota(jnp.int32, (D_HY, FILT_TB), 1) + pl.program_id(0) * FILT_TB
    hbs_ref[...] = jnp.where(pos == 0, 0.0, o[D_HY:] * window)

    def to_q(cg, carry):
        rows = pl.ds(pl.multiple_of(cg * CH_GROUP, CH_GROUP), CH_GROUP)
        for n1 in range(FILT_TB // FFT_N2):
            lanes = slice(n1 * FFT_N2, (n1 + 1) * FFT_N2)
            hf_ref[cg, n1] = hfs_ref[rows, lanes]
            hb_ref[cg, n1] = hbs_ref[rows, lanes]
        return carry

    lax.fori_loop(0, D_HY // CH_GROUP, to_q, 0)


def _filters(emb_t, t_row, w1t, b1, w2t, b2, fr, w3t, b3, absdelta):
    full = lambda a: pl.BlockSpec(a.shape, lambda i: (0,) * a.ndim)
    out = jax.ShapeDtypeStruct(Q8_SHAPE, F32)
    qspec = pl.BlockSpec((Q8_SHAPE[0], FILT_TB // FFT_N2, CH_GROUP, FFT_N2), lambda i: (0, i, 0, 0))
    return pl.pallas_call(
        _filter_kernel,
        grid=(SEQ // FILT_TB,),
        in_specs=[pl.BlockSpec((EMB_PAD, FILT_TB), lambda i: (0, i)),
                  pl.BlockSpec((1, FILT_TB), lambda i: (0, i)),
                  full(w1t), full(b1), full(w2t), full(b2), full(fr), full(w3t), full(b3),
                  full(absdelta)],
        out_specs=[qspec, qspec],
        out_shape=[out, out],
        scratch_shapes=[pltpu.VMEM((D_HY, FILT_TB), F32)] * 2,
        compiler_params=_cparams(1),
        name="filters",
    )(emb_t, t_row, w1t, b1, w2t, b2, fr, w3t, b3, absdelta)


HY_CB = 32
HY_UNROLL = 4
GROUP_ROWS = CH_GROUP * FFT_N1_LIVE
N_K1 = FFT_N1 // 2 + 1
RE_ROWS = N_K1 * CH_GROUP
IM_ROWS = (N_K1 - 2) * CH_GROUP


def _dft_tables():
    n1 = np.arange(FFT_N1_LIVE, dtype=np.float64)
    k1 = np.arange(N_K1, dtype=np.float64)
    th = 2.0 * np.pi * np.outer(k1, n1) / FFT_N1
    eye = np.eye(CH_GROUP)
    bd1 = np.concatenate([np.einsum("kn,cd->kcnd", np.cos(th), eye).reshape(RE_ROWS, GROUP_ROWS),
                          np.einsum("kn,cd->kcnd", -np.sin(th)[1:-1], eye).reshape(IM_ROWS, GROUP_ROWS)])
    wgt = np.where((k1 == 0) | (k1 == FFT_N1 // 2), 1.0, 2.0)[:, None] / FFT_N
    bd1inv = np.concatenate(
        [np.einsum("kn,cd->nckd", wgt * np.cos(th), eye).reshape(GROUP_ROWS, RE_ROWS),
         np.einsum("kn,cd->nckd", (-wgt * np.sin(th))[1:-1], eye).reshape(GROUP_ROWS, IM_ROWS)], axis=1)
    n2 = np.arange(FFT_N2, dtype=np.float64)
    ph = 2.0 * np.pi * np.outer(k1, n2) / FFT_N
    twr = np.repeat(np.cos(ph), CH_GROUP, axis=0)
    twi = np.repeat(-np.sin(ph), CH_GROUP, axis=0)
    ps = 2.0 * np.pi * np.outer(n2, n2) / FFT_N2
    cr, ci = np.cos(ps), -np.sin(ps)
    w3 = np.block([[cr, ci], [-ci, cr]])
    w3inv = np.block([[cr, -ci], [ci, cr]])
    f32 = lambda a: jnp.asarray(a, F32)
    return (f32(bd1).astype(BF16), f32(bd1inv).astype(BF16), f32(twr), f32(twi),
            f32(w3).astype(BF16), f32(w3inv).astype(BF16))


def _hyena_kernel(u_ref, hf_ref, hb_ref, x0_ref, bias_ref, bd1_ref, bd1inv_ref, twr_ref, twi_ref,
                  w3_ref, w3inv_ref, w1_ref, w2_ref, ya_ref, w1b_ref, w2b_ref):
    w1b_ref[...] = w1_ref[...].astype(BF16)
    w2b_ref[...] = w2_ref[...].astype(BF16)
    twr, twi = twr_ref[...], twi_ref[...]
    zrow = jnp.zeros((CH_GROUP, FFT_N2), F32)

    def forward(xq):
        xq = xq.reshape(GROUP_ROWS, FFT_N2).astype(BF16)
        a = jnp.dot(bd1_ref[...], xq, preferred_element_type=F32)
        ar = a[:RE_ROWS]
        ai = jnp.concatenate([zrow, a[RE_ROWS:], zrow], axis=0)
        br = ar * twr - ai * twi
        bi = ar * twi + ai * twr
        bcat = jnp.concatenate([br, bi], axis=1).astype(BF16)
        return jnp.dot(bcat, w3_ref[...], preferred_element_type=F32)

    def group(gi, carry):
        u = u_ref[gi]
        xs = forward(u)
        fs = forward(hf_ref[gi])
        bs = forward(hb_ref[gi])
        xr, xi = xs[:, :FFT_N2], xs[:, FFT_N2:]
        kr = fs[:, :FFT_N2] + bs[:, :FFT_N2]
        ki = fs[:, FFT_N2:] - bs[:, FFT_N2:]
        ycat = jnp.concatenate([xr * kr - xi * ki, xr * ki + xi * kr], axis=1).astype(BF16)
        ap = jnp.dot(ycat, w3inv_ref[...], preferred_element_type=F32)
        apr, api = ap[:, :FFT_N2], ap[:, FFT_N2:]
        bpr = apr * twr + api * twi
        bpi = api * twr - apr * twi
        b2 = jnp.concatenate([bpr, bpi[CH_GROUP:RE_ROWS - CH_GROUP]], axis=0).astype(BF16)
        y = jnp.dot(bd1inv_ref[...], b2, preferred_element_type=F32)
        y3 = y.reshape(FFT_N1_LIVE, CH_GROUP, FFT_N2)
        bias = bias_ref[pl.ds(pl.multiple_of(gi * CH_GROUP, CH_GROUP), CH_GROUP), :]
        ya_ref[gi] = x0_ref[gi] * (y3 + bias[None] * u)
        return carry

    lax.fori_loop(0, HY_CB // CH_GROUP, group, 0, unroll=HY_UNROLL)


def _hyena(u_q, hf_q, hb_q, x0_q, bias2, tables, mlp_w1, mlp_w2, layer):
    bd1, bd1inv, twr, twi, w3, w3inv = tables
    steps = D_HY // HY_CB
    qspec = pl.BlockSpec((HY_CB // CH_GROUP,) + Q8_SHAPE[1:], lambda j: (j, 0, 0, 0))
    full = lambda a: pl.BlockSpec(a.shape, lambda j: (0,) * a.ndim)
    w1_blk = (D_MODEL // steps, D_FF)
    w2_blk = (D_FF // steps, D_MODEL)
    return pl.pallas_call(
        _hyena_kernel,
        grid=(steps,),
        in_specs=[qspec, qspec, qspec, qspec,
                  pl.BlockSpec((HY_CB, FFT_N2), lambda j: (j, 0)),
                  full(bd1), full(bd1inv), full(twr), full(twi), full(w3), full(w3inv),
                  pl.BlockSpec((None,) + w1_blk, lambda j: (layer, j, 0)),
                  pl.BlockSpec((None,) + w2_blk, lambda j: (layer, j, 0))],
        out_specs=[qspec, pl.BlockSpec(w1_blk, lambda j: (j, 0)), pl.BlockSpec(w2_blk, lambda j: (j, 0))],
        out_shape=[jax.ShapeDtypeStruct(Q8_SHAPE, F32),
                   jax.ShapeDtypeStruct((D_MODEL, D_FF), BF16),
                   jax.ShapeDtypeStruct((D_FF, D_MODEL), BF16)],
        compiler_params=_cparams(1),
        name="hyena_fftconv",
    )(u_q, hf_q, hb_q, x0_q, bias2, bd1, bd1inv, twr, twi, w3, w3inv, mlp_w1, mlp_w2)


OUT_TM = 512


def _out_proj_kernel(ya_ref, yb_ref, wa_ref, wb_ref, x_ref, g_ref, o_ref):
    tn = (((0,), (0,)), ((), ()))
    ya = jnp.concatenate([ya_ref[:, a].reshape(D_HY, FFT_N2) for a in range(OUT_TM // FFT_N2)],
                         axis=1).astype(BF16)
    m = lax.dot_general(ya, wa_ref[...], tn, preferred_element_type=F32)
    m = m + lax.dot_general(yb_ref[...], wb_ref[...], tn, preferred_element_type=F32)
    o_ref[...] = x_ref[...] + _rms_rows(m, g_ref[...])


def _out_proj(ya_q, yb_t, wa, wb, x2d, g):
    return pl.pallas_call(
        _out_proj_kernel,
        grid=(SEQ // OUT_TM,),
        in_specs=[
            pl.BlockSpec((Q8_SHAPE[0], OUT_TM // FFT_N2, CH_GROUP, FFT_N2), lambda i: (0, i, 0, 0)),
            pl.BlockSpec((D_SC, OUT_TM), lambda i: (0, i)),
            pl.BlockSpec((D_HY, D_MODEL), lambda i: (0, 0)),
            pl.BlockSpec((D_SC, D_MODEL), lambda i: (0, 0)),
            pl.BlockSpec((OUT_TM, D_MODEL), lambda i: (i, 0)),
            pl.BlockSpec((1, D_MODEL), lambda i: (0, 0)),
        ],
        out_specs=pl.BlockSpec((OUT_TM, D_MODEL), lambda i: (i, 0)),
        out_shape=jax.ShapeDtypeStruct((SEQ, D_MODEL), F32),
        compiler_params=_cparams(1),
        name="out_proj",
    )(ya_q, yb_t, wa, wb, x2d, g)


MLP_TM = 1024
MLP_TK = 512
MLP_TN = 512


def _mlp_kernel(x_ref, gin_ref, gout_ref, w1_ref, w2_ref, *rest):
    if len(rest) == 6:
        nw1_ref, nw2_ref, o_ref, nw1b_ref, nw2b_ref, h_ref = rest
        nw1b_ref[...] = nw1_ref[...].astype(BF16)
        nw2b_ref[...] = nw2_ref[...].astype(BF16)
    else:
        o_ref, h_ref = rest
    k = pl.program_id(1)

    @pl.when(k == 0)
    def _():
        h_ref[...] = _rms_rows(x_ref[...], gin_ref[...]).astype(BF16)
        o_ref[...] = jnp.zeros_like(o_ref)

    a = jnp.dot(h_ref[...], w1_ref[...], preferred_element_type=F32)
    a = jnp.square(jnp.maximum(a, 0.0)).astype(BF16)
    for n in range(D_MODEL // MLP_TN):
        cols = slice(n * MLP_TN, (n + 1) * MLP_TN)
        o_ref[:, cols] += jnp.dot(a, w2_ref[:, cols], preferred_element_type=F32)

    @pl.when(k == pl.num_programs(1) - 1)
    def _():
        o_ref[...] = x_ref[...] + _rms_rows(o_ref[...], gout_ref[...])


def _mlp(x2d, gin, gout, w1, w2, narrow=None):
    ni, nk = SEQ // MLP_TM, D_FF // MLP_TK
    in_specs = [
        pl.BlockSpec((MLP_TM, D_MODEL), lambda i, k: (i, 0)),
        pl.BlockSpec((1, D_MODEL), lambda i, k: (0, 0)),
        pl.BlockSpec((1, D_MODEL), lambda i, k: (0, 0)),
        pl.BlockSpec((D_MODEL, MLP_TK), lambda i, k: (0, k)),
        pl.BlockSpec((MLP_TK, D_MODEL), lambda i, k: (k, 0)),
    ]
    out_specs = [pl.BlockSpec((MLP_TM, D_MODEL), lambda i, k: (i, 0))]
    out_shape = [jax.ShapeDtypeStruct((SEQ, D_MODEL), F32)]
    args = [x2d, gin, gout, w1, w2]
    if narrow is not None:
        nw1, nw2, layer = narrow
        w1_blk = (D_MODEL // ni, D_FF // nk)
        w2_blk = (D_FF // ni, D_MODEL // nk)
        in_specs += [pl.BlockSpec((None,) + w1_blk, lambda i, k: (layer, i, k)),
                     pl.BlockSpec((None,) + w2_blk, lambda i, k: (layer, i, k))]
        out_specs += [pl.BlockSpec(w1_blk, lambda i, k: (i, k)), pl.BlockSpec(w2_blk, lambda i, k: (i, k))]
        out_shape += [jax.ShapeDtypeStruct((D_MODEL, D_FF), BF16),
                      jax.ShapeDtypeStruct((D_FF, D_MODEL), BF16)]
        args += [nw1, nw2]
    return pl.pallas_call(
        _mlp_kernel,
        grid=(ni, nk),
        in_specs=in_specs,
        out_specs=out_specs,
        out_shape=out_shape,
        scratch_shapes=[pltpu.VMEM((MLP_TM, D_MODEL), BF16)],
        compiler_params=_cparams(2),
        name="mlp",
    )(*args)


POOL_TM = 512
HALO = 8


def _pool_kernel(xp_ref, xm_ref, xn_ref, gin_ref, gout_ref, pw_ref, ps_ref, o_ref, hs_ref, m_ref):
    i = pl.program_id(0)
    gin = gin_ref[...]
    hs_ref[0:HALO, :] = jnp.where(i > 0, _rms_rows(xp_ref[...], gin), 0.0)
    hs_ref[HALO:HALO + POOL_TM, :] = _rms_rows(xm_ref[...], gin)
    hs_ref[HALO + POOL_TM:, :] = jnp.where(i < pl.num_programs(0) - 1, _rms_rows(xn_ref[...], gin), 0.0)
    row = lax.broadcasted_iota(jnp.int32, (POOL_TM, D_POOL_G), 0) + i * POOL_TM
    for gi, w in enumerate(POOL_WINDOWS):
        r = w // 2
        cols = slice(gi * D_POOL_G, (gi + 1) * D_POOL_G)
        s = hs_ref[HALO - r:HALO - r + POOL_TM, cols]
        for j in range(-r + 1, r + 1):
            s = s + hs_ref[HALO + j:HALO + j + POOL_TM, cols]
        cnt = jnp.minimum(row + r + 1, SEQ) - jnp.maximum(row - r, 0)
        d = s / cnt.astype(F32) - hs_ref[HALO:HALO + POOL_TM, cols]
        mg = jnp.dot(d.astype(BF16), pw_ref[gi], preferred_element_type=F32)
        m_ref[:, cols] = mg * ps_ref[:, cols]
    o_ref[...] = xm_ref[...] + _rms_rows(m_ref[...], gout_ref[...])


def _pool(x2d, gin, gout, pw, ps):
    nb8 = POOL_TM // HALO
    last8 = SEQ // HALO - 1
    return pl.pallas_call(
        _pool_kernel,
        grid=(SEQ // POOL_TM,),
        in_specs=[
            pl.BlockSpec((HALO, D_MODEL), lambda i: (jnp.maximum(i * nb8 - 1, 0), 0)),
            pl.BlockSpec((POOL_TM, D_MODEL), lambda i: (i, 0)),
            pl.BlockSpec((HALO, D_MODEL), lambda i: (jnp.minimum((i + 1) * nb8, last8), 0)),
            pl.BlockSpec((1, D_MODEL), lambda i: (0, 0)),
            pl.BlockSpec((1, D_MODEL), lambda i: (0, 0)),
            pl.BlockSpec((len(POOL_WINDOWS), D_POOL_G, D_POOL_G), lambda i: (0, 0, 0)),
            pl.BlockSpec((1, D_MODEL), lambda i: (0, 0)),
        ],
        out_specs=pl.BlockSpec((POOL_TM, D_MODEL), lambda i: (i, 0)),
        out_shape=jax.ShapeDtypeStruct((SEQ, D_MODEL), F32),
        scratch_shapes=[pltpu.VMEM((POOL_TM + 2 * HALO, D_MODEL), F32),
                        pltpu.VMEM((POOL_TM, D_MODEL), F32)],
        compiler_params=_cparams(1),
        name="pool_mixer",
    )(x2d, x2d, x2d, gin, gout, pw, ps)


def _filter_constants():
    L = SEQ
    t = np.linspace(0.0, 1.0, L)
    w_pos = 2.0 * np.pi * np.arange(L) / L
    bands = np.linspace(1e-4, FILTER_BANDS - 1, FILTER_BANDS)
    ang = w_pos[:, None] * bands[None, :]
    emb = np.concatenate([t[:, None], np.cos(ang), -np.sin(ang)], axis=-1)
    emb_t = np.zeros((EMB_PAD, L))
    emb_t[:EMB_DIM] = emb.T
    max_decay = math.log(DECAY_TARGET) / FAST_DECAY_PCT
    min_decay = math.log(DECAY_TARGET) / SLOW_DECAY_PCT
    absdelta = np.abs(np.linspace(min_decay, max_decay, D_HY))[:, None]
    return (jnp.asarray(emb_t, F32), jnp.asarray(t[None, :], F32), jnp.asarray(absdelta, F32))


def _mixer_layer0(x2d, g, w_in, hy_short_w, hy_short_b, f_w1, f_b1, f_w2, f_b2, f_w3, f_b3,
                  freq, hy_bias, sc_conv_w, w_out, mlp_w1, mlp_w2, layer):
    wt = w_in.T.astype(BF16)
    zt = _in_proj(x2d, g[0][None, :], wt)

    hy_par = jnp.concatenate([hy_short_w.T, hy_short_b[:, None]], axis=1)
    sc_par = jnp.concatenate([sc_conv_w.T, jnp.zeros((D_SC, 1), F32)], axis=1)
    x0_q, u_q, yb_t = _gate(zt, hy_par, sc_par)

    emb_t, t_row, absdelta = _filter_constants()
    w1t = jnp.zeros((FILTER_HIDDEN, EMB_PAD), F32).at[:, :EMB_DIM].set(f_w1.T)
    col = lambda v: v[:, None].astype(F32)
    hf_q, hb_q = _filters(emb_t, t_row, w1t, col(f_b1), f_w2.T, col(f_b2), col(freq),
                          f_w3.T.astype(BF16), col(f_b3), absdelta)

    bias2 = jnp.broadcast_to(hy_bias[:, None].astype(F32), (D_HY, FFT_N2))
    ya_q, w1_this, w2_this = _hyena(u_q, hf_q, hb_q, x0_q, bias2, _dft_tables(), mlp_w1, mlp_w2, layer)

    wo = w_out.astype(BF16)
    x2d = _out_proj(ya_q, yb_t, wo[:D_HY], wo[D_HY:], x2d, g[1][None, :])
    return x2d, (w1_this, w2_this)


def kernel(x, norm_g, mix_w_in, hy_short_w, hy_short_b, hy_filt_w1, hy_filt_b1, hy_filt_w2,
           hy_filt_b2, hy_filt_w3, hy_filt_b3, hy_freq, hy_bias, sc_conv_w, mix_w_out,
           pool_w, pool_scale, mlp_w1, mlp_w2):
    x2d = x.reshape(SEQ, D_MODEL)
    depth = norm_g.shape[0]
    assert depth % 2 == 0, "an even layer narrows its own MLP weights and, in its MLP, the next layer's"
    for i in range(depth):
        g = norm_g[i]
        j = i // 2
        gin, gout = g[2][None, :], g[3][None, :]
        if i % 2 == 0:
            x2d, w_this = _mixer_layer0(
                x2d, g, mix_w_in[j], hy_short_w[j], hy_short_b[j], hy_filt_w1[j], hy_filt_b1[j],
                hy_filt_w2[j], hy_filt_b2[j], hy_filt_w3[j], hy_filt_b3[j], hy_freq[j], hy_bias[j],
                sc_conv_w[j], mix_w_out[j], mlp_w1, mlp_w2, i)
            x2d, *w_next = _mlp(x2d, gin, gout, *w_this, narrow=(mlp_w1, mlp_w2, i + 1))
        else:
            x2d = _pool(x2d, g[0][None, :], g[1][None, :], pool_w[j].astype(BF16),
                        pool_scale[j][None, :])
            x2d, = _mlp(x2d, gin, gout, *w_next)
    return x2d.reshape(x.shape)
```

```python
import math

import numpy as np
import jax
import jax.numpy as jnp
from jax import lax
from jax.experimental import pallas as pl
from jax.experimental.pallas import tpu as pltpu

F32 = jnp.float32
BF16 = jnp.bfloat16

D_MODEL = 2048
SEQ = 8192
D_HY = D_MODEL // 2
D_SC = D_MODEL // 2
D_IN = 3 * D_HY + 3 * D_SC
FILTER_BANDS = 16
EMB_DIM = 1 + 2 * FILTER_BANDS
EMB_PAD = 40
FILTER_HIDDEN = 64
DECAY_TARGET = 1e-2
FAST_DECAY_PCT = 0.3
SLOW_DECAY_PCT = 1.5
POOL_WINDOWS = (2, 4, 8, 16)
D_POOL_G = D_MODEL // len(POOL_WINDOWS)
D_FF = 4 * D_MODEL
NORM_EPS = 1e-6
LANES = 128

FFT_N = 2 * SEQ
FFT_N2 = 256
FFT_N1 = FFT_N // FFT_N2
FFT_N1_LIVE = SEQ // FFT_N2
CH_GROUP = 8
Q8_SHAPE = (D_MODEL // 2 // CH_GROUP, FFT_N1_LIVE, CH_GROUP, FFT_N2)

VMEM_LIMIT = 56 * 1024 * 1024


def _cparams(n_axes):
    return pltpu.CompilerParams(
        dimension_semantics=("arbitrary",) * n_axes, vmem_limit_bytes=VMEM_LIMIT)


def _rms_rows(x, g):
    r = lax.rsqrt(jnp.mean(x * x, axis=-1, keepdims=True) + NORM_EPS)
    return x * r * g


IN_TM = 1024
IN_TN = 1024


def _in_proj_kernel(x_ref, g_ref, wt_ref, zt_ref, h_ref):
    @pl.when(pl.program_id(1) == 0)
    def _():
        h_ref[...] = _rms_rows(x_ref[...], g_ref[...]).astype(BF16)

    zt_ref[...] = lax.dot_general(
        wt_ref[...], h_ref[...], (((1,), (1,)), ((), ())), preferred_element_type=F32)


def _in_proj(x2d, g, wt):
    return pl.pallas_call(
        _in_proj_kernel,
        grid=(SEQ // IN_TM, D_IN // IN_TN),
        in_specs=[
            pl.BlockSpec((IN_TM, D_MODEL), lambda i, j: (i, 0)),
            pl.BlockSpec((1, D_MODEL), lambda i, j: (0, 0)),
            pl.BlockSpec((IN_TN, D_MODEL), lambda i, j: (j, 0)),
        ],
        out_specs=pl.BlockSpec((IN_TN, IN_TM), lambda i, j: (j, i)),
        out_shape=jax.ShapeDtypeStruct((D_IN, SEQ), F32),
        scratch_shapes=[pltpu.VMEM((IN_TM, D_MODEL), BF16)],
        compiler_params=_cparams(2),
        name="in_proj",
    )(x2d, g, wt)


GATE_CB = 32
GATE_ROWS = 16
GATE_CHUNK = 1024


def _conv3_chunk(zwin, w, off, first, last):
    width = zwin.shape[1]
    zm = pltpu.roll(zwin, 1, 1)[:, off:off + GATE_CHUNK]
    zp = pltpu.roll(zwin, width - 1, 1)[:, off:off + GATE_CHUNK]
    z = zwin[:, off:off + GATE_CHUNK]
    lane = lax.broadcasted_iota(jnp.int32, z.shape, 1)
    if first:
        zm = jnp.where(lane == 0, 0.0, zm)
    if last:
        zp = jnp.where(lane == GATE_CHUNK - 1, 0.0, zp)
    return zm * w[:, 0:1] + z * w[:, 1:2] + zp * w[:, 2:3]


def _gate_kernel(zx0_ref, zx1_ref, zv_ref, zgb_ref, zgc_ref, zxv_ref,
                 px0_ref, px1_ref, pv_ref, psc_ref, x0_ref, u_ref, yb_ref):
    n_chunks = SEQ // GATE_CHUNK
    groups = GATE_ROWS // CH_GROUP
    slabs = GATE_CHUNK // FFT_N2

    def row_tile(rt, carry):
        rows = pl.ds(pl.multiple_of(rt * GATE_ROWS, GATE_ROWS), GATE_ROWS)
        px0, px1, pv, psc = px0_ref[rows, :], px1_ref[rows, :], pv_ref[rows, :], psc_ref[rows, :]
        for c in range(n_chunks):
            lo = max(c * GATE_CHUNK - LANES, 0)
            hi = min((c + 1) * GATE_CHUNK + LANES, SEQ)
            edge = (c * GATE_CHUNK - lo, c == 0, c == n_chunks - 1)
            cur = slice(c * GATE_CHUNK, (c + 1) * GATE_CHUNK)
            x0 = _conv3_chunk(zx0_ref[rows, lo:hi], px0, *edge) + px0[:, 3:4]
            x1 = _conv3_chunk(zx1_ref[rows, lo:hi], px1, *edge) + px1[:, 3:4]
            v = _conv3_chunk(zv_ref[rows, lo:hi], pv, *edge) + pv[:, 3:4]
            u = v * x1
            for g in range(groups):
                for a in range(slabs):
                    sl = (slice(g * CH_GROUP, (g + 1) * CH_GROUP), slice(a * FFT_N2, (a + 1) * FFT_N2))
                    x0_ref[rt * groups + g, c * slabs + a] = x0[sl]
                    u_ref[rt * groups + g, c * slabs + a] = u[sl]
            p = zgc_ref[rows, lo:hi] * zxv_ref[rows, lo:hi]
            yb_ref[rows, cur] = (zgb_ref[rows, cur] * _conv3_chunk(p, psc, *edge)).astype(BF16)
        return carry

    lax.fori_loop(0, GATE_CB // GATE_ROWS, row_tile, 0)


def _gate(zt, hy_par, sc_par):
    nb = D_HY // GATE_CB
    zspec = lambda s: pl.BlockSpec((GATE_CB, SEQ), lambda j, s=s: (s * nb + j, 0))
    pspec = lambda s: pl.BlockSpec((GATE_CB, 4), lambda j, s=s: (s * nb + j, 0))
    out = jax.ShapeDtypeStruct((D_HY, SEQ), BF16)
    ospec = pl.BlockSpec((GATE_CB, SEQ), lambda j: (j, 0))
    qout = jax.ShapeDtypeStruct(Q8_SHAPE, F32)
    qspec = pl.BlockSpec((GATE_CB // CH_GROUP,) + Q8_SHAPE[1:], lambda j: (j, 0, 0, 0))
    return pl.pallas_call(
        _gate_kernel,
        grid=(nb,),
        in_specs=[zspec(s) for s in range(6)] + [pspec(0), pspec(1), pspec(2),
                                                 pl.BlockSpec((GATE_CB, 4), lambda j: (j, 0))],
        out_specs=[qspec, qspec, ospec],
        out_shape=[qout, qout, out],
        compiler_params=_cparams(1),
        name="gate",
    )(zt, zt, zt, zt, zt, zt, hy_par, hy_par, hy_par, sc_par)


FILT_TB = 1024


def _filter_kernel(emb_ref, t_ref, w1_ref, b1_ref, w2_ref, b2_ref, fr_ref, w3_ref, b3_ref,
                   ad_ref, hf_ref, hb_ref, hfs_ref, hbs_ref):
    hi = lax.Precision.HIGHEST
    fr = fr_ref[...]
    a1 = jnp.dot(w1_ref[...], emb_ref[...], precision=hi, preferred_element_type=F32) + b1_ref[...]
    h1 = jnp.sin(fr * a1)
    a2 = jnp.dot(w2_ref[...], h1, precision=hi, preferred_element_type=F32) + b2_ref[...]
    h2 = jnp.sin(fr * a2)
    o = jnp.dot(w3_ref[...], h2.astype(BF16), preferred_element_type=F32) + b3_ref[...]
    window = jnp.exp(-(ad_ref[...] * t_ref[...]))
    hfs_ref[...] = o[:D_HY] * window
    pos = lax.broadcasted_iota(jnp.int32, (D_HY, FILT_TB), 1) + pl.program_id(0) * FILT_TB
    hbs_ref[...] = jnp.where(pos == 0, 0.0, o[D_HY:] * window)

    def to_q(cg, carry):
        rows = pl.ds(pl.multiple_of(cg * CH_GROUP, CH_GROUP), CH_GROUP)
        for n1 in range(FILT_TB // FFT_N2):
            lanes = slice(n1 * FFT_N2, (n1 + 1) * FFT_N2)
            hf_ref[cg, n1] = hfs_ref[rows, lanes]
            hb_ref[cg, n1] = hbs_ref[rows, lanes]
        return carry

    lax.fori_loop(0, D_HY // CH_GROUP, to_q, 0)


def _filters(emb_t, t_row, w1t, b1, w2t, b2, fr, w3t, b3, absdelta):
    full = lambda a: pl.BlockSpec(a.shape, lambda i: (0,) * a.ndim)
    out = jax.ShapeDtypeStruct(Q8_SHAPE, F32)
    qspec = pl.BlockSpec((Q8_SHAPE[0], FILT_TB // FFT_N2, CH_GROUP, FFT_N2), lambda i: (0, i, 0, 0))
    return pl.pallas_call(
        _filter_kernel,
        grid=(SEQ // FILT_TB,),
        in_specs=[pl.BlockSpec((EMB_PAD, FILT_TB), lambda i: (0, i)),
                  pl.BlockSpec((1, FILT_TB), lambda i: (0, i)),
                  full(w1t), full(b1), full(w2t), full(b2), full(fr), full(w3t), full(b3),
                  full(absdelta)],
        out_specs=[qspec, qspec],
        out_shape=[out, out],
        scratch_shapes=[pltpu.VMEM((D_HY, FILT_TB), F32)] * 2,
        compiler_params=_cparams(1),
        name="filters",
    )(emb_t, t_row, w1t, b1, w2t, b2, fr, w3t, b3, absdelta)


HY_CB = 32
HY_UNROLL = 4
GROUP_ROWS = CH_GROUP * FFT_N1_LIVE
N_K1 = FFT_N1 // 2 + 1
RE_ROWS = N_K1 * CH_GROUP
IM_ROWS = (N_K1 - 2) * CH_GROUP


def _dft_tables():
    n1 = np.arange(FFT_N1_LIVE, dtype=np.float64)
    k1 = np.arange(N_K1, dtype=np.float64)
    th = 2.0 * np.pi * np.outer(k1, n1) / FFT_N1
    eye = np.eye(CH_GROUP)
    bd1 = np.concatenate([np.einsum("kn,cd->kcnd", np.cos(th), eye).reshape(RE_ROWS, GROUP_ROWS),
                          np.einsum("kn,cd->kcnd", -np.sin(th)[1:-1], eye).reshape(IM_ROWS, GROUP_ROWS)])
    wgt = np.where((k1 == 0) | (k1 == FFT_N1 // 2), 1.0, 2.0)[:, None] / FFT_N
    bd1inv = np.concatenate(
        [np.einsum("kn,cd->nckd", wgt * np.cos(th), eye).reshape(GROUP_ROWS, RE_ROWS),
         np.einsum("kn,cd->nckd", (-wgt * np.sin(th))[1:-1], eye).reshape(GROUP_ROWS, IM_ROWS)], axis=1)
    n2 = np.arange(FFT_N2, dtype=np.float64)
    ph = 2.0 * np.pi * np.outer(k1, n2) / FFT_N
    twr = np.repeat(np.cos(ph), CH_GROUP, axis=0)
    twi = np.repeat(-np.sin(ph), CH_GROUP, axis=0)
    ps = 2.0 * np.pi * np.outer(n2, n2) / FFT_N2
    cr, ci = np.cos(ps), -np.sin(ps)
    w3 = np.block([[cr, ci], [-ci, cr]])
    w3inv = np.block([[cr, -ci], [ci, cr]])
    f32 = lambda a: jnp.asarray(a, F32)
    return (f32(bd1).astype(BF16), f32(bd1inv).astype(BF16), f32(twr), f32(twi),
            f32(w3).astype(BF16), f32(w3inv).astype(BF16))


def _hyena_kernel(u_ref, hf_ref, hb_ref, x0_ref, bias_ref, bd1_ref, bd1inv_ref, twr_ref, twi_ref,
                  w3_ref, w3inv_ref, w1_ref, w2_ref, ya_ref, w1b_ref, w2b_ref):
    w1b_ref[...] = w1_ref[...].astype(BF16)
    w2b_ref[...] = w2_ref[...].astype(BF16)
    twr, twi = twr_ref[...], twi_ref[...]
    zrow = jnp.zeros((CH_GROUP, FFT_N2), F32)

    def stage1(xq):
        xq = xq.reshape(GROUP_ROWS, FFT_N2).astype(BF16)
        a = jnp.dot(bd1_ref[...], xq, preferred_element_type=F32)
        ar = a[:RE_ROWS]
        ai = jnp.concatenate([zrow, a[RE_ROWS:], zrow], axis=0)
        br = ar * twr - ai * twi
        bi = ar * twi + ai * twr
        return jnp.concatenate([br, bi], axis=1)

    def group(gi, carry):
        u = u_ref[gi]
        b_all = jnp.concatenate([stage1(u), stage1(hf_ref[gi]), stage1(hb_ref[gi])], axis=0)
        s_all = jnp.dot(b_all.astype(BF16), w3_ref[...], preferred_element_type=F32)
        xs, fs, bs = s_all[:RE_ROWS], s_all[RE_ROWS:2 * RE_ROWS], s_all[2 * RE_ROWS:]
        xr, xi = xs[:, :FFT_N2], xs[:, FFT_N2:]
        kr = fs[:, :FFT_N2] + bs[:, :FFT_N2]
        ki = fs[:, FFT_N2:] - bs[:, FFT_N2:]
        ycat = jnp.concatenate([xr * kr - xi * ki, xr * ki + xi * kr], axis=1).astype(BF16)
        ap = jnp.dot(ycat, w3inv_ref[...], preferred_element_type=F32)
        apr, api = ap[:, :FFT_N2], ap[:, FFT_N2:]
        bpr = apr * twr + api * twi
        bpi = api * twr - apr * twi
        b2 = jnp.concatenate([bpr, bpi[CH_GROUP:RE_ROWS - CH_GROUP]], axis=0).astype(BF16)
        y = jnp.dot(bd1inv_ref[...], b2, preferred_element_type=F32)
        y3 = y.reshape(FFT_N1_LIVE, CH_GROUP, FFT_N2)
        bias = bias_ref[pl.ds(pl.multiple_of(gi * CH_GROUP, CH_GROUP), CH_GROUP), :]
        ya_ref[gi] = x0_ref[gi] * (y3 + bias[None] * u)
        return carry

    lax.fori_loop(0, HY_CB // CH_GROUP, group, 0, unroll=HY_UNROLL)


def _hyena(u_q, hf_q, hb_q, x0_q, bias2, tables, mlp_w1, mlp_w2, layer):
    bd1, bd1inv, twr, twi, w3, w3inv = tables
    steps = D_HY // HY_CB
    qspec = pl.BlockSpec((HY_CB // CH_GROUP,) + Q8_SHAPE[1:], lambda j: (j, 0, 0, 0))
    full = lambda a: pl.BlockSpec(a.shape, lambda j: (0,) * a.ndim)
    w1_blk = (D_MODEL // steps, D_FF)
    w2_blk = (D_FF // steps, D_MODEL)
    return pl.pallas_call(
        _hyena_kernel,
        grid=(steps,),
        in_specs=[qspec, qspec, qspec, qspec,
                  pl.BlockSpec((HY_CB, FFT_N2), lambda j: (j, 0)),
                  full(bd1), full(bd1inv), full(twr), full(twi), full(w3), full(w3inv),
                  pl.BlockSpec((None,) + w1_blk, lambda j: (layer, j, 0)),
                  pl.BlockSpec((None,) + w2_blk, lambda j: (layer, j, 0))],
        out_specs=[qspec, pl.BlockSpec(w1_blk, lambda j: (j, 0)), pl.BlockSpec(w2_blk, lambda j: (j, 0))],
        out_shape=[jax.ShapeDtypeStruct(Q8_SHAPE, F32),
                   jax.ShapeDtypeStruct((D_MODEL, D_FF), BF16),
                   jax.ShapeDtypeStruct((D_FF, D_MODEL), BF16)],
        compiler_params=_cparams(1),
        name="hyena_fftconv",
    )(u_q, hf_q, hb_q, x0_q, bias2, bd1, bd1inv, twr, twi, w3, w3inv, mlp_w1, mlp_w2)


OUT_TM = 512


def _out_proj_kernel(ya_ref, yb_ref, wa_ref, wb_ref, x_ref, g_ref, o_ref):
    tn = (((0,), (0,)), ((), ()))
    ya = jnp.concatenate([ya_ref[:, a].reshape(D_HY, FFT_N2) for a in range(OUT_TM // FFT_N2)],
                         axis=1).astype(BF16)
    m = lax.dot_general(ya, wa_ref[...], tn, preferred_element_type=F32)
    m = m + lax.dot_general(yb_ref[...], wb_ref[...], tn, preferred_element_type=F32)
    o_ref[...] = x_ref[...] + _rms_rows(m, g_ref[...])


def _out_proj(ya_q, yb_t, wa, wb, x2d, g):
    return pl.pallas_call(
        _out_proj_kernel,
        grid=(SEQ // OUT_TM,),
        in_specs=[
            pl.BlockSpec((Q8_SHAPE[0], OUT_TM // FFT_N2, CH_GROUP, FFT_N2), lambda i: (0, i, 0, 0)),
            pl.BlockSpec((D_SC, OUT_TM), lambda i: (0, i)),
            pl.BlockSpec((D_HY, D_MODEL), lambda i: (0, 0)),
            pl.BlockSpec((D_SC, D_MODEL), lambda i: (0, 0)),
            pl.BlockSpec((OUT_TM, D_MODEL), lambda i: (i, 0)),
            pl.BlockSpec((1, D_MODEL), lambda i: (0, 0)),
        ],
        out_specs=pl.BlockSpec((OUT_TM, D_MODEL), lambda i: (i, 0)),
        out_shape=jax.ShapeDtypeStruct((SEQ, D_MODEL), F32),
        compiler_params=_cparams(1),
        name="out_proj",
    )(ya_q, yb_t, wa, wb, x2d, g)


MLP_TM = 1024
MLP_TK = 512
MLP_TN = 512


def _mlp_kernel(x_ref, gin_ref, gout_ref, w1_ref, w2_ref, *rest):
    if len(rest) == 6:
        nw1_ref, nw2_ref, o_ref, nw1b_ref, nw2b_ref, h_ref = rest
        nw1b_ref[...] = nw1_ref[...].astype(BF16)
        nw2b_ref[...] = nw2_ref[...].astype(BF16)
    else:
        o_ref, h_ref = rest
    k = pl.program_id(1)

    @pl.when(k == 0)
    def _():
        h_ref[...] = _rms_rows(x_ref[...], gin_ref[...]).astype(BF16)
        o_ref[...] = jnp.zeros_like(o_ref)

    a = jnp.dot(h_ref[...], w1_ref[...], preferred_element_type=F32)
    a = jnp.square(jnp.maximum(a, 0.0)).astype(BF16)
    for n in range(D_MODEL // MLP_TN):
        cols = slice(n * MLP_TN, (n + 1) * MLP_TN)
        o_ref[:, cols] += jnp.dot(a, w2_ref[:, cols], preferred_element_type=F32)

    @pl.when(k == pl.num_programs(1) - 1)
    def _():
        o_ref[...] = x_ref[...] + _rms_rows(o_ref[...], gout_ref[...])


def _mlp(x2d, gin, gout, w1, w2, narrow=None):
    ni, nk = SEQ // MLP_TM, D_FF // MLP_TK
    in_specs = [
        pl.BlockSpec((MLP_TM, D_MODEL), lambda i, k: (i, 0)),
        pl.BlockSpec((1, D_MODEL), lambda i, k: (0, 0)),
        pl.BlockSpec((1, D_MODEL), lambda i, k: (0, 0)),
        pl.BlockSpec((D_MODEL, MLP_TK), lambda i, k: (0, k)),
        pl.BlockSpec((MLP_TK, D_MODEL), lambda i, k: (k, 0)),
    ]
    out_specs = [pl.BlockSpec((MLP_TM, D_MODEL), lambda i, k: (i, 0))]
    out_shape = [jax.ShapeDtypeStruct((SEQ, D_MODEL), F32)]
    args = [x2d, gin, gout, w1, w2]
    if narrow is not None:
        nw1, nw2, layer = narrow
        w1_blk = (D_MODEL // ni, D_FF // nk)
        w2_blk = (D_FF // ni, D_MODEL // nk)
        in_specs += [pl.BlockSpec((None,) + w1_blk, lambda i, k: (layer, i, k)),
                     pl.BlockSpec((None,) + w2_blk, lambda i, k: (layer, i, k))]
        out_specs += [pl.BlockSpec(w1_blk, lambda i, k: (i, k)), pl.BlockSpec(w2_blk, lambda i, k: (i, k))]
        out_shape += [jax.ShapeDtypeStruct((D_MODEL, D_FF), BF16),
                      jax.ShapeDtypeStruct((D_FF, D_MODEL), BF16)]
        args += [nw1, nw2]
    return pl.pallas_call(
        _mlp_kernel,
        grid=(ni, nk),
        in_specs=in_specs,
        out_specs=out_specs,
        out_shape=out_shape,
        scratch_shapes=[pltpu.VMEM((MLP_TM, D_MODEL), BF16)],
        compiler_params=_cparams(2),
        name="mlp",
    )(*args)


POOL_TM = 512
POOL_SUB = 128
HALO = 8


def _pool_bands():
    t = np.arange(POOL_SUB)[:, None]
    j = np.arange(POOL_SUB + 2 * HALO)[None, :]
    return np.stack([(np.abs(j - HALO - t) <= w // 2) for w in POOL_WINDOWS]).astype(np.float32)


def _pool_kernel(xp_ref, xm_ref, xn_ref, gin_ref, gout_ref, band_ref, pw_ref, ps_ref, o_ref,
                 hs_ref, hi_ref, lo_ref, m_ref):
    i = pl.program_id(0)
    gin = gin_ref[...]
    hs_ref[0:HALO, :] = jnp.where(i > 0, _rms_rows(xp_ref[...], gin), 0.0)
    hs_ref[HALO:HALO + POOL_TM, :] = _rms_rows(xm_ref[...], gin)
    hs_ref[HALO + POOL_TM:, :] = jnp.where(i < pl.num_programs(0) - 1, _rms_rows(xn_ref[...], gin), 0.0)
    hs = hs_ref[...]
    hi = hs.astype(BF16)
    hi_ref[...] = hi
    lo_ref[...] = (hs - hi.astype(F32)).astype(BF16)
    row = lax.broadcasted_iota(jnp.int32, (POOL_SUB, D_POOL_G), 0) + i * POOL_TM
    for gi, w in enumerate(POOL_WINDOWS):
        r = w // 2
        cols = slice(gi * D_POOL_G, (gi + 1) * D_POOL_G)
        band = band_ref[gi]
        parts = []
        for sb in range(POOL_TM // POOL_SUB):
            win = slice(sb * POOL_SUB, (sb + 1) * POOL_SUB + 2 * HALO)
            s = (jnp.dot(band, hi_ref[win, cols], preferred_element_type=F32)
                 + jnp.dot(band, lo_ref[win, cols], preferred_element_type=F32))
            t = row + sb * POOL_SUB
            cnt = jnp.minimum(t + r + 1, SEQ) - jnp.maximum(t - r, 0)
            u = hs_ref[sb * POOL_SUB + HALO:(sb + 1) * POOL_SUB + HALO, cols]
            parts.append((s / cnt.astype(F32) - u).astype(BF16))
        d = jnp.concatenate(parts, axis=0)
        mg = jnp.dot(d, pw_ref[gi], preferred_element_type=F32)
        m_ref[:, cols] = mg * ps_ref[:, cols]
    o_ref[...] = xm_ref[...] + _rms_rows(m_ref[...], gout_ref[...])


def _pool(x2d, gin, gout, pw, ps):
    nb8 = POOL_TM // HALO
    last8 = SEQ // HALO - 1
    band = jnp.asarray(_pool_bands(), F32).astype(BF16)
    ext = POOL_TM + 2 * HALO
    return pl.pallas_call(
        _pool_kernel,
        grid=(SEQ // POOL_TM,),
        in_specs=[
            pl.BlockSpec((HALO, D_MODEL), lambda i: (jnp.maximum(i * nb8 - 1, 0), 0)),
            pl.BlockSpec((POOL_TM, D_MODEL), lambda i: (i, 0)),
            pl.BlockSpec((HALO, D_MODEL), lambda i: (jnp.minimum((i + 1) * nb8, last8), 0)),
            pl.BlockSpec((1, D_MODEL), lambda i: (0, 0)),
            pl.BlockSpec((1, D_MODEL), lambda i: (0, 0)),
            pl.BlockSpec(band.shape, lambda i: (0, 0, 0)),
            pl.BlockSpec((len(POOL_WINDOWS), D_POOL_G, D_POOL_G), lambda i: (0, 0, 0)),
            pl.BlockSpec((1, D_MODEL), lambda i: (0, 0)),
        ],
        out_specs=pl.BlockSpec((POOL_TM, D_MODEL), lambda i: (i, 0)),
        out_shape=jax.ShapeDtypeStruct((SEQ, D_MODEL), F32),
        scratch_shapes=[pltpu.VMEM((ext, D_MODEL), F32), pltpu.VMEM((ext, D_MODEL), BF16),
                        pltpu.VMEM((ext, D_MODEL), BF16), pltpu.VMEM((POOL_TM, D_MODEL), F32)],
        compiler_params=_cparams(1),
        name="pool_mixer",
    )(x2d, x2d, x2d, gin, gout, band, pw, ps)


def _filter_constants():
    L = SEQ
    t = np.linspace(0.0, 1.0, L)
    w_pos = 2.0 * np.pi * np.arange(L) / L
    bands = np.linspace(1e-4, FILTER_BANDS - 1, FILTER_BANDS)
    ang = w_pos[:, None] * bands[None, :]
    emb = np.concatenate([t[:, None], np.cos(ang), -np.sin(ang)], axis=-1)
    emb_t = np.zeros((EMB_PAD, L))
    emb_t[:EMB_DIM] = emb.T
    max_decay = math.log(DECAY_TARGET) / FAST_DECAY_PCT
    min_decay = math.log(DECAY_TARGET) / SLOW_DECAY_PCT
    absdelta = np.abs(np.linspace(min_decay, max_decay, D_HY))[:, None]
    return (jnp.asarray(emb_t, F32), jnp.asarray(t[None, :], F32), jnp.asarray(absdelta, F32))


def _mixer_layer0(x2d, g, w_in, hy_short_w, hy_short_b, f_w1, f_b1, f_w2, f_b2, f_w3, f_b3,
                  freq, hy_bias, sc_conv_w, w_out, mlp_w1, mlp_w2, layer):
    wt = w_in.T.astype(BF16)
    zt = _in_proj(x2d, g[0][None, :], wt)

    hy_par = jnp.concatenate([hy_short_w.T, hy_short_b[:, None]], axis=1)
    sc_par = jnp.concatenate([sc_conv_w.T, jnp.zeros((D_SC, 1), F32)], axis=1)
    x0_q, u_q, yb_t = _gate(zt, hy_par, sc_par)

    emb_t, t_row, absdelta = _filter_constants()
    w1t = jnp.zeros((FILTER_HIDDEN, EMB_PAD), F32).at[:, :EMB_DIM].set(f_w1.T)
    col = lambda v: v[:, None].astype(F32)
    hf_q, hb_q = _filters(emb_t, t_row, w1t, col(f_b1), f_w2.T, col(f_b2), col(freq),
                          f_w3.T.astype(BF16), col(f_b3), absdelta)

    bias2 = jnp.broadcast_to(hy_bias[:, None].astype(F32), (D_HY, FFT_N2))
    ya_q, w1_this, w2_this = _hyena(u_q, hf_q, hb_q, x0_q, bias2, _dft_tables(), mlp_w1, mlp_w2, layer)

    wo = w_out.astype(BF16)
    x2d = _out_proj(ya_q, yb_t, wo[:D_HY], wo[D_HY:], x2d, g[1][None, :])
    return x2d, (w1_this, w2_this)


def kernel(x, norm_g, mix_w_in, hy_short_w, hy_short_b, hy_filt_w1, hy_filt_b1, hy_filt_w2,
           hy_filt_b2, hy_filt_w3, hy_filt_b3, hy_freq, hy_bias, sc_conv_w, mix_w_out,
           pool_w, pool_scale, mlp_w1, mlp_w2):
    x2d = x.reshape(SEQ, D_MODEL)
    depth = norm_g.shape[0]
    assert depth % 2 == 0, "an even layer narrows its own MLP weights and, in its MLP, the next layer's"
    for i in range(depth):
        g = norm_g[i]
        j = i // 2
        gin, gout = g[2][None, :], g[3][None, :]
        if i % 2 == 0:
            x2d, w_this = _mixer_layer0(
                x2d, g, mix_w_in[j], hy_short_w[j], hy_short_b[j], hy_filt_w1[j], hy_filt_b1[j],
                hy_filt_w2[j], hy_filt_b2[j], hy_filt_w3[j], hy_filt_b3[j], hy_freq[j], hy_bias[j],
                sc_conv_w[j], mix_w_out[j], mlp_w1, mlp_w2, i)
            x2d, *w_next = _mlp(x2d, gin, gout, *w_this, narrow=(mlp_w1, mlp_w2, i + 1))
        else:
            x2d = _pool(x2d, g[0][None, :], g[1][None, :], pool_w[j].astype(BF16),
                        pool_scale[j][None, :])
            x2d, = _mlp(x2d, gin, gout, *w_next)
    return x2d.reshape(x.shape)
```

```python
import math

import numpy as np
import jax
import jax.numpy as jnp
from jax import lax
from jax.experimental import pallas as pl
from jax.experimental.pallas import tpu as pltpu

F32 = jnp.float32
BF16 = jnp.bfloat16

D_MODEL = 2048
SEQ = 8192
D_HY = D_MODEL // 2
D_SC = D_MODEL // 2
D_IN = 3 * D_HY + 3 * D_SC
FILTER_BANDS = 16
EMB_DIM = 1 + 2 * FILTER_BANDS
EMB_PAD = 40
FILTER_HIDDEN = 64
DECAY_TARGET = 1e-2
FAST_DECAY_PCT = 0.3
SLOW_DECAY_PCT = 1.5
POOL_WINDOWS = (2, 4, 8, 16)
D_POOL_G = D_MODEL // len(POOL_WINDOWS)
D_FF = 4 * D_MODEL
NORM_EPS = 1e-6
LANES = 128

FFT_N = 2 * SEQ
FFT_N2 = 256
FFT_N1 = FFT_N // FFT_N2
FFT_N1_LIVE = SEQ // FFT_N2
CH_GROUP = 8
Q8_SHAPE = (D_MODEL // 2 // CH_GROUP, FFT_N1_LIVE, CH_GROUP, FFT_N2)

VMEM_LIMIT = 56 * 1024 * 1024


def _cparams(n_axes):
    return pltpu.CompilerParams(
        dimension_semantics=("arbitrary",) * n_axes, vmem_limit_bytes=VMEM_LIMIT)


def _rms_rows(x, g):
    r = lax.rsqrt(jnp.mean(x * x, axis=-1, keepdims=True) + NORM_EPS)
    return x * r * g


IN_TM = 1024
IN_TN = 1024


def _in_proj_kernel(x_ref, g_ref, wt_ref, zt_ref, h_ref):
    @pl.when(pl.program_id(1) == 0)
    def _():
        h_ref[...] = _rms_rows(x_ref[...], g_ref[...]).astype(BF16)

    zt_ref[...] = lax.dot_general(
        wt_ref[...], h_ref[...], (((1,), (1,)), ((), ())), preferred_element_type=F32).astype(BF16)


def _in_proj(x2d, g, wt):
    return pl.pallas_call(
        _in_proj_kernel,
        grid=(SEQ // IN_TM, D_IN // IN_TN),
        in_specs=[
            pl.BlockSpec((IN_TM, D_MODEL), lambda i, j: (i, 0)),
            pl.BlockSpec((1, D_MODEL), lambda i, j: (0, 0)),
            pl.BlockSpec((IN_TN, D_MODEL), lambda i, j: (j, 0)),
        ],
        out_specs=pl.BlockSpec((IN_TN, IN_TM), lambda i, j: (j, i)),
        out_shape=jax.ShapeDtypeStruct((D_IN, SEQ), BF16),
        scratch_shapes=[pltpu.VMEM((IN_TM, D_MODEL), BF16)],
        compiler_params=_cparams(2),
        name="in_proj",
    )(x2d, g, wt)


GATE_CB = 32
GATE_ROWS = 16
GATE_CHUNK = 1024


def _conv3_chunk(zwin, w, off, first, last):
    width = zwin.shape[1]
    zm = pltpu.roll(zwin, 1, 1)[:, off:off + GATE_CHUNK]
    zp = pltpu.roll(zwin, width - 1, 1)[:, off:off + GATE_CHUNK]
    z = zwin[:, off:off + GATE_CHUNK]
    lane = lax.broadcasted_iota(jnp.int32, z.shape, 1)
    if first:
        zm = jnp.where(lane == 0, 0.0, zm)
    if last:
        zp = jnp.where(lane == GATE_CHUNK - 1, 0.0, zp)
    return zm * w[:, 0:1] + z * w[:, 1:2] + zp * w[:, 2:3]


def _gate_kernel(zx0_ref, zx1_ref, zv_ref, zgb_ref, zgc_ref, zxv_ref,
                 px0_ref, px1_ref, pv_ref, psc_ref, x0_ref, u_ref, yb_ref):
    n_chunks = SEQ // GATE_CHUNK
    groups = GATE_ROWS // CH_GROUP
    slabs = GATE_CHUNK // FFT_N2

    def row_tile(rt, carry):
        rows = pl.ds(pl.multiple_of(rt * GATE_ROWS, GATE_ROWS), GATE_ROWS)
        px0, px1, pv, psc = px0_ref[rows, :], px1_ref[rows, :], pv_ref[rows, :], psc_ref[rows, :]
        for c in range(n_chunks):
            lo = max(c * GATE_CHUNK - LANES, 0)
            hi = min((c + 1) * GATE_CHUNK + LANES, SEQ)
            edge = (c * GATE_CHUNK - lo, c == 0, c == n_chunks - 1)
            cur = slice(c * GATE_CHUNK, (c + 1) * GATE_CHUNK)
            win = lambda ref: ref[rows, lo:hi].astype(F32)
            x0 = _conv3_chunk(win(zx0_ref), px0, *edge) + px0[:, 3:4]
            x1 = _conv3_chunk(win(zx1_ref), px1, *edge) + px1[:, 3:4]
            v = _conv3_chunk(win(zv_ref), pv, *edge) + pv[:, 3:4]
            u = v * x1
            for g in range(groups):
                for a in range(slabs):
                    sl = (slice(g * CH_GROUP, (g + 1) * CH_GROUP), slice(a * FFT_N2, (a + 1) * FFT_N2))
                    x0_ref[rt * groups + g, c * slabs + a] = x0[sl]
                    u_ref[rt * groups + g, c * slabs + a] = u[sl]
            p = win(zgc_ref) * win(zxv_ref)
            yb = zgb_ref[rows, cur].astype(F32) * _conv3_chunk(p, psc, *edge)
            yb_ref[rows, cur] = yb.astype(BF16)
        return carry

    lax.fori_loop(0, GATE_CB // GATE_ROWS, row_tile, 0)


def _gate(zt, hy_par, sc_par):
    nb = D_HY // GATE_CB
    zspec = lambda s: pl.BlockSpec((GATE_CB, SEQ), lambda j, s=s: (s * nb + j, 0))
    pspec = lambda s: pl.BlockSpec((GATE_CB, 4), lambda j, s=s: (s * nb + j, 0))
    out = jax.ShapeDtypeStruct((D_HY, SEQ), BF16)
    ospec = pl.BlockSpec((GATE_CB, SEQ), lambda j: (j, 0))
    qout = jax.ShapeDtypeStruct(Q8_SHAPE, F32)
    qspec = pl.BlockSpec((GATE_CB // CH_GROUP,) + Q8_SHAPE[1:], lambda j: (j, 0, 0, 0))
    return pl.pallas_call(
        _gate_kernel,
        grid=(nb,),
        in_specs=[zspec(s) for s in range(6)] + [pspec(0), pspec(1), pspec(2),
                                                 pl.BlockSpec((GATE_CB, 4), lambda j: (j, 0))],
        out_specs=[qspec, qspec, ospec],
        out_shape=[qout, qout, out],
        compiler_params=_cparams(1),
        name="gate",
    )(zt, zt, zt, zt, zt, zt, hy_par, hy_par, hy_par, sc_par)


FILT_TB = 1024


def _filter_kernel(emb_ref, t_ref, w1_ref, b1_ref, w2_ref, b2_ref, fr_ref, w3_ref, b3_ref,
                   ad_ref, hf_ref, hb_ref, hfs_ref, hbs_ref):
    hi = lax.Precision.HIGHEST
    fr = fr_ref[...]
    a1 = jnp.dot(w1_ref[...], emb_ref[...], precision=hi, preferred_element_type=F32) + b1_ref[...]
    h1 = jnp.sin(fr * a1)
    a2 = jnp.dot(w2_ref[...], h1, precision=hi, preferred_element_type=F32) + b2_ref[...]
    h2 = jnp.sin(fr * a2)
    o = jnp.dot(w3_ref[...], h2.astype(BF16), preferred_element_type=F32) + b3_ref[...]
    window = jnp.exp(-(ad_ref[...] * t_ref[...]))
    hfs_ref[...] = o[:D_HY] * window
    pos = lax.broadcasted_iota(jnp.int32, (D_HY, FILT_TB), 1) + pl.program_id(0) * FILT_TB
    hbs_ref[...] = jnp.where(pos == 0, 0.0, o[D_HY:] * window)

    def to_q(cg, carry):
        rows = pl.ds(pl.multiple_of(cg * CH_GROUP, CH_GROUP), CH_GROUP)
        for n1 in range(FILT_TB // FFT_N2):
            lanes = slice(n1 * FFT_N2, (n1 + 1) * FFT_N2)
            hf_ref[cg, n1] = hfs_ref[rows, lanes]
            hb_ref[cg, n1] = hbs_ref[rows, lanes]
        return carry

    lax.fori_loop(0, D_HY // CH_GROUP, to_q, 0)


def _filters(emb_t, t_row, w1t, b1, w2t, b2, fr, w3t, b3, absdelta):
    full = lambda a: pl.BlockSpec(a.shape, lambda i: (0,) * a.ndim)
    out = jax.ShapeDtypeStruct(Q8_SHAPE, F32)
    qspec = pl.BlockSpec((Q8_SHAPE[0], FILT_TB // FFT_N2, CH_GROUP, FFT_N2), lambda i: (0, i, 0, 0))
    return pl.pallas_call(
        _filter_kernel,
        grid=(SEQ // FILT_TB,),
        in_specs=[pl.BlockSpec((EMB_PAD, FILT_TB), lambda i: (0, i)),
                  pl.BlockSpec((1, FILT_TB), lambda i: (0, i)),
                  full(w1t), full(b1), full(w2t), full(b2), full(fr), full(w3t), full(b3),
                  full(absdelta)],
        out_specs=[qspec, qspec],
        out_shape=[out, out],
        scratch_shapes=[pltpu.VMEM((D_HY, FILT_TB), F32)] * 2,
        compiler_params=_cparams(1),
        name="filters",
    )(emb_t, t_row, w1t, b1, w2t, b2, fr, w3t, b3, absdelta)


HY_CB = 32
HY_UNROLL = 4
GROUP_ROWS = CH_GROUP * FFT_N1_LIVE
N_K1 = FFT_N1 // 2 + 1
RE_ROWS = N_K1 * CH_GROUP
IM_ROWS = (N_K1 - 2) * CH_GROUP


def _dft_tables():
    n1 = np.arange(FFT_N1_LIVE, dtype=np.float64)
    k1 = np.arange(N_K1, dtype=np.float64)
    th = 2.0 * np.pi * np.outer(k1, n1) / FFT_N1
    eye = np.eye(CH_GROUP)
    bd1 = np.concatenate([np.einsum("kn,cd->kcnd", np.cos(th), eye).reshape(RE_ROWS, GROUP_ROWS),
                          np.einsum("kn,cd->kcnd", -np.sin(th)[1:-1], eye).reshape(IM_ROWS, GROUP_ROWS)])
    wgt = np.where((k1 == 0) | (k1 == FFT_N1 // 2), 1.0, 2.0)[:, None] / FFT_N
    bd1inv = np.concatenate(
        [np.einsum("kn,cd->nckd", wgt * np.cos(th), eye).reshape(GROUP_ROWS, RE_ROWS),
         np.einsum("kn,cd->nckd", (-wgt * np.sin(th))[1:-1], eye).reshape(GROUP_ROWS, IM_ROWS)], axis=1)
    n2 = np.arange(FFT_N2, dtype=np.float64)
    ph = 2.0 * np.pi * np.outer(k1, n2) / FFT_N
    twr = np.repeat(np.cos(ph), CH_GROUP, axis=0)
    twi = np.repeat(-np.sin(ph), CH_GROUP, axis=0)
    ps = 2.0 * np.pi * np.outer(n2, n2) / FFT_N2
    cr, ci = np.cos(ps), -np.sin(ps)
    w3 = np.block([[cr, ci], [-ci, cr]])
    w3inv = np.block([[cr, -ci], [ci, cr]])
    f32 = lambda a: jnp.asarray(a, F32)
    return (f32(bd1).astype(BF16), f32(bd1inv).astype(BF16), f32(twr), f32(twi),
            f32(w3).astype(BF16), f32(w3inv).astype(BF16))


def _hyena_kernel(u_ref, hf_ref, hb_ref, x0_ref, bias_ref, bd1_ref, bd1inv_ref, twr_ref, twi_ref,
                  w3_ref, w3inv_ref, w1_ref, w2_ref, ya_ref, w1b_ref, w2b_ref):
    w1b_ref[...] = w1_ref[...].astype(BF16)
    w2b_ref[...] = w2_ref[...].astype(BF16)
    twr, twi = twr_ref[...], twi_ref[...]
    zrow = jnp.zeros((CH_GROUP, FFT_N2), F32)

    def stage1(xq):
        xq = xq.reshape(GROUP_ROWS, FFT_N2).astype(BF16)
        a = jnp.dot(bd1_ref[...], xq, preferred_element_type=F32)
        ar = a[:RE_ROWS]
        ai = jnp.concatenate([zrow, a[RE_ROWS:], zrow], axis=0)
        br = ar * twr - ai * twi
        bi = ar * twi + ai * twr
        return jnp.concatenate([br, bi], axis=1)

    def group(gi, carry):
        u = u_ref[gi]
        b_all = jnp.concatenate([stage1(u), stage1(hf_ref[gi]), stage1(hb_ref[gi])], axis=0)
        s_all = jnp.dot(b_all.astype(BF16), w3_ref[...], preferred_element_type=F32)
        xs, fs, bs = s_all[:RE_ROWS], s_all[RE_ROWS:2 * RE_ROWS], s_all[2 * RE_ROWS:]
        xr, xi = xs[:, :FFT_N2], xs[:, FFT_N2:]
        kr = fs[:, :FFT_N2] + bs[:, :FFT_N2]
        ki = fs[:, FFT_N2:] - bs[:, FFT_N2:]
        ycat = jnp.concatenate([xr * kr - xi * ki, xr * ki + xi * kr], axis=1).astype(BF16)
        ap = jnp.dot(ycat, w3inv_ref[...], preferred_element_type=F32)
        apr, api = ap[:, :FFT_N2], ap[:, FFT_N2:]
        bpr = apr * twr + api * twi
        bpi = api * twr - apr * twi
        b2 = jnp.concatenate([bpr, bpi[CH_GROUP:RE_ROWS - CH_GROUP]], axis=0).astype(BF16)
        y = jnp.dot(bd1inv_ref[...], b2, preferred_element_type=F32)
        y3 = y.reshape(FFT_N1_LIVE, CH_GROUP, FFT_N2)
        bias = bias_ref[pl.ds(pl.multiple_of(gi * CH_GROUP, CH_GROUP), CH_GROUP), :]
        ya_ref[gi] = x0_ref[gi] * (y3 + bias[None] * u)
        return carry

    lax.fori_loop(0, HY_CB // CH_GROUP, group, 0, unroll=HY_UNROLL)


def _hyena(u_q, hf_q, hb_q, x0_q, bias2, tables, mlp_w1, mlp_w2, layer):
    bd1, bd1inv, twr, twi, w3, w3inv = tables
    steps = D_HY // HY_CB
    qspec = pl.BlockSpec((HY_CB // CH_GROUP,) + Q8_SHAPE[1:], lambda j: (j, 0, 0, 0))
    full = lambda a: pl.BlockSpec(a.shape, lambda j: (0,) * a.ndim)
    w1_blk = (D_MODEL // steps, D_FF)
    w2_blk = (D_FF // steps, D_MODEL)
    return pl.pallas_call(
        _hyena_kernel,
        grid=(steps,),
        in_specs=[qspec, qspec, qspec, qspec,
                  pl.BlockSpec((HY_CB, FFT_N2), lambda j: (j, 0)),
                  full(bd1), full(bd1inv), full(twr), full(twi), full(w3), full(w3inv),
                  pl.BlockSpec((None,) + w1_blk, lambda j: (layer, j, 0)),
                  pl.BlockSpec((None,) + w2_blk, lambda j: (layer, j, 0))],
        out_specs=[qspec, pl.BlockSpec(w1_blk, lambda j: (j, 0)), pl.BlockSpec(w2_blk, lambda j: (j, 0))],
        out_shape=[jax.ShapeDtypeStruct(Q8_SHAPE, F32),
                   jax.ShapeDtypeStruct((D_MODEL, D_FF), BF16),
                   jax.ShapeDtypeStruct((D_FF, D_MODEL), BF16)],
        compiler_params=_cparams(1),
        name="hyena_fftconv",
    )(u_q, hf_q, hb_q, x0_q, bias2, bd1, bd1inv, twr, twi, w3, w3inv, mlp_w1, mlp_w2)


OUT_TM = 512


def _out_proj_kernel(ya_ref, yb_ref, wa_ref, wb_ref, x_ref, g_ref, o_ref):
    tn = (((0,), (0,)), ((), ()))
    ya = jnp.concatenate([ya_ref[:, a].reshape(D_HY, FFT_N2) for a in range(OUT_TM // FFT_N2)],
                         axis=1).astype(BF16)
    m = lax.dot_general(ya, wa_ref[...], tn, preferred_element_type=F32)
    m = m + lax.dot_general(yb_ref[...], wb_ref[...], tn, preferred_element_type=F32)
    o_ref[...] = x_ref[...] + _rms_rows(m, g_ref[...])


def _out_proj(ya_q, yb_t, wa, wb, x2d, g):
    return pl.pallas_call(
        _out_proj_kernel,
        grid=(SEQ // OUT_TM,),
        in_specs=[
            pl.BlockSpec((Q8_SHAPE[0], OUT_TM // FFT_N2, CH_GROUP, FFT_N2), lambda i: (0, i, 0, 0)),
            pl.BlockSpec((D_SC, OUT_TM), lambda i: (0, i)),
            pl.BlockSpec((D_HY, D_MODEL), lambda i: (0, 0)),
            pl.BlockSpec((D_SC, D_MODEL), lambda i: (0, 0)),
            pl.BlockSpec((OUT_TM, D_MODEL), lambda i: (i, 0)),
            pl.BlockSpec((1, D_MODEL), lambda i: (0, 0)),
        ],
        out_specs=pl.BlockSpec((OUT_TM, D_MODEL), lambda i: (i, 0)),
        out_shape=jax.ShapeDtypeStruct((SEQ, D_MODEL), F32),
        compiler_params=_cparams(1),
        name="out_proj",
    )(ya_q, yb_t, wa, wb, x2d, g)


MLP_TM = 1024
MLP_TK = 1024
MLP_TA = 512
MLP_TN = 512
MLP_ROWS = 128


def _mlp_kernel(x_ref, gin_ref, gout_ref, w1_ref, w2_ref, *rest):
    if len(rest) == 6:
        nw1_ref, nw2_ref, o_ref, nw1b_ref, nw2b_ref, h_ref = rest
        nw1b_ref[...] = nw1_ref[...].astype(BF16)
        nw2b_ref[...] = nw2_ref[...].astype(BF16)
    else:
        o_ref, h_ref = rest
    k = pl.program_id(1)

    def row_chunks(fn):
        def body(c, carry):
            fn(pl.ds(pl.multiple_of(c * MLP_ROWS, MLP_ROWS), MLP_ROWS))
            return carry
        lax.fori_loop(0, MLP_TM // MLP_ROWS, body, 0)

    @pl.when(k == 0)
    def _():
        def prologue(rows):
            h_ref[rows, :] = _rms_rows(x_ref[rows, :], gin_ref[...]).astype(BF16)
            o_ref[rows, :] = jnp.zeros((MLP_ROWS, D_MODEL), F32)
        row_chunks(prologue)

    for c in range(w1_ref.shape[1] // MLP_TA):
        mid = slice(c * MLP_TA, (c + 1) * MLP_TA)
        a = jnp.dot(h_ref[...], w1_ref[:, mid], preferred_element_type=F32)
        a = jnp.square(jnp.maximum(a, 0.0)).astype(BF16)
        for n in range(D_MODEL // MLP_TN):
            cols = slice(n * MLP_TN, (n + 1) * MLP_TN)
            o_ref[:, cols] += jnp.dot(a, w2_ref[mid, cols], preferred_element_type=F32)

    @pl.when(k == pl.num_programs(1) - 1)
    def _():
        def epilogue(rows):
            o_ref[rows, :] = x_ref[rows, :] + _rms_rows(o_ref[rows, :], gout_ref[...])
        row_chunks(epilogue)


def _mlp(x2d, gin, gout, w1, w2, narrow=None):
    tk = MLP_TK if narrow is None else MLP_TK // 2
    ni, nk = SEQ // MLP_TM, D_FF // tk
    in_specs = [
        pl.BlockSpec((MLP_TM, D_MODEL), lambda i, k: (i, 0)),
        pl.BlockSpec((1, D_MODEL), lambda i, k: (0, 0)),
        pl.BlockSpec((1, D_MODEL), lambda i, k: (0, 0)),
        pl.BlockSpec((D_MODEL, tk), lambda i, k: (0, k)),
        pl.BlockSpec((tk, D_MODEL), lambda i, k: (k, 0)),
    ]
    out_specs = [pl.BlockSpec((MLP_TM, D_MODEL), lambda i, k: (i, 0))]
    out_shape = [jax.ShapeDtypeStruct((SEQ, D_MODEL), F32)]
    args = [x2d, gin, gout, w1, w2]
    if narrow is not None:
        nw1, nw2, layer = narrow
        w1_blk = (D_MODEL // ni, D_FF // nk)
        w2_blk = (D_FF // ni, D_MODEL // nk)
        in_specs += [pl.BlockSpec((None,) + w1_blk, lambda i, k: (layer, i, k)),
                     pl.BlockSpec((None,) + w2_blk, lambda i, k: (layer, i, k))]
        out_specs += [pl.BlockSpec(w1_blk, lambda i, k: (i, k)), pl.BlockSpec(w2_blk, lambda i, k: (i, k))]
        out_shape += [jax.ShapeDtypeStruct((D_MODEL, D_FF), BF16),
                      jax.ShapeDtypeStruct((D_FF, D_MODEL), BF16)]
        args += [nw1, nw2]
    return pl.pallas_call(
        _mlp_kernel,
        grid=(ni, nk),
        in_specs=in_specs,
        out_specs=out_specs,
        out_shape=out_shape,
        scratch_shapes=[pltpu.VMEM((MLP_TM, D_MODEL), BF16)],
        compiler_params=_cparams(2),
        name="mlp",
    )(*args)


POOL_TM = 512
POOL_SUB = 128
HALO = 8


def _pool_bands():
    t = np.arange(POOL_SUB)[:, None]
    j = np.arange(POOL_SUB + 2 * HALO)[None, :]
    return np.stack([(np.abs(j - HALO - t) <= w // 2) for w in POOL_WINDOWS]).astype(np.float32)


def _pool_kernel(xp_ref, xm_ref, xn_ref, gin_ref, gout_ref, band_ref, pw_ref, ps_ref, o_ref,
                 hs_ref, hi_ref, lo_ref, m_ref):
    i = pl.program_id(0)
    gin = gin_ref[...]
    hs_ref[0:HALO, :] = jnp.where(i > 0, _rms_rows(xp_ref[...], gin), 0.0)
    hs_ref[HALO:HALO + POOL_TM, :] = _rms_rows(xm_ref[...], gin)
    hs_ref[HALO + POOL_TM:, :] = jnp.where(i < pl.num_programs(0) - 1, _rms_rows(xn_ref[...], gin), 0.0)
    hs = hs_ref[...]
    hi = hs.astype(BF16)
    hi_ref[...] = hi
    lo_ref[...] = (hs - hi.astype(F32)).astype(BF16)
    row = lax.broadcasted_iota(jnp.int32, (POOL_SUB, D_POOL_G), 0) + i * POOL_TM
    for gi, w in enumerate(POOL_WINDOWS):
        r = w // 2
        cols = slice(gi * D_POOL_G, (gi + 1) * D_POOL_G)
        band = band_ref[gi]
        parts = []
        for sb in range(POOL_TM // POOL_SUB):
            win = slice(sb * POOL_SUB, (sb + 1) * POOL_SUB + 2 * HALO)
            s = (jnp.dot(band, hi_ref[win, cols], preferred_element_type=F32)
                 + jnp.dot(band, lo_ref[win, cols], preferred_element_type=F32))
            t = row + sb * POOL_SUB
            cnt = jnp.minimum(t + r + 1, SEQ) - jnp.maximum(t - r, 0)
            u = hs_ref[sb * POOL_SUB + HALO:(sb + 1) * POOL_SUB + HALO, cols]
            parts.append((s / cnt.astype(F32) - u).astype(BF16))
        d = jnp.concatenate(parts, axis=0)
        mg = jnp.dot(d, pw_ref[gi], preferred_element_type=F32)
        m_ref[:, cols] = mg * ps_ref[:, cols]
    o_ref[...] = xm_ref[...] + _rms_rows(m_ref[...], gout_ref[...])


def _pool(x2d, gin, gout, pw, ps):
    nb8 = POOL_TM // HALO
    last8 = SEQ // HALO - 1
    band = jnp.asarray(_pool_bands(), F32).astype(BF16)
    ext = POOL_TM + 2 * HALO
    return pl.pallas_call(
        _pool_kernel,
        grid=(SEQ // POOL_TM,),
        in_specs=[
            pl.BlockSpec((HALO, D_MODEL), lambda i: (jnp.maximum(i * nb8 - 1, 0), 0)),
            pl.BlockSpec((POOL_TM, D_MODEL), lambda i: (i, 0)),
            pl.BlockSpec((HALO, D_MODEL), lambda i: (jnp.minimum((i + 1) * nb8, last8), 0)),
            pl.BlockSpec((1, D_MODEL), lambda i: (0, 0)),
            pl.BlockSpec((1, D_MODEL), lambda i: (0, 0)),
            pl.BlockSpec(band.shape, lambda i: (0, 0, 0)),
            pl.BlockSpec((len(POOL_WINDOWS), D_POOL_G, D_POOL_G), lambda i: (0, 0, 0)),
            pl.BlockSpec((1, D_MODEL), lambda i: (0, 0)),
        ],
        out_specs=pl.BlockSpec((POOL_TM, D_MODEL), lambda i: (i, 0)),
        out_shape=jax.ShapeDtypeStruct((SEQ, D_MODEL), F32),
        scratch_shapes=[pltpu.VMEM((ext, D_MODEL), F32), pltpu.VMEM((ext, D_MODEL), BF16),
                        pltpu.VMEM((ext, D_MODEL), BF16), pltpu.VMEM((POOL_TM, D_MODEL), F32)],
        compiler_params=_cparams(1),
        name="pool_mixer",
    )(x2d, x2d, x2d, gin, gout, band, pw, ps)


def _filter_constants():
    L = SEQ
    t = np.linspace(0.0, 1.0, L)
    w_pos = 2.0 * np.pi * np.arange(L) / L
    bands = np.linspace(1e-4, FILTER_BANDS - 1, FILTER_BANDS)
    ang = w_pos[:, None] * bands[None, :]
    emb = np.concatenate([t[:, None], np.cos(ang), -np.sin(ang)], axis=-1)
    emb_t = np.zeros((EMB_PAD, L))
    emb_t[:EMB_DIM] = emb.T
    max_decay = math.log(DECAY_TARGET) / FAST_DECAY_PCT
    min_decay = math.log(DECAY_TARGET) / SLOW_DECAY_PCT
    absdelta = np.abs(np.linspace(min_decay, max_decay, D_HY))[:, None]
    return (jnp.asarray(emb_t, F32), jnp.asarray(t[None, :], F32), jnp.asarray(absdelta, F32))


def _mixer_layer0(x2d, g, w_in, hy_short_w, hy_short_b, f_w1, f_b1, f_w2, f_b2, f_w3, f_b3,
                  freq, hy_bias, sc_conv_w, w_out, mlp_w1, mlp_w2, layer):
    wt = w_in.T.astype(BF16)
    zt = _in_proj(x2d, g[0][None, :], wt)

    hy_par = jnp.concatenate([hy_short_w.T, hy_short_b[:, None]], axis=1)
    sc_par = jnp.concatenate([sc_conv_w.T, jnp.zeros((D_SC, 1), F32)], axis=1)
    x0_q, u_q, yb_t = _gate(zt, hy_par, sc_par)

    emb_t, t_row, absdelta = _filter_constants()
    w1t = jnp.zeros((FILTER_HIDDEN, EMB_PAD), F32).at[:, :EMB_DIM].set(f_w1.T)
    col = lambda v: v[:, None].astype(F32)
    hf_q, hb_q = _filters(emb_t, t_row, w1t, col(f_b1), f_w2.T, col(f_b2), col(freq),
                          f_w3.T.astype(BF16), col(f_b3), absdelta)

    bias2 = jnp.broadcast_to(hy_bias[:, None].astype(F32), (D_HY, FFT_N2))
    ya_q, w1_this, w2_this = _hyena(u_q, hf_q, hb_q, x0_q, bias2, _dft_tables(), mlp_w1, mlp_w2, layer)

    wo = w_out.astype(BF16)
    x2d = _out_proj(ya_q, yb_t, wo[:D_HY], wo[D_HY:], x2d, g[1][None, :])
    return x2d, (w1_this, w2_this)


def kernel(x, norm_g, mix_w_in, hy_short_w, hy_short_b, hy_filt_w1, hy_filt_b1, hy_filt_w2,
           hy_filt_b2, hy_filt_w3, hy_filt_b3, hy_freq, hy_bias, sc_conv_w, mix_w_out,
           pool_w, pool_scale, mlp_w1, mlp_w2):
    x2d = x.reshape(SEQ, D_MODEL)
    depth = norm_g.shape[0]
    assert depth % 2 == 0, "an even layer narrows its own MLP weights and, in its MLP, the next layer's"
    for i in range(depth):
        g = norm_g[i]
        j = i // 2
        gin, gout = g[2][None, :], g[3][None, :]
        if i % 2 == 0:
            x2d, w_this = _mixer_layer0(
                x2d, g, mix_w_in[j], hy_short_w[j], hy_short_b[j], hy_filt_w1[j], hy_filt_b1[j],
                hy_filt_w2[j], hy_filt_b2[j], hy_filt_w3[j], hy_filt_b3[j], hy_freq[j], hy_bias[j],
                sc_conv_w[j], mix_w_out[j], mlp_w1, mlp_w2, i)
            x2d, *w_next = _mlp(x2d, gin, gout, *w_this, narrow=(mlp_w1, mlp_w2, i + 1))
        else:
            x2d = _pool(x2d, g[0][None, :], g[1][None, :], pool_w[j].astype(BF16),
                        pool_scale[j][None, :])
            x2d, = _mlp(x2d, gin, gout, *w_next)
    return x2d.reshape(x.shape)
```

```python
import math

import numpy as np
import jax
import jax.numpy as jnp
from jax import lax
from jax.experimental import pallas as pl
from jax.experimental.pallas import tpu as pltpu

F32 = jnp.float32
BF16 = jnp.bfloat16

D_MODEL = 2048
SEQ = 8192
D_HY = D_MODEL // 2
D_SC = D_MODEL // 2
D_IN = 3 * D_HY + 3 * D_SC
FILTER_BANDS = 16
EMB_DIM = 1 + 2 * FILTER_BANDS
EMB_PAD = 40
FILTER_HIDDEN = 64
DECAY_TARGET = 1e-2
FAST_DECAY_PCT = 0.3
SLOW_DECAY_PCT = 1.5
POOL_WINDOWS = (2, 4, 8, 16)
D_POOL_G = D_MODEL // len(POOL_WINDOWS)
D_FF = 4 * D_MODEL
NORM_EPS = 1e-6
LANES = 128

FFT_N = 2 * SEQ
FFT_N2 = 256
FFT_N1 = FFT_N // FFT_N2
FFT_N1_LIVE = SEQ // FFT_N2
CH_GROUP = 8
Q8_SHAPE = (D_MODEL // 2 // CH_GROUP, FFT_N1_LIVE, CH_GROUP, FFT_N2)

VMEM_LIMIT = 56 * 1024 * 1024


def _cparams(n_axes):
    return pltpu.CompilerParams(
        dimension_semantics=("arbitrary",) * n_axes, vmem_limit_bytes=VMEM_LIMIT)


def _rms_rows(x, g):
    r = lax.rsqrt(jnp.mean(x * x, axis=-1, keepdims=True) + NORM_EPS)
    return x * r * g


IN_TM = 1024
IN_TN = 1024


def _in_proj_kernel(x_ref, g_ref, wt_ref, zt_ref, h_ref):
    @pl.when(pl.program_id(1) == 0)
    def _():
        h_ref[...] = _rms_rows(x_ref[...], g_ref[...]).astype(BF16)

    zt_ref[...] = lax.dot_general(
        wt_ref[...], h_ref[...], (((1,), (1,)), ((), ())), preferred_element_type=F32).astype(BF16)


def _in_proj(x2d, g, wt):
    return pl.pallas_call(
        _in_proj_kernel,
        grid=(SEQ // IN_TM, D_IN // IN_TN),
        in_specs=[
            pl.BlockSpec((IN_TM, D_MODEL), lambda i, j: (i, 0)),
            pl.BlockSpec((1, D_MODEL), lambda i, j: (0, 0)),
            pl.BlockSpec((IN_TN, D_MODEL), lambda i, j: (j, 0)),
        ],
        out_specs=pl.BlockSpec((IN_TN, IN_TM), lambda i, j: (j, i)),
        out_shape=jax.ShapeDtypeStruct((D_IN, SEQ), BF16),
        scratch_shapes=[pltpu.VMEM((IN_TM, D_MODEL), BF16)],
        compiler_params=_cparams(2),
        name="in_proj",
    )(x2d, g, wt)


GATE_CB = 32
GATE_ROWS = 16
GATE_CHUNK = 1024


def _conv3_chunk(zwin, w, off, first, last):
    width = zwin.shape[1]
    zm = pltpu.roll(zwin, 1, 1)[:, off:off + GATE_CHUNK]
    zp = pltpu.roll(zwin, width - 1, 1)[:, off:off + GATE_CHUNK]
    z = zwin[:, off:off + GATE_CHUNK]
    lane = lax.broadcasted_iota(jnp.int32, z.shape, 1)
    if first:
        zm = jnp.where(lane == 0, 0.0, zm)
    if last:
        zp = jnp.where(lane == GATE_CHUNK - 1, 0.0, zp)
    return zm * w[:, 0:1] + z * w[:, 1:2] + zp * w[:, 2:3]


def _gate_kernel(zx0_ref, zx1_ref, zv_ref, zgb_ref, zgc_ref, zxv_ref,
                 px0_ref, px1_ref, pv_ref, psc_ref, x0_ref, u_ref, yb_ref):
    n_chunks = SEQ // GATE_CHUNK
    groups = GATE_ROWS // CH_GROUP
    slabs = GATE_CHUNK // FFT_N2

    def row_tile(rt, carry):
        rows = pl.ds(pl.multiple_of(rt * GATE_ROWS, GATE_ROWS), GATE_ROWS)
        px0, px1, pv, psc = px0_ref[rows, :], px1_ref[rows, :], pv_ref[rows, :], psc_ref[rows, :]
        for c in range(n_chunks):
            lo = max(c * GATE_CHUNK - LANES, 0)
            hi = min((c + 1) * GATE_CHUNK + LANES, SEQ)
            edge = (c * GATE_CHUNK - lo, c == 0, c == n_chunks - 1)
            cur = slice(c * GATE_CHUNK, (c + 1) * GATE_CHUNK)
            win = lambda ref: ref[rows, lo:hi].astype(F32)
            x0 = _conv3_chunk(win(zx0_ref), px0, *edge) + px0[:, 3:4]
            x1 = _conv3_chunk(win(zx1_ref), px1, *edge) + px1[:, 3:4]
            v = _conv3_chunk(win(zv_ref), pv, *edge) + pv[:, 3:4]
            u = v * x1
            for g in range(groups):
                for a in range(slabs):
                    sl = (slice(g * CH_GROUP, (g + 1) * CH_GROUP), slice(a * FFT_N2, (a + 1) * FFT_N2))
                    x0_ref[rt * groups + g, c * slabs + a] = x0[sl]
                    u_ref[rt * groups + g, c * slabs + a] = u[sl]
            p = win(zgc_ref) * win(zxv_ref)
            yb = zgb_ref[rows, cur].astype(F32) * _conv3_chunk(p, psc, *edge)
            yb_ref[rows, cur] = yb.astype(BF16)
        return carry

    lax.fori_loop(0, GATE_CB // GATE_ROWS, row_tile, 0)


def _gate(zt, hy_par, sc_par):
    nb = D_HY // GATE_CB
    zspec = lambda s: pl.BlockSpec((GATE_CB, SEQ), lambda j, s=s: (s * nb + j, 0))
    pspec = lambda s: pl.BlockSpec((GATE_CB, 4), lambda j, s=s: (s * nb + j, 0))
    out = jax.ShapeDtypeStruct((D_HY, SEQ), BF16)
    ospec = pl.BlockSpec((GATE_CB, SEQ), lambda j: (j, 0))
    qout = jax.ShapeDtypeStruct(Q8_SHAPE, F32)
    qspec = pl.BlockSpec((GATE_CB // CH_GROUP,) + Q8_SHAPE[1:], lambda j: (j, 0, 0, 0))
    return pl.pallas_call(
        _gate_kernel,
        grid=(nb,),
        in_specs=[zspec(s) for s in range(6)] + [pspec(0), pspec(1), pspec(2),
                                                 pl.BlockSpec((GATE_CB, 4), lambda j: (j, 0))],
        out_specs=[qspec, qspec, ospec],
        out_shape=[qout, qout, out],
        compiler_params=_cparams(1),
        name="gate",
    )(zt, zt, zt, zt, zt, zt, hy_par, hy_par, hy_par, sc_par)


FILT_TB = 1024


def _filter_kernel(emb_ref, t_ref, w1_ref, b1_ref, w2_ref, b2_ref, fr_ref, w3_ref, b3_ref,
                   ad_ref, win_ref, hf_ref, hb_ref, wt_ref, hfs_ref, hbs_ref):
    wt_ref[...] = win_ref[...].T.astype(BF16)
    hi = lax.Precision.HIGHEST
    fr = fr_ref[...]
    a1 = jnp.dot(w1_ref[...], emb_ref[...], precision=hi, preferred_element_type=F32) + b1_ref[...]
    h1 = jnp.sin(fr * a1)
    a2 = jnp.dot(w2_ref[...], h1, precision=hi, preferred_element_type=F32) + b2_ref[...]
    h2 = jnp.sin(fr * a2)
    o = jnp.dot(w3_ref[...], h2.astype(BF16), preferred_element_type=F32) + b3_ref[...]
    window = jnp.exp(-(ad_ref[...] * t_ref[...]))
    hfs_ref[...] = o[:D_HY] * window
    pos = lax.broadcasted_iota(jnp.int32, (D_HY, FILT_TB), 1) + pl.program_id(0) * FILT_TB
    hbs_ref[...] = jnp.where(pos == 0, 0.0, o[D_HY:] * window)

    def to_q(cg, carry):
        rows = pl.ds(pl.multiple_of(cg * CH_GROUP, CH_GROUP), CH_GROUP)
        for n1 in range(FILT_TB // FFT_N2):
            lanes = slice(n1 * FFT_N2, (n1 + 1) * FFT_N2)
            hf_ref[cg, n1] = hfs_ref[rows, lanes]
            hb_ref[cg, n1] = hbs_ref[rows, lanes]
        return carry

    lax.fori_loop(0, D_HY // CH_GROUP, to_q, 0)


def _filters(emb_t, t_row, w1t, b1, w2t, b2, fr, w3t, b3, absdelta, w_in):
    full = lambda a: pl.BlockSpec(a.shape, lambda i: (0,) * a.ndim)
    out = jax.ShapeDtypeStruct(Q8_SHAPE, F32)
    qspec = pl.BlockSpec((Q8_SHAPE[0], FILT_TB // FFT_N2, CH_GROUP, FFT_N2), lambda i: (0, i, 0, 0))
    steps = SEQ // FILT_TB
    return pl.pallas_call(
        _filter_kernel,
        grid=(steps,),
        in_specs=[pl.BlockSpec((EMB_PAD, FILT_TB), lambda i: (0, i)),
                  pl.BlockSpec((1, FILT_TB), lambda i: (0, i)),
                  full(w1t), full(b1), full(w2t), full(b2), full(fr), full(w3t), full(b3),
                  full(absdelta),
                  pl.BlockSpec((D_MODEL, D_IN // steps), lambda i: (0, i))],
        out_specs=[qspec, qspec, pl.BlockSpec((D_IN // steps, D_MODEL), lambda i: (i, 0))],
        out_shape=[out, out, jax.ShapeDtypeStruct((D_IN, D_MODEL), BF16)],
        scratch_shapes=[pltpu.VMEM((D_HY, FILT_TB), F32)] * 2,
        compiler_params=_cparams(1),
        name="filters",
    )(emb_t, t_row, w1t, b1, w2t, b2, fr, w3t, b3, absdelta, w_in)


HY_CB = 32
HY_UNROLL = 4
GROUP_ROWS = CH_GROUP * FFT_N1_LIVE
N_K1 = FFT_N1 // 2 + 1
RE_ROWS = N_K1 * CH_GROUP
IM_ROWS = (N_K1 - 2) * CH_GROUP


def _dft_tables():
    n1 = np.arange(FFT_N1_LIVE, dtype=np.float64)
    k1 = np.arange(N_K1, dtype=np.float64)
    th = 2.0 * np.pi * np.outer(k1, n1) / FFT_N1
    eye = np.eye(CH_GROUP)
    bd1 = np.concatenate([np.einsum("kn,cd->kcnd", np.cos(th), eye).reshape(RE_ROWS, GROUP_ROWS),
                          np.einsum("kn,cd->kcnd", -np.sin(th)[1:-1], eye).reshape(IM_ROWS, GROUP_ROWS)])
    wgt = np.where((k1 == 0) | (k1 == FFT_N1 // 2), 1.0, 2.0)[:, None] / FFT_N
    bd1inv = np.concatenate(
        [np.einsum("kn,cd->nckd", wgt * np.cos(th), eye).reshape(GROUP_ROWS, RE_ROWS),
         np.einsum("kn,cd->nckd", (-wgt * np.sin(th))[1:-1], eye).reshape(GROUP_ROWS, IM_ROWS)], axis=1)
    n2 = np.arange(FFT_N2, dtype=np.float64)
    ph = 2.0 * np.pi * np.outer(k1, n2) / FFT_N
    twr = np.repeat(np.cos(ph), CH_GROUP, axis=0)
    twi = np.repeat(-np.sin(ph), CH_GROUP, axis=0)
    ps = 2.0 * np.pi * np.outer(n2, n2) / FFT_N2
    cr, ci = np.cos(ps), -np.sin(ps)
    w3 = np.block([[cr, ci], [-ci, cr]])
    w3inv = np.block([[cr, -ci], [ci, cr]])
    f32 = lambda a: jnp.asarray(a, F32)
    return (f32(bd1).astype(BF16), f32(bd1inv).astype(BF16), f32(twr), f32(twi),
            f32(w3).astype(BF16), f32(w3inv).astype(BF16))


def _hyena_kernel(u_ref, hf_ref, hb_ref, x0_ref, bias_ref, bd1_ref, bd1inv_ref, twr_ref, twi_ref,
                  w3_ref, w3inv_ref, w1_ref, w2_ref, ya_ref, w1b_ref, w2b_ref):
    w1b_ref[...] = w1_ref[...].astype(BF16)
    w2b_ref[...] = w2_ref[...].astype(BF16)
    twr, twi = twr_ref[...], twi_ref[...]
    zrow = jnp.zeros((CH_GROUP, FFT_N2), F32)

    def stage1(xq):
        xq = xq.reshape(GROUP_ROWS, FFT_N2).astype(BF16)
        a = jnp.dot(bd1_ref[...], xq, preferred_element_type=F32)
        ar = a[:RE_ROWS]
        ai = jnp.concatenate([zrow, a[RE_ROWS:], zrow], axis=0)
        br = ar * twr - ai * twi
        bi = ar * twi + ai * twr
        return jnp.concatenate([br, bi], axis=1)

    def group(gi, carry):
        u = u_ref[gi]
        b_all = jnp.concatenate([stage1(u), stage1(hf_ref[gi]), stage1(hb_ref[gi])], axis=0)
        s_all = jnp.dot(b_all.astype(BF16), w3_ref[...], preferred_element_type=F32)
        xs, fs, bs = s_all[:RE_ROWS], s_all[RE_ROWS:2 * RE_ROWS], s_all[2 * RE_ROWS:]
        xr, xi = xs[:, :FFT_N2], xs[:, FFT_N2:]
        kr = fs[:, :FFT_N2] + bs[:, :FFT_N2]
        ki = fs[:, FFT_N2:] - bs[:, FFT_N2:]
        ycat = jnp.concatenate([xr * kr - xi * ki, xr * ki + xi * kr], axis=1).astype(BF16)
        ap = jnp.dot(ycat, w3inv_ref[...], preferred_element_type=F32)
        apr, api = ap[:, :FFT_N2], ap[:, FFT_N2:]
        bpr = apr * twr + api * twi
        bpi = api * twr - apr * twi
        b2 = jnp.concatenate([bpr, bpi[CH_GROUP:RE_ROWS - CH_GROUP]], axis=0).astype(BF16)
        y = jnp.dot(bd1inv_ref[...], b2, preferred_element_type=F32)
        y3 = y.reshape(FFT_N1_LIVE, CH_GROUP, FFT_N2)
        bias = bias_ref[pl.ds(pl.multiple_of(gi * CH_GROUP, CH_GROUP), CH_GROUP), :]
        ya_ref[gi] = x0_ref[gi] * (y3 + bias[None] * u)
        return carry

    lax.fori_loop(0, HY_CB // CH_GROUP, group, 0, unroll=HY_UNROLL)


def _hyena(u_q, hf_q, hb_q, x0_q, bias2, tables, mlp_w1, mlp_w2, layer):
    bd1, bd1inv, twr, twi, w3, w3inv = tables
    steps = D_HY // HY_CB
    qspec = pl.BlockSpec((HY_CB // CH_GROUP,) + Q8_SHAPE[1:], lambda j: (j, 0, 0, 0))
    full = lambda a: pl.BlockSpec(a.shape, lambda j: (0,) * a.ndim)
    w1_blk = (D_MODEL // steps, D_FF)
    w2_blk = (D_FF // steps, D_MODEL)
    return pl.pallas_call(
        _hyena_kernel,
        grid=(steps,),
        in_specs=[qspec, qspec, qspec, qspec,
                  pl.BlockSpec((HY_CB, FFT_N2), lambda j: (j, 0)),
                  full(bd1), full(bd1inv), full(twr), full(twi), full(w3), full(w3inv),
                  pl.BlockSpec((None,) + w1_blk, lambda j: (layer, j, 0)),
                  pl.BlockSpec((None,) + w2_blk, lambda j: (layer, j, 0))],
        out_specs=[qspec, pl.BlockSpec(w1_blk, lambda j: (j, 0)), pl.BlockSpec(w2_blk, lambda j: (j, 0))],
        out_shape=[jax.ShapeDtypeStruct(Q8_SHAPE, F32),
                   jax.ShapeDtypeStruct((D_MODEL, D_FF), BF16),
                   jax.ShapeDtypeStruct((D_FF, D_MODEL), BF16)],
        compiler_params=_cparams(1),
        name="hyena_fftconv",
    )(u_q, hf_q, hb_q, x0_q, bias2, bd1, bd1inv, twr, twi, w3, w3inv, mlp_w1, mlp_w2)


OUT_TM = 512


def _out_proj_kernel(ya_ref, yb_ref, wa_ref, wb_ref, x_ref, g_ref, o_ref):
    tn = (((0,), (0,)), ((), ()))
    ya = jnp.concatenate([ya_ref[:, a].reshape(D_HY, FFT_N2) for a in range(OUT_TM // FFT_N2)],
                         axis=1).astype(BF16)
    m = lax.dot_general(ya, wa_ref[...], tn, preferred_element_type=F32)
    m = m + lax.dot_general(yb_ref[...], wb_ref[...], tn, preferred_element_type=F32)
    o_ref[...] = x_ref[...] + _rms_rows(m, g_ref[...])


def _out_proj(ya_q, yb_t, wa, wb, x2d, g):
    return pl.pallas_call(
        _out_proj_kernel,
        grid=(SEQ // OUT_TM,),
        in_specs=[
            pl.BlockSpec((Q8_SHAPE[0], OUT_TM // FFT_N2, CH_GROUP, FFT_N2), lambda i: (0, i, 0, 0)),
            pl.BlockSpec((D_SC, OUT_TM), lambda i: (0, i)),
            pl.BlockSpec((D_HY, D_MODEL), lambda i: (0, 0)),
            pl.BlockSpec((D_SC, D_MODEL), lambda i: (0, 0)),
            pl.BlockSpec((OUT_TM, D_MODEL), lambda i: (i, 0)),
            pl.BlockSpec((1, D_MODEL), lambda i: (0, 0)),
        ],
        out_specs=pl.BlockSpec((OUT_TM, D_MODEL), lambda i: (i, 0)),
        out_shape=jax.ShapeDtypeStruct((SEQ, D_MODEL), F32),
        compiler_params=_cparams(1),
        name="out_proj",
    )(ya_q, yb_t, wa, wb, x2d, g)


MLP_TM = 1024
MLP_TK = 1024
MLP_TA = 512
MLP_TN = 512
MLP_ROWS = 128


def _mlp_kernel(x_ref, gin_ref, gout_ref, w1_ref, w2_ref, *rest):
    if len(rest) == 6:
        nw1_ref, nw2_ref, o_ref, nw1b_ref, nw2b_ref, h_ref = rest
        nw1b_ref[...] = nw1_ref[...].astype(BF16)
        nw2b_ref[...] = nw2_ref[...].astype(BF16)
    else:
        o_ref, h_ref = rest
    k = pl.program_id(1)

    def row_chunks(fn):
        def body(c, carry):
            fn(pl.ds(pl.multiple_of(c * MLP_ROWS, MLP_ROWS), MLP_ROWS))
            return carry
        lax.fori_loop(0, MLP_TM // MLP_ROWS, body, 0)

    @pl.when(k == 0)
    def _():
        def prologue(rows):
            h_ref[rows, :] = _rms_rows(x_ref[rows, :], gin_ref[...]).astype(BF16)
            o_ref[rows, :] = jnp.zeros((MLP_ROWS, D_MODEL), F32)
        row_chunks(prologue)

    for c in range(w1_ref.shape[1] // MLP_TA):
        mid = slice(c * MLP_TA, (c + 1) * MLP_TA)
        a = jnp.dot(h_ref[...], w1_ref[:, mid], preferred_element_type=F32)
        a = jnp.square(jnp.maximum(a, 0.0)).astype(BF16)
        for n in range(D_MODEL // MLP_TN):
            cols = slice(n * MLP_TN, (n + 1) * MLP_TN)
            o_ref[:, cols] += jnp.dot(a, w2_ref[mid, cols], preferred_element_type=F32)

    @pl.when(k == pl.num_programs(1) - 1)
    def _():
        def epilogue(rows):
            o_ref[rows, :] = x_ref[rows, :] + _rms_rows(o_ref[rows, :], gout_ref[...])
        row_chunks(epilogue)


def _mlp(x2d, gin, gout, w1, w2, narrow=None):
    tk = MLP_TK if narrow is None else MLP_TK // 2
    ni, nk = SEQ // MLP_TM, D_FF // tk
    in_specs = [
        pl.BlockSpec((MLP_TM, D_MODEL), lambda i, k: (i, 0)),
        pl.BlockSpec((1, D_MODEL), lambda i, k: (0, 0)),
        pl.BlockSpec((1, D_MODEL), lambda i, k: (0, 0)),
        pl.BlockSpec((D_MODEL, tk), lambda i, k: (0, k)),
        pl.BlockSpec((tk, D_MODEL), lambda i, k: (k, 0)),
    ]
    out_specs = [pl.BlockSpec((MLP_TM, D_MODEL), lambda i, k: (i, 0))]
    out_shape = [jax.ShapeDtypeStruct((SEQ, D_MODEL), F32)]
    args = [x2d, gin, gout, w1, w2]
    if narrow is not None:
        nw1, nw2, layer = narrow
        w1_blk = (D_MODEL // ni, D_FF // nk)
        w2_blk = (D_FF // ni, D_MODEL // nk)
        in_specs += [pl.BlockSpec((None,) + w1_blk, lambda i, k: (layer, i, k)),
                     pl.BlockSpec((None,) + w2_blk, lambda i, k: (layer, i, k))]
        out_specs += [pl.BlockSpec(w1_blk, lambda i, k: (i, k)), pl.BlockSpec(w2_blk, lambda i, k: (i, k))]
        out_shape += [jax.ShapeDtypeStruct((D_MODEL, D_FF), BF16),
                      jax.ShapeDtypeStruct((D_FF, D_MODEL), BF16)]
        args += [nw1, nw2]
    return pl.pallas_call(
        _mlp_kernel,
        grid=(ni, nk),
        in_specs=in_specs,
        out_specs=out_specs,
        out_shape=out_shape,
        scratch_shapes=[pltpu.VMEM((MLP_TM, D_MODEL), BF16)],
        compiler_params=_cparams(2),
        name="mlp",
    )(*args)


POOL_TM = 512
POOL_SUB = 128
HALO = 8


def _pool_bands():
    t = np.arange(POOL_SUB)[:, None]
    j = np.arange(POOL_SUB + 2 * HALO)[None, :]
    return np.stack([(np.abs(j - HALO - t) <= w // 2) for w in POOL_WINDOWS]).astype(np.float32)


def _pool_kernel(xp_ref, xm_ref, xn_ref, gin_ref, gout_ref, band_ref, pw_ref, ps_ref, o_ref,
                 hs_ref, hi_ref, lo_ref, m_ref):
    i = pl.program_id(0)
    gin = gin_ref[...]
    hs_ref[0:HALO, :] = jnp.where(i > 0, _rms_rows(xp_ref[...], gin), 0.0)
    hs_ref[HALO:HALO + POOL_TM, :] = _rms_rows(xm_ref[...], gin)
    hs_ref[HALO + POOL_TM:, :] = jnp.where(i < pl.num_programs(0) - 1, _rms_rows(xn_ref[...], gin), 0.0)
    hs = hs_ref[...]
    hi = hs.astype(BF16)
    hi_ref[...] = hi
    lo_ref[...] = (hs - hi.astype(F32)).astype(BF16)
    row = lax.broadcasted_iota(jnp.int32, (POOL_SUB, D_POOL_G), 0) + i * POOL_TM
    for gi, w in enumerate(POOL_WINDOWS):
        r = w // 2
        cols = slice(gi * D_POOL_G, (gi + 1) * D_POOL_G)
        band = band_ref[gi]
        parts = []
        for sb in range(POOL_TM // POOL_SUB):
            win = slice(sb * POOL_SUB, (sb + 1) * POOL_SUB + 2 * HALO)
            s = (jnp.dot(band, hi_ref[win, cols], preferred_element_type=F32)
                 + jnp.dot(band, lo_ref[win, cols], preferred_element_type=F32))
            t = row + sb * POOL_SUB
            cnt = jnp.minimum(t + r + 1, SEQ) - jnp.maximum(t - r, 0)
            u = hs_ref[sb * POOL_SUB + HALO:(sb + 1) * POOL_SUB + HALO, cols]
            parts.append((s / cnt.astype(F32) - u).astype(BF16))
        d = jnp.concatenate(parts, axis=0)
        mg = jnp.dot(d, pw_ref[gi], preferred_element_type=F32)
        m_ref[:, cols] = mg * ps_ref[:, cols]
    o_ref[...] = xm_ref[...] + _rms_rows(m_ref[...], gout_ref[...])


def _pool(x2d, gin, gout, pw, ps):
    nb8 = POOL_TM // HALO
    last8 = SEQ // HALO - 1
    band = jnp.asarray(_pool_bands(), F32).astype(BF16)
    ext = POOL_TM + 2 * HALO
    return pl.pallas_call(
        _pool_kernel,
        grid=(SEQ // POOL_TM,),
        in_specs=[
            pl.BlockSpec((HALO, D_MODEL), lambda i: (jnp.maximum(i * nb8 - 1, 0), 0)),
            pl.BlockSpec((POOL_TM, D_MODEL), lambda i: (i, 0)),
            pl.BlockSpec((HALO, D_MODEL), lambda i: (jnp.minimum((i + 1) * nb8, last8), 0)),
            pl.BlockSpec((1, D_MODEL), lambda i: (0, 0)),
            pl.BlockSpec((1, D_MODEL), lambda i: (0, 0)),
            pl.BlockSpec(band.shape, lambda i: (0, 0, 0)),
            pl.BlockSpec((len(POOL_WINDOWS), D_POOL_G, D_POOL_G), lambda i: (0, 0, 0)),
            pl.BlockSpec((1, D_MODEL), lambda i: (0, 0)),
        ],
        out_specs=pl.BlockSpec((POOL_TM, D_MODEL), lambda i: (i, 0)),
        out_shape=jax.ShapeDtypeStruct((SEQ, D_MODEL), F32),
        scratch_shapes=[pltpu.VMEM((ext, D_MODEL), F32), pltpu.VMEM((ext, D_MODEL), BF16),
                        pltpu.VMEM((ext, D_MODEL), BF16), pltpu.VMEM((POOL_TM, D_MODEL), F32)],
        compiler_params=_cparams(1),
        name="pool_mixer",
    )(x2d, x2d, x2d, gin, gout, band, pw, ps)


def _filter_constants():
    L = SEQ
    t = np.linspace(0.0, 1.0, L)
    w_pos = 2.0 * np.pi * np.arange(L) / L
    bands = np.linspace(1e-4, FILTER_BANDS - 1, FILTER_BANDS)
    ang = w_pos[:, None] * bands[None, :]
    emb = np.concatenate([t[:, None], np.cos(ang), -np.sin(ang)], axis=-1)
    emb_t = np.zeros((EMB_PAD, L))
    emb_t[:EMB_DIM] = emb.T
    max_decay = math.log(DECAY_TARGET) / FAST_DECAY_PCT
    min_decay = math.log(DECAY_TARGET) / SLOW_DECAY_PCT
    absdelta = np.abs(np.linspace(min_decay, max_decay, D_HY))[:, None]
    return (jnp.asarray(emb_t, F32), jnp.asarray(t[None, :], F32), jnp.asarray(absdelta, F32))


def _mixer_layer0(x2d, g, w_in, hy_short_w, hy_short_b, f_w1, f_b1, f_w2, f_b2, f_w3, f_b3,
                  freq, hy_bias, sc_conv_w, w_out, mlp_w1, mlp_w2, layer):
    emb_t, t_row, absdelta = _filter_constants()
    w1t = jnp.zeros((FILTER_HIDDEN, EMB_PAD), F32).at[:, :EMB_DIM].set(f_w1.T)
    col = lambda v: v[:, None].astype(F32)
    hf_q, hb_q, wt = _filters(emb_t, t_row, w1t, col(f_b1), f_w2.T, col(f_b2), col(freq),
                              f_w3.T.astype(BF16), col(f_b3), absdelta, w_in)

    zt = _in_proj(x2d, g[0][None, :], wt)
    hy_par = jnp.concatenate([hy_short_w.T, hy_short_b[:, None]], axis=1)
    sc_par = jnp.concatenate([sc_conv_w.T, jnp.zeros((D_SC, 1), F32)], axis=1)
    x0_q, u_q, yb_t = _gate(zt, hy_par, sc_par)

    bias2 = jnp.broadcast_to(hy_bias[:, None].astype(F32), (D_HY, FFT_N2))
    ya_q, w1_this, w2_this = _hyena(u_q, hf_q, hb_q, x0_q, bias2, _dft_tables(), mlp_w1, mlp_w2, layer)

    wo = w_out.astype(BF16)
    x2d = _out_proj(ya_q, yb_t, wo[:D_HY], wo[D_HY:], x2d, g[1][None, :])
    return x2d, (w1_this, w2_this)


def kernel(x, norm_g, mix_w_in, hy_short_w, hy_short_b, hy_filt_w1, hy_filt_b1, hy_filt_w2,
           hy_filt_b2, hy_filt_w3, hy_filt_b3, hy_freq, hy_bias, sc_conv_w, mix_w_out,
           pool_w, pool_scale, mlp_w1, mlp_w2):
    x2d = x.reshape(SEQ, D_MODEL)
    depth = norm_g.shape[0]
    assert depth % 2 == 0, "an even layer narrows its own MLP weights and, in its MLP, the next layer's"
    for i in range(depth):
        g = norm_g[i]
        j = i // 2
        gin, gout = g[2][None, :], g[3][None, :]
        if i % 2 == 0:
            x2d, w_this = _mixer_layer0(
                x2d, g, mix_w_in[j], hy_short_w[j], hy_short_b[j], hy_filt_w1[j], hy_filt_b1[j],
                hy_filt_w2[j], hy_filt_b2[j], hy_filt_w3[j], hy_filt_b3[j], hy_freq[j], hy_bias[j],
                sc_conv_w[j], mix_w_out[j], mlp_w1, mlp_w2, i)
            x2d, *w_next = _mlp(x2d, gin, gout, *w_this, narrow=(mlp_w1, mlp_w2, i + 1))
        else:
            x2d = _pool(x2d, g[0][None, :], g[1][None, :], pool_w[j].astype(BF16),
                        pool_scale[j][None, :])
            x2d, = _mlp(x2d, gin, gout, *w_next)
    return x2d.reshape(x.shape)
```

```python
import math

import numpy as np
import jax
import jax.numpy as jnp
from jax import lax
from jax.experimental import pallas as pl
from jax.experimental.pallas import tpu as pltpu

F32 = jnp.float32
BF16 = jnp.bfloat16

D_MODEL = 2048
SEQ = 8192
D_HY = D_MODEL // 2
D_SC = D_MODEL // 2
D_IN = 3 * D_HY + 3 * D_SC
FILTER_BANDS = 16
EMB_DIM = 1 + 2 * FILTER_BANDS
EMB_PAD = 40
FILTER_HIDDEN = 64
DECAY_TARGET = 1e-2
FAST_DECAY_PCT = 0.3
SLOW_DECAY_PCT = 1.5
POOL_WINDOWS = (2, 4, 8, 16)
D_POOL_G = D_MODEL // len(POOL_WINDOWS)
D_FF = 4 * D_MODEL
NORM_EPS = 1e-6
LANES = 128

FFT_N = 2 * SEQ
FFT_N2 = 256
FFT_N1 = FFT_N // FFT_N2
FFT_N1_LIVE = SEQ // FFT_N2
CH_GROUP = 8
Q8_SHAPE = (D_MODEL // 2 // CH_GROUP, FFT_N1_LIVE, CH_GROUP, FFT_N2)

VMEM_LIMIT = 56 * 1024 * 1024


def _cparams(n_axes):
    return pltpu.CompilerParams(
        dimension_semantics=("arbitrary",) * n_axes, vmem_limit_bytes=VMEM_LIMIT)


def _rms_rows(x, g):
    r = lax.rsqrt(jnp.mean(x * x, axis=-1, keepdims=True) + NORM_EPS)
    return x * r * g


IN_TM = 1024
IN_TN = 2048


def _in_proj_kernel(x_ref, g_ref, wt_ref, zt_ref, h_ref):
    @pl.when(pl.program_id(1) == 0)
    def _():
        h_ref[...] = _rms_rows(x_ref[...], g_ref[...]).astype(BF16)

    zt_ref[...] = lax.dot_general(
        wt_ref[...], h_ref[...], (((1,), (1,)), ((), ())), preferred_element_type=F32).astype(BF16)


def _in_proj(x2d, g, wt):
    return pl.pallas_call(
        _in_proj_kernel,
        grid=(SEQ // IN_TM, D_IN // IN_TN),
        in_specs=[
            pl.BlockSpec((IN_TM, D_MODEL), lambda i, j: (i, 0)),
            pl.BlockSpec((1, D_MODEL), lambda i, j: (0, 0)),
            pl.BlockSpec((IN_TN, D_MODEL), lambda i, j: (j, 0)),
        ],
        out_specs=pl.BlockSpec((IN_TN, IN_TM), lambda i, j: (j, i)),
        out_shape=jax.ShapeDtypeStruct((D_IN, SEQ), BF16),
        scratch_shapes=[pltpu.VMEM((IN_TM, D_MODEL), BF16)],
        compiler_params=_cparams(2),
        name="in_proj",
    )(x2d, g, wt)


GATE_CB = 32
GATE_ROWS = 16
GATE_CHUNK = 1024


def _conv3_chunk(zwin, w, off, first, last):
    width = zwin.shape[1]
    zm = pltpu.roll(zwin, 1, 1)[:, off:off + GATE_CHUNK]
    zp = pltpu.roll(zwin, width - 1, 1)[:, off:off + GATE_CHUNK]
    z = zwin[:, off:off + GATE_CHUNK]
    lane = lax.broadcasted_iota(jnp.int32, z.shape, 1)
    if first:
        zm = jnp.where(lane == 0, 0.0, zm)
    if last:
        zp = jnp.where(lane == GATE_CHUNK - 1, 0.0, zp)
    return zm * w[:, 0:1] + z * w[:, 1:2] + zp * w[:, 2:3]


def _gate_kernel(zx0_ref, zx1_ref, zv_ref, zgb_ref, zgc_ref, zxv_ref,
                 px0_ref, px1_ref, pv_ref, psc_ref, x0_ref, u_ref, yb_ref):
    n_chunks = SEQ // GATE_CHUNK
    groups = GATE_ROWS // CH_GROUP
    slabs = GATE_CHUNK // FFT_N2

    def row_tile(rt, carry):
        rows = pl.ds(pl.multiple_of(rt * GATE_ROWS, GATE_ROWS), GATE_ROWS)
        px0, px1, pv, psc = px0_ref[rows, :], px1_ref[rows, :], pv_ref[rows, :], psc_ref[rows, :]
        for c in range(n_chunks):
            lo = max(c * GATE_CHUNK - LANES, 0)
            hi = min((c + 1) * GATE_CHUNK + LANES, SEQ)
            edge = (c * GATE_CHUNK - lo, c == 0, c == n_chunks - 1)
            cur = slice(c * GATE_CHUNK, (c + 1) * GATE_CHUNK)
            win = lambda ref: ref[rows, lo:hi].astype(F32)
            x0 = _conv3_chunk(win(zx0_ref), px0, *edge) + px0[:, 3:4]
            x1 = _conv3_chunk(win(zx1_ref), px1, *edge) + px1[:, 3:4]
            v = _conv3_chunk(win(zv_ref), pv, *edge) + pv[:, 3:4]
            u = v * x1
            for g in range(groups):
                for a in range(slabs):
                    sl = (slice(g * CH_GROUP, (g + 1) * CH_GROUP), slice(a * FFT_N2, (a + 1) * FFT_N2))
                    x0_ref[rt * groups + g, c * slabs + a] = x0[sl]
                    u_ref[rt * groups + g, c * slabs + a] = u[sl]
            p = win(zgc_ref) * win(zxv_ref)
            yb = zgb_ref[rows, cur].astype(F32) * _conv3_chunk(p, psc, *edge)
            yb_ref[rows, cur] = yb.astype(BF16)
        return carry

    lax.fori_loop(0, GATE_CB // GATE_ROWS, row_tile, 0)


def _gate(zt, hy_par, sc_par):
    nb = D_HY // GATE_CB
    zspec = lambda s: pl.BlockSpec((GATE_CB, SEQ), lambda j, s=s: (s * nb + j, 0))
    pspec = lambda s: pl.BlockSpec((GATE_CB, 4), lambda j, s=s: (s * nb + j, 0))
    out = jax.ShapeDtypeStruct((D_HY, SEQ), BF16)
    ospec = pl.BlockSpec((GATE_CB, SEQ), lambda j: (j, 0))
    qout = jax.ShapeDtypeStruct(Q8_SHAPE, F32)
    qspec = pl.BlockSpec((GATE_CB // CH_GROUP,) + Q8_SHAPE[1:], lambda j: (j, 0, 0, 0))
    return pl.pallas_call(
        _gate_kernel,
        grid=(nb,),
        in_specs=[zspec(s) for s in range(6)] + [pspec(0), pspec(1), pspec(2),
                                                 pl.BlockSpec((GATE_CB, 4), lambda j: (j, 0))],
        out_specs=[qspec, qspec, ospec],
        out_shape=[qout, qout, out],
        compiler_params=_cparams(1),
        name="gate",
    )(zt, zt, zt, zt, zt, zt, hy_par, hy_par, hy_par, sc_par)


FILT_TB = 1024


def _filter_kernel(emb_ref, t_ref, w1_ref, b1_ref, w2_ref, b2_ref, fr_ref, w3_ref, b3_ref,
                   ad_ref, win_ref, hf_ref, hb_ref, wt_ref, hfs_ref, hbs_ref):
    wt_ref[...] = win_ref[...].T.astype(BF16)
    hi = lax.Precision.HIGHEST
    fr = fr_ref[...]
    a1 = jnp.dot(w1_ref[...], emb_ref[...], precision=hi, preferred_element_type=F32) + b1_ref[...]
    h1 = jnp.sin(fr * a1)
    a2 = jnp.dot(w2_ref[...], h1, precision=hi, preferred_element_type=F32) + b2_ref[...]
    h2 = jnp.sin(fr * a2)
    o = jnp.dot(w3_ref[...], h2.astype(BF16), preferred_element_type=F32) + b3_ref[...]
    window = jnp.exp(-(ad_ref[...] * t_ref[...]))
    hfs_ref[...] = o[:D_HY] * window
    pos = lax.broadcasted_iota(jnp.int32, (D_HY, FILT_TB), 1) + pl.program_id(0) * FILT_TB
    hbs_ref[...] = jnp.where(pos == 0, 0.0, o[D_HY:] * window)

    def to_q(cg, carry):
        rows = pl.ds(pl.multiple_of(cg * CH_GROUP, CH_GROUP), CH_GROUP)
        for n1 in range(FILT_TB // FFT_N2):
            lanes = slice(n1 * FFT_N2, (n1 + 1) * FFT_N2)
            hf_ref[cg, n1] = hfs_ref[rows, lanes]
            hb_ref[cg, n1] = hbs_ref[rows, lanes]
        return carry

    lax.fori_loop(0, D_HY // CH_GROUP, to_q, 0)


def _filters(emb_t, t_row, w1t, b1, w2t, b2, fr, w3t, b3, absdelta, w_in):
    full = lambda a: pl.BlockSpec(a.shape, lambda i: (0,) * a.ndim)
    out = jax.ShapeDtypeStruct(Q8_SHAPE, F32)
    qspec = pl.BlockSpec((Q8_SHAPE[0], FILT_TB // FFT_N2, CH_GROUP, FFT_N2), lambda i: (0, i, 0, 0))
    steps = SEQ // FILT_TB
    return pl.pallas_call(
        _filter_kernel,
        grid=(steps,),
        in_specs=[pl.BlockSpec((EMB_PAD, FILT_TB), lambda i: (0, i)),
                  pl.BlockSpec((1, FILT_TB), lambda i: (0, i)),
                  full(w1t), full(b1), full(w2t), full(b2), full(fr), full(w3t), full(b3),
                  full(absdelta),
                  pl.BlockSpec((D_MODEL, D_IN // steps), lambda i: (0, i))],
        out_specs=[qspec, qspec, pl.BlockSpec((D_IN // steps, D_MODEL), lambda i: (i, 0))],
        out_shape=[out, out, jax.ShapeDtypeStruct((D_IN, D_MODEL), BF16)],
        scratch_shapes=[pltpu.VMEM((D_HY, FILT_TB), F32)] * 2,
        compiler_params=_cparams(1),
        name="filters",
    )(emb_t, t_row, w1t, b1, w2t, b2, fr, w3t, b3, absdelta, w_in)


HY_CB = 32
HY_UNROLL = 4
GROUP_ROWS = CH_GROUP * FFT_N1_LIVE
N_K1 = FFT_N1 // 2 + 1
RE_ROWS = N_K1 * CH_GROUP
IM_ROWS = (N_K1 - 2) * CH_GROUP


def _dft_tables():
    n1 = np.arange(FFT_N1_LIVE, dtype=np.float64)
    k1 = np.arange(N_K1, dtype=np.float64)
    th = 2.0 * np.pi * np.outer(k1, n1) / FFT_N1
    eye = np.eye(CH_GROUP)
    bd1 = np.concatenate([np.einsum("kn,cd->kcnd", np.cos(th), eye).reshape(RE_ROWS, GROUP_ROWS),
                          np.einsum("kn,cd->kcnd", -np.sin(th)[1:-1], eye).reshape(IM_ROWS, GROUP_ROWS)])
    wgt = np.where((k1 == 0) | (k1 == FFT_N1 // 2), 1.0, 2.0)[:, None] / FFT_N
    bd1inv = np.concatenate(
        [np.einsum("kn,cd->nckd", wgt * np.cos(th), eye).reshape(GROUP_ROWS, RE_ROWS),
         np.einsum("kn,cd->nckd", (-wgt * np.sin(th))[1:-1], eye).reshape(GROUP_ROWS, IM_ROWS)], axis=1)
    n2 = np.arange(FFT_N2, dtype=np.float64)
    ph = 2.0 * np.pi * np.outer(k1, n2) / FFT_N
    twr = np.repeat(np.cos(ph), CH_GROUP, axis=0)
    twi = np.repeat(-np.sin(ph), CH_GROUP, axis=0)
    ps = 2.0 * np.pi * np.outer(n2, n2) / FFT_N2
    cr, ci = np.cos(ps), -np.sin(ps)
    w3 = np.block([[cr, ci], [-ci, cr]])
    w3inv = np.block([[cr, -ci], [ci, cr]])
    f32 = lambda a: jnp.asarray(a, F32)
    return (f32(bd1).astype(BF16), f32(bd1inv).astype(BF16), f32(twr), f32(twi),
            f32(w3).astype(BF16), f32(w3inv).astype(BF16))


def _hyena_kernel(u_ref, hf_ref, hb_ref, x0_ref, bias_ref, bd1_ref, bd1inv_ref, twr_ref, twi_ref,
                  w3_ref, w3inv_ref, w1_ref, w2_ref, wo_ref, ya_ref, w1b_ref, w2b_ref, wob_ref):
    w1b_ref[...] = w1_ref[...].astype(BF16)
    w2b_ref[...] = w2_ref[...].astype(BF16)
    wob_ref[...] = wo_ref[...].astype(BF16)
    twr, twi = twr_ref[...], twi_ref[...]
    zrow = jnp.zeros((CH_GROUP, FFT_N2), F32)

    def stage1(xq):
        xq = xq.reshape(GROUP_ROWS, FFT_N2).astype(BF16)
        a = jnp.dot(bd1_ref[...], xq, preferred_element_type=F32)
        ar = a[:RE_ROWS]
        ai = jnp.concatenate([zrow, a[RE_ROWS:], zrow], axis=0)
        br = ar * twr - ai * twi
        bi = ar * twi + ai * twr
        return jnp.concatenate([br, bi], axis=1)

    def group(gi, carry):
        u = u_ref[gi]
        b_all = jnp.concatenate([stage1(u), stage1(hf_ref[gi]), stage1(hb_ref[gi])], axis=0)
        s_all = jnp.dot(b_all.astype(BF16), w3_ref[...], preferred_element_type=F32)
        xs, fs, bs = s_all[:RE_ROWS], s_all[RE_ROWS:2 * RE_ROWS], s_all[2 * RE_ROWS:]
        xr, xi = xs[:, :FFT_N2], xs[:, FFT_N2:]
        kr = fs[:, :FFT_N2] + bs[:, :FFT_N2]
        ki = fs[:, FFT_N2:] - bs[:, FFT_N2:]
        ycat = jnp.concatenate([xr * kr - xi * ki, xr * ki + xi * kr], axis=1).astype(BF16)
        ap = jnp.dot(ycat, w3inv_ref[...], preferred_element_type=F32)
        apr, api = ap[:, :FFT_N2], ap[:, FFT_N2:]
        bpr = apr * twr + api * twi
        bpi = api * twr - apr * twi
        b2 = jnp.concatenate([bpr, bpi[CH_GROUP:RE_ROWS - CH_GROUP]], axis=0).astype(BF16)
        y = jnp.dot(bd1inv_ref[...], b2, preferred_element_type=F32)
        y3 = y.reshape(FFT_N1_LIVE, CH_GROUP, FFT_N2)
        bias = bias_ref[pl.ds(pl.multiple_of(gi * CH_GROUP, CH_GROUP), CH_GROUP), :]
        ya_ref[gi] = x0_ref[gi] * (y3 + bias[None] * u)
        return carry

    lax.fori_loop(0, HY_CB // CH_GROUP, group, 0, unroll=HY_UNROLL)


def _hyena(u_q, hf_q, hb_q, x0_q, bias2, tables, mlp_w1, mlp_w2, layer, w_out):
    bd1, bd1inv, twr, twi, w3, w3inv = tables
    steps = D_HY // HY_CB
    qspec = pl.BlockSpec((HY_CB // CH_GROUP,) + Q8_SHAPE[1:], lambda j: (j, 0, 0, 0))
    full = lambda a: pl.BlockSpec(a.shape, lambda j: (0,) * a.ndim)
    w1_blk = (D_MODEL // steps, D_FF)
    w2_blk = (D_FF // steps, D_MODEL)
    wo_blk = (D_MODEL // steps, D_MODEL)
    return pl.pallas_call(
        _hyena_kernel,
        grid=(steps,),
        in_specs=[qspec, qspec, qspec, qspec,
                  pl.BlockSpec((HY_CB, FFT_N2), lambda j: (j, 0)),
                  full(bd1), full(bd1inv), full(twr), full(twi), full(w3), full(w3inv),
                  pl.BlockSpec((None,) + w1_blk, lambda j: (layer, j, 0)),
                  pl.BlockSpec((None,) + w2_blk, lambda j: (layer, j, 0)),
                  pl.BlockSpec(wo_blk, lambda j: (j, 0))],
        out_specs=[qspec, pl.BlockSpec(w1_blk, lambda j: (j, 0)), pl.BlockSpec(w2_blk, lambda j: (j, 0)),
                   pl.BlockSpec(wo_blk, lambda j: (j, 0))],
        out_shape=[jax.ShapeDtypeStruct(Q8_SHAPE, F32),
                   jax.ShapeDtypeStruct((D_MODEL, D_FF), BF16),
                   jax.ShapeDtypeStruct((D_FF, D_MODEL), BF16),
                   jax.ShapeDtypeStruct((D_MODEL, D_MODEL), BF16)],
        compiler_params=_cparams(1),
        name="hyena_fftconv",
    )(u_q, hf_q, hb_q, x0_q, bias2, bd1, bd1inv, twr, twi, w3, w3inv, mlp_w1, mlp_w2, w_out)


OUT_TM = 512


def _out_proj_kernel(ya_ref, yb_ref, wa_ref, wb_ref, x_ref, g_ref, o_ref):
    tn = (((0,), (0,)), ((), ()))
    ya = jnp.concatenate([ya_ref[:, a].reshape(D_HY, FFT_N2) for a in range(OUT_TM // FFT_N2)],
                         axis=1).astype(BF16)
    m = lax.dot_general(ya, wa_ref[...], tn, preferred_element_type=F32)
    m = m + lax.dot_general(yb_ref[...], wb_ref[...], tn, preferred_element_type=F32)
    o_ref[...] = x_ref[...] + _rms_rows(m, g_ref[...])


def _out_proj(ya_q, yb_t, wo, x2d, g):
    return pl.pallas_call(
        _out_proj_kernel,
        grid=(SEQ // OUT_TM,),
        in_specs=[
            pl.BlockSpec((Q8_SHAPE[0], OUT_TM // FFT_N2, CH_GROUP, FFT_N2), lambda i: (0, i, 0, 0)),
            pl.BlockSpec((D_SC, OUT_TM), lambda i: (0, i)),
            pl.BlockSpec((D_HY, D_MODEL), lambda i: (0, 0)),
            pl.BlockSpec((D_SC, D_MODEL), lambda i: (1, 0)),
            pl.BlockSpec((OUT_TM, D_MODEL), lambda i: (i, 0)),
            pl.BlockSpec((1, D_MODEL), lambda i: (0, 0)),
        ],
        out_specs=pl.BlockSpec((OUT_TM, D_MODEL), lambda i: (i, 0)),
        out_shape=jax.ShapeDtypeStruct((SEQ, D_MODEL), F32),
        compiler_params=_cparams(1),
        name="out_proj",
    )(ya_q, yb_t, wo, wo, x2d, g)


MLP_TM = 1024
MLP_TK = 1024
MLP_TA = 512
MLP_TN = 512
MLP_ROWS = 128


def _mlp_kernel(x_ref, gin_ref, gout_ref, w1_ref, w2_ref, *rest):
    if len(rest) == 6:
        nw1_ref, nw2_ref, o_ref, nw1b_ref, nw2b_ref, h_ref = rest
        nw1b_ref[...] = nw1_ref[...].astype(BF16)
        nw2b_ref[...] = nw2_ref[...].astype(BF16)
    else:
        o_ref, h_ref = rest
    k = pl.program_id(1)

    def row_chunks(fn):
        def body(c, carry):
            fn(pl.ds(pl.multiple_of(c * MLP_ROWS, MLP_ROWS), MLP_ROWS))
            return carry
        lax.fori_loop(0, MLP_TM // MLP_ROWS, body, 0)

    @pl.when(k == 0)
    def _():
        def prologue(rows):
            h_ref[rows, :] = _rms_rows(x_ref[rows, :], gin_ref[...]).astype(BF16)
            o_ref[rows, :] = jnp.zeros((MLP_ROWS, D_MODEL), F32)
        row_chunks(prologue)

    for c in range(w1_ref.shape[1] // MLP_TA):
        mid = slice(c * MLP_TA, (c + 1) * MLP_TA)
        a = jnp.dot(h_ref[...], w1_ref[:, mid], preferred_element_type=F32)
        a = jnp.square(jnp.maximum(a, 0.0)).astype(BF16)
        for n in range(D_MODEL // MLP_TN):
            cols = slice(n * MLP_TN, (n + 1) * MLP_TN)
            o_ref[:, cols] += jnp.dot(a, w2_ref[mid, cols], preferred_element_type=F32)

    @pl.when(k == pl.num_programs(1) - 1)
    def _():
        def epilogue(rows):
            o_ref[rows, :] = x_ref[rows, :] + _rms_rows(o_ref[rows, :], gout_ref[...])
        row_chunks(epilogue)


def _mlp(x2d, gin, gout, w1, w2, narrow=None):
    tk = MLP_TK if narrow is None else MLP_TK // 2
    ni, nk = SEQ // MLP_TM, D_FF // tk
    in_specs = [
        pl.BlockSpec((MLP_TM, D_MODEL), lambda i, k: (i, 0)),
        pl.BlockSpec((1, D_MODEL), lambda i, k: (0, 0)),
        pl.BlockSpec((1, D_MODEL), lambda i, k: (0, 0)),
        pl.BlockSpec((D_MODEL, tk), lambda i, k: (0, k)),
        pl.BlockSpec((tk, D_MODEL), lambda i, k: (k, 0)),
    ]
    out_specs = [pl.BlockSpec((MLP_TM, D_MODEL), lambda i, k: (i, 0))]
    out_shape = [jax.ShapeDtypeStruct((SEQ, D_MODEL), F32)]
    args = [x2d, gin, gout, w1, w2]
    if narrow is not None:
        nw1, nw2, layer = narrow
        w1_blk = (D_MODEL // ni, D_FF // nk)
        w2_blk = (D_FF // ni, D_MODEL // nk)
        in_specs += [pl.BlockSpec((None,) + w1_blk, lambda i, k: (layer, i, k)),
                     pl.BlockSpec((None,) + w2_blk, lambda i, k: (layer, i, k))]
        out_specs += [pl.BlockSpec(w1_blk, lambda i, k: (i, k)), pl.BlockSpec(w2_blk, lambda i, k: (i, k))]
        out_shape += [jax.ShapeDtypeStruct((D_MODEL, D_FF), BF16),
                      jax.ShapeDtypeStruct((D_FF, D_MODEL), BF16)]
        args += [nw1, nw2]
    return pl.pallas_call(
        _mlp_kernel,
        grid=(ni, nk),
        in_specs=in_specs,
        out_specs=out_specs,
        out_shape=out_shape,
        scratch_shapes=[pltpu.VMEM((MLP_TM, D_MODEL), BF16)],
        compiler_params=_cparams(2),
        name="mlp",
    )(*args)


POOL_TM = 512
POOL_SUB = 128
HALO = 8


def _pool_bands():
    t = np.arange(POOL_SUB)[:, None]
    j = np.arange(POOL_SUB + 2 * HALO)[None, :]
    return np.stack([(np.abs(j - HALO - t) <= w // 2) for w in POOL_WINDOWS]).astype(np.float32)


def _pool_kernel(xp_ref, xm_ref, xn_ref, gin_ref, gout_ref, band_ref, pw_ref, ps_ref, o_ref,
                 hs_ref, hi_ref, lo_ref, m_ref):
    i = pl.program_id(0)
    gin = gin_ref[...]
    hs_ref[0:HALO, :] = jnp.where(i > 0, _rms_rows(xp_ref[...], gin), 0.0)
    hs_ref[HALO:HALO + POOL_TM, :] = _rms_rows(xm_ref[...], gin)
    hs_ref[HALO + POOL_TM:, :] = jnp.where(i < pl.num_programs(0) - 1, _rms_rows(xn_ref[...], gin), 0.0)
    hs = hs_ref[...]
    hi = hs.astype(BF16)
    hi_ref[...] = hi
    lo_ref[...] = (hs - hi.astype(F32)).astype(BF16)
    row = lax.broadcasted_iota(jnp.int32, (POOL_SUB, D_POOL_G), 0) + i * POOL_TM
    for gi, w in enumerate(POOL_WINDOWS):
        r = w // 2
        cols = slice(gi * D_POOL_G, (gi + 1) * D_POOL_G)
        band = band_ref[gi]
        parts = []
        for sb in range(POOL_TM // POOL_SUB):
            win = slice(sb * POOL_SUB, (sb + 1) * POOL_SUB + 2 * HALO)
            s = (jnp.dot(band, hi_ref[win, cols], preferred_element_type=F32)
                 + jnp.dot(band, lo_ref[win, cols], preferred_element_type=F32))
            t = row + sb * POOL_SUB
            cnt = jnp.minimum(t + r + 1, SEQ) - jnp.maximum(t - r, 0)
            u = hs_ref[sb * POOL_SUB + HALO:(sb + 1) * POOL_SUB + HALO, cols]
            parts.append((s / cnt.astype(F32) - u).astype(BF16))
        d = jnp.concatenate(parts, axis=0)
        mg = jnp.dot(d, pw_ref[gi], preferred_element_type=F32)
        m_ref[:, cols] = mg * ps_ref[:, cols]
    o_ref[...] = xm_ref[...] + _rms_rows(m_ref[...], gout_ref[...])


def _pool(x2d, gin, gout, pw, ps):
    nb8 = POOL_TM // HALO
    last8 = SEQ // HALO - 1
    band = jnp.asarray(_pool_bands(), F32).astype(BF16)
    ext = POOL_TM + 2 * HALO
    return pl.pallas_call(
        _pool_kernel,
        grid=(SEQ // POOL_TM,),
        in_specs=[
            pl.BlockSpec((HALO, D_MODEL), lambda i: (jnp.maximum(i * nb8 - 1, 0), 0)),
            pl.BlockSpec((POOL_TM, D_MODEL), lambda i: (i, 0)),
            pl.BlockSpec((HALO, D_MODEL), lambda i: (jnp.minimum((i + 1) * nb8, last8), 0)),
            pl.BlockSpec((1, D_MODEL), lambda i: (0, 0)),
            pl.BlockSpec((1, D_MODEL), lambda i: (0, 0)),
            pl.BlockSpec(band.shape, lambda i: (0, 0, 0)),
            pl.BlockSpec((len(POOL_WINDOWS), D_POOL_G, D_POOL_G), lambda i: (0, 0, 0)),
            pl.BlockSpec((1, D_MODEL), lambda i: (0, 0)),
        ],
        out_specs=pl.BlockSpec((POOL_TM, D_MODEL), lambda i: (i, 0)),
        out_shape=jax.ShapeDtypeStruct((SEQ, D_MODEL), F32),
        scratch_shapes=[pltpu.VMEM((ext, D_MODEL), F32), pltpu.VMEM((ext, D_MODEL), BF16),
                        pltpu.VMEM((ext, D_MODEL), BF16), pltpu.VMEM((POOL_TM, D_MODEL), F32)],
        compiler_params=_cparams(1),
        name="pool_mixer",
    )(x2d, x2d, x2d, gin, gout, band, pw, ps)


def _filter_constants():
    L = SEQ
    t = np.linspace(0.0, 1.0, L)
    w_pos = 2.0 * np.pi * np.arange(L) / L
    bands = np.linspace(1e-4, FILTER_BANDS - 1, FILTER_BANDS)
    ang = w_pos[:, None] * bands[None, :]
    emb = np.concatenate([t[:, None], np.cos(ang), -np.sin(ang)], axis=-1)
    emb_t = np.zeros((EMB_PAD, L))
    emb_t[:EMB_DIM] = emb.T
    max_decay = math.log(DECAY_TARGET) / FAST_DECAY_PCT
    min_decay = math.log(DECAY_TARGET) / SLOW_DECAY_PCT
    absdelta = np.abs(np.linspace(min_decay, max_decay, D_HY))[:, None]
    return (jnp.asarray(emb_t, F32), jnp.asarray(t[None, :], F32), jnp.asarray(absdelta, F32))


def _mixer_layer0(x2d, g, w_in, hy_short_w, hy_short_b, f_w1, f_b1, f_w2, f_b2, f_w3, f_b3,
                  freq, hy_bias, sc_conv_w, w_out, mlp_w1, mlp_w2, layer):
    emb_t, t_row, absdelta = _filter_constants()
    w1t = jnp.zeros((FILTER_HIDDEN, EMB_PAD), F32).at[:, :EMB_DIM].set(f_w1.T)
    col = lambda v: v[:, None].astype(F32)
    hf_q, hb_q, wt = _filters(emb_t, t_row, w1t, col(f_b1), f_w2.T, col(f_b2), col(freq),
                              f_w3.T.astype(BF16), col(f_b3), absdelta, w_in)

    zt = _in_proj(x2d, g[0][None, :], wt)
    hy_par = jnp.concatenate([hy_short_w.T, hy_short_b[:, None]], axis=1)
    sc_par = jnp.concatenate([sc_conv_w.T, jnp.zeros((D_SC, 1), F32)], axis=1)
    x0_q, u_q, yb_t = _gate(zt, hy_par, sc_par)

    bias2 = jnp.broadcast_to(hy_bias[:, None].astype(F32), (D_HY, FFT_N2))
    ya_q, w1_this, w2_this, wo = _hyena(u_q, hf_q, hb_q, x0_q, bias2, _dft_tables(), mlp_w1, mlp_w2,
                                        layer, w_out)
    x2d = _out_proj(ya_q, yb_t, wo, x2d, g[1][None, :])
    return x2d, (w1_this, w2_this)


def kernel(x, norm_g, mix_w_in, hy_short_w, hy_short_b, hy_filt_w1, hy_filt_b1, hy_filt_w2,
           hy_filt_b2, hy_filt_w3, hy_filt_b3, hy_freq, hy_bias, sc_conv_w, mix_w_out,
           pool_w, pool_scale, mlp_w1, mlp_w2):
    x2d = x.reshape(SEQ, D_MODEL)
    depth = norm_g.shape[0]
    assert depth % 2 == 0, "an even layer narrows its own MLP weights and, in its MLP, the next layer's"
    for i in range(depth):
        g = norm_g[i]
        j = i // 2
        gin, gout = g[2][None, :], g[3][None, :]
        if i % 2 == 0:
            x2d, w_this = _mixer_layer0(
                x2d, g, mix_w_in[j], hy_short_w[j], hy_short_b[j], hy_filt_w1[j], hy_filt_b1[j],
                hy_filt_w2[j], hy_filt_b2[j], hy_filt_w3[j], hy_filt_b3[j], hy_freq[j], hy_bias[j],
                sc_conv_w[j], mix_w_out[j], mlp_w1, mlp_w2, i)
            x2d, *w_next = _mlp(x2d, gin, gout, *w_this, narrow=(mlp_w1, mlp_w2, i + 1))
        else:
            x2d = _pool(x2d, g[0][None, :], g[1][None, :], pool_w[j].astype(BF16),
                        pool_scale[j][None, :])
            x2d, = _mlp(x2d, gin, gout, *w_next)
    return x2d.reshape(x.shape)
```

```python
import math

import numpy as np
import jax
import jax.numpy as jnp
from jax import lax
from jax.experimental import pallas as pl
from jax.experimental.pallas import tpu as pltpu

F32 = jnp.float32
BF16 = jnp.bfloat16

D_MODEL = 2048
SEQ = 8192
D_HY = D_MODEL // 2
D_SC = D_MODEL // 2
D_IN = 3 * D_HY + 3 * D_SC
FILTER_BANDS = 16
EMB_DIM = 1 + 2 * FILTER_BANDS
EMB_PAD = 40
FILTER_HIDDEN = 64
DECAY_TARGET = 1e-2
FAST_DECAY_PCT = 0.3
SLOW_DECAY_PCT = 1.5
POOL_WINDOWS = (2, 4, 8, 16)
D_POOL_G = D_MODEL // len(POOL_WINDOWS)
D_FF = 4 * D_MODEL
NORM_EPS = 1e-6
LANES = 128

FFT_N = 2 * SEQ
FFT_N2 = 256
FFT_N1 = FFT_N // FFT_N2
FFT_N1_LIVE = SEQ // FFT_N2
CH_GROUP = 8
Q8_SHAPE = (D_MODEL // 2 // CH_GROUP, FFT_N1_LIVE, CH_GROUP, FFT_N2)
KQ_SHAPE = (D_MODEL // 2 // CH_GROUP, FFT_N1, CH_GROUP, FFT_N2)

VMEM_LIMIT = 56 * 1024 * 1024


def _cparams(n_axes):
    return pltpu.CompilerParams(
        dimension_semantics=("arbitrary",) * n_axes, vmem_limit_bytes=VMEM_LIMIT)


def _rms_rows(x, g):
    r = lax.rsqrt(jnp.mean(x * x, axis=-1, keepdims=True) + NORM_EPS)
    return x * r * g


IN_TM = 1024
IN_TN = 2048


def _in_proj_kernel(x_ref, g_ref, wt_ref, zt_ref, h_ref):
    @pl.when(pl.program_id(1) == 0)
    def _():
        h_ref[...] = _rms_rows(x_ref[...], g_ref[...]).astype(BF16)

    zt_ref[...] = lax.dot_general(
        wt_ref[...], h_ref[...], (((1,), (1,)), ((), ())), preferred_element_type=F32).astype(BF16)


def _in_proj(x2d, g, wt):
    return pl.pallas_call(
        _in_proj_kernel,
        grid=(SEQ // IN_TM, D_IN // IN_TN),
        in_specs=[
            pl.BlockSpec((IN_TM, D_MODEL), lambda i, j: (i, 0)),
            pl.BlockSpec((1, D_MODEL), lambda i, j: (0, 0)),
            pl.BlockSpec((IN_TN, D_MODEL), lambda i, j: (j, 0)),
        ],
        out_specs=pl.BlockSpec((IN_TN, IN_TM), lambda i, j: (j, i)),
        out_shape=jax.ShapeDtypeStruct((D_IN, SEQ), BF16),
        scratch_shapes=[pltpu.VMEM((IN_TM, D_MODEL), BF16)],
        compiler_params=_cparams(2),
        name="in_proj",
    )(x2d, g, wt)


GATE_CB = 32
GATE_ROWS = 16
GATE_CHUNK = 1024


def _conv3_chunk(zwin, w, off, first, last):
    width = zwin.shape[1]
    zm = pltpu.roll(zwin, 1, 1)[:, off:off + GATE_CHUNK]
    zp = pltpu.roll(zwin, width - 1, 1)[:, off:off + GATE_CHUNK]
    z = zwin[:, off:off + GATE_CHUNK]
    lane = lax.broadcasted_iota(jnp.int32, z.shape, 1)
    if first:
        zm = jnp.where(lane == 0, 0.0, zm)
    if last:
        zp = jnp.where(lane == GATE_CHUNK - 1, 0.0, zp)
    return zm * w[:, 0:1] + z * w[:, 1:2] + zp * w[:, 2:3]


def _gate_kernel(zx0_ref, zx1_ref, zv_ref, zgb_ref, zgc_ref, zxv_ref,
                 px0_ref, px1_ref, pv_ref, psc_ref, x0_ref, u_ref, yb_ref):
    n_chunks = SEQ // GATE_CHUNK
    groups = GATE_ROWS // CH_GROUP
    slabs = GATE_CHUNK // FFT_N2

    def row_tile(rt, carry):
        rows = pl.ds(pl.multiple_of(rt * GATE_ROWS, GATE_ROWS), GATE_ROWS)
        px0, px1, pv, psc = px0_ref[rows, :], px1_ref[rows, :], pv_ref[rows, :], psc_ref[rows, :]
        for c in range(n_chunks):
            lo = max(c * GATE_CHUNK - LANES, 0)
            hi = min((c + 1) * GATE_CHUNK + LANES, SEQ)
            edge = (c * GATE_CHUNK - lo, c == 0, c == n_chunks - 1)
            cur = slice(c * GATE_CHUNK, (c + 1) * GATE_CHUNK)
            win = lambda ref: ref[rows, lo:hi].astype(F32)
            x0 = _conv3_chunk(win(zx0_ref), px0, *edge) + px0[:, 3:4]
            x1 = _conv3_chunk(win(zx1_ref), px1, *edge) + px1[:, 3:4]
            v = _conv3_chunk(win(zv_ref), pv, *edge) + pv[:, 3:4]
            u = v * x1
            for g in range(groups):
                for a in range(slabs):
                    sl = (slice(g * CH_GROUP, (g + 1) * CH_GROUP), slice(a * FFT_N2, (a + 1) * FFT_N2))
                    x0_ref[rt * groups + g, c * slabs + a] = x0[sl]
                    u_ref[rt * groups + g, c * slabs + a] = u[sl]
            p = win(zgc_ref) * win(zxv_ref)
            yb = zgb_ref[rows, cur].astype(F32) * _conv3_chunk(p, psc, *edge)
            yb_ref[rows, cur] = yb.astype(BF16)
        return carry

    lax.fori_loop(0, GATE_CB // GATE_ROWS, row_tile, 0)


def _gate(zt, hy_par, sc_par):
    nb = D_HY // GATE_CB
    zspec = lambda s: pl.BlockSpec((GATE_CB, SEQ), lambda j, s=s: (s * nb + j, 0))
    pspec = lambda s: pl.BlockSpec((GATE_CB, 4), lambda j, s=s: (s * nb + j, 0))
    out = jax.ShapeDtypeStruct((D_HY, SEQ), BF16)
    ospec = pl.BlockSpec((GATE_CB, SEQ), lambda j: (j, 0))
    qout = jax.ShapeDtypeStruct(Q8_SHAPE, F32)
    qspec = pl.BlockSpec((GATE_CB // CH_GROUP,) + Q8_SHAPE[1:], lambda j: (j, 0, 0, 0))
    return pl.pallas_call(
        _gate_kernel,
        grid=(nb,),
        in_specs=[zspec(s) for s in range(6)] + [pspec(0), pspec(1), pspec(2),
                                                 pl.BlockSpec((GATE_CB, 4), lambda j: (j, 0))],
        out_specs=[qspec, qspec, ospec],
        out_shape=[qout, qout, out],
        compiler_params=_cparams(1),
        name="gate",
    )(zt, zt, zt, zt, zt, zt, hy_par, hy_par, hy_par, sc_par)


FILT_TB = 1024


def _filter_kernel(emb_ref, t_ref, w1_ref, b1_ref, w2_ref, b2_ref, fr_ref, w3_ref, b3_ref,
                   ad_ref, win_ref, k_ref, wt_ref, ks_ref):
    wt_ref[...] = win_ref[...].T.astype(BF16)
    hi = lax.Precision.HIGHEST
    fr = fr_ref[...]
    a1 = jnp.dot(w1_ref[...], emb_ref[...], precision=hi, preferred_element_type=F32) + b1_ref[...]
    h1 = jnp.sin(fr * a1)
    a2 = jnp.dot(w2_ref[...], h1, precision=hi, preferred_element_type=F32) + b2_ref[...]
    h2 = jnp.sin(fr * a2)
    o = jnp.dot(w3_ref[...], h2.astype(BF16), preferred_element_type=F32) + b3_ref[...]
    window = jnp.exp(-(ad_ref[...] * t_ref[...]))
    pos = lax.broadcasted_iota(jnp.int32, (D_HY, FILT_TB), 1) + pl.program_id(0) * FILT_TB
    ks_ref[...] = jnp.where(pos == SEQ, 0.0, o * window)

    def to_q(cg, carry):
        rows = pl.ds(pl.multiple_of(cg * CH_GROUP, CH_GROUP), CH_GROUP)
        for n1 in range(FILT_TB // FFT_N2):
            k_ref[cg, n1] = ks_ref[rows, n1 * FFT_N2:(n1 + 1) * FFT_N2]
        return carry

    lax.fori_loop(0, D_HY // CH_GROUP, to_q, 0)


def _filters(emb_t, t_row, w1t, b1, w2t, b2, fr, w3t, b3, absdelta, w_in):
    full = lambda a: pl.BlockSpec(a.shape, lambda i: (0,) * a.ndim)
    steps = FFT_N // FILT_TB
    half = steps // 2
    return pl.pallas_call(
        _filter_kernel,
        grid=(steps,),
        in_specs=[pl.BlockSpec((EMB_PAD, FILT_TB), lambda i: (0, i)),
                  pl.BlockSpec((1, FILT_TB), lambda i: (0, i)),
                  full(w1t), full(b1), full(w2t), full(b2), full(fr),
                  pl.BlockSpec((D_HY, FILTER_HIDDEN), lambda i: (i // half, 0)),
                  pl.BlockSpec((D_HY, 1), lambda i: (i // half, 0)),
                  full(absdelta),
                  pl.BlockSpec((D_MODEL, D_IN // steps), lambda i: (0, i))],
        out_specs=[pl.BlockSpec((KQ_SHAPE[0], FILT_TB // FFT_N2, CH_GROUP, FFT_N2), lambda i: (0, i, 0, 0)),
                   pl.BlockSpec((D_IN // steps, D_MODEL), lambda i: (i, 0))],
        out_shape=[jax.ShapeDtypeStruct(KQ_SHAPE, F32), jax.ShapeDtypeStruct((D_IN, D_MODEL), BF16)],
        scratch_shapes=[pltpu.VMEM((D_HY, FILT_TB), F32)],
        compiler_params=_cparams(1),
        name="filters",
    )(emb_t, t_row, w1t, b1, w2t, b2, fr, w3t, b3, absdelta, w_in)


HY_CB = 32
GROUP_ROWS = CH_GROUP * FFT_N1_LIVE
N_K1 = FFT_N1 // 2 + 1
RE_ROWS = N_K1 * CH_GROUP
IM_ROWS = (N_K1 - 2) * CH_GROUP


def _dft_tables():
    n1 = np.arange(FFT_N1_LIVE, dtype=np.float64)
    k1 = np.arange(N_K1, dtype=np.float64)
    th = 2.0 * np.pi * np.outer(k1, n1) / FFT_N1
    eye = np.eye(CH_GROUP)
    def stage1(cos, sin):
        return np.concatenate([np.einsum("kn,cd->kcnd", cos, eye).reshape(RE_ROWS, -1),
                               np.einsum("kn,cd->kcnd", -sin[1:-1], eye).reshape(IM_ROWS, -1)])
    bd1 = stage1(np.cos(th), np.sin(th))
    thk = 2.0 * np.pi * np.outer(k1, np.arange(FFT_N1, dtype=np.float64)) / FFT_N1
    bd1k = stage1(np.cos(thk), np.sin(thk))
    wgt = np.where((k1 == 0) | (k1 == FFT_N1 // 2), 1.0, 2.0)[:, None] / FFT_N
    bd1inv = np.concatenate(
        [np.einsum("kn,cd->nckd", wgt * np.cos(th), eye).reshape(GROUP_ROWS, RE_ROWS),
         np.einsum("kn,cd->nckd", (-wgt * np.sin(th))[1:-1], eye).reshape(GROUP_ROWS, IM_ROWS)], axis=1)
    n2 = np.arange(FFT_N2, dtype=np.float64)
    ph = 2.0 * np.pi * np.outer(k1, n2) / FFT_N
    twr = np.repeat(np.cos(ph), CH_GROUP, axis=0)
    twi = np.repeat(-np.sin(ph), CH_GROUP, axis=0)
    ps = 2.0 * np.pi * np.outer(n2, n2) / FFT_N2
    cr, ci = np.cos(ps), -np.sin(ps)
    w3 = np.block([[cr, ci], [-ci, cr]])
    w3inv = np.block([[cr, -ci], [ci, cr]])
    f32 = lambda a: jnp.asarray(a, F32)
    return (f32(bd1).astype(BF16), f32(bd1k).astype(BF16), f32(bd1inv).astype(BF16), f32(twr), f32(twi),
            f32(w3).astype(BF16), f32(w3inv).astype(BF16))


def _hyena_kernel(u_ref, k_ref, x0_ref, bias_ref, bd1_ref, bd1k_ref, bd1inv_ref, twr_ref, twi_ref,
                  w3_ref, w3inv_ref, w1_ref, w2_ref, wo_ref, ya_ref, w1b_ref, w2b_ref, wob_ref):
    w1b_ref[...] = w1_ref[...].astype(BF16)
    w2b_ref[...] = w2_ref[...].astype(BF16)
    wob_ref[...] = wo_ref[...].astype(BF16)
    twr, twi = twr_ref[...], twi_ref[...]
    zrow = jnp.zeros((CH_GROUP, FFT_N2), F32)

    def stage1(xq, bd_ref):
        xq = xq.reshape(xq.shape[0] * CH_GROUP, FFT_N2).astype(BF16)
        a = jnp.dot(bd_ref[...], xq, preferred_element_type=F32)
        ar = a[:RE_ROWS]
        ai = jnp.concatenate([zrow, a[RE_ROWS:], zrow], axis=0)
        br = ar * twr - ai * twi
        bi = ar * twi + ai * twr
        return jnp.concatenate([br, bi], axis=1)

    def pair(pi, carry):
        gs = [2 * pi, 2 * pi + 1]
        us = [u_ref[g] for g in gs]
        b_all = jnp.concatenate(
            [blk for g, u in zip(gs, us) for blk in (stage1(u, bd1_ref), stage1(k_ref[g], bd1k_ref))], axis=0)
        s_all = jnp.dot(b_all.astype(BF16), w3_ref[...], preferred_element_type=F32)
        ycats = []
        for i in range(2):
            xs = s_all[(2 * i) * RE_ROWS:(2 * i + 1) * RE_ROWS]
            ks = s_all[(2 * i + 1) * RE_ROWS:(2 * i + 2) * RE_ROWS]
            xr, xi = xs[:, :FFT_N2], xs[:, FFT_N2:]
            kr, ki = ks[:, :FFT_N2], ks[:, FFT_N2:]
            ycats.append(jnp.concatenate([xr * kr - xi * ki, xr * ki + xi * kr], axis=1))
        ap_all = jnp.dot(jnp.concatenate(ycats, axis=0).astype(BF16), w3inv_ref[...],
                         preferred_element_type=F32)
        for i, (g, u) in enumerate(zip(gs, us)):
            ap = ap_all[i * RE_ROWS:(i + 1) * RE_ROWS]
            apr, api = ap[:, :FFT_N2], ap[:, FFT_N2:]
            bpr = apr * twr + api * twi
            bpi = api * twr - apr * twi
            b2 = jnp.concatenate([bpr, bpi[CH_GROUP:RE_ROWS - CH_GROUP]], axis=0).astype(BF16)
            y = jnp.dot(bd1inv_ref[...], b2, preferred_element_type=F32)
            y3 = y.reshape(FFT_N1_LIVE, CH_GROUP, FFT_N2)
            bias = bias_ref[pl.ds(pl.multiple_of(g * CH_GROUP, CH_GROUP), CH_GROUP), :]
            ya_ref[g] = x0_ref[g] * (y3 + bias[None] * u)
        return carry

    lax.fori_loop(0, HY_CB // (2 * CH_GROUP), pair, 0, unroll=True)


def _hyena(u_q, k_q, x0_q, bias2, tables, mlp_w1, mlp_w2, layer, w_out):
    bd1, bd1k, bd1inv, twr, twi, w3, w3inv = tables
    steps = D_HY // HY_CB
    qspec = pl.BlockSpec((HY_CB // CH_GROUP,) + Q8_SHAPE[1:], lambda j: (j, 0, 0, 0))
    kspec = pl.BlockSpec((HY_CB // CH_GROUP,) + KQ_SHAPE[1:], lambda j: (j, 0, 0, 0))
    full = lambda a: pl.BlockSpec(a.shape, lambda j: (0,) * a.ndim)
    w1_blk = (D_MODEL // steps, D_FF)
    w2_blk = (D_FF // steps, D_MODEL)
    wo_blk = (D_MODEL // steps, D_MODEL)
    return pl.pallas_call(
        _hyena_kernel,
        grid=(steps,),
        in_specs=[qspec, kspec, qspec,
                  pl.BlockSpec((HY_CB, FFT_N2), lambda j: (j, 0)),
                  full(bd1), full(bd1k), full(bd1inv), full(twr), full(twi), full(w3), full(w3inv),
                  pl.BlockSpec((None,) + w1_blk, lambda j: (layer, j, 0)),
                  pl.BlockSpec((None,) + w2_blk, lambda j: (layer, j, 0)),
                  pl.BlockSpec(wo_blk, lambda j: (j, 0))],
        out_specs=[qspec, pl.BlockSpec(w1_blk, lambda j: (j, 0)), pl.BlockSpec(w2_blk, lambda j: (j, 0)),
                   pl.BlockSpec(wo_blk, lambda j: (j, 0))],
        out_shape=[jax.ShapeDtypeStruct(Q8_SHAPE, F32),
                   jax.ShapeDtypeStruct((D_MODEL, D_FF), BF16),
                   jax.ShapeDtypeStruct((D_FF, D_MODEL), BF16),
                   jax.ShapeDtypeStruct((D_MODEL, D_MODEL), BF16)],
        compiler_params=_cparams(1),
        name="hyena_fftconv",
    )(u_q, k_q, x0_q, bias2, bd1, bd1k, bd1inv, twr, twi, w3, w3inv, mlp_w1, mlp_w2, w_out)


OUT_TM = 512


def _out_proj_kernel(ya_ref, yb_ref, wa_ref, wb_ref, x_ref, g_ref, o_ref):
    tn = (((0,), (0,)), ((), ()))
    ya = jnp.concatenate([ya_ref[:, a].reshape(D_HY, FFT_N2) for a in range(OUT_TM // FFT_N2)],
                         axis=1).astype(BF16)
    m = lax.dot_general(ya, wa_ref[...], tn, preferred_element_type=F32)
    m = m + lax.dot_general(yb_ref[...], wb_ref[...], tn, preferred_element_type=F32)
    o_ref[...] = x_ref[...] + _rms_rows(m, g_ref[...])


def _out_proj(ya_q, yb_t, wo, x2d, g):
    return pl.pallas_call(
        _out_proj_kernel,
        grid=(SEQ // OUT_TM,),
        in_specs=[
            pl.BlockSpec((Q8_SHAPE[0], OUT_TM // FFT_N2, CH_GROUP, FFT_N2), lambda i: (0, i, 0, 0)),
            pl.BlockSpec((D_SC, OUT_TM), lambda i: (0, i)),
            pl.BlockSpec((D_HY, D_MODEL), lambda i: (0, 0)),
            pl.BlockSpec((D_SC, D_MODEL), lambda i: (1, 0)),
            pl.BlockSpec((OUT_TM, D_MODEL), lambda i: (i, 0)),
            pl.BlockSpec((1, D_MODEL), lambda i: (0, 0)),
        ],
        out_specs=pl.BlockSpec((OUT_TM, D_MODEL), lambda i: (i, 0)),
        out_shape=jax.ShapeDtypeStruct((SEQ, D_MODEL), F32),
        compiler_params=_cparams(1),
        name="out_proj",
    )(ya_q, yb_t, wo, wo, x2d, g)


MLP_TM = 1024
MLP_TK = 1024
MLP_TA = 512
MLP_TN = 512
MLP_ROWS = 128


def _mlp_kernel(x_ref, gin_ref, gout_ref, w1_ref, w2_ref, *rest):
    if len(rest) == 6:
        nw1_ref, nw2_ref, o_ref, nw1b_ref, nw2b_ref, h_ref = rest
        nw1b_ref[...] = nw1_ref[...].astype(BF16)
        nw2b_ref[...] = nw2_ref[...].astype(BF16)
    else:
        o_ref, h_ref = rest
    k = pl.program_id(1)

    def row_chunks(fn):
        def body(c, carry):
            fn(pl.ds(pl.multiple_of(c * MLP_ROWS, MLP_ROWS), MLP_ROWS))
            return carry
        lax.fori_loop(0, MLP_TM // MLP_ROWS, body, 0)

    @pl.when(k == 0)
    def _():
        def prologue(rows):
            h_ref[rows, :] = _rms_rows(x_ref[rows, :], gin_ref[...]).astype(BF16)
            o_ref[rows, :] = jnp.zeros((MLP_ROWS, D_MODEL), F32)
        row_chunks(prologue)

    for c in range(w1_ref.shape[1] // MLP_TA):
        mid = slice(c * MLP_TA, (c + 1) * MLP_TA)
        a = jnp.dot(h_ref[...], w1_ref[:, mid], preferred_element_type=F32)
        a = jnp.square(jnp.maximum(a, 0.0)).astype(BF16)
        for n in range(D_MODEL // MLP_TN):
            cols = slice(n * MLP_TN, (n + 1) * MLP_TN)
            o_ref[:, cols] += jnp.dot(a, w2_ref[mid, cols], preferred_element_type=F32)

    @pl.when(k == pl.num_programs(1) - 1)
    def _():
        def epilogue(rows):
            o_ref[rows, :] = x_ref[rows, :] + _rms_rows(o_ref[rows, :], gout_ref[...])
        row_chunks(epilogue)


def _mlp(x2d, gin, gout, w1, w2, narrow=None):
    tk = MLP_TK if narrow is None else MLP_TK // 2
    ni, nk = SEQ // MLP_TM, D_FF // tk
    in_specs = [
        pl.BlockSpec((MLP_TM, D_MODEL), lambda i, k: (i, 0)),
        pl.BlockSpec((1, D_MODEL), lambda i, k: (0, 0)),
        pl.BlockSpec((1, D_MODEL), lambda i, k: (0, 0)),
        pl.BlockSpec((D_MODEL, tk), lambda i, k: (0, k)),
        pl.BlockSpec((tk, D_MODEL), lambda i, k: (k, 0)),
    ]
    out_specs = [pl.BlockSpec((MLP_TM, D_MODEL), lambda i, k: (i, 0))]
    out_shape = [jax.ShapeDtypeStruct((SEQ, D_MODEL), F32)]
    args = [x2d, gin, gout, w1, w2]
    if narrow is not None:
        nw1, nw2, layer = narrow
        w1_blk = (D_MODEL // ni, D_FF // nk)
        w2_blk = (D_FF // ni, D_MODEL // nk)
        in_specs += [pl.BlockSpec((None,) + w1_blk, lambda i, k: (layer, i, k)),
                     pl.BlockSpec((None,) + w2_blk, lambda i, k: (layer, i, k))]
        out_specs += [pl.BlockSpec(w1_blk, lambda i, k: (i, k)), pl.BlockSpec(w2_blk, lambda i, k: (i, k))]
        out_shape += [jax.ShapeDtypeStruct((D_MODEL, D_FF), BF16),
                      jax.ShapeDtypeStruct((D_FF, D_MODEL), BF16)]
        args += [nw1, nw2]
    return pl.pallas_call(
        _mlp_kernel,
        grid=(ni, nk),
        in_specs=in_specs,
        out_specs=out_specs,
        out_shape=out_shape,
        scratch_shapes=[pltpu.VMEM((MLP_TM, D_MODEL), BF16)],
        compiler_params=_cparams(2),
        name="mlp",
    )(*args)


POOL_TM = 512
POOL_SUB = 128
HALO = 8


def _pool_bands():
    t = np.arange(POOL_SUB)[:, None]
    j = np.arange(POOL_SUB + 2 * HALO)[None, :]
    return np.stack([(np.abs(j - HALO - t) <= w // 2) for w in POOL_WINDOWS]).astype(np.float32)


def _pool_kernel(xp_ref, xm_ref, xn_ref, gin_ref, gout_ref, band_ref, pw_ref, ps_ref, o_ref,
                 hs_ref, hi_ref, lo_ref, m_ref):
    i = pl.program_id(0)
    gin = gin_ref[...]
    hs_ref[0:HALO, :] = jnp.where(i > 0, _rms_rows(xp_ref[...], gin), 0.0)
    hs_ref[HALO:HALO + POOL_TM, :] = _rms_rows(xm_ref[...], gin)
    hs_ref[HALO + POOL_TM:, :] = jnp.where(i < pl.num_programs(0) - 1, _rms_rows(xn_ref[...], gin), 0.0)
    hs = hs_ref[...]
    hi = hs.astype(BF16)
    hi_ref[...] = hi
    lo_ref[...] = (hs - hi.astype(F32)).astype(BF16)
    row = lax.broadcasted_iota(jnp.int32, (POOL_SUB, D_POOL_G), 0) + i * POOL_TM
    for gi, w in enumerate(POOL_WINDOWS):
        r = w // 2
        cols = slice(gi * D_POOL_G, (gi + 1) * D_POOL_G)
        band = band_ref[gi]
        parts = []
        for sb in range(POOL_TM // POOL_SUB):
            win = slice(sb * POOL_SUB, (sb + 1) * POOL_SUB + 2 * HALO)
            s = (jnp.dot(band, hi_ref[win, cols], preferred_element_type=F32)
                 + jnp.dot(band, lo_ref[win, cols], preferred_element_type=F32))
            t = row + sb * POOL_SUB
            cnt = jnp.minimum(t + r + 1, SEQ) - jnp.maximum(t - r, 0)
            u = hs_ref[sb * POOL_SUB + HALO:(sb + 1) * POOL_SUB + HALO, cols]
            parts.append((s / cnt.astype(F32) - u).astype(BF16))
        d = jnp.concatenate(parts, axis=0)
        mg = jnp.dot(d, pw_ref[gi], preferred_element_type=F32)
        m_ref[:, cols] = mg * ps_ref[:, cols]
    o_ref[...] = xm_ref[...] + _rms_rows(m_ref[...], gout_ref[...])


def _pool(x2d, gin, gout, pw, ps):
    nb8 = POOL_TM // HALO
    last8 = SEQ // HALO - 1
    band = jnp.asarray(_pool_bands(), F32).astype(BF16)
    ext = POOL_TM + 2 * HALO
    return pl.pallas_call(
        _pool_kernel,
        grid=(SEQ // POOL_TM,),
        in_specs=[
            pl.BlockSpec((HALO, D_MODEL), lambda i: (jnp.maximum(i * nb8 - 1, 0), 0)),
            pl.BlockSpec((POOL_TM, D_MODEL), lambda i: (i, 0)),
            pl.BlockSpec((HALO, D_MODEL), lambda i: (jnp.minimum((i + 1) * nb8, last8), 0)),
            pl.BlockSpec((1, D_MODEL), lambda i: (0, 0)),
            pl.BlockSpec((1, D_MODEL), lambda i: (0, 0)),
            pl.BlockSpec(band.shape, lambda i: (0, 0, 0)),
            pl.BlockSpec((len(POOL_WINDOWS), D_POOL_G, D_POOL_G), lambda i: (0, 0, 0)),
            pl.BlockSpec((1, D_MODEL), lambda i: (0, 0)),
        ],
        out_specs=pl.BlockSpec((POOL_TM, D_MODEL), lambda i: (i, 0)),
        out_shape=jax.ShapeDtypeStruct((SEQ, D_MODEL), F32),
        scratch_shapes=[pltpu.VMEM((ext, D_MODEL), F32), pltpu.VMEM((ext, D_MODEL), BF16),
                        pltpu.VMEM((ext, D_MODEL), BF16), pltpu.VMEM((POOL_TM, D_MODEL), F32)],
        compiler_params=_cparams(1),
        name="pool_mixer",
    )(x2d, x2d, x2d, gin, gout, band, pw, ps)


def _filter_constants():
    L = SEQ
    t = np.linspace(0.0, 1.0, L)
    w_pos = 2.0 * np.pi * np.arange(L) / L
    bands = np.linspace(1e-4, FILTER_BANDS - 1, FILTER_BANDS)
    ang = w_pos[:, None] * bands[None, :]
    emb = np.concatenate([t[:, None], np.cos(ang), -np.sin(ang)], axis=-1)
    src = np.concatenate([np.arange(L), (L - np.arange(L)) % L])
    emb_t = np.zeros((EMB_PAD, 2 * L))
    emb_t[:EMB_DIM] = emb[src].T
    max_decay = math.log(DECAY_TARGET) / FAST_DECAY_PCT
    min_decay = math.log(DECAY_TARGET) / SLOW_DECAY_PCT
    absdelta = np.abs(np.linspace(min_decay, max_decay, D_HY))[:, None]
    return (jnp.asarray(emb_t, F32), jnp.asarray(t[src][None, :], F32), jnp.asarray(absdelta, F32))


def _mixer_layer0(x2d, g, w_in, hy_short_w, hy_short_b, f_w1, f_b1, f_w2, f_b2, f_w3, f_b3,
                  freq, hy_bias, sc_conv_w, w_out, mlp_w1, mlp_w2, layer):
    emb_t, t_row, absdelta = _filter_constants()
    w1t = jnp.zeros((FILTER_HIDDEN, EMB_PAD), F32).at[:, :EMB_DIM].set(f_w1.T)
    col = lambda v: v[:, None].astype(F32)
    k_q, wt = _filters(emb_t, t_row, w1t, col(f_b1), f_w2.T, col(f_b2), col(freq),
                       f_w3.T.astype(BF16), col(f_b3), absdelta, w_in)

    zt = _in_proj(x2d, g[0][None, :], wt)
    hy_par = jnp.concatenate([hy_short_w.T, hy_short_b[:, None]], axis=1)
    sc_par = jnp.concatenate([sc_conv_w.T, jnp.zeros((D_SC, 1), F32)], axis=1)
    x0_q, u_q, yb_t = _gate(zt, hy_par, sc_par)

    bias2 = jnp.broadcast_to(hy_bias[:, None].astype(F32), (D_HY, FFT_N2))
    ya_q, w1_this, w2_this, wo = _hyena(u_q, k_q, x0_q, bias2, _dft_tables(), mlp_w1, mlp_w2,
                                        layer, w_out)
    x2d = _out_proj(ya_q, yb_t, wo, x2d, g[1][None, :])
    return x2d, (w1_this, w2_this)


def kernel(x, norm_g, mix_w_in, hy_short_w, hy_short_b, hy_filt_w1, hy_filt_b1, hy_filt_w2,
           hy_filt_b2, hy_filt_w3, hy_filt_b3, hy_freq, hy_bias, sc_conv_w, mix_w_out,
           pool_w, pool_scale, mlp_w1, mlp_w2):
    x2d = x.reshape(SEQ, D_MODEL)
    depth = norm_g.shape[0]
    assert depth % 2 == 0, "an even layer narrows its own MLP weights and, in its MLP, the next layer's"
    for i in range(depth):
        g = norm_g[i]
        j = i // 2
        gin, gout = g[2][None, :], g[3][None, :]
        if i % 2 == 0:
            x2d, w_this = _mixer_layer0(
                x2d, g, mix_w_in[j], hy_short_w[j], hy_short_b[j], hy_filt_w1[j], hy_filt_b1[j],
                hy_filt_w2[j], hy_filt_b2[j], hy_filt_w3[j], hy_filt_b3[j], hy_freq[j], hy_bias[j],
                sc_conv_w[j], mix_w_out[j], mlp_w1, mlp_w2, i)
            x2d, *w_next = _mlp(x2d, gin, gout, *w_this, narrow=(mlp_w1, mlp_w2, i + 1))
        else:
            x2d = _pool(x2d, g[0][None, :], g[1][None, :], pool_w[j].astype(BF16),
                        pool_scale[j][None, :])
            x2d, = _mlp(x2d, gin, gout, *w_next)
    return x2d.reshape(x.shape)
```

```python
import math

import numpy as np
import jax
import jax.numpy as jnp
from jax import lax
from jax.experimental import pallas as pl
from jax.experimental.pallas import tpu as pltpu

F32 = jnp.float32
BF16 = jnp.bfloat16

D_MODEL = 2048
SEQ = 8192
D_HY = D_MODEL // 2
D_SC = D_MODEL // 2
D_IN = 3 * D_HY + 3 * D_SC
FILTER_BANDS = 16
EMB_DIM = 1 + 2 * FILTER_BANDS
EMB_PAD = 40
FILTER_HIDDEN = 64
DECAY_TARGET = 1e-2
FAST_DECAY_PCT = 0.3
SLOW_DECAY_PCT = 1.5
POOL_WINDOWS = (2, 4, 8, 16)
D_POOL_G = D_MODEL // len(POOL_WINDOWS)
D_FF = 4 * D_MODEL
NORM_EPS = 1e-6
LANES = 128

FFT_N = 2 * SEQ
FFT_N2 = 256
FFT_N1 = FFT_N // FFT_N2
FFT_N1_LIVE = SEQ // FFT_N2
CH_GROUP = 8
Q8_SHAPE = (D_MODEL // 2 // CH_GROUP, FFT_N1_LIVE, CH_GROUP, FFT_N2)
KQ_SHAPE = (D_MODEL // 2 // CH_GROUP, FFT_N1, CH_GROUP, FFT_N2)

VMEM_LIMIT = 56 * 1024 * 1024


def _cparams(n_axes):
    return pltpu.CompilerParams(
        dimension_semantics=("arbitrary",) * n_axes, vmem_limit_bytes=VMEM_LIMIT)


def _rms_rows(x, g):
    r = lax.rsqrt(jnp.mean(x * x, axis=-1, keepdims=True) + NORM_EPS)
    return x * r * g


IN_TM = 1024
IN_TN = 2048


def _in_proj_kernel(x_ref, g_ref, wt_ref, zt_ref, h_ref):
    @pl.when(pl.program_id(1) == 0)
    def _():
        h_ref[...] = _rms_rows(x_ref[...], g_ref[...]).astype(BF16)

    zt_ref[...] = lax.dot_general(
        wt_ref[...], h_ref[...], (((1,), (1,)), ((), ())), preferred_element_type=F32).astype(BF16)


def _in_proj(x2d, g, wt):
    return pl.pallas_call(
        _in_proj_kernel,
        grid=(SEQ // IN_TM, D_IN // IN_TN),
        in_specs=[
            pl.BlockSpec((IN_TM, D_MODEL), lambda i, j: (i, 0)),
            pl.BlockSpec((1, D_MODEL), lambda i, j: (0, 0)),
            pl.BlockSpec((IN_TN, D_MODEL), lambda i, j: (j, 0)),
        ],
        out_specs=pl.BlockSpec((IN_TN, IN_TM), lambda i, j: (j, i)),
        out_shape=jax.ShapeDtypeStruct((D_IN, SEQ), BF16),
        scratch_shapes=[pltpu.VMEM((IN_TM, D_MODEL), BF16)],
        compiler_params=_cparams(2),
        name="in_proj",
    )(x2d, g, wt)


GATE_CB = 32
GATE_ROWS = 16
GATE_CHUNK = 1024


def _conv3_chunk(zwin, w, off, first, last):
    width = zwin.shape[1]
    zm = pltpu.roll(zwin, 1, 1)[:, off:off + GATE_CHUNK]
    zp = pltpu.roll(zwin, width - 1, 1)[:, off:off + GATE_CHUNK]
    z = zwin[:, off:off + GATE_CHUNK]
    lane = lax.broadcasted_iota(jnp.int32, z.shape, 1)
    if first:
        zm = jnp.where(lane == 0, 0.0, zm)
    if last:
        zp = jnp.where(lane == GATE_CHUNK - 1, 0.0, zp)
    return zm * w[:, 0:1] + z * w[:, 1:2] + zp * w[:, 2:3]


def _conv3_bias_chunks(z_ref, p_ref, rows):
    n_chunks = SEQ // GATE_CHUNK
    p = p_ref[rows, :]
    out = []
    for c in range(n_chunks):
        lo = max(c * GATE_CHUNK - LANES, 0)
        hi = min((c + 1) * GATE_CHUNK + LANES, SEQ)
        zwin = z_ref[rows, lo:hi].astype(F32)
        out.append(_conv3_chunk(zwin, p, c * GATE_CHUNK - lo, c == 0, c == n_chunks - 1) + p[:, 3:4])
    return out


def _to_q_groups(chunks):
    groups = []
    for g in range(GATE_ROWS // CH_GROUP):
        slabs = [ch[g * CH_GROUP:(g + 1) * CH_GROUP, a * FFT_N2:(a + 1) * FFT_N2]
                 for ch in chunks for a in range(GATE_CHUNK // FFT_N2)]
        groups.append(jnp.stack(slabs, axis=0))
    return groups


def _gate_kernel(zgb_ref, zgc_ref, zxv_ref, psc_ref, yb_ref):
    n_chunks = SEQ // GATE_CHUNK

    def row_tile(rt, carry):
        rows = pl.ds(pl.multiple_of(rt * GATE_ROWS, GATE_ROWS), GATE_ROWS)
        psc = psc_ref[rows, :]
        for c in range(n_chunks):
            lo = max(c * GATE_CHUNK - LANES, 0)
            hi = min((c + 1) * GATE_CHUNK + LANES, SEQ)
            cur = slice(c * GATE_CHUNK, (c + 1) * GATE_CHUNK)
            p = zgc_ref[rows, lo:hi].astype(F32) * zxv_ref[rows, lo:hi].astype(F32)
            conv = _conv3_chunk(p, psc, c * GATE_CHUNK - lo, c == 0, c == n_chunks - 1)
            yb_ref[rows, cur] = (zgb_ref[rows, cur].astype(F32) * conv).astype(BF16)
        return carry

    lax.fori_loop(0, GATE_CB // GATE_ROWS, row_tile, 0)


def _gate(zt, sc_par):
    nb = D_SC // GATE_CB
    zspec = lambda s: pl.BlockSpec((GATE_CB, SEQ), lambda j, s=s: (s * nb + j, 0))
    return pl.pallas_call(
        _gate_kernel,
        grid=(nb,),
        in_specs=[zspec(3), zspec(4), zspec(5), pl.BlockSpec((GATE_CB, 4), lambda j: (j, 0))],
        out_specs=pl.BlockSpec((GATE_CB, SEQ), lambda j: (j, 0)),
        out_shape=jax.ShapeDtypeStruct((D_SC, SEQ), BF16),
        compiler_params=_cparams(1),
        name="gate",
    )(zt, zt, zt, sc_par)


FILT_TB = 1024


def _filter_kernel(emb_ref, t_ref, w1_ref, b1_ref, w2_ref, b2_ref, fr_ref, w3_ref, b3_ref,
                   ad_ref, win_ref, k_ref, wt_ref, ks_ref):
    wt_ref[...] = win_ref[...].T.astype(BF16)
    hi = lax.Precision.HIGHEST
    fr = fr_ref[...]
    a1 = jnp.dot(w1_ref[...], emb_ref[...], precision=hi, preferred_element_type=F32) + b1_ref[...]
    h1 = jnp.sin(fr * a1)
    a2 = jnp.dot(w2_ref[...], h1, precision=hi, preferred_element_type=F32) + b2_ref[...]
    h2 = jnp.sin(fr * a2)
    o = jnp.dot(w3_ref[...], h2.astype(BF16), preferred_element_type=F32) + b3_ref[...]
    window = jnp.exp(-(ad_ref[...] * t_ref[...]))
    pos = lax.broadcasted_iota(jnp.int32, (D_HY, FILT_TB), 1) + pl.program_id(0) * FILT_TB
    ks_ref[...] = jnp.where(pos == SEQ, 0.0, o * window)

    def to_q(cg, carry):
        rows = pl.ds(pl.multiple_of(cg * CH_GROUP, CH_GROUP), CH_GROUP)
        for n1 in range(FILT_TB // FFT_N2):
            k_ref[cg, n1] = ks_ref[rows, n1 * FFT_N2:(n1 + 1) * FFT_N2]
        return carry

    lax.fori_loop(0, D_HY // CH_GROUP, to_q, 0)


def _filters(emb_t, t_row, w1t, b1, w2t, b2, fr, w3t, b3, absdelta, w_in):
    full = lambda a: pl.BlockSpec(a.shape, lambda i: (0,) * a.ndim)
    steps = FFT_N // FILT_TB
    half = steps // 2
    return pl.pallas_call(
        _filter_kernel,
        grid=(steps,),
        in_specs=[pl.BlockSpec((EMB_PAD, FILT_TB), lambda i: (0, i)),
                  pl.BlockSpec((1, FILT_TB), lambda i: (0, i)),
                  full(w1t), full(b1), full(w2t), full(b2), full(fr),
                  pl.BlockSpec((D_HY, FILTER_HIDDEN), lambda i: (i // half, 0)),
                  pl.BlockSpec((D_HY, 1), lambda i: (i // half, 0)),
                  full(absdelta),
                  pl.BlockSpec((D_MODEL, D_IN // steps), lambda i: (0, i))],
        out_specs=[pl.BlockSpec((KQ_SHAPE[0], FILT_TB // FFT_N2, CH_GROUP, FFT_N2), lambda i: (0, i, 0, 0)),
                   pl.BlockSpec((D_IN // steps, D_MODEL), lambda i: (i, 0))],
        out_shape=[jax.ShapeDtypeStruct(KQ_SHAPE, F32), jax.ShapeDtypeStruct((D_IN, D_MODEL), BF16)],
        scratch_shapes=[pltpu.VMEM((D_HY, FILT_TB), F32)],
        compiler_params=_cparams(1),
        name="filters",
    )(emb_t, t_row, w1t, b1, w2t, b2, fr, w3t, b3, absdelta, w_in)


HY_CB = 32
GROUP_ROWS = CH_GROUP * FFT_N1_LIVE
N_K1 = FFT_N1 // 2 + 1
RE_ROWS = N_K1 * CH_GROUP
IM_ROWS = (N_K1 - 2) * CH_GROUP


def _dft_tables():
    n1 = np.arange(FFT_N1_LIVE, dtype=np.float64)
    k1 = np.arange(N_K1, dtype=np.float64)
    th = 2.0 * np.pi * np.outer(k1, n1) / FFT_N1
    eye = np.eye(CH_GROUP)
    def stage1(cos, sin):
        return np.concatenate([np.einsum("kn,cd->kcnd", cos, eye).reshape(RE_ROWS, -1),
                               np.einsum("kn,cd->kcnd", -sin[1:-1], eye).reshape(IM_ROWS, -1)])
    bd1 = stage1(np.cos(th), np.sin(th))
    thk = 2.0 * np.pi * np.outer(k1, np.arange(FFT_N1, dtype=np.float64)) / FFT_N1
    bd1k = stage1(np.cos(thk), np.sin(thk))
    wgt = np.where((k1 == 0) | (k1 == FFT_N1 // 2), 1.0, 2.0)[:, None] / FFT_N
    bd1inv = np.concatenate(
        [np.einsum("kn,cd->nckd", wgt * np.cos(th), eye).reshape(GROUP_ROWS, RE_ROWS),
         np.einsum("kn,cd->nckd", (-wgt * np.sin(th))[1:-1], eye).reshape(GROUP_ROWS, IM_ROWS)], axis=1)
    n2 = np.arange(FFT_N2, dtype=np.float64)
    ph = 2.0 * np.pi * np.outer(k1, n2) / FFT_N
    twr = np.repeat(np.cos(ph), CH_GROUP, axis=0)
    twi = np.repeat(-np.sin(ph), CH_GROUP, axis=0)
    ps = 2.0 * np.pi * np.outer(n2, n2) / FFT_N2
    cr, ci = np.cos(ps), -np.sin(ps)
    w3 = np.block([[cr, ci], [-ci, cr]])
    w3inv = np.block([[cr, -ci], [ci, cr]])
    f32 = lambda a: jnp.asarray(a, F32)
    return (f32(bd1).astype(BF16), f32(bd1k).astype(BF16), f32(bd1inv).astype(BF16), f32(twr), f32(twi),
            f32(w3).astype(BF16), f32(w3inv).astype(BF16))


def _hyena_kernel(zx0_ref, zx1_ref, zv_ref, px0_ref, px1_ref, pv_ref, k_ref, bias_ref,
                  bd1_ref, bd1k_ref, bd1inv_ref, twr_ref, twi_ref, w3_ref, w3inv_ref,
                  w1_ref, w2_ref, wo_ref, ya_ref, w1b_ref, w2b_ref, wob_ref):
    w1b_ref[...] = w1_ref[...].astype(BF16)
    w2b_ref[...] = w2_ref[...].astype(BF16)
    wob_ref[...] = wo_ref[...].astype(BF16)
    twr, twi = twr_ref[...], twi_ref[...]
    zrow = jnp.zeros((CH_GROUP, FFT_N2), F32)

    def stage1(xq, bd_ref):
        xq = xq.reshape(xq.shape[0] * CH_GROUP, FFT_N2).astype(BF16)
        a = jnp.dot(bd_ref[...], xq, preferred_element_type=F32)
        ar = a[:RE_ROWS]
        ai = jnp.concatenate([zrow, a[RE_ROWS:], zrow], axis=0)
        br = ar * twr - ai * twi
        bi = ar * twi + ai * twr
        return jnp.concatenate([br, bi], axis=1)

    def pair(pi):
        gs = [2 * pi, 2 * pi + 1]
        rows = slice(pi * GATE_ROWS, (pi + 1) * GATE_ROWS)
        x1c = _conv3_bias_chunks(zx1_ref, px1_ref, rows)
        vc = _conv3_bias_chunks(zv_ref, pv_ref, rows)
        us = _to_q_groups([v * x1 for v, x1 in zip(vc, x1c)])
        x0s = _to_q_groups(_conv3_bias_chunks(zx0_ref, px0_ref, rows))
        b_all = jnp.concatenate(
            [blk for g, u in zip(gs, us) for blk in (stage1(u, bd1_ref), stage1(k_ref[g], bd1k_ref))], axis=0)
        s_all = jnp.dot(b_all.astype(BF16), w3_ref[...], preferred_element_type=F32)
        ycats = []
        for i in range(2):
            xs = s_all[(2 * i) * RE_ROWS:(2 * i + 1) * RE_ROWS]
            ks = s_all[(2 * i + 1) * RE_ROWS:(2 * i + 2) * RE_ROWS]
            xr, xi = xs[:, :FFT_N2], xs[:, FFT_N2:]
            kr, ki = ks[:, :FFT_N2], ks[:, FFT_N2:]
            ycats.append(jnp.concatenate([xr * kr - xi * ki, xr * ki + xi * kr], axis=1))
        ap_all = jnp.dot(jnp.concatenate(ycats, axis=0).astype(BF16), w3inv_ref[...],
                         preferred_element_type=F32)
        for i, (g, u) in enumerate(zip(gs, us)):
            ap = ap_all[i * RE_ROWS:(i + 1) * RE_ROWS]
            apr, api = ap[:, :FFT_N2], ap[:, FFT_N2:]
            bpr = apr * twr + api * twi
            bpi = api * twr - apr * twi
            b2 = jnp.concatenate([bpr, bpi[CH_GROUP:RE_ROWS - CH_GROUP]], axis=0).astype(BF16)
            y = jnp.dot(bd1inv_ref[...], b2, preferred_element_type=F32)
            y3 = y.reshape(FFT_N1_LIVE, CH_GROUP, FFT_N2)
            bias = bias_ref[g * CH_GROUP:(g + 1) * CH_GROUP, :]
            ya_ref[g] = x0s[i] * (y3 + bias[None] * u)

    for pi in range(HY_CB // GATE_ROWS):
        pair(pi)


def _hyena(zt, hy_par, k_q, bias2, tables, mlp_w1, mlp_w2, layer, w_out):
    bd1, bd1k, bd1inv, twr, twi, w3, w3inv = tables
    steps = D_HY // HY_CB
    zspec = lambda s: pl.BlockSpec((HY_CB, SEQ), lambda j, s=s: (s * steps + j, 0))
    pspec = lambda s: pl.BlockSpec((HY_CB, 4), lambda j, s=s: (s * steps + j, 0))
    qspec = pl.BlockSpec((HY_CB // CH_GROUP,) + Q8_SHAPE[1:], lambda j: (j, 0, 0, 0))
    kspec = pl.BlockSpec((HY_CB // CH_GROUP,) + KQ_SHAPE[1:], lambda j: (j, 0, 0, 0))
    full = lambda a: pl.BlockSpec(a.shape, lambda j: (0,) * a.ndim)
    w1_blk = (D_MODEL // steps, D_FF)
    w2_blk = (D_FF // steps, D_MODEL)
    wo_blk = (D_MODEL // steps, D_MODEL)
    return pl.pallas_call(
        _hyena_kernel,
        grid=(steps,),
        in_specs=[zspec(0), zspec(1), zspec(2), pspec(0), pspec(1), pspec(2), kspec,
                  pl.BlockSpec((HY_CB, FFT_N2), lambda j: (j, 0)),
                  full(bd1), full(bd1k), full(bd1inv), full(twr), full(twi), full(w3), full(w3inv),
                  pl.BlockSpec((None,) + w1_blk, lambda j: (layer, j, 0)),
                  pl.BlockSpec((None,) + w2_blk, lambda j: (layer, j, 0)),
                  pl.BlockSpec(wo_blk, lambda j: (j, 0))],
        out_specs=[qspec, pl.BlockSpec(w1_blk, lambda j: (j, 0)), pl.BlockSpec(w2_blk, lambda j: (j, 0)),
                   pl.BlockSpec(wo_blk, lambda j: (j, 0))],
        out_shape=[jax.ShapeDtypeStruct(Q8_SHAPE, F32),
                   jax.ShapeDtypeStruct((D_MODEL, D_FF), BF16),
                   jax.ShapeDtypeStruct((D_FF, D_MODEL), BF16),
                   jax.ShapeDtypeStruct((D_MODEL, D_MODEL), BF16)],
        compiler_params=_cparams(1),
        name="hyena_fftconv",
    )(zt, zt, zt, hy_par, hy_par, hy_par, k_q, bias2, bd1, bd1k, bd1inv, twr, twi, w3, w3inv,
      mlp_w1, mlp_w2, w_out)


OUT_TM = 512


def _out_proj_kernel(ya_ref, yb_ref, wa_ref, wb_ref, x_ref, g_ref, o_ref):
    tn = (((0,), (0,)), ((), ()))
    ya = jnp.concatenate([ya_ref[:, a].reshape(D_HY, FFT_N2) for a in range(OUT_TM // FFT_N2)],
                         axis=1).astype(BF16)
    m = lax.dot_general(ya, wa_ref[...], tn, preferred_element_type=F32)
    m = m + lax.dot_general(yb_ref[...], wb_ref[...], tn, preferred_element_type=F32)
    o_ref[...] = x_ref[...] + _rms_rows(m, g_ref[...])


def _out_proj(ya_q, yb_t, wo, x2d, g):
    return pl.pallas_call(
        _out_proj_kernel,
        grid=(SEQ // OUT_TM,),
        in_specs=[
            pl.BlockSpec((Q8_SHAPE[0], OUT_TM // FFT_N2, CH_GROUP, FFT_N2), lambda i: (0, i, 0, 0)),
            pl.BlockSpec((D_SC, OUT_TM), lambda i: (0, i)),
            pl.BlockSpec((D_HY, D_MODEL), lambda i: (0, 0)),
            pl.BlockSpec((D_SC, D_MODEL), lambda i: (1, 0)),
            pl.BlockSpec((OUT_TM, D_MODEL), lambda i: (i, 0)),
            pl.BlockSpec((1, D_MODEL), lambda i: (0, 0)),
        ],
        out_specs=pl.BlockSpec((OUT_TM, D_MODEL), lambda i: (i, 0)),
        out_shape=jax.ShapeDtypeStruct((SEQ, D_MODEL), F32),
        compiler_params=_cparams(1),
        name="out_proj",
    )(ya_q, yb_t, wo, wo, x2d, g)


MLP_TM = 1024
MLP_TK = 1024
MLP_TA = 512
MLP_TN = 512
MLP_ROWS = 128


def _mlp_kernel(x_ref, gin_ref, gout_ref, w1_ref, w2_ref, *rest):
    if len(rest) == 6:
        nw1_ref, nw2_ref, o_ref, nw1b_ref, nw2b_ref, h_ref = rest
        nw1b_ref[...] = nw1_ref[...].astype(BF16)
        nw2b_ref[...] = nw2_ref[...].astype(BF16)
    else:
        o_ref, h_ref = rest
    k = pl.program_id(1)

    def row_chunks(fn):
        def body(c, carry):
            fn(pl.ds(pl.multiple_of(c * MLP_ROWS, MLP_ROWS), MLP_ROWS))
            return carry
        lax.fori_loop(0, MLP_TM // MLP_ROWS, body, 0)

    @pl.when(k == 0)
    def _():
        def prologue(rows):
            h_ref[rows, :] = _rms_rows(x_ref[rows, :], gin_ref[...]).astype(BF16)
            o_ref[rows, :] = jnp.zeros((MLP_ROWS, D_MODEL), F32)
        row_chunks(prologue)

    for c in range(w1_ref.shape[1] // MLP_TA):
        mid = slice(c * MLP_TA, (c + 1) * MLP_TA)
        a = jnp.dot(h_ref[...], w1_ref[:, mid], preferred_element_type=F32)
        a = jnp.square(jnp.maximum(a, 0.0)).astype(BF16)
        for n in range(D_MODEL // MLP_TN):
            cols = slice(n * MLP_TN, (n + 1) * MLP_TN)
            o_ref[:, cols] += jnp.dot(a, w2_ref[mid, cols], preferred_element_type=F32)

    @pl.when(k == pl.num_programs(1) - 1)
    def _():
        def epilogue(rows):
            o_ref[rows, :] = x_ref[rows, :] + _rms_rows(o_ref[rows, :], gout_ref[...])
        row_chunks(epilogue)


def _mlp(x2d, gin, gout, w1, w2, narrow=None):
    tk = MLP_TK if narrow is None else MLP_TK // 2
    ni, nk = SEQ // MLP_TM, D_FF // tk
    in_specs = [
        pl.BlockSpec((MLP_TM, D_MODEL), lambda i, k: (i, 0)),
        pl.BlockSpec((1, D_MODEL), lambda i, k: (0, 0)),
        pl.BlockSpec((1, D_MODEL), lambda i, k: (0, 0)),
        pl.BlockSpec((D_MODEL, tk), lambda i, k: (0, k)),
        pl.BlockSpec((tk, D_MODEL), lambda i, k: (k, 0)),
    ]
    out_specs = [pl.BlockSpec((MLP_TM, D_MODEL), lambda i, k: (i, 0))]
    out_shape = [jax.ShapeDtypeStruct((SEQ, D_MODEL), F32)]
    args = [x2d, gin, gout, w1, w2]
    if narrow is not None:
        nw1, nw2, layer = narrow
        w1_blk = (D_MODEL // ni, D_FF // nk)
        w2_blk = (D_FF // ni, D_MODEL // nk)
        in_specs += [pl.BlockSpec((None,) + w1_blk, lambda i, k: (layer, i, k)),
                     pl.BlockSpec((None,) + w2_blk, lambda i, k: (layer, i, k))]
        out_specs += [pl.BlockSpec(w1_blk, lambda i, k: (i, k)), pl.BlockSpec(w2_blk, lambda i, k: (i, k))]
        out_shape += [jax.ShapeDtypeStruct((D_MODEL, D_FF), BF16),
                      jax.ShapeDtypeStruct((D_FF, D_MODEL), BF16)]
        args += [nw1, nw2]
    return pl.pallas_call(
        _mlp_kernel,
        grid=(ni, nk),
        in_specs=in_specs,
        out_specs=out_specs,
        out_shape=out_shape,
        scratch_shapes=[pltpu.VMEM((MLP_TM, D_MODEL), BF16)],
        compiler_params=_cparams(2),
        name="mlp",
    )(*args)


POOL_TM = 512
POOL_SUB = 128
HALO = 8


def _pool_bands():
    t = np.arange(POOL_SUB)[:, None]
    j = np.arange(POOL_SUB + 2 * HALO)[None, :]
    return np.stack([(np.abs(j - HALO - t) <= w // 2) for w in POOL_WINDOWS]).astype(np.float32)


def _pool_kernel(xp_ref, xm_ref, xn_ref, gin_ref, gout_ref, band_ref, pw_ref, ps_ref, o_ref,
                 hs_ref, hi_ref, lo_ref, m_ref):
    i = pl.program_id(0)
    gin = gin_ref[...]
    hs_ref[0:HALO, :] = jnp.where(i > 0, _rms_rows(xp_ref[...], gin), 0.0)
    hs_ref[HALO:HALO + POOL_TM, :] = _rms_rows(xm_ref[...], gin)
    hs_ref[HALO + POOL_TM:, :] = jnp.where(i < pl.num_programs(0) - 1, _rms_rows(xn_ref[...], gin), 0.0)
    hs = hs_ref[...]
    hi = hs.astype(BF16)
    hi_ref[...] = hi
    lo_ref[...] = (hs - hi.astype(F32)).astype(BF16)
    row = lax.broadcasted_iota(jnp.int32, (POOL_SUB, D_POOL_G), 0) + i * POOL_TM
    for gi, w in enumerate(POOL_WINDOWS):
        r = w // 2
        cols = slice(gi * D_POOL_G, (gi + 1) * D_POOL_G)
        band = band_ref[gi]
        parts = []
        for sb in range(POOL_TM // POOL_SUB):
            win = slice(sb * POOL_SUB, (sb + 1) * POOL_SUB + 2 * HALO)
            s = (jnp.dot(band, hi_ref[win, cols], preferred_element_type=F32)
                 + jnp.dot(band, lo_ref[win, cols], preferred_element_type=F32))
            t = row + sb * POOL_SUB
            cnt = jnp.minimum(t + r + 1, SEQ) - jnp.maximum(t - r, 0)
            u = hs_ref[sb * POOL_SUB + HALO:(sb + 1) * POOL_SUB + HALO, cols]
            parts.append((s / cnt.astype(F32) - u).astype(BF16))
        d = jnp.concatenate(parts, axis=0)
        mg = jnp.dot(d, pw_ref[gi], preferred_element_type=F32)
        m_ref[:, cols] = mg * ps_ref[:, cols]
    o_ref[...] = xm_ref[...] + _rms_rows(m_ref[...], gout_ref[...])


def _pool(x2d, gin, gout, pw, ps):
    nb8 = POOL_TM // HALO
    last8 = SEQ // HALO - 1
    band = jnp.asarray(_pool_bands(), F32).astype(BF16)
    ext = POOL_TM + 2 * HALO
    return pl.pallas_call(
        _pool_kernel,
        grid=(SEQ // POOL_TM,),
        in_specs=[
            pl.BlockSpec((HALO, D_MODEL), lambda i: (jnp.maximum(i * nb8 - 1, 0), 0)),
            pl.BlockSpec((POOL_TM, D_MODEL), lambda i: (i, 0)),
            pl.BlockSpec((HALO, D_MODEL), lambda i: (jnp.minimum((i + 1) * nb8, last8), 0)),
            pl.BlockSpec((1, D_MODEL), lambda i: (0, 0)),
            pl.BlockSpec((1, D_MODEL), lambda i: (0, 0)),
            pl.BlockSpec(band.shape, lambda i: (0, 0, 0)),
            pl.BlockSpec((len(POOL_WINDOWS), D_POOL_G, D_POOL_G), lambda i: (0, 0, 0)),
            pl.BlockSpec((1, D_MODEL), lambda i: (0, 0)),
        ],
        out_specs=pl.BlockSpec((POOL_TM, D_MODEL), lambda i: (i, 0)),
        out_shape=jax.ShapeDtypeStruct((SEQ, D_MODEL), F32),
        scratch_shapes=[pltpu.VMEM((ext, D_MODEL), F32), pltpu.VMEM((ext, D_MODEL), BF16),
                        pltpu.VMEM((ext, D_MODEL), BF16), pltpu.VMEM((POOL_TM, D_MODEL), F32)],
        compiler_params=_cparams(1),
        name="pool_mixer",
    )(x2d, x2d, x2d, gin, gout, band, pw, ps)


def _filter_constants():
    L = SEQ
    t = np.linspace(0.0, 1.0, L)
    w_pos = 2.0 * np.pi * np.arange(L) / L
    bands = np.linspace(1e-4, FILTER_BANDS - 1, FILTER_BANDS)
    ang = w_pos[:, None] * bands[None, :]
    emb = np.concatenate([t[:, None], np.cos(ang), -np.sin(ang)], axis=-1)
    src = np.concatenate([np.arange(L), (L - np.arange(L)) % L])
    emb_t = np.zeros((EMB_PAD, 2 * L))
    emb_t[:EMB_DIM] = emb[src].T
    max_decay = math.log(DECAY_TARGET) / FAST_DECAY_PCT
    min_decay = math.log(DECAY_TARGET) / SLOW_DECAY_PCT
    absdelta = np.abs(np.linspace(min_decay, max_decay, D_HY))[:, None]
    return (jnp.asarray(emb_t, F32), jnp.asarray(t[src][None, :], F32), jnp.asarray(absdelta, F32))


def _mixer_layer0(x2d, g, w_in, hy_short_w, hy_short_b, f_w1, f_b1, f_w2, f_b2, f_w3, f_b3,
                  freq, hy_bias, sc_conv_w, w_out, mlp_w1, mlp_w2, layer):
    emb_t, t_row, absdelta = _filter_constants()
    w1t = jnp.zeros((FILTER_HIDDEN, EMB_PAD), F32).at[:, :EMB_DIM].set(f_w1.T)
    col = lambda v: v[:, None].astype(F32)
    k_q, wt = _filters(emb_t, t_row, w1t, col(f_b1), f_w2.T, col(f_b2), col(freq),
                       f_w3.T.astype(BF16), col(f_b3), absdelta, w_in)

    zt = _in_proj(x2d, g[0][None, :], wt)
    hy_par = jnp.concatenate([hy_short_w.T, hy_short_b[:, None]], axis=1)
    sc_par = jnp.concatenate([sc_conv_w.T, jnp.zeros((D_SC, 1), F32)], axis=1)
    yb_t = _gate(zt, sc_par)

    bias2 = jnp.broadcast_to(hy_bias[:, None].astype(F32), (D_HY, FFT_N2))
    ya_q, w1_this, w2_this, wo = _hyena(zt, hy_par, k_q, bias2, _dft_tables(), mlp_w1, mlp_w2,
                                        layer, w_out)
    x2d = _out_proj(ya_q, yb_t, wo, x2d, g[1][None, :])
    return x2d, (w1_this, w2_this)


def kernel(x, norm_g, mix_w_in, hy_short_w, hy_short_b, hy_filt_w1, hy_filt_b1, hy_filt_w2,
           hy_filt_b2, hy_filt_w3, hy_filt_b3, hy_freq, hy_bias, sc_conv_w, mix_w_out,
           pool_w, pool_scale, mlp_w1, mlp_w2):
    x2d = x.reshape(SEQ, D_MODEL)
    depth = norm_g.shape[0]
    assert depth % 2 == 0, "an even layer narrows its own MLP weights and, in its MLP, the next layer's"
    for i in range(depth):
        g = norm_g[i]
        j = i // 2
        gin, gout = g[2][None, :], g[3][None, :]
        if i % 2 == 0:
            x2d, w_this = _mixer_layer0(
                x2d, g, mix_w_in[j], hy_short_w[j], hy_short_b[j], hy_filt_w1[j], hy_filt_b1[j],
                hy_filt_w2[j], hy_filt_b2[j], hy_filt_w3[j], hy_filt_b3[j], hy_freq[j], hy_bias[j],
                sc_conv_w[j], mix_w_out[j], mlp_w1, mlp_w2, i)
            x2d, *w_next = _mlp(x2d, gin, gout, *w_this, narrow=(mlp_w1, mlp_w2, i + 1))
        else:
            x2d = _pool(x2d, g[0][None, :], g[1][None, :], pool_w[j].astype(BF16),
                        pool_scale[j][None, :])
            x2d, = _mlp(x2d, gin, gout, *w_next)
    return x2d.reshape(x.shape)
```

```python
import math

import numpy as np
import jax
import jax.numpy as jnp
from jax import lax
from jax.experimental import pallas as pl
from jax.experimental.pallas import tpu as pltpu

F32 = jnp.float32
BF16 = jnp.bfloat16

D_MODEL = 2048
SEQ = 8192
D_HY = D_MODEL // 2
D_SC = D_MODEL // 2
D_IN = 3 * D_HY + 3 * D_SC
FILTER_BANDS = 16
EMB_DIM = 1 + 2 * FILTER_BANDS
EMB_PAD = 40
FILTER_HIDDEN = 64
DECAY_TARGET = 1e-2
FAST_DECAY_PCT = 0.3
SLOW_DECAY_PCT = 1.5
POOL_WINDOWS = (2, 4, 8, 16)
D_POOL_G = D_MODEL // len(POOL_WINDOWS)
D_FF = 4 * D_MODEL
NORM_EPS = 1e-6
LANES = 128

FFT_N = 2 * SEQ
FFT_N2 = 256
FFT_N1 = FFT_N // FFT_N2
FFT_N1_LIVE = SEQ // FFT_N2
CH_GROUP = 8
Q8_SHAPE = (D_MODEL // 2 // CH_GROUP, FFT_N1_LIVE, CH_GROUP, FFT_N2)
KQ_SHAPE = (D_MODEL // 2 // CH_GROUP, FFT_N1, CH_GROUP, FFT_N2)

VMEM_LIMIT = 56 * 1024 * 1024


def _cparams(n_axes):
    return pltpu.CompilerParams(
        dimension_semantics=("arbitrary",) * n_axes, vmem_limit_bytes=VMEM_LIMIT)


def _rms_rows(x, g):
    r = lax.rsqrt(jnp.mean(x * x, axis=-1, keepdims=True) + NORM_EPS)
    return x * r * g


IN_TM = 1024
IN_TN = 2048


def _in_proj_kernel(x_ref, g_ref, wt_ref, zt_ref, h_ref):
    @pl.when(pl.program_id(1) == 0)
    def _():
        h_ref[...] = _rms_rows(x_ref[...], g_ref[...]).astype(BF16)

    zt_ref[...] = lax.dot_general(
        wt_ref[...], h_ref[...], (((1,), (1,)), ((), ())), preferred_element_type=F32).astype(BF16)


def _in_proj(x2d, g, wt):
    return pl.pallas_call(
        _in_proj_kernel,
        grid=(SEQ // IN_TM, D_IN // IN_TN),
        in_specs=[
            pl.BlockSpec((IN_TM, D_MODEL), lambda i, j: (i, 0)),
            pl.BlockSpec((1, D_MODEL), lambda i, j: (0, 0)),
            pl.BlockSpec((IN_TN, D_MODEL), lambda i, j: (j, 0)),
        ],
        out_specs=pl.BlockSpec((IN_TN, IN_TM), lambda i, j: (j, i)),
        out_shape=jax.ShapeDtypeStruct((D_IN, SEQ), BF16),
        scratch_shapes=[pltpu.VMEM((IN_TM, D_MODEL), BF16)],
        compiler_params=_cparams(2),
        name="in_proj",
    )(x2d, g, wt)


GATE_ROWS = 16
GATE_CHUNK = 1024


def _conv3_chunk(zwin, w, off, first, last):
    width = zwin.shape[1]
    zm = pltpu.roll(zwin, 1, 1)[:, off:off + GATE_CHUNK]
    zp = pltpu.roll(zwin, width - 1, 1)[:, off:off + GATE_CHUNK]
    z = zwin[:, off:off + GATE_CHUNK]
    lane = lax.broadcasted_iota(jnp.int32, z.shape, 1)
    if first:
        zm = jnp.where(lane == 0, 0.0, zm)
    if last:
        zp = jnp.where(lane == GATE_CHUNK - 1, 0.0, zp)
    return zm * w[:, 0:1] + z * w[:, 1:2] + zp * w[:, 2:3]


def _conv3_bias_chunks(z_ref, p_ref, rows):
    n_chunks = SEQ // GATE_CHUNK
    p = p_ref[rows, :]
    out = []
    for c in range(n_chunks):
        lo = max(c * GATE_CHUNK - LANES, 0)
        hi = min((c + 1) * GATE_CHUNK + LANES, SEQ)
        zwin = z_ref[rows, lo:hi].astype(F32)
        out.append(_conv3_chunk(zwin, p, c * GATE_CHUNK - lo, c == 0, c == n_chunks - 1) + p[:, 3:4])
    return out


def _to_q_groups(chunks):
    groups = []
    for g in range(GATE_ROWS // CH_GROUP):
        slabs = [ch[g * CH_GROUP:(g + 1) * CH_GROUP, a * FFT_N2:(a + 1) * FFT_N2]
                 for ch in chunks for a in range(GATE_CHUNK // FFT_N2)]
        groups.append(jnp.stack(slabs, axis=0))
    return groups


FILT_TB = 1024


def _filter_kernel(emb_ref, t_ref, w1_ref, b1_ref, w2_ref, b2_ref, fr_ref, w3_ref, b3_ref,
                   ad_ref, win_ref, k_ref, wt_ref, ks_ref):
    wt_ref[...] = win_ref[...].T.astype(BF16)
    hi = lax.Precision.HIGHEST
    fr = fr_ref[...]
    a1 = jnp.dot(w1_ref[...], emb_ref[...], precision=hi, preferred_element_type=F32) + b1_ref[...]
    h1 = jnp.sin(fr * a1)
    a2 = jnp.dot(w2_ref[...], h1, precision=hi, preferred_element_type=F32) + b2_ref[...]
    h2 = jnp.sin(fr * a2)
    o = jnp.dot(w3_ref[...], h2.astype(BF16), preferred_element_type=F32) + b3_ref[...]
    window = jnp.exp(-(ad_ref[...] * t_ref[...]))
    pos = lax.broadcasted_iota(jnp.int32, (D_HY, FILT_TB), 1) + pl.program_id(0) * FILT_TB
    ks_ref[...] = jnp.where(pos == SEQ, 0.0, o * window)

    def to_q(cg, carry):
        rows = pl.ds(pl.multiple_of(cg * CH_GROUP, CH_GROUP), CH_GROUP)
        for n1 in range(FILT_TB // FFT_N2):
            k_ref[cg, n1] = ks_ref[rows, n1 * FFT_N2:(n1 + 1) * FFT_N2]
        return carry

    lax.fori_loop(0, D_HY // CH_GROUP, to_q, 0)


def _filters(emb_t, t_row, w1t, b1, w2t, b2, fr, w3t, b3, absdelta, w_in):
    full = lambda a: pl.BlockSpec(a.shape, lambda i: (0,) * a.ndim)
    steps = FFT_N // FILT_TB
    half = steps // 2
    return pl.pallas_call(
        _filter_kernel,
        grid=(steps,),
        in_specs=[pl.BlockSpec((EMB_PAD, FILT_TB), lambda i: (0, i)),
                  pl.BlockSpec((1, FILT_TB), lambda i: (0, i)),
                  full(w1t), full(b1), full(w2t), full(b2), full(fr),
                  pl.BlockSpec((D_HY, FILTER_HIDDEN), lambda i: (i // half, 0)),
                  pl.BlockSpec((D_HY, 1), lambda i: (i // half, 0)),
                  full(absdelta),
                  pl.BlockSpec((D_MODEL, D_IN // steps), lambda i: (0, i))],
        out_specs=[pl.BlockSpec((KQ_SHAPE[0], FILT_TB // FFT_N2, CH_GROUP, FFT_N2), lambda i: (0, i, 0, 0)),
                   pl.BlockSpec((D_IN // steps, D_MODEL), lambda i: (i, 0))],
        out_shape=[jax.ShapeDtypeStruct(KQ_SHAPE, F32), jax.ShapeDtypeStruct((D_IN, D_MODEL), BF16)],
        scratch_shapes=[pltpu.VMEM((D_HY, FILT_TB), F32)],
        compiler_params=_cparams(1),
        name="filters",
    )(emb_t, t_row, w1t, b1, w2t, b2, fr, w3t, b3, absdelta, w_in)


HY_CB = 32
GROUP_ROWS = CH_GROUP * FFT_N1_LIVE
N_K1 = FFT_N1 // 2 + 1
RE_ROWS = N_K1 * CH_GROUP
IM_ROWS = (N_K1 - 2) * CH_GROUP


def _dft_tables():
    n1 = np.arange(FFT_N1_LIVE, dtype=np.float64)
    k1 = np.arange(N_K1, dtype=np.float64)
    th = 2.0 * np.pi * np.outer(k1, n1) / FFT_N1
    eye = np.eye(CH_GROUP)
    def stage1(cos, sin):
        return np.concatenate([np.einsum("kn,cd->kcnd", cos, eye).reshape(RE_ROWS, -1),
                               np.einsum("kn,cd->kcnd", -sin[1:-1], eye).reshape(IM_ROWS, -1)])
    bd1 = stage1(np.cos(th), np.sin(th))
    thk = 2.0 * np.pi * np.outer(k1, np.arange(FFT_N1, dtype=np.float64)) / FFT_N1
    bd1k = stage1(np.cos(thk), np.sin(thk))
    wgt = np.where((k1 == 0) | (k1 == FFT_N1 // 2), 1.0, 2.0)[:, None] / FFT_N
    bd1inv = np.concatenate(
        [np.einsum("kn,cd->nckd", wgt * np.cos(th), eye).reshape(GROUP_ROWS, RE_ROWS),
         np.einsum("kn,cd->nckd", (-wgt * np.sin(th))[1:-1], eye).reshape(GROUP_ROWS, IM_ROWS)], axis=1)
    n2 = np.arange(FFT_N2, dtype=np.float64)
    ph = 2.0 * np.pi * np.outer(k1, n2) / FFT_N
    twr = np.repeat(np.cos(ph), CH_GROUP, axis=0)
    twi = np.repeat(-np.sin(ph), CH_GROUP, axis=0)
    ps = 2.0 * np.pi * np.outer(n2, n2) / FFT_N2
    cr, ci = np.cos(ps), -np.sin(ps)
    w3 = np.block([[cr, ci], [-ci, cr]])
    w3inv = np.block([[cr, -ci], [ci, cr]])
    f32 = lambda a: jnp.asarray(a, F32)
    return (f32(bd1).astype(BF16), f32(bd1k).astype(BF16), f32(bd1inv).astype(BF16), f32(twr), f32(twi),
            f32(w3).astype(BF16), f32(w3inv).astype(BF16))


def _hyena_kernel(zx0_ref, zx1_ref, zv_ref, px0_ref, px1_ref, pv_ref, k_ref, bias_ref,
                  bd1_ref, bd1k_ref, bd1inv_ref, twr_ref, twi_ref, w3_ref, w3inv_ref,
                  w1_ref, w2_ref, wo_ref, ya_ref, w1b_ref, w2b_ref, wob_ref):
    w1b_ref[...] = w1_ref[...].astype(BF16)
    w2b_ref[...] = w2_ref[...].astype(BF16)
    wob_ref[...] = wo_ref[...].astype(BF16)
    twr, twi = twr_ref[...], twi_ref[...]
    zrow = jnp.zeros((CH_GROUP, FFT_N2), F32)

    def stage1(xq, bd_ref):
        xq = xq.reshape(xq.shape[0] * CH_GROUP, FFT_N2).astype(BF16)
        a = jnp.dot(bd_ref[...], xq, preferred_element_type=F32)
        ar = a[:RE_ROWS]
        ai = jnp.concatenate([zrow, a[RE_ROWS:], zrow], axis=0)
        br = ar * twr - ai * twi
        bi = ar * twi + ai * twr
        return jnp.concatenate([br, bi], axis=1)

    def pair(pi):
        gs = [2 * pi, 2 * pi + 1]
        rows = slice(pi * GATE_ROWS, (pi + 1) * GATE_ROWS)
        x1c = _conv3_bias_chunks(zx1_ref, px1_ref, rows)
        vc = _conv3_bias_chunks(zv_ref, pv_ref, rows)
        us = _to_q_groups([v * x1 for v, x1 in zip(vc, x1c)])
        x0s = _to_q_groups(_conv3_bias_chunks(zx0_ref, px0_ref, rows))
        b_all = jnp.concatenate(
            [blk for g, u in zip(gs, us) for blk in (stage1(u, bd1_ref), stage1(k_ref[g], bd1k_ref))], axis=0)
        s_all = jnp.dot(b_all.astype(BF16), w3_ref[...], preferred_element_type=F32)
        ycats = []
        for i in range(2):
            xs = s_all[(2 * i) * RE_ROWS:(2 * i + 1) * RE_ROWS]
            ks = s_all[(2 * i + 1) * RE_ROWS:(2 * i + 2) * RE_ROWS]
            xr, xi = xs[:, :FFT_N2], xs[:, FFT_N2:]
            kr, ki = ks[:, :FFT_N2], ks[:, FFT_N2:]
            ycats.append(jnp.concatenate([xr * kr - xi * ki, xr * ki + xi * kr], axis=1))
        ap_all = jnp.dot(jnp.concatenate(ycats, axis=0).astype(BF16), w3inv_ref[...],
                         preferred_element_type=F32)
        for i, (g, u) in enumerate(zip(gs, us)):
            ap = ap_all[i * RE_ROWS:(i + 1) * RE_ROWS]
            apr, api = ap[:, :FFT_N2], ap[:, FFT_N2:]
            bpr = apr * twr + api * twi
            bpi = api * twr - apr * twi
            b2 = jnp.concatenate([bpr, bpi[CH_GROUP:RE_ROWS - CH_GROUP]], axis=0).astype(BF16)
            y = jnp.dot(bd1inv_ref[...], b2, preferred_element_type=F32)
            y3 = y.reshape(FFT_N1_LIVE, CH_GROUP, FFT_N2)
            bias = bias_ref[g * CH_GROUP:(g + 1) * CH_GROUP, :]
            ya_ref[g] = x0s[i] * (y3 + bias[None] * u)

    for pi in range(HY_CB // GATE_ROWS):
        pair(pi)


def _hyena(zt, hy_par, k_q, bias2, tables, mlp_w1, mlp_w2, layer, w_out):
    bd1, bd1k, bd1inv, twr, twi, w3, w3inv = tables
    steps = D_HY // HY_CB
    zspec = lambda s: pl.BlockSpec((HY_CB, SEQ), lambda j, s=s: (s * steps + j, 0))
    pspec = lambda s: pl.BlockSpec((HY_CB, 4), lambda j, s=s: (s * steps + j, 0))
    qspec = pl.BlockSpec((HY_CB // CH_GROUP,) + Q8_SHAPE[1:], lambda j: (j, 0, 0, 0))
    kspec = pl.BlockSpec((HY_CB // CH_GROUP,) + KQ_SHAPE[1:], lambda j: (j, 0, 0, 0))
    full = lambda a: pl.BlockSpec(a.shape, lambda j: (0,) * a.ndim)
    w1_blk = (D_MODEL // steps, D_FF)
    w2_blk = (D_FF // steps, D_MODEL)
    wo_blk = (D_MODEL // steps, D_MODEL)
    return pl.pallas_call(
        _hyena_kernel,
        grid=(steps,),
        in_specs=[zspec(0), zspec(1), zspec(2), pspec(0), pspec(1), pspec(2), kspec,
                  pl.BlockSpec((HY_CB, FFT_N2), lambda j: (j, 0)),
                  full(bd1), full(bd1k), full(bd1inv), full(twr), full(twi), full(w3), full(w3inv),
                  pl.BlockSpec((None,) + w1_blk, lambda j: (layer, j, 0)),
                  pl.BlockSpec((None,) + w2_blk, lambda j: (layer, j, 0)),
                  pl.BlockSpec(wo_blk, lambda j: (j, 0))],
        out_specs=[qspec, pl.BlockSpec(w1_blk, lambda j: (j, 0)), pl.BlockSpec(w2_blk, lambda j: (j, 0)),
                   pl.BlockSpec(wo_blk, lambda j: (j, 0))],
        out_shape=[jax.ShapeDtypeStruct(Q8_SHAPE, F32),
                   jax.ShapeDtypeStruct((D_MODEL, D_FF), BF16),
                   jax.ShapeDtypeStruct((D_FF, D_MODEL), BF16),
                   jax.ShapeDtypeStruct((D_MODEL, D_MODEL), BF16)],
        compiler_params=_cparams(1),
        name="hyena_fftconv",
    )(zt, zt, zt, hy_par, hy_par, hy_par, k_q, bias2, bd1, bd1k, bd1inv, twr, twi, w3, w3inv,
      mlp_w1, mlp_w2, w_out)


OUT_TM = 512
OUT_CH = 256


def _out_proj_kernel(ya_ref, gb_ref, gcl_ref, gc_ref, gcr_ref, xvl_ref, xv_ref, xvr_ref, psc_ref,
                     wa_ref, wb_ref, x_ref, g_ref, o_ref):
    i = pl.program_id(0)
    tn = (((0,), (0,)), ((), ()))
    ya = jnp.concatenate([ya_ref[:, a].reshape(D_HY, FFT_N2) for a in range(OUT_TM // FFT_N2)],
                         axis=1).astype(BF16)
    cur = slice(LANES, LANES + OUT_TM)
    lane = lax.broadcasted_iota(jnp.int32, (OUT_CH, OUT_TM), 1)
    first = (lane == 0) & (i == 0)
    final = (lane == OUT_TM - 1) & (i == pl.num_programs(0) - 1)
    m = lax.dot_general(ya, wa_ref[...], tn, preferred_element_type=F32)
    for c in range(D_SC // OUT_CH):
        rows = slice(c * OUT_CH, (c + 1) * OUT_CH)
        window = lambda l, mid, r: jnp.concatenate([l[rows, :], mid[rows, :], r[rows, :]], axis=1).astype(F32)
        p = window(gcl_ref, gc_ref, gcr_ref) * window(xvl_ref, xv_ref, xvr_ref)
        pm = jnp.where(first, 0.0, pltpu.roll(p, 1, 1)[:, cur])
        pp = jnp.where(final, 0.0, pltpu.roll(p, p.shape[1] - 1, 1)[:, cur])
        w = psc_ref[rows, :]
        conv = pm * w[:, 0:1] + p[:, cur] * w[:, 1:2] + pp * w[:, 2:3]
        yb = (gb_ref[rows, :].astype(F32) * conv).astype(BF16)
        m = m + lax.dot_general(yb, wb_ref[rows, :], tn, preferred_element_type=F32)
    o_ref[...] = x_ref[...] + _rms_rows(m, g_ref[...])


def _out_proj(ya_q, zt, sc_par, wo, x2d, g):
    per = OUT_TM // LANES
    last = SEQ // LANES - 1
    main = lambda s: pl.BlockSpec((D_SC, OUT_TM), lambda i, s=s: (s, i))
    left = lambda s: pl.BlockSpec((D_SC, LANES), lambda i, s=s: (s, jnp.maximum(i * per - 1, 0)))
    right = lambda s: pl.BlockSpec((D_SC, LANES), lambda i, s=s: (s, jnp.minimum((i + 1) * per, last)))
    return pl.pallas_call(
        _out_proj_kernel,
        grid=(SEQ // OUT_TM,),
        in_specs=[
            pl.BlockSpec((Q8_SHAPE[0], OUT_TM // FFT_N2, CH_GROUP, FFT_N2), lambda i: (0, i, 0, 0)),
            main(3), left(4), main(4), right(4), left(5), main(5), right(5),
            pl.BlockSpec((D_SC, 4), lambda i: (0, 0)),
            pl.BlockSpec((D_HY, D_MODEL), lambda i: (0, 0)),
            pl.BlockSpec((D_SC, D_MODEL), lambda i: (1, 0)),
            pl.BlockSpec((OUT_TM, D_MODEL), lambda i: (i, 0)),
            pl.BlockSpec((1, D_MODEL), lambda i: (0, 0)),
        ],
        out_specs=pl.BlockSpec((OUT_TM, D_MODEL), lambda i: (i, 0)),
        out_shape=jax.ShapeDtypeStruct((SEQ, D_MODEL), F32),
        compiler_params=_cparams(1),
        name="out_proj",
    )(ya_q, zt, zt, zt, zt, zt, zt, zt, sc_par, wo, wo, x2d, g)


MLP_TM = 1024
MLP_TK = 1024
MLP_TA = 512
MLP_TN = 512
MLP_ROWS = 128


def _mlp_kernel(x_ref, gin_ref, gout_ref, w1_ref, w2_ref, *rest):
    if len(rest) == 6:
        nw1_ref, nw2_ref, o_ref, nw1b_ref, nw2b_ref, h_ref = rest
        nw1b_ref[...] = nw1_ref[...].astype(BF16)
        nw2b_ref[...] = nw2_ref[...].astype(BF16)
    else:
        o_ref, h_ref = rest
    k = pl.program_id(1)

    def row_chunks(fn):
        def body(c, carry):
            fn(pl.ds(pl.multiple_of(c * MLP_ROWS, MLP_ROWS), MLP_ROWS))
            return carry
        lax.fori_loop(0, MLP_TM // MLP_ROWS, body, 0)

    @pl.when(k == 0)
    def _():
        def prologue(rows):
            h_ref[rows, :] = _rms_rows(x_ref[rows, :], gin_ref[...]).astype(BF16)
            o_ref[rows, :] = jnp.zeros((MLP_ROWS, D_MODEL), F32)
        row_chunks(prologue)

    for c in range(w1_ref.shape[1] // MLP_TA):
        mid = slice(c * MLP_TA, (c + 1) * MLP_TA)
        a = jnp.dot(h_ref[...], w1_ref[:, mid], preferred_element_type=F32)
        a = jnp.square(jnp.maximum(a, 0.0)).astype(BF16)
        for n in range(D_MODEL // MLP_TN):
            cols = slice(n * MLP_TN, (n + 1) * MLP_TN)
            o_ref[:, cols] += jnp.dot(a, w2_ref[mid, cols], preferred_element_type=F32)

    @pl.when(k == pl.num_programs(1) - 1)
    def _():
        def epilogue(rows):
            o_ref[rows, :] = x_ref[rows, :] + _rms_rows(o_ref[rows, :], gout_ref[...])
        row_chunks(epilogue)


def _mlp(x2d, gin, gout, w1, w2, narrow=None):
    tk = MLP_TK if narrow is None else MLP_TK // 2
    ni, nk = SEQ // MLP_TM, D_FF // tk
    in_specs = [
        pl.BlockSpec((MLP_TM, D_MODEL), lambda i, k: (i, 0)),
        pl.BlockSpec((1, D_MODEL), lambda i, k: (0, 0)),
        pl.BlockSpec((1, D_MODEL), lambda i, k: (0, 0)),
        pl.BlockSpec((D_MODEL, tk), lambda i, k: (0, k)),
        pl.BlockSpec((tk, D_MODEL), lambda i, k: (k, 0)),
    ]
    out_specs = [pl.BlockSpec((MLP_TM, D_MODEL), lambda i, k: (i, 0))]
    out_shape = [jax.ShapeDtypeStruct((SEQ, D_MODEL), F32)]
    args = [x2d, gin, gout, w1, w2]
    if narrow is not None:
        nw1, nw2, layer = narrow
        w1_blk = (D_MODEL // ni, D_FF // nk)
        w2_blk = (D_FF // ni, D_MODEL // nk)
        in_specs += [pl.BlockSpec((None,) + w1_blk, lambda i, k: (layer, i, k)),
                     pl.BlockSpec((None,) + w2_blk, lambda i, k: (layer, i, k))]
        out_specs += [pl.BlockSpec(w1_blk, lambda i, k: (i, k)), pl.BlockSpec(w2_blk, lambda i, k: (i, k))]
        out_shape += [jax.ShapeDtypeStruct((D_MODEL, D_FF), BF16),
                      jax.ShapeDtypeStruct((D_FF, D_MODEL), BF16)]
        args += [nw1, nw2]
    return pl.pallas_call(
        _mlp_kernel,
        grid=(ni, nk),
        in_specs=in_specs,
        out_specs=out_specs,
        out_shape=out_shape,
        scratch_shapes=[pltpu.VMEM((MLP_TM, D_MODEL), BF16)],
        compiler_params=_cparams(2),
        name="mlp",
    )(*args)


POOL_TM = 512
POOL_SUB = 128
HALO = 8


def _pool_bands():
    t = np.arange(POOL_SUB)[:, None]
    j = np.arange(POOL_SUB + 2 * HALO)[None, :]
    return np.stack([(np.abs(j - HALO - t) <= w // 2) for w in POOL_WINDOWS]).astype(np.float32)


def _pool_kernel(xp_ref, xm_ref, xn_ref, gin_ref, gout_ref, band_ref, pw_ref, ps_ref, o_ref,
                 hs_ref, hi_ref, lo_ref, m_ref):
    i = pl.program_id(0)
    gin = gin_ref[...]
    hs_ref[0:HALO, :] = jnp.where(i > 0, _rms_rows(xp_ref[...], gin), 0.0)
    hs_ref[HALO:HALO + POOL_TM, :] = _rms_rows(xm_ref[...], gin)
    hs_ref[HALO + POOL_TM:, :] = jnp.where(i < pl.num_programs(0) - 1, _rms_rows(xn_ref[...], gin), 0.0)
    hs = hs_ref[...]
    hi = hs.astype(BF16)
    hi_ref[...] = hi
    lo_ref[...] = (hs - hi.astype(F32)).astype(BF16)
    row = lax.broadcasted_iota(jnp.int32, (POOL_SUB, D_POOL_G), 0) + i * POOL_TM
    for gi, w in enumerate(POOL_WINDOWS):
        r = w // 2
        cols = slice(gi * D_POOL_G, (gi + 1) * D_POOL_G)
        band = band_ref[gi]
        parts = []
        for sb in range(POOL_TM // POOL_SUB):
            win = slice(sb * POOL_SUB, (sb + 1) * POOL_SUB + 2 * HALO)
            s = (jnp.dot(band, hi_ref[win, cols], preferred_element_type=F32)
                 + jnp.dot(band, lo_ref[win, cols], preferred_element_type=F32))
            t = row + sb * POOL_SUB
            cnt = jnp.minimum(t + r + 1, SEQ) - jnp.maximum(t - r, 0)
            u = hs_ref[sb * POOL_SUB + HALO:(sb + 1) * POOL_SUB + HALO, cols]
            parts.append((s / cnt.astype(F32) - u).astype(BF16))
        d = jnp.concatenate(parts, axis=0)
        mg = jnp.dot(d, pw_ref[gi], preferred_element_type=F32)
        m_ref[:, cols] = mg * ps_ref[:, cols]
    o_ref[...] = xm_ref[...] + _rms_rows(m_ref[...], gout_ref[...])


def _pool(x2d, gin, gout, pw, ps):
    nb8 = POOL_TM // HALO
    last8 = SEQ // HALO - 1
    band = jnp.asarray(_pool_bands(), F32).astype(BF16)
    ext = POOL_TM + 2 * HALO
    return pl.pallas_call(
        _pool_kernel,
        grid=(SEQ // POOL_TM,),
        in_specs=[
            pl.BlockSpec((HALO, D_MODEL), lambda i: (jnp.maximum(i * nb8 - 1, 0), 0)),
            pl.BlockSpec((POOL_TM, D_MODEL), lambda i: (i, 0)),
            pl.BlockSpec((HALO, D_MODEL), lambda i: (jnp.minimum((i + 1) * nb8, last8), 0)),
            pl.BlockSpec((1, D_MODEL), lambda i: (0, 0)),
            pl.BlockSpec((1, D_MODEL), lambda i: (0, 0)),
            pl.BlockSpec(band.shape, lambda i: (0, 0, 0)),
            pl.BlockSpec((len(POOL_WINDOWS), D_POOL_G, D_POOL_G), lambda i: (0, 0, 0)),
            pl.BlockSpec((1, D_MODEL), lambda i: (0, 0)),
        ],
        out_specs=pl.BlockSpec((POOL_TM, D_MODEL), lambda i: (i, 0)),
        out_shape=jax.ShapeDtypeStruct((SEQ, D_MODEL), F32),
        scratch_shapes=[pltpu.VMEM((ext, D_MODEL), F32), pltpu.VMEM((ext, D_MODEL), BF16),
                        pltpu.VMEM((ext, D_MODEL), BF16), pltpu.VMEM((POOL_TM, D_MODEL), F32)],
        compiler_params=_cparams(1),
        name="pool_mixer",
    )(x2d, x2d, x2d, gin, gout, band, pw, ps)


def _filter_constants():
    L = SEQ
    t = np.linspace(0.0, 1.0, L)
    w_pos = 2.0 * np.pi * np.arange(L) / L
    bands = np.linspace(1e-4, FILTER_BANDS - 1, FILTER_BANDS)
    ang = w_pos[:, None] * bands[None, :]
    emb = np.concatenate([t[:, None], np.cos(ang), -np.sin(ang)], axis=-1)
    src = np.concatenate([np.arange(L), (L - np.arange(L)) % L])
    emb_t = np.zeros((EMB_PAD, 2 * L))
    emb_t[:EMB_DIM] = emb[src].T
    max_decay = math.log(DECAY_TARGET) / FAST_DECAY_PCT
    min_decay = math.log(DECAY_TARGET) / SLOW_DECAY_PCT
    absdelta = np.abs(np.linspace(min_decay, max_decay, D_HY))[:, None]
    return (jnp.asarray(emb_t, F32), jnp.asarray(t[src][None, :], F32), jnp.asarray(absdelta, F32))


def _mixer_layer0(x2d, g, w_in, hy_short_w, hy_short_b, f_w1, f_b1, f_w2, f_b2, f_w3, f_b3,
                  freq, hy_bias, sc_conv_w, w_out, mlp_w1, mlp_w2, layer):
    emb_t, t_row, absdelta = _filter_constants()
    w1t = jnp.zeros((FILTER_HIDDEN, EMB_PAD), F32).at[:, :EMB_DIM].set(f_w1.T)
    col = lambda v: v[:, None].astype(F32)
    k_q, wt = _filters(emb_t, t_row, w1t, col(f_b1), f_w2.T, col(f_b2), col(freq),
                       f_w3.T.astype(BF16), col(f_b3), absdelta, w_in)

    zt = _in_proj(x2d, g[0][None, :], wt)
    hy_par = jnp.concatenate([hy_short_w.T, hy_short_b[:, None]], axis=1)
    sc_par = jnp.concatenate([sc_conv_w.T, jnp.zeros((D_SC, 1), F32)], axis=1)

    bias2 = jnp.broadcast_to(hy_bias[:, None].astype(F32), (D_HY, FFT_N2))
    ya_q, w1_this, w2_this, wo = _hyena(zt, hy_par, k_q, bias2, _dft_tables(), mlp_w1, mlp_w2,
                                        layer, w_out)
    x2d = _out_proj(ya_q, zt, sc_par, wo, x2d, g[1][None, :])
    return x2d, (w1_this, w2_this)


def kernel(x, norm_g, mix_w_in, hy_short_w, hy_short_b, hy_filt_w1, hy_filt_b1, hy_filt_w2,
           hy_filt_b2, hy_filt_w3, hy_filt_b3, hy_freq, hy_bias, sc_conv_w, mix_w_out,
           pool_w, pool_scale, mlp_w1, mlp_w2):
    x2d = x.reshape(SEQ, D_MODEL)
    depth = norm_g.shape[0]
    assert depth % 2 == 0, "an even layer narrows its own MLP weights and, in its MLP, the next layer's"
    for i in range(depth):
        g = norm_g[i]
        j = i // 2
        gin, gout = g[2][None, :], g[3][None, :]
        if i % 2 == 0:
            x2d, w_this = _mixer_layer0(
                x2d, g, mix_w_in[j], hy_short_w[j], hy_short_b[j], hy_filt_w1[j], hy_filt_b1[j],
                hy_filt_w2[j], hy_filt_b2[j], hy_filt_w3[j], hy_filt_b3[j], hy_freq[j], hy_bias[j],
                sc_conv_w[j], mix_w_out[j], mlp_w1, mlp_w2, i)
            x2d, *w_next = _mlp(x2d, gin, gout, *w_this, narrow=(mlp_w1, mlp_w2, i + 1))
        else:
            x2d = _pool(x2d, g[0][None, :], g[1][None, :], pool_w[j].astype(BF16),
                        pool_scale[j][None, :])
            x2d, = _mlp(x2d, gin, gout, *w_next)
    return x2d.reshape(x.shape)
```

```python
import math

import numpy as np
import jax
import jax.numpy as jnp
from jax import lax
from jax.experimental import pallas as pl
from jax.experimental.pallas import tpu as pltpu

F32 = jnp.float32
BF16 = jnp.bfloat16

D_MODEL = 2048
SEQ = 8192
D_HY = D_MODEL // 2
D_SC = D_MODEL // 2
D_IN = 3 * D_HY + 3 * D_SC
FILTER_BANDS = 16
EMB_DIM = 1 + 2 * FILTER_BANDS
EMB_PAD = 40
FILTER_HIDDEN = 64
DECAY_TARGET = 1e-2
FAST_DECAY_PCT = 0.3
SLOW_DECAY_PCT = 1.5
POOL_WINDOWS = (2, 4, 8, 16)
D_POOL_G = D_MODEL // len(POOL_WINDOWS)
D_FF = 4 * D_MODEL
NORM_EPS = 1e-6
LANES = 128

FFT_N = 2 * SEQ
FFT_N2 = 256
FFT_N1 = FFT_N // FFT_N2
FFT_N1_LIVE = SEQ // FFT_N2
CH_GROUP = 8
Q8_SHAPE = (D_MODEL // 2 // CH_GROUP, FFT_N1_LIVE, CH_GROUP, FFT_N2)

VMEM_LIMIT = 56 * 1024 * 1024


def _cparams(n_axes):
    return pltpu.CompilerParams(
        dimension_semantics=("arbitrary",) * n_axes, vmem_limit_bytes=VMEM_LIMIT)


def _rms_rows(x, g):
    r = lax.rsqrt(jnp.mean(x * x, axis=-1, keepdims=True) + NORM_EPS)
    return x * r * g


IN_TM = 1024
IN_TN = 2048


def _in_proj_kernel(x_ref, g_ref, wt_ref, zt_ref, h_ref):
    @pl.when(pl.program_id(1) == 0)
    def _():
        h_ref[...] = _rms_rows(x_ref[...], g_ref[...]).astype(BF16)

    zt_ref[...] = lax.dot_general(
        wt_ref[...], h_ref[...], (((1,), (1,)), ((), ())), preferred_element_type=F32).astype(BF16)


def _in_proj(x2d, g, wt):
    return pl.pallas_call(
        _in_proj_kernel,
        grid=(SEQ // IN_TM, D_IN // IN_TN),
        in_specs=[
            pl.BlockSpec((IN_TM, D_MODEL), lambda i, j: (i, 0)),
            pl.BlockSpec((1, D_MODEL), lambda i, j: (0, 0)),
            pl.BlockSpec((IN_TN, D_MODEL), lambda i, j: (j, 0)),
        ],
        out_specs=pl.BlockSpec((IN_TN, IN_TM), lambda i, j: (j, i)),
        out_shape=jax.ShapeDtypeStruct((D_IN, SEQ), BF16),
        scratch_shapes=[pltpu.VMEM((IN_TM, D_MODEL), BF16)],
        compiler_params=_cparams(2),
        name="in_proj",
    )(x2d, g, wt)


GATE_ROWS = 16
GATE_CHUNK = 1024


def _conv3_chunk(zwin, w, off, first, last):
    width = zwin.shape[1]
    zm = pltpu.roll(zwin, 1, 1)[:, off:off + GATE_CHUNK]
    zp = pltpu.roll(zwin, width - 1, 1)[:, off:off + GATE_CHUNK]
    z = zwin[:, off:off + GATE_CHUNK]
    lane = lax.broadcasted_iota(jnp.int32, z.shape, 1)
    if first:
        zm = jnp.where(lane == 0, 0.0, zm)
    if last:
        zp = jnp.where(lane == GATE_CHUNK - 1, 0.0, zp)
    return zm * w[:, 0:1] + z * w[:, 1:2] + zp * w[:, 2:3]


def _conv3_bias_chunks(z_ref, p_ref, rows):
    n_chunks = SEQ // GATE_CHUNK
    p = p_ref[rows, :]
    out = []
    for c in range(n_chunks):
        lo = max(c * GATE_CHUNK - LANES, 0)
        hi = min((c + 1) * GATE_CHUNK + LANES, SEQ)
        zwin = z_ref[rows, lo:hi].astype(F32)
        out.append(_conv3_chunk(zwin, p, c * GATE_CHUNK - lo, c == 0, c == n_chunks - 1) + p[:, 3:4])
    return out


def _to_q_groups(chunks):
    groups = []
    for g in range(GATE_ROWS // CH_GROUP):
        slabs = [ch[g * CH_GROUP:(g + 1) * CH_GROUP, a * FFT_N2:(a + 1) * FFT_N2]
                 for ch in chunks for a in range(GATE_CHUNK // FFT_N2)]
        groups.append(jnp.stack(slabs, axis=0))
    return groups


FILT_TB = 1024


def _filter_kernel(emb_ref, embn_ref, tf_ref, tb_ref, w1_ref, b1_ref, w2_ref, b2_ref, fr_ref, w3_ref, b3_ref,
                   ad_ref, rev_ref, win_ref, kf_ref, kb_ref, wt_ref, kfs_ref, kbs_ref):
    wt_ref[...] = win_ref[...].T.astype(BF16)
    hi = lax.Precision.HIGHEST
    fr = fr_ref[...]
    emb = jnp.concatenate([emb_ref[...], embn_ref[...]], axis=1)
    a1 = jnp.dot(w1_ref[...], emb, precision=hi, preferred_element_type=F32) + b1_ref[...]
    h1 = jnp.sin(fr * a1)
    a2 = jnp.dot(w2_ref[...], h1, precision=hi, preferred_element_type=F32) + b2_ref[...]
    h2 = jnp.sin(fr * a2)
    ad = ad_ref[...]
    h2f = h2[:, :FILT_TB].astype(BF16)
    of = jnp.dot(w3_ref[:D_HY, :], h2f, preferred_element_type=F32) + b3_ref[:D_HY, :]
    kfs_ref[...] = of * jnp.exp(-(ad * tf_ref[...]))
    h2b = pltpu.roll(h2, FILT_TB + LANES - 1, 1)[:, :FILT_TB].astype(BF16)
    h2r = jnp.dot(h2b, rev_ref[...], preferred_element_type=F32).astype(BF16)
    ob = jnp.dot(w3_ref[D_HY:, :], h2r, preferred_element_type=F32) + b3_ref[D_HY:, :]
    lane = lax.broadcasted_iota(jnp.int32, (D_HY, FILT_TB), 1)
    is_sep = (lane == 0) & (pl.program_id(0) == pl.num_programs(0) - 1)
    kbs_ref[...] = jnp.where(is_sep, 0.0, ob * jnp.exp(-(ad * tb_ref[...])))

    def to_q(cg, carry):
        rows = pl.ds(pl.multiple_of(cg * CH_GROUP, CH_GROUP), CH_GROUP)
        for n1 in range(FILT_TB // FFT_N2):
            lanes = slice(n1 * FFT_N2, (n1 + 1) * FFT_N2)
            kf_ref[cg, n1] = kfs_ref[rows, lanes]
            kb_ref[cg, n1] = kbs_ref[rows, lanes]
        return carry

    lax.fori_loop(0, D_HY // CH_GROUP, to_q, 0)


def _filters(emb_t, t_fwd, t_rev, w1t, b1, w2t, b2, fr, w3t, b3, absdelta, w_in):
    full = lambda a: pl.BlockSpec(a.shape, lambda i: (0,) * a.ndim)
    steps = SEQ // FILT_TB
    per = FILT_TB // LANES
    rev = jnp.asarray(np.eye(FILT_TB)[::-1], F32).astype(BF16)
    out = jax.ShapeDtypeStruct(Q8_SHAPE, F32)
    qblk = (Q8_SHAPE[0], FILT_TB // FFT_N2, CH_GROUP, FFT_N2)
    return pl.pallas_call(
        _filter_kernel,
        grid=(steps,),
        in_specs=[pl.BlockSpec((EMB_PAD, FILT_TB), lambda i: (0, i)),
                  pl.BlockSpec((EMB_PAD, LANES), lambda i: (0, (i + 1) * per)),
                  pl.BlockSpec((1, FILT_TB), lambda i: (0, i)),
                  pl.BlockSpec((1, FILT_TB), lambda i: (0, steps - 1 - i)),
                  full(w1t), full(b1), full(w2t), full(b2), full(fr), full(w3t), full(b3),
                  full(absdelta), full(rev),
                  pl.BlockSpec((D_MODEL, D_IN // steps), lambda i: (0, i))],
        out_specs=[pl.BlockSpec(qblk, lambda i: (0, i, 0, 0)),
                   pl.BlockSpec(qblk, lambda i: (0, steps - 1 - i, 0, 0)),
                   pl.BlockSpec((D_IN // steps, D_MODEL), lambda i: (i, 0))],
        out_shape=[out, out, jax.ShapeDtypeStruct((D_IN, D_MODEL), BF16)],
        scratch_shapes=[pltpu.VMEM((D_HY, FILT_TB), F32)] * 2,
        compiler_params=_cparams(1),
        name="filters",
    )(emb_t, emb_t, t_fwd, t_rev, w1t, b1, w2t, b2, fr, w3t, b3, absdelta, rev, w_in)


HY_CB = 32
GROUP_ROWS = CH_GROUP * FFT_N1_LIVE
N_K1 = FFT_N1 // 2 + 1
RE_ROWS = N_K1 * CH_GROUP
IM_ROWS = (N_K1 - 2) * CH_GROUP


def _dft_tables():
    n1 = np.arange(FFT_N1_LIVE, dtype=np.float64)
    k1 = np.arange(N_K1, dtype=np.float64)
    th = 2.0 * np.pi * np.outer(k1, n1) / FFT_N1
    eye = np.eye(CH_GROUP)
    def stage1(cos, sin):
        return np.concatenate([np.einsum("kn,cd->kcnd", cos, eye).reshape(RE_ROWS, -1),
                               np.einsum("kn,cd->kcnd", -sin[1:-1], eye).reshape(IM_ROWS, -1)])
    bd1 = stage1(np.cos(th), np.sin(th))
    thk = 2.0 * np.pi * np.outer(k1, np.arange(FFT_N1, dtype=np.float64)) / FFT_N1
    bd1k = stage1(np.cos(thk), np.sin(thk))
    wgt = np.where((k1 == 0) | (k1 == FFT_N1 // 2), 1.0, 2.0)[:, None] / FFT_N
    bd1inv = np.concatenate(
        [np.einsum("kn,cd->nckd", wgt * np.cos(th), eye).reshape(GROUP_ROWS, RE_ROWS),
         np.einsum("kn,cd->nckd", (-wgt * np.sin(th))[1:-1], eye).reshape(GROUP_ROWS, IM_ROWS)], axis=1)
    n2 = np.arange(FFT_N2, dtype=np.float64)
    ph = 2.0 * np.pi * np.outer(k1, n2) / FFT_N
    twr = np.repeat(np.cos(ph), CH_GROUP, axis=0)
    twi = np.repeat(-np.sin(ph), CH_GROUP, axis=0)
    ps = 2.0 * np.pi * np.outer(n2, n2) / FFT_N2
    cr, ci = np.cos(ps), -np.sin(ps)
    w3 = np.block([[cr, ci], [-ci, cr]])
    w3inv = np.block([[cr, -ci], [ci, cr]])
    f32 = lambda a: jnp.asarray(a, F32)
    return (f32(bd1).astype(BF16), f32(bd1k).astype(BF16), f32(bd1inv).astype(BF16), f32(twr), f32(twi),
            f32(w3).astype(BF16), f32(w3inv).astype(BF16))


def _hyena_kernel(zx0_ref, zx1_ref, zv_ref, px0_ref, px1_ref, pv_ref, kf_ref, kb_ref, bias_ref,
                  bd1_ref, bd1k_ref, bd1inv_ref, twr_ref, twi_ref, w3_ref, w3inv_ref,
                  w1_ref, w2_ref, wo_ref, ya_ref, w1b_ref, w2b_ref, wob_ref):
    w1b_ref[...] = w1_ref[...].astype(BF16)
    w2b_ref[...] = w2_ref[...].astype(BF16)
    wob_ref[...] = wo_ref[...].astype(BF16)
    twr, twi = twr_ref[...], twi_ref[...]
    zrow = jnp.zeros((CH_GROUP, FFT_N2), F32)

    def stage1(xq, bd_ref):
        xq = xq.reshape(xq.shape[0] * CH_GROUP, FFT_N2).astype(BF16)
        a = jnp.dot(bd_ref[...], xq, preferred_element_type=F32)
        ar = a[:RE_ROWS]
        ai = jnp.concatenate([zrow, a[RE_ROWS:], zrow], axis=0)
        br = ar * twr - ai * twi
        bi = ar * twi + ai * twr
        return jnp.concatenate([br, bi], axis=1)

    def pair(pi):
        gs = [2 * pi, 2 * pi + 1]
        rows = slice(pi * GATE_ROWS, (pi + 1) * GATE_ROWS)
        x1c = _conv3_bias_chunks(zx1_ref, px1_ref, rows)
        vc = _conv3_bias_chunks(zv_ref, pv_ref, rows)
        us = _to_q_groups([v * x1 for v, x1 in zip(vc, x1c)])
        x0s = _to_q_groups(_conv3_bias_chunks(zx0_ref, px0_ref, rows))
        b_all = jnp.concatenate(
            [blk for g, u in zip(gs, us)
             for blk in (stage1(u, bd1_ref),
                         stage1(jnp.concatenate([kf_ref[g], kb_ref[g]], axis=0), bd1k_ref))], axis=0)
        s_all = jnp.dot(b_all.astype(BF16), w3_ref[...], preferred_element_type=F32)
        ycats = []
        for i in range(2):
            xs = s_all[(2 * i) * RE_ROWS:(2 * i + 1) * RE_ROWS]
            ks = s_all[(2 * i + 1) * RE_ROWS:(2 * i + 2) * RE_ROWS]
            xr, xi = xs[:, :FFT_N2], xs[:, FFT_N2:]
            kr, ki = ks[:, :FFT_N2], ks[:, FFT_N2:]
            ycats.append(jnp.concatenate([xr * kr - xi * ki, xr * ki + xi * kr], axis=1))
        ap_all = jnp.dot(jnp.concatenate(ycats, axis=0).astype(BF16), w3inv_ref[...],
                         preferred_element_type=F32)
        for i, (g, u) in enumerate(zip(gs, us)):
            ap = ap_all[i * RE_ROWS:(i + 1) * RE_ROWS]
            apr, api = ap[:, :FFT_N2], ap[:, FFT_N2:]
            bpr = apr * twr + api * twi
            bpi = api * twr - apr * twi
            b2 = jnp.concatenate([bpr, bpi[CH_GROUP:RE_ROWS - CH_GROUP]], axis=0).astype(BF16)
            y = jnp.dot(bd1inv_ref[...], b2, preferred_element_type=F32)
            y3 = y.reshape(FFT_N1_LIVE, CH_GROUP, FFT_N2)
            bias = bias_ref[g * CH_GROUP:(g + 1) * CH_GROUP, :]
            ya_ref[g] = x0s[i] * (y3 + bias[None] * u)

    for pi in range(HY_CB // GATE_ROWS):
        pair(pi)


def _hyena(zt, hy_par, kf_q, kb_q, bias2, tables, mlp_w1, mlp_w2, layer, w_out):
    bd1, bd1k, bd1inv, twr, twi, w3, w3inv = tables
    steps = D_HY // HY_CB
    zspec = lambda s: pl.BlockSpec((HY_CB, SEQ), lambda j, s=s: (s * steps + j, 0))
    pspec = lambda s: pl.BlockSpec((HY_CB, 4), lambda j, s=s: (s * steps + j, 0))
    qspec = pl.BlockSpec((HY_CB // CH_GROUP,) + Q8_SHAPE[1:], lambda j: (j, 0, 0, 0))
    full = lambda a: pl.BlockSpec(a.shape, lambda j: (0,) * a.ndim)
    w1_blk = (D_MODEL // steps, D_FF)
    w2_blk = (D_FF // steps, D_MODEL)
    wo_blk = (D_MODEL // steps, D_MODEL)
    return pl.pallas_call(
        _hyena_kernel,
        grid=(steps,),
        in_specs=[zspec(0), zspec(1), zspec(2), pspec(0), pspec(1), pspec(2), qspec, qspec,
                  pl.BlockSpec((HY_CB, FFT_N2), lambda j: (j, 0)),
                  full(bd1), full(bd1k), full(bd1inv), full(twr), full(twi), full(w3), full(w3inv),
                  pl.BlockSpec((None,) + w1_blk, lambda j: (layer, j, 0)),
                  pl.BlockSpec((None,) + w2_blk, lambda j: (layer, j, 0)),
                  pl.BlockSpec(wo_blk, lambda j: (j, 0))],
        out_specs=[qspec, pl.BlockSpec(w1_blk, lambda j: (j, 0)), pl.BlockSpec(w2_blk, lambda j: (j, 0)),
                   pl.BlockSpec(wo_blk, lambda j: (j, 0))],
        out_shape=[jax.ShapeDtypeStruct(Q8_SHAPE, F32),
                   jax.ShapeDtypeStruct((D_MODEL, D_FF), BF16),
                   jax.ShapeDtypeStruct((D_FF, D_MODEL), BF16),
                   jax.ShapeDtypeStruct((D_MODEL, D_MODEL), BF16)],
        compiler_params=_cparams(1),
        name="hyena_fftconv",
    )(zt, zt, zt, hy_par, hy_par, hy_par, kf_q, kb_q, bias2, bd1, bd1k, bd1inv, twr, twi, w3, w3inv,
      mlp_w1, mlp_w2, w_out)


OUT_TM = 512
OUT_CH = 256


def _out_proj_kernel(ya_ref, gb_ref, gcl_ref, gc_ref, gcr_ref, xvl_ref, xv_ref, xvr_ref, psc_ref,
                     wa_ref, wb_ref, x_ref, g_ref, o_ref):
    i = pl.program_id(0)
    tn = (((0,), (0,)), ((), ()))
    ya = jnp.concatenate([ya_ref[:, a].reshape(D_HY, FFT_N2) for a in range(OUT_TM // FFT_N2)],
                         axis=1).astype(BF16)
    cur = slice(LANES, LANES + OUT_TM)
    lane = lax.broadcasted_iota(jnp.int32, (OUT_CH, OUT_TM), 1)
    first = (lane == 0) & (i == 0)
    final = (lane == OUT_TM - 1) & (i == pl.num_programs(0) - 1)
    m = lax.dot_general(ya, wa_ref[...], tn, preferred_element_type=F32)
    for c in range(D_SC // OUT_CH):
        rows = slice(c * OUT_CH, (c + 1) * OUT_CH)
        window = lambda l, mid, r: jnp.concatenate([l[rows, :], mid[rows, :], r[rows, :]], axis=1).astype(F32)
        p = window(gcl_ref, gc_ref, gcr_ref) * window(xvl_ref, xv_ref, xvr_ref)
        pm = jnp.where(first, 0.0, pltpu.roll(p, 1, 1)[:, cur])
        pp = jnp.where(final, 0.0, pltpu.roll(p, p.shape[1] - 1, 1)[:, cur])
        w = psc_ref[rows, :]
        conv = pm * w[:, 0:1] + p[:, cur] * w[:, 1:2] + pp * w[:, 2:3]
        yb = (gb_ref[rows, :].astype(F32) * conv).astype(BF16)
        m = m + lax.dot_general(yb, wb_ref[rows, :], tn, preferred_element_type=F32)
    o_ref[...] = x_ref[...] + _rms_rows(m, g_ref[...])


def _out_proj(ya_q, zt, sc_par, wo, x2d, g):
    per = OUT_TM // LANES
    last = SEQ // LANES - 1
    main = lambda s: pl.BlockSpec((D_SC, OUT_TM), lambda i, s=s: (s, i))
    left = lambda s: pl.BlockSpec((D_SC, LANES), lambda i, s=s: (s, jnp.maximum(i * per - 1, 0)))
    right = lambda s: pl.BlockSpec((D_SC, LANES), lambda i, s=s: (s, jnp.minimum((i + 1) * per, last)))
    return pl.pallas_call(
        _out_proj_kernel,
        grid=(SEQ // OUT_TM,),
        in_specs=[
            pl.BlockSpec((Q8_SHAPE[0], OUT_TM // FFT_N2, CH_GROUP, FFT_N2), lambda i: (0, i, 0, 0)),
            main(3), left(4), main(4), right(4), left(5), main(5), right(5),
            pl.BlockSpec((D_SC, 4), lambda i: (0, 0)),
            pl.BlockSpec((D_HY, D_MODEL), lambda i: (0, 0)),
            pl.BlockSpec((D_SC, D_MODEL), lambda i: (1, 0)),
            pl.BlockSpec((OUT_TM, D_MODEL), lambda i: (i, 0)),
            pl.BlockSpec((1, D_MODEL), lambda i: (0, 0)),
        ],
        out_specs=pl.BlockSpec((OUT_TM, D_MODEL), lambda i: (i, 0)),
        out_shape=jax.ShapeDtypeStruct((SEQ, D_MODEL), F32),
        compiler_params=_cparams(1),
        name="out_proj",
    )(ya_q, zt, zt, zt, zt, zt, zt, zt, sc_par, wo, wo, x2d, g)


MLP_TM = 1024
MLP_TK = 1024
MLP_TA = 512
MLP_TN = 512
MLP_ROWS = 128


def _mlp_kernel(x_ref, gin_ref, gout_ref, w1_ref, w2_ref, *rest):
    if len(rest) == 6:
        nw1_ref, nw2_ref, o_ref, nw1b_ref, nw2b_ref, h_ref = rest
        nw1b_ref[...] = nw1_ref[...].astype(BF16)
        nw2b_ref[...] = nw2_ref[...].astype(BF16)
    else:
        o_ref, h_ref = rest
    k = pl.program_id(1)

    def row_chunks(fn):
        def body(c, carry):
            fn(pl.ds(pl.multiple_of(c * MLP_ROWS, MLP_ROWS), MLP_ROWS))
            return carry
        lax.fori_loop(0, MLP_TM // MLP_ROWS, body, 0)

    @pl.when(k == 0)
    def _():
        def prologue(rows):
            h_ref[rows, :] = _rms_rows(x_ref[rows, :], gin_ref[...]).astype(BF16)
            o_ref[rows, :] = jnp.zeros((MLP_ROWS, D_MODEL), F32)
        row_chunks(prologue)

    for c in range(w1_ref.shape[1] // MLP_TA):
        mid = slice(c * MLP_TA, (c + 1) * MLP_TA)
        a = jnp.dot(h_ref[...], w1_ref[:, mid], preferred_element_type=F32)
        a = jnp.square(jnp.maximum(a, 0.0)).astype(BF16)
        for n in range(D_MODEL // MLP_TN):
            cols = slice(n * MLP_TN, (n + 1) * MLP_TN)
            o_ref[:, cols] += jnp.dot(a, w2_ref[mid, cols], preferred_element_type=F32)

    @pl.when(k == pl.num_programs(1) - 1)
    def _():
        def epilogue(rows):
            o_ref[rows, :] = x_ref[rows, :] + _rms_rows(o_ref[rows, :], gout_ref[...])
        row_chunks(epilogue)


def _mlp(x2d, gin, gout, w1, w2, narrow=None):
    tk = MLP_TK if narrow is None else MLP_TK // 2
    ni, nk = SEQ // MLP_TM, D_FF // tk
    in_specs = [
        pl.BlockSpec((MLP_TM, D_MODEL), lambda i, k: (i, 0)),
        pl.BlockSpec((1, D_MODEL), lambda i, k: (0, 0)),
        pl.BlockSpec((1, D_MODEL), lambda i, k: (0, 0)),
        pl.BlockSpec((D_MODEL, tk), lambda i, k: (0, k)),
        pl.BlockSpec((tk, D_MODEL), lambda i, k: (k, 0)),
    ]
    out_specs = [pl.BlockSpec((MLP_TM, D_MODEL), lambda i, k: (i, 0))]
    out_shape = [jax.ShapeDtypeStruct((SEQ, D_MODEL), F32)]
    args = [x2d, gin, gout, w1, w2]
    if narrow is not None:
        nw1, nw2, layer = narrow
        w1_blk = (D_MODEL // ni, D_FF // nk)
        w2_blk = (D_FF // ni, D_MODEL // nk)
        in_specs += [pl.BlockSpec((None,) + w1_blk, lambda i, k: (layer, i, k)),
                     pl.BlockSpec((None,) + w2_blk, lambda i, k: (layer, i, k))]
        out_specs += [pl.BlockSpec(w1_blk, lambda i, k: (i, k)), pl.BlockSpec(w2_blk, lambda i, k: (i, k))]
        out_shape += [jax.ShapeDtypeStruct((D_MODEL, D_FF), BF16),
                      jax.ShapeDtypeStruct((D_FF, D_MODEL), BF16)]
        args += [nw1, nw2]
    return pl.pallas_call(
        _mlp_kernel,
        grid=(ni, nk),
        in_specs=in_specs,
        out_specs=out_specs,
        out_shape=out_shape,
        scratch_shapes=[pltpu.VMEM((MLP_TM, D_MODEL), BF16)],
        compiler_params=_cparams(2),
        name="mlp",
    )(*args)


POOL_TM = 512
POOL_SUB = 128
HALO = 8


def _pool_bands():
    t = np.arange(POOL_SUB)[:, None]
    j = np.arange(POOL_SUB + 2 * HALO)[None, :]
    return np.stack([(np.abs(j - HALO - t) <= w // 2) for w in POOL_WINDOWS]).astype(np.float32)


def _pool_kernel(xp_ref, xm_ref, xn_ref, gin_ref, gout_ref, band_ref, pw_ref, ps_ref, o_ref,
                 hs_ref, hi_ref, lo_ref, m_ref):
    i = pl.program_id(0)
    gin = gin_ref[...]
    hs_ref[0:HALO, :] = jnp.where(i > 0, _rms_rows(xp_ref[...], gin), 0.0)
    hs_ref[HALO:HALO + POOL_TM, :] = _rms_rows(xm_ref[...], gin)
    hs_ref[HALO + POOL_TM:, :] = jnp.where(i < pl.num_programs(0) - 1, _rms_rows(xn_ref[...], gin), 0.0)
    hs = hs_ref[...]
    hi = hs.astype(BF16)
    hi_ref[...] = hi
    lo_ref[...] = (hs - hi.astype(F32)).astype(BF16)
    row = lax.broadcasted_iota(jnp.int32, (POOL_SUB, D_POOL_G), 0) + i * POOL_TM
    for gi, w in enumerate(POOL_WINDOWS):
        r = w // 2
        cols = slice(gi * D_POOL_G, (gi + 1) * D_POOL_G)
        band = band_ref[gi]
        parts = []
        for sb in range(POOL_TM // POOL_SUB):
            win = slice(sb * POOL_SUB, (sb + 1) * POOL_SUB + 2 * HALO)
            s = (jnp.dot(band, hi_ref[win, cols], preferred_element_type=F32)
                 + jnp.dot(band, lo_ref[win, cols], preferred_element_type=F32))
            t = row + sb * POOL_SUB
            cnt = jnp.minimum(t + r + 1, SEQ) - jnp.maximum(t - r, 0)
            u = hs_ref[sb * POOL_SUB + HALO:(sb + 1) * POOL_SUB + HALO, cols]
            parts.append((s / cnt.astype(F32) - u).astype(BF16))
        d = jnp.concatenate(parts, axis=0)
        mg = jnp.dot(d, pw_ref[gi], preferred_element_type=F32)
        m_ref[:, cols] = mg * ps_ref[:, cols]
    o_ref[...] = xm_ref[...] + _rms_rows(m_ref[...], gout_ref[...])


def _pool(x2d, gin, gout, pw, ps):
    nb8 = POOL_TM // HALO
    last8 = SEQ // HALO - 1
    band = jnp.asarray(_pool_bands(), F32).astype(BF16)
    ext = POOL_TM + 2 * HALO
    return pl.pallas_call(
        _pool_kernel,
        grid=(SEQ // POOL_TM,),
        in_specs=[
            pl.BlockSpec((HALO, D_MODEL), lambda i: (jnp.maximum(i * nb8 - 1, 0), 0)),
            pl.BlockSpec((POOL_TM, D_MODEL), lambda i: (i, 0)),
            pl.BlockSpec((HALO, D_MODEL), lambda i: (jnp.minimum((i + 1) * nb8, last8), 0)),
            pl.BlockSpec((1, D_MODEL), lambda i: (0, 0)),
            pl.BlockSpec((1, D_MODEL), lambda i: (0, 0)),
            pl.BlockSpec(band.shape, lambda i: (0, 0, 0)),
            pl.BlockSpec((len(POOL_WINDOWS), D_POOL_G, D_POOL_G), lambda i: (0, 0, 0)),
            pl.BlockSpec((1, D_MODEL), lambda i: (0, 0)),
        ],
        out_specs=pl.BlockSpec((POOL_TM, D_MODEL), lambda i: (i, 0)),
        out_shape=jax.ShapeDtypeStruct((SEQ, D_MODEL), F32),
        scratch_shapes=[pltpu.VMEM((ext, D_MODEL), F32), pltpu.VMEM((ext, D_MODEL), BF16),
                        pltpu.VMEM((ext, D_MODEL), BF16), pltpu.VMEM((POOL_TM, D_MODEL), F32)],
        compiler_params=_cparams(1),
        name="pool_mixer",
    )(x2d, x2d, x2d, gin, gout, band, pw, ps)


def _filter_constants():
    L = SEQ
    t = np.linspace(0.0, 1.0, L)
    w_pos = 2.0 * np.pi * np.arange(L) / L
    bands = np.linspace(1e-4, FILTER_BANDS - 1, FILTER_BANDS)
    ang = w_pos[:, None] * bands[None, :]
    emb = np.concatenate([t[:, None], np.cos(ang), -np.sin(ang)], axis=-1)
    emb_t = np.zeros((EMB_PAD, L + LANES))
    emb_t[:EMB_DIM, :L] = emb.T
    t_rev = t[(L - np.arange(L)) % L]
    max_decay = math.log(DECAY_TARGET) / FAST_DECAY_PCT
    min_decay = math.log(DECAY_TARGET) / SLOW_DECAY_PCT
    absdelta = np.abs(np.linspace(min_decay, max_decay, D_HY))[:, None]
    f32 = lambda a: jnp.asarray(a, F32)
    return f32(emb_t), f32(t[None, :]), f32(t_rev[None, :]), f32(absdelta)


def _mixer_layer0(x2d, g, w_in, hy_short_w, hy_short_b, f_w1, f_b1, f_w2, f_b2, f_w3, f_b3,
                  freq, hy_bias, sc_conv_w, w_out, mlp_w1, mlp_w2, layer):
    emb_t, t_fwd, t_rev, absdelta = _filter_constants()
    w1t = jnp.zeros((FILTER_HIDDEN, EMB_PAD), F32).at[:, :EMB_DIM].set(f_w1.T)
    col = lambda v: v[:, None].astype(F32)
    kf_q, kb_q, wt = _filters(emb_t, t_fwd, t_rev, w1t, col(f_b1), f_w2.T, col(f_b2), col(freq),
                              f_w3.T.astype(BF16), col(f_b3), absdelta, w_in)

    zt = _in_proj(x2d, g[0][None, :], wt)
    hy_par = jnp.concatenate([hy_short_w.T, hy_short_b[:, None]], axis=1)
    sc_par = jnp.concatenate([sc_conv_w.T, jnp.zeros((D_SC, 1), F32)], axis=1)

    bias2 = jnp.broadcast_to(hy_bias[:, None].astype(F32), (D_HY, FFT_N2))
    ya_q, w1_this, w2_this, wo = _hyena(zt, hy_par, kf_q, kb_q, bias2, _dft_tables(), mlp_w1, mlp_w2,
                                        layer, w_out)
    x2d = _out_proj(ya_q, zt, sc_par, wo, x2d, g[1][None, :])
    return x2d, (w1_this, w2_this)


def kernel(x, norm_g, mix_w_in, hy_short_w, hy_short_b, hy_filt_w1, hy_filt_b1, hy_filt_w2,
           hy_filt_b2, hy_filt_w3, hy_filt_b3, hy_freq, hy_bias, sc_conv_w, mix_w_out,
           pool_w, pool_scale, mlp_w1, mlp_w2):
    x2d = x.reshape(SEQ, D_MODEL)
    depth = norm_g.shape[0]
    assert depth % 2 == 0, "an even layer narrows its own MLP weights and, in its MLP, the next layer's"
    for i in range(depth):
        g = norm_g[i]
        j = i // 2
        gin, gout = g[2][None, :], g[3][None, :]
        if i % 2 == 0:
            x2d, w_this = _mixer_layer0(
                x2d, g, mix_w_in[j], hy_short_w[j], hy_short_b[j], hy_filt_w1[j], hy_filt_b1[j],
                hy_filt_w2[j], hy_filt_b2[j], hy_filt_w3[j], hy_filt_b3[j], hy_freq[j], hy_bias[j],
                sc_conv_w[j], mix_w_out[j], mlp_w1, mlp_w2, i)
            x2d, *w_next = _mlp(x2d, gin, gout, *w_this, narrow=(mlp_w1, mlp_w2, i + 1))
        else:
            x2d = _pool(x2d, g[0][None, :], g[1][None, :], pool_w[j].astype(BF16),
                        pool_scale[j][None, :])
            x2d, = _mlp(x2d, gin, gout, *w_next)
    return x2d.reshape(x.shape)
```

```python
import math

import numpy as np
import jax
import jax.numpy as jnp
from jax import lax
from jax.experimental import pallas as pl
from jax.experimental.pallas import tpu as pltpu

F32 = jnp.float32
BF16 = jnp.bfloat16

D_MODEL = 2048
SEQ = 8192
D_HY = D_MODEL // 2
D_SC = D_MODEL // 2
D_IN = 3 * D_HY + 3 * D_SC
FILTER_BANDS = 16
EMB_DIM = 1 + 2 * FILTER_BANDS
EMB_PAD = 40
FILTER_HIDDEN = 64
DECAY_TARGET = 1e-2
FAST_DECAY_PCT = 0.3
SLOW_DECAY_PCT = 1.5
POOL_WINDOWS = (2, 4, 8, 16)
D_POOL_G = D_MODEL // len(POOL_WINDOWS)
D_FF = 4 * D_MODEL
NORM_EPS = 1e-6
LANES = 128

FFT_N = 2 * SEQ
FFT_N2 = 256
FFT_N1 = FFT_N // FFT_N2
FFT_N1_LIVE = SEQ // FFT_N2
CH_GROUP = 8
Q8_SHAPE = (D_MODEL // 2 // CH_GROUP, FFT_N1_LIVE, CH_GROUP, FFT_N2)

VMEM_LIMIT = 56 * 1024 * 1024


def _cparams(n_axes):
    return pltpu.CompilerParams(
        dimension_semantics=("arbitrary",) * n_axes, vmem_limit_bytes=VMEM_LIMIT)


def _rms_rows(x, g):
    r = lax.rsqrt(jnp.mean(x * x, axis=-1, keepdims=True) + NORM_EPS)
    return x * r * g


IN_TM = 1024
IN_TN = 2048


def _in_proj_kernel(x_ref, g_ref, wt_ref, zt_ref, h_ref):
    @pl.when(pl.program_id(1) == 0)
    def _():
        h_ref[...] = _rms_rows(x_ref[...], g_ref[...]).astype(BF16)

    zt_ref[...] = lax.dot_general(
        wt_ref[...], h_ref[...], (((1,), (1,)), ((), ())), preferred_element_type=F32).astype(BF16)


def _in_proj(x2d, g, wt):
    return pl.pallas_call(
        _in_proj_kernel,
        grid=(SEQ // IN_TM, D_IN // IN_TN),
        in_specs=[
            pl.BlockSpec((IN_TM, D_MODEL), lambda i, j: (i, 0)),
            pl.BlockSpec((1, D_MODEL), lambda i, j: (0, 0)),
            pl.BlockSpec((IN_TN, D_MODEL), lambda i, j: (j, 0)),
        ],
        out_specs=pl.BlockSpec((IN_TN, IN_TM), lambda i, j: (j, i)),
        out_shape=jax.ShapeDtypeStruct((D_IN, SEQ), BF16),
        scratch_shapes=[pltpu.VMEM((IN_TM, D_MODEL), BF16)],
        compiler_params=_cparams(2),
        name="in_proj",
    )(x2d, g, wt)


GATE_ROWS = 16
GATE_CHUNK = 1024


def _conv3_chunk(zwin, w, off, first, last):
    width = zwin.shape[1]
    zm = pltpu.roll(zwin, 1, 1)[:, off:off + GATE_CHUNK]
    zp = pltpu.roll(zwin, width - 1, 1)[:, off:off + GATE_CHUNK]
    z = zwin[:, off:off + GATE_CHUNK]
    lane = lax.broadcasted_iota(jnp.int32, z.shape, 1)
    if first:
        zm = jnp.where(lane == 0, 0.0, zm)
    if last:
        zp = jnp.where(lane == GATE_CHUNK - 1, 0.0, zp)
    return zm * w[:, 0:1] + z * w[:, 1:2] + zp * w[:, 2:3]


def _conv3_bias_chunks(z_ref, p_ref, rows):
    n_chunks = SEQ // GATE_CHUNK
    p = p_ref[rows, :]
    out = []
    for c in range(n_chunks):
        lo = max(c * GATE_CHUNK - LANES, 0)
        hi = min((c + 1) * GATE_CHUNK + LANES, SEQ)
        zwin = z_ref[rows, lo:hi].astype(F32)
        out.append(_conv3_chunk(zwin, p, c * GATE_CHUNK - lo, c == 0, c == n_chunks - 1) + p[:, 3:4])
    return out


def _to_q_groups(chunks):
    groups = []
    for g in range(GATE_ROWS // CH_GROUP):
        slabs = [ch[g * CH_GROUP:(g + 1) * CH_GROUP, a * FFT_N2:(a + 1) * FFT_N2]
                 for ch in chunks for a in range(GATE_CHUNK // FFT_N2)]
        groups.append(jnp.stack(slabs, axis=0))
    return groups


FILT_TB = 1024


def _filter_kernel(emb_ref, embn_ref, tf_ref, tb_ref, w1_ref, b1_ref, w2_ref, b2_ref, fr_ref, w3_ref, b3_ref,
                   ad_ref, rev_ref, win_ref, kf_ref, kb_ref, wt_ref, kfs_ref, kbs_ref):
    wt_ref[...] = win_ref[...].T.astype(BF16)
    hi = lax.Precision.HIGHEST
    fr = fr_ref[...]
    emb = jnp.concatenate([emb_ref[...], embn_ref[...]], axis=1)
    a1 = jnp.dot(w1_ref[...], emb, precision=hi, preferred_element_type=F32) + b1_ref[...]
    h1 = jnp.sin(fr * a1)
    a2 = jnp.dot(w2_ref[...], h1, precision=hi, preferred_element_type=F32) + b2_ref[...]
    h2 = jnp.sin(fr * a2)
    ad = ad_ref[...]
    h2f = h2[:, :FILT_TB].astype(BF16)
    of = jnp.dot(w3_ref[:D_HY, :], h2f, preferred_element_type=F32) + b3_ref[:D_HY, :]
    kfs_ref[...] = of * jnp.exp(-(ad * tf_ref[...]))
    h2b = pltpu.roll(h2, FILT_TB + LANES - 1, 1)[:, :FILT_TB].astype(BF16)
    h2r = jnp.dot(h2b, rev_ref[...], preferred_element_type=F32).astype(BF16)
    ob = jnp.dot(w3_ref[D_HY:, :], h2r, preferred_element_type=F32) + b3_ref[D_HY:, :]
    lane = lax.broadcasted_iota(jnp.int32, (D_HY, FILT_TB), 1)
    is_sep = (lane == 0) & (pl.program_id(0) == pl.num_programs(0) - 1)
    kbs_ref[...] = jnp.where(is_sep, 0.0, ob * jnp.exp(-(ad * tb_ref[...])))

    def to_q(cg, carry):
        rows = pl.ds(pl.multiple_of(cg * CH_GROUP, CH_GROUP), CH_GROUP)
        for n1 in range(FILT_TB // FFT_N2):
            lanes = slice(n1 * FFT_N2, (n1 + 1) * FFT_N2)
            kf_ref[cg, n1] = kfs_ref[rows, lanes]
            kb_ref[cg, n1] = kbs_ref[rows, lanes]
        return carry

    lax.fori_loop(0, D_HY // CH_GROUP, to_q, 0)


def _filters(emb_t, t_fwd, t_rev, w1t, b1, w2t, b2, fr, w3t, b3, absdelta, w_in):
    full = lambda a: pl.BlockSpec(a.shape, lambda i: (0,) * a.ndim)
    steps = SEQ // FILT_TB
    per = FILT_TB // LANES
    rev = jnp.asarray(np.eye(FILT_TB)[::-1], F32).astype(BF16)
    out = jax.ShapeDtypeStruct(Q8_SHAPE, F32)
    qblk = (Q8_SHAPE[0], FILT_TB // FFT_N2, CH_GROUP, FFT_N2)
    return pl.pallas_call(
        _filter_kernel,
        grid=(steps,),
        in_specs=[pl.BlockSpec((EMB_PAD, FILT_TB), lambda i: (0, i)),
                  pl.BlockSpec((EMB_PAD, LANES), lambda i: (0, (i + 1) * per)),
                  pl.BlockSpec((1, FILT_TB), lambda i: (0, i)),
                  pl.BlockSpec((1, FILT_TB), lambda i: (0, steps - 1 - i)),
                  full(w1t), full(b1), full(w2t), full(b2), full(fr), full(w3t), full(b3),
                  full(absdelta), full(rev),
                  pl.BlockSpec((D_MODEL, D_IN // steps), lambda i: (0, i))],
        out_specs=[pl.BlockSpec(qblk, lambda i: (0, i, 0, 0)),
                   pl.BlockSpec(qblk, lambda i: (0, steps - 1 - i, 0, 0)),
                   pl.BlockSpec((D_IN // steps, D_MODEL), lambda i: (i, 0))],
        out_shape=[out, out, jax.ShapeDtypeStruct((D_IN, D_MODEL), BF16)],
        scratch_shapes=[pltpu.VMEM((D_HY, FILT_TB), F32)] * 2,
        compiler_params=_cparams(1),
        name="filters",
    )(emb_t, emb_t, t_fwd, t_rev, w1t, b1, w2t, b2, fr, w3t, b3, absdelta, rev, w_in)


HY_CB = 32
GROUP_ROWS = CH_GROUP * FFT_N1_LIVE
N_K1 = FFT_N1 // 2 + 1
RE_ROWS = N_K1 * CH_GROUP
IM_ROWS = (N_K1 - 2) * CH_GROUP


def _dft_tables():
    n1 = np.arange(FFT_N1_LIVE, dtype=np.float64)
    k1 = np.arange(N_K1, dtype=np.float64)
    th = 2.0 * np.pi * np.outer(k1, n1) / FFT_N1
    eye = np.eye(CH_GROUP)
    def stage1(cos, sin):
        return np.concatenate([np.einsum("kn,cd->kcnd", cos, eye).reshape(RE_ROWS, -1),
                               np.einsum("kn,cd->kcnd", -sin[1:-1], eye).reshape(IM_ROWS, -1)])
    bd1 = stage1(np.cos(th), np.sin(th))
    thk = 2.0 * np.pi * np.outer(k1, np.arange(FFT_N1, dtype=np.float64)) / FFT_N1
    bd1k = stage1(np.cos(thk), np.sin(thk))
    wgt = np.where((k1 == 0) | (k1 == FFT_N1 // 2), 1.0, 2.0)[:, None] / FFT_N
    bd1inv = np.concatenate(
        [np.einsum("kn,cd->nckd", wgt * np.cos(th), eye).reshape(GROUP_ROWS, RE_ROWS),
         np.einsum("kn,cd->nckd", (-wgt * np.sin(th))[1:-1], eye).reshape(GROUP_ROWS, IM_ROWS)], axis=1)
    n2 = np.arange(FFT_N2, dtype=np.float64)
    ph = 2.0 * np.pi * np.outer(k1, n2) / FFT_N
    twr = np.repeat(np.cos(ph), CH_GROUP, axis=0)
    twi = np.repeat(-np.sin(ph), CH_GROUP, axis=0)
    ps = 2.0 * np.pi * np.outer(n2, n2) / FFT_N2
    cr, ci = np.cos(ps), -np.sin(ps)
    w3 = np.block([[cr, ci], [-ci, cr]])
    w3inv = np.block([[cr, -ci], [ci, cr]])
    f32 = lambda a: jnp.asarray(a, F32)
    return (f32(bd1).astype(BF16), f32(bd1k).astype(BF16), f32(bd1inv).astype(BF16), f32(twr), f32(twi),
            f32(w3).astype(BF16), f32(w3inv).astype(BF16))


def _hyena_kernel(zx0_ref, zx1_ref, zv_ref, px0_ref, px1_ref, pv_ref, kf_ref, kb_ref, bias_ref,
                  bd1_ref, bd1k_ref, bd1inv_ref, twr_ref, twi_ref, w3_ref, w3inv_ref,
                  w1_ref, w2_ref, w1n_ref, w2n_ref, wo_ref,
                  ya_ref, w1b_ref, w2b_ref, w1nb_ref, w2nb_ref, wob_ref):
    for src, dst in ((w1_ref, w1b_ref), (w2_ref, w2b_ref), (w1n_ref, w1nb_ref), (w2n_ref, w2nb_ref),
                     (wo_ref, wob_ref)):
        dst[...] = src[...].astype(BF16)
    twr, twi = twr_ref[...], twi_ref[...]
    zrow = jnp.zeros((CH_GROUP, FFT_N2), F32)

    def stage1(xq, bd_ref):
        xq = xq.reshape(xq.shape[0] * CH_GROUP, FFT_N2).astype(BF16)
        a = jnp.dot(bd_ref[...], xq, preferred_element_type=F32)
        ar = a[:RE_ROWS]
        ai = jnp.concatenate([zrow, a[RE_ROWS:], zrow], axis=0)
        br = ar * twr - ai * twi
        bi = ar * twi + ai * twr
        return jnp.concatenate([br, bi], axis=1)

    def pair(pi):
        gs = [2 * pi, 2 * pi + 1]
        rows = slice(pi * GATE_ROWS, (pi + 1) * GATE_ROWS)
        x1c = _conv3_bias_chunks(zx1_ref, px1_ref, rows)
        vc = _conv3_bias_chunks(zv_ref, pv_ref, rows)
        us = _to_q_groups([v * x1 for v, x1 in zip(vc, x1c)])
        x0s = _to_q_groups(_conv3_bias_chunks(zx0_ref, px0_ref, rows))
        b_all = jnp.concatenate(
            [blk for g, u in zip(gs, us)
             for blk in (stage1(u, bd1_ref),
                         stage1(jnp.concatenate([kf_ref[g], kb_ref[g]], axis=0), bd1k_ref))], axis=0)
        s_all = jnp.dot(b_all.astype(BF16), w3_ref[...], preferred_element_type=F32)
        ycats = []
        for i in range(2):
            xs = s_all[(2 * i) * RE_ROWS:(2 * i + 1) * RE_ROWS]
            ks = s_all[(2 * i + 1) * RE_ROWS:(2 * i + 2) * RE_ROWS]
            xr, xi = xs[:, :FFT_N2], xs[:, FFT_N2:]
            kr, ki = ks[:, :FFT_N2], ks[:, FFT_N2:]
            ycats.append(jnp.concatenate([xr * kr - xi * ki, xr * ki + xi * kr], axis=1))
        ap_all = jnp.dot(jnp.concatenate(ycats, axis=0).astype(BF16), w3inv_ref[...],
                         preferred_element_type=F32)
        for i, (g, u) in enumerate(zip(gs, us)):
            ap = ap_all[i * RE_ROWS:(i + 1) * RE_ROWS]
            apr, api = ap[:, :FFT_N2], ap[:, FFT_N2:]
            bpr = apr * twr + api * twi
            bpi = api * twr - apr * twi
            b2 = jnp.concatenate([bpr, bpi[CH_GROUP:RE_ROWS - CH_GROUP]], axis=0).astype(BF16)
            y = jnp.dot(bd1inv_ref[...], b2, preferred_element_type=F32)
            y3 = y.reshape(FFT_N1_LIVE, CH_GROUP, FFT_N2)
            bias = bias_ref[g * CH_GROUP:(g + 1) * CH_GROUP, :]
            ya_ref[g] = x0s[i] * (y3 + bias[None] * u)

    for pi in range(HY_CB // GATE_ROWS):
        pair(pi)


def _hyena(zt, hy_par, kf_q, kb_q, bias2, tables, mlp_w1, mlp_w2, layer, w_out):
    bd1, bd1k, bd1inv, twr, twi, w3, w3inv = tables
    steps = D_HY // HY_CB
    zspec = lambda s: pl.BlockSpec((HY_CB, SEQ), lambda j, s=s: (s * steps + j, 0))
    pspec = lambda s: pl.BlockSpec((HY_CB, 4), lambda j, s=s: (s * steps + j, 0))
    qspec = pl.BlockSpec((HY_CB // CH_GROUP,) + Q8_SHAPE[1:], lambda j: (j, 0, 0, 0))
    full = lambda a: pl.BlockSpec(a.shape, lambda j: (0,) * a.ndim)
    w1_blk = (D_MODEL // steps, D_FF)
    w2_blk = (D_FF // steps, D_MODEL)
    wo_blk = (D_MODEL // steps, D_MODEL)
    return pl.pallas_call(
        _hyena_kernel,
        grid=(steps,),
        in_specs=[zspec(0), zspec(1), zspec(2), pspec(0), pspec(1), pspec(2), qspec, qspec,
                  pl.BlockSpec((HY_CB, FFT_N2), lambda j: (j, 0)),
                  full(bd1), full(bd1k), full(bd1inv), full(twr), full(twi), full(w3), full(w3inv),
                  pl.BlockSpec((None,) + w1_blk, lambda j: (layer, j, 0)),
                  pl.BlockSpec((None,) + w2_blk, lambda j: (layer, j, 0)),
                  pl.BlockSpec((None,) + w1_blk, lambda j: (layer + 1, j, 0)),
                  pl.BlockSpec((None,) + w2_blk, lambda j: (layer + 1, j, 0)),
                  pl.BlockSpec(wo_blk, lambda j: (j, 0))],
        out_specs=[qspec, pl.BlockSpec(w1_blk, lambda j: (j, 0)), pl.BlockSpec(w2_blk, lambda j: (j, 0)),
                   pl.BlockSpec(w1_blk, lambda j: (j, 0)), pl.BlockSpec(w2_blk, lambda j: (j, 0)),
                   pl.BlockSpec(wo_blk, lambda j: (j, 0))],
        out_shape=[jax.ShapeDtypeStruct(Q8_SHAPE, F32),
                   jax.ShapeDtypeStruct((D_MODEL, D_FF), BF16),
                   jax.ShapeDtypeStruct((D_FF, D_MODEL), BF16),
                   jax.ShapeDtypeStruct((D_MODEL, D_FF), BF16),
                   jax.ShapeDtypeStruct((D_FF, D_MODEL), BF16),
                   jax.ShapeDtypeStruct((D_MODEL, D_MODEL), BF16)],
        compiler_params=_cparams(1),
        name="hyena_fftconv",
    )(zt, zt, zt, hy_par, hy_par, hy_par, kf_q, kb_q, bias2, bd1, bd1k, bd1inv, twr, twi, w3, w3inv,
      mlp_w1, mlp_w2, mlp_w1, mlp_w2, w_out)


OUT_TM = 512
OUT_CH = 256


def _out_proj_kernel(ya_ref, gb_ref, gcl_ref, gc_ref, gcr_ref, xvl_ref, xv_ref, xvr_ref, psc_ref,
                     wa_ref, wb_ref, x_ref, g_ref, o_ref):
    i = pl.program_id(0)
    tn = (((0,), (0,)), ((), ()))
    ya = jnp.concatenate([ya_ref[:, a].reshape(D_HY, FFT_N2) for a in range(OUT_TM // FFT_N2)],
                         axis=1).astype(BF16)
    cur = slice(LANES, LANES + OUT_TM)
    lane = lax.broadcasted_iota(jnp.int32, (OUT_CH, OUT_TM), 1)
    first = (lane == 0) & (i == 0)
    final = (lane == OUT_TM - 1) & (i == pl.num_programs(0) - 1)
    m = lax.dot_general(ya, wa_ref[...], tn, preferred_element_type=F32)
    for c in range(D_SC // OUT_CH):
        rows = slice(c * OUT_CH, (c + 1) * OUT_CH)
        window = lambda l, mid, r: jnp.concatenate([l[rows, :], mid[rows, :], r[rows, :]], axis=1).astype(F32)
        p = window(gcl_ref, gc_ref, gcr_ref) * window(xvl_ref, xv_ref, xvr_ref)
        pm = jnp.where(first, 0.0, pltpu.roll(p, 1, 1)[:, cur])
        pp = jnp.where(final, 0.0, pltpu.roll(p, p.shape[1] - 1, 1)[:, cur])
        w = psc_ref[rows, :]
        conv = pm * w[:, 0:1] + p[:, cur] * w[:, 1:2] + pp * w[:, 2:3]
        yb = (gb_ref[rows, :].astype(F32) * conv).astype(BF16)
        m = m + lax.dot_general(yb, wb_ref[rows, :], tn, preferred_element_type=F32)
    o_ref[...] = x_ref[...] + _rms_rows(m, g_ref[...])


def _out_proj(ya_q, zt, sc_par, wo, x2d, g):
    per = OUT_TM // LANES
    last = SEQ // LANES - 1
    main = lambda s: pl.BlockSpec((D_SC, OUT_TM), lambda i, s=s: (s, i))
    left = lambda s: pl.BlockSpec((D_SC, LANES), lambda i, s=s: (s, jnp.maximum(i * per - 1, 0)))
    right = lambda s: pl.BlockSpec((D_SC, LANES), lambda i, s=s: (s, jnp.minimum((i + 1) * per, last)))
    return pl.pallas_call(
        _out_proj_kernel,
        grid=(SEQ // OUT_TM,),
        in_specs=[
            pl.BlockSpec((Q8_SHAPE[0], OUT_TM // FFT_N2, CH_GROUP, FFT_N2), lambda i: (0, i, 0, 0)),
            main(3), left(4), main(4), right(4), left(5), main(5), right(5),
            pl.BlockSpec((D_SC, 4), lambda i: (0, 0)),
            pl.BlockSpec((D_HY, D_MODEL), lambda i: (0, 0)),
            pl.BlockSpec((D_SC, D_MODEL), lambda i: (1, 0)),
            pl.BlockSpec((OUT_TM, D_MODEL), lambda i: (i, 0)),
            pl.BlockSpec((1, D_MODEL), lambda i: (0, 0)),
        ],
        out_specs=pl.BlockSpec((OUT_TM, D_MODEL), lambda i: (i, 0)),
        out_shape=jax.ShapeDtypeStruct((SEQ, D_MODEL), F32),
        compiler_params=_cparams(1),
        name="out_proj",
    )(ya_q, zt, zt, zt, zt, zt, zt, zt, sc_par, wo, wo, x2d, g)


MLP_TM = 1024
MLP_TK = 1024
MLP_TA = 512
MLP_TN = 512
MLP_ROWS = 128


def _mlp_kernel(x_ref, gin_ref, gout_ref, w1_ref, w2_ref, *rest):
    if len(rest) == 6:
        nw1_ref, nw2_ref, o_ref, nw1b_ref, nw2b_ref, h_ref = rest
        nw1b_ref[...] = nw1_ref[...].astype(BF16)
        nw2b_ref[...] = nw2_ref[...].astype(BF16)
    else:
        o_ref, h_ref = rest
    k = pl.program_id(1)

    def row_chunks(fn):
        def body(c, carry):
            fn(pl.ds(pl.multiple_of(c * MLP_ROWS, MLP_ROWS), MLP_ROWS))
            return carry
        lax.fori_loop(0, MLP_TM // MLP_ROWS, body, 0)

    @pl.when(k == 0)
    def _():
        def prologue(rows):
            h_ref[rows, :] = _rms_rows(x_ref[rows, :], gin_ref[...]).astype(BF16)
            o_ref[rows, :] = jnp.zeros((MLP_ROWS, D_MODEL), F32)
        row_chunks(prologue)

    for c in range(w1_ref.shape[1] // MLP_TA):
        mid = slice(c * MLP_TA, (c + 1) * MLP_TA)
        a = jnp.dot(h_ref[...], w1_ref[:, mid], preferred_element_type=F32)
        a = jnp.square(jnp.maximum(a, 0.0)).astype(BF16)
        for n in range(D_MODEL // MLP_TN):
            cols = slice(n * MLP_TN, (n + 1) * MLP_TN)
            o_ref[:, cols] += jnp.dot(a, w2_ref[mid, cols], preferred_element_type=F32)

    @pl.when(k == pl.num_programs(1) - 1)
    def _():
        def epilogue(rows):
            o_ref[rows, :] = x_ref[rows, :] + _rms_rows(o_ref[rows, :], gout_ref[...])
        row_chunks(epilogue)


def _mlp(x2d, gin, gout, w1, w2, narrow=None):
    tk = MLP_TK if narrow is None else MLP_TK // 2
    ni, nk = SEQ // MLP_TM, D_FF // tk
    in_specs = [
        pl.BlockSpec((MLP_TM, D_MODEL), lambda i, k: (i, 0)),
        pl.BlockSpec((1, D_MODEL), lambda i, k: (0, 0)),
        pl.BlockSpec((1, D_MODEL), lambda i, k: (0, 0)),
        pl.BlockSpec((D_MODEL, tk), lambda i, k: (0, k)),
        pl.BlockSpec((tk, D_MODEL), lambda i, k: (k, 0)),
    ]
    out_specs = [pl.BlockSpec((MLP_TM, D_MODEL), lambda i, k: (i, 0))]
    out_shape = [jax.ShapeDtypeStruct((SEQ, D_MODEL), F32)]
    args = [x2d, gin, gout, w1, w2]
    if narrow is not None:
        nw1, nw2, layer = narrow
        w1_blk = (D_MODEL // ni, D_FF // nk)
        w2_blk = (D_FF // ni, D_MODEL // nk)
        in_specs += [pl.BlockSpec((None,) + w1_blk, lambda i, k: (layer, i, k)),
                     pl.BlockSpec((None,) + w2_blk, lambda i, k: (layer, i, k))]
        out_specs += [pl.BlockSpec(w1_blk, lambda i, k: (i, k)), pl.BlockSpec(w2_blk, lambda i, k: (i, k))]
        out_shape += [jax.ShapeDtypeStruct((D_MODEL, D_FF), BF16),
                      jax.ShapeDtypeStruct((D_FF, D_MODEL), BF16)]
        args += [nw1, nw2]
    return pl.pallas_call(
        _mlp_kernel,
        grid=(ni, nk),
        in_specs=in_specs,
        out_specs=out_specs,
        out_shape=out_shape,
        scratch_shapes=[pltpu.VMEM((MLP_TM, D_MODEL), BF16)],
        compiler_params=_cparams(2),
        name="mlp",
    )(*args)


POOL_TM = 512
POOL_SUB = 128
HALO = 8


def _pool_bands():
    t = np.arange(POOL_SUB)[:, None]
    j = np.arange(POOL_SUB + 2 * HALO)[None, :]
    return np.stack([(np.abs(j - HALO - t) <= w // 2) for w in POOL_WINDOWS]).astype(np.float32)


def _pool_kernel(xp_ref, xm_ref, xn_ref, gin_ref, gout_ref, band_ref, pw_ref, ps_ref, o_ref,
                 hs_ref, hi_ref, lo_ref, m_ref):
    i = pl.program_id(0)
    gin = gin_ref[...]
    hs_ref[0:HALO, :] = jnp.where(i > 0, _rms_rows(xp_ref[...], gin), 0.0)
    hs_ref[HALO:HALO + POOL_TM, :] = _rms_rows(xm_ref[...], gin)
    hs_ref[HALO + POOL_TM:, :] = jnp.where(i < pl.num_programs(0) - 1, _rms_rows(xn_ref[...], gin), 0.0)
    hs = hs_ref[...]
    hi = hs.astype(BF16)
    hi_ref[...] = hi
    lo_ref[...] = (hs - hi.astype(F32)).astype(BF16)
    row = lax.broadcasted_iota(jnp.int32, (POOL_SUB, D_POOL_G), 0) + i * POOL_TM
    for gi, w in enumerate(POOL_WINDOWS):
        r = w // 2
        cols = slice(gi * D_POOL_G, (gi + 1) * D_POOL_G)
        band = band_ref[gi]
        parts = []
        for sb in range(POOL_TM // POOL_SUB):
            win = slice(sb * POOL_SUB, (sb + 1) * POOL_SUB + 2 * HALO)
            s = (jnp.dot(band, hi_ref[win, cols], preferred_element_type=F32)
                 + jnp.dot(band, lo_ref[win, cols], preferred_element_type=F32))
            t = row + sb * POOL_SUB
            cnt = jnp.minimum(t + r + 1, SEQ) - jnp.maximum(t - r, 0)
            u = hs_ref[sb * POOL_SUB + HALO:(sb + 1) * POOL_SUB + HALO, cols]
            parts.append((s / cnt.astype(F32) - u).astype(BF16))
        d = jnp.concatenate(parts, axis=0)
        mg = jnp.dot(d, pw_ref[gi], preferred_element_type=F32)
        m_ref[:, cols] = mg * ps_ref[:, cols]
    o_ref[...] = xm_ref[...] + _rms_rows(m_ref[...], gout_ref[...])


def _pool(x2d, gin, gout, pw, ps):
    nb8 = POOL_TM // HALO
    last8 = SEQ // HALO - 1
    band = jnp.asarray(_pool_bands(), F32).astype(BF16)
    ext = POOL_TM + 2 * HALO
    return pl.pallas_call(
        _pool_kernel,
        grid=(SEQ // POOL_TM,),
        in_specs=[
            pl.BlockSpec((HALO, D_MODEL), lambda i: (jnp.maximum(i * nb8 - 1, 0), 0)),
            pl.BlockSpec((POOL_TM, D_MODEL), lambda i: (i, 0)),
            pl.BlockSpec((HALO, D_MODEL), lambda i: (jnp.minimum((i + 1) * nb8, last8), 0)),
            pl.BlockSpec((1, D_MODEL), lambda i: (0, 0)),
            pl.BlockSpec((1, D_MODEL), lambda i: (0, 0)),
            pl.BlockSpec(band.shape, lambda i: (0, 0, 0)),
            pl.BlockSpec((len(POOL_WINDOWS), D_POOL_G, D_POOL_G), lambda i: (0, 0, 0)),
            pl.BlockSpec((1, D_MODEL), lambda i: (0, 0)),
        ],
        out_specs=pl.BlockSpec((POOL_TM, D_MODEL), lambda i: (i, 0)),
        out_shape=jax.ShapeDtypeStruct((SEQ, D_MODEL), F32),
        scratch_shapes=[pltpu.VMEM((ext, D_MODEL), F32), pltpu.VMEM((ext, D_MODEL), BF16),
                        pltpu.VMEM((ext, D_MODEL), BF16), pltpu.VMEM((POOL_TM, D_MODEL), F32)],
        compiler_params=_cparams(1),
        name="pool_mixer",
    )(x2d, x2d, x2d, gin, gout, band, pw, ps)


def _filter_constants():
    L = SEQ
    t = np.linspace(0.0, 1.0, L)
    w_pos = 2.0 * np.pi * np.arange(L) / L
    bands = np.linspace(1e-4, FILTER_BANDS - 1, FILTER_BANDS)
    ang = w_pos[:, None] * bands[None, :]
    emb = np.concatenate([t[:, None], np.cos(ang), -np.sin(ang)], axis=-1)
    emb_t = np.zeros((EMB_PAD, L + LANES))
    emb_t[:EMB_DIM, :L] = emb.T
    t_rev = t[(L - np.arange(L)) % L]
    max_decay = math.log(DECAY_TARGET) / FAST_DECAY_PCT
    min_decay = math.log(DECAY_TARGET) / SLOW_DECAY_PCT
    absdelta = np.abs(np.linspace(min_decay, max_decay, D_HY))[:, None]
    f32 = lambda a: jnp.asarray(a, F32)
    return f32(emb_t), f32(t[None, :]), f32(t_rev[None, :]), f32(absdelta)


def _mixer_layer0(x2d, g, w_in, hy_short_w, hy_short_b, f_w1, f_b1, f_w2, f_b2, f_w3, f_b3,
                  freq, hy_bias, sc_conv_w, w_out, mlp_w1, mlp_w2, layer):
    emb_t, t_fwd, t_rev, absdelta = _filter_constants()
    w1t = jnp.zeros((FILTER_HIDDEN, EMB_PAD), F32).at[:, :EMB_DIM].set(f_w1.T)
    col = lambda v: v[:, None].astype(F32)
    kf_q, kb_q, wt = _filters(emb_t, t_fwd, t_rev, w1t, col(f_b1), f_w2.T, col(f_b2), col(freq),
                              f_w3.T.astype(BF16), col(f_b3), absdelta, w_in)

    zt = _in_proj(x2d, g[0][None, :], wt)
    hy_par = jnp.concatenate([hy_short_w.T, hy_short_b[:, None]], axis=1)
    sc_par = jnp.concatenate([sc_conv_w.T, jnp.zeros((D_SC, 1), F32)], axis=1)

    bias2 = jnp.broadcast_to(hy_bias[:, None].astype(F32), (D_HY, FFT_N2))
    ya_q, w1_this, w2_this, w1_next, w2_next, wo = _hyena(
        zt, hy_par, kf_q, kb_q, bias2, _dft_tables(), mlp_w1, mlp_w2, layer, w_out)
    x2d = _out_proj(ya_q, zt, sc_par, wo, x2d, g[1][None, :])
    return x2d, (w1_this, w2_this), (w1_next, w2_next)


def kernel(x, norm_g, mix_w_in, hy_short_w, hy_short_b, hy_filt_w1, hy_filt_b1, hy_filt_w2,
           hy_filt_b2, hy_filt_w3, hy_filt_b3, hy_freq, hy_bias, sc_conv_w, mix_w_out,
           pool_w, pool_scale, mlp_w1, mlp_w2):
    x2d = x.reshape(SEQ, D_MODEL)
    depth = norm_g.shape[0]
    assert depth % 2 == 0, "an even layer narrows its own MLP weights and, in its MLP, the next layer's"
    for i in range(depth):
        g = norm_g[i]
        j = i // 2
        gin, gout = g[2][None, :], g[3][None, :]
        if i % 2 == 0:
            x2d, w_this, w_next = _mixer_layer0(
                x2d, g, mix_w_in[j], hy_short_w[j], hy_short_b[j], hy_filt_w1[j], hy_filt_b1[j],
                hy_filt_w2[j], hy_filt_b2[j], hy_filt_w3[j], hy_filt_b3[j], hy_freq[j], hy_bias[j],
                sc_conv_w[j], mix_w_out[j], mlp_w1, mlp_w2, i)
            x2d, = _mlp(x2d, gin, gout, *w_this)
        else:
            x2d = _pool(x2d, g[0][None, :], g[1][None, :], pool_w[j].astype(BF16),
                        pool_scale[j][None, :])
            x2d, = _mlp(x2d, gin, gout, *w_next)
    return x2d.reshape(x.shape)
```

```python
import math

import numpy as np
import jax
import jax.numpy as jnp
from jax import lax
from jax.experimental import pallas as pl
from jax.experimental.pallas import tpu as pltpu

F32 = jnp.float32
BF16 = jnp.bfloat16

D_MODEL = 2048
SEQ = 8192
D_HY = D_MODEL // 2
D_SC = D_MODEL // 2
D_IN = 3 * D_HY + 3 * D_SC
FILTER_BANDS = 16
EMB_DIM = 1 + 2 * FILTER_BANDS
EMB_PAD = 40
FILTER_HIDDEN = 64
DECAY_TARGET = 1e-2
FAST_DECAY_PCT = 0.3
SLOW_DECAY_PCT = 1.5
POOL_WINDOWS = (2, 4, 8, 16)
D_POOL_G = D_MODEL // len(POOL_WINDOWS)
D_FF = 4 * D_MODEL
NORM_EPS = 1e-6
LANES = 128

FFT_N = 2 * SEQ
FFT_N2 = 256
FFT_N1 = FFT_N // FFT_N2
FFT_N1_LIVE = SEQ // FFT_N2
CH_GROUP = 8
PAIR = 2 * CH_GROUP
Q16_SHAPE = (D_MODEL // 2 // PAIR, FFT_N1_LIVE, PAIR, FFT_N2)

VMEM_LIMIT = 56 * 1024 * 1024


def _cparams(n_axes):
    return pltpu.CompilerParams(
        dimension_semantics=("arbitrary",) * n_axes, vmem_limit_bytes=VMEM_LIMIT)


def _rms_rows(x, g):
    r = lax.rsqrt(jnp.mean(x * x, axis=-1, keepdims=True) + NORM_EPS)
    return x * r * g


IN_TM = 1024
IN_TN = 2048


def _in_proj_kernel(x_ref, g_ref, wt_ref, zt_ref, h_ref):
    @pl.when(pl.program_id(1) == 0)
    def _():
        h_ref[...] = _rms_rows(x_ref[...], g_ref[...]).astype(BF16)

    zt_ref[...] = lax.dot_general(
        wt_ref[...], h_ref[...], (((1,), (1,)), ((), ())), preferred_element_type=F32).astype(BF16)


def _in_proj(x2d, g, wt):
    return pl.pallas_call(
        _in_proj_kernel,
        grid=(SEQ // IN_TM, D_IN // IN_TN),
        in_specs=[
            pl.BlockSpec((IN_TM, D_MODEL), lambda i, j: (i, 0)),
            pl.BlockSpec((1, D_MODEL), lambda i, j: (0, 0)),
            pl.BlockSpec((IN_TN, D_MODEL), lambda i, j: (j, 0)),
        ],
        out_specs=pl.BlockSpec((IN_TN, IN_TM), lambda i, j: (j, i)),
        out_shape=jax.ShapeDtypeStruct((D_IN, SEQ), BF16),
        scratch_shapes=[pltpu.VMEM((IN_TM, D_MODEL), BF16)],
        compiler_params=_cparams(2),
        name="in_proj",
    )(x2d, g, wt)


GATE_ROWS = 16
GATE_CHUNK = 1024


def _conv3_chunk(zwin, w, off, first, last):
    width = zwin.shape[1]
    zm = pltpu.roll(zwin, 1, 1)[:, off:off + GATE_CHUNK]
    zp = pltpu.roll(zwin, width - 1, 1)[:, off:off + GATE_CHUNK]
    z = zwin[:, off:off + GATE_CHUNK]
    lane = lax.broadcasted_iota(jnp.int32, z.shape, 1)
    if first:
        zm = jnp.where(lane == 0, 0.0, zm)
    if last:
        zp = jnp.where(lane == GATE_CHUNK - 1, 0.0, zp)
    return zm * w[:, 0:1] + z * w[:, 1:2] + zp * w[:, 2:3]


def _conv3_bias_chunks(z_ref, p_ref, rows):
    n_chunks = SEQ // GATE_CHUNK
    p = p_ref[rows, :]
    out = []
    for c in range(n_chunks):
        lo = max(c * GATE_CHUNK - LANES, 0)
        hi = min((c + 1) * GATE_CHUNK + LANES, SEQ)
        zwin = z_ref[rows, lo:hi].astype(F32)
        out.append(_conv3_chunk(zwin, p, c * GATE_CHUNK - lo, c == 0, c == n_chunks - 1) + p[:, 3:4])
    return out


def _to_q_groups(chunks):
    groups = []
    for g in range(GATE_ROWS // CH_GROUP):
        slabs = [ch[g * CH_GROUP:(g + 1) * CH_GROUP, a * FFT_N2:(a + 1) * FFT_N2]
                 for ch in chunks for a in range(GATE_CHUNK // FFT_N2)]
        groups.append(jnp.stack(slabs, axis=0))
    return groups


FILT_TB = 1024


def _filter_kernel(emb_ref, embn_ref, tf_ref, tb_ref, w1_ref, b1_ref, w2_ref, b2_ref, fr_ref, w3_ref, b3_ref,
                   ad_ref, rev_ref, win_ref, kf_ref, kb_ref, wt_ref, kfs_ref, kbs_ref):
    wt_ref[...] = win_ref[...].T.astype(BF16)
    hi = lax.Precision.HIGHEST
    fr = fr_ref[...]
    emb = jnp.concatenate([emb_ref[...], embn_ref[...]], axis=1)
    a1 = jnp.dot(w1_ref[...], emb, precision=hi, preferred_element_type=F32) + b1_ref[...]
    h1 = jnp.sin(fr * a1)
    a2 = jnp.dot(w2_ref[...], h1, precision=hi, preferred_element_type=F32) + b2_ref[...]
    h2 = jnp.sin(fr * a2)
    ad = ad_ref[...]
    h2f = h2[:, :FILT_TB].astype(BF16)
    of = jnp.dot(w3_ref[:D_HY, :], h2f, preferred_element_type=F32) + b3_ref[:D_HY, :]
    kfs_ref[...] = of * jnp.exp(-(ad * tf_ref[...]))
    h2b = pltpu.roll(h2, FILT_TB + LANES - 1, 1)[:, :FILT_TB].astype(BF16)
    h2r = jnp.dot(h2b, rev_ref[...], preferred_element_type=F32).astype(BF16)
    ob = jnp.dot(w3_ref[D_HY:, :], h2r, preferred_element_type=F32) + b3_ref[D_HY:, :]
    lane = lax.broadcasted_iota(jnp.int32, (D_HY, FILT_TB), 1)
    is_sep = (lane == 0) & (pl.program_id(0) == pl.num_programs(0) - 1)
    kbs_ref[...] = jnp.where(is_sep, 0.0, ob * jnp.exp(-(ad * tb_ref[...])))

    def to_q(cp, carry):
        rows = pl.ds(pl.multiple_of(cp * PAIR, PAIR), PAIR)
        for n1 in range(FILT_TB // FFT_N2):
            lanes = slice(n1 * FFT_N2, (n1 + 1) * FFT_N2)
            kf_ref[cp, n1] = kfs_ref[rows, lanes].astype(BF16)
            kb_ref[cp, n1] = kbs_ref[rows, lanes].astype(BF16)
        return carry

    lax.fori_loop(0, D_HY // PAIR, to_q, 0)


def _filters(emb_t, t_fwd, t_rev, w1t, b1, w2t, b2, fr, w3t, b3, absdelta, w_in):
    full = lambda a: pl.BlockSpec(a.shape, lambda i: (0,) * a.ndim)
    steps = SEQ // FILT_TB
    per = FILT_TB // LANES
    rev = jnp.asarray(np.eye(FILT_TB)[::-1], F32).astype(BF16)
    out = jax.ShapeDtypeStruct(Q16_SHAPE, BF16)
    qblk = (Q16_SHAPE[0], FILT_TB // FFT_N2, PAIR, FFT_N2)
    return pl.pallas_call(
        _filter_kernel,
        grid=(steps,),
        in_specs=[pl.BlockSpec((EMB_PAD, FILT_TB), lambda i: (0, i)),
                  pl.BlockSpec((EMB_PAD, LANES), lambda i: (0, (i + 1) * per)),
                  pl.BlockSpec((1, FILT_TB), lambda i: (0, i)),
                  pl.BlockSpec((1, FILT_TB), lambda i: (0, steps - 1 - i)),
                  full(w1t), full(b1), full(w2t), full(b2), full(fr), full(w3t), full(b3),
                  full(absdelta), full(rev),
                  pl.BlockSpec((D_MODEL, D_IN // steps), lambda i: (0, i))],
        out_specs=[pl.BlockSpec(qblk, lambda i: (0, i, 0, 0)),
                   pl.BlockSpec(qblk, lambda i: (0, steps - 1 - i, 0, 0)),
                   pl.BlockSpec((D_IN // steps, D_MODEL), lambda i: (i, 0))],
        out_shape=[out, out, jax.ShapeDtypeStruct((D_IN, D_MODEL), BF16)],
        scratch_shapes=[pltpu.VMEM((D_HY, FILT_TB), F32)] * 2,
        compiler_params=_cparams(1),
        name="filters",
    )(emb_t, emb_t, t_fwd, t_rev, w1t, b1, w2t, b2, fr, w3t, b3, absdelta, rev, w_in)


HY_CB = 32
GROUP_ROWS = CH_GROUP * FFT_N1_LIVE
N_K1 = FFT_N1 // 2 + 1
RE_ROWS = N_K1 * CH_GROUP
IM_ROWS = (N_K1 - 2) * CH_GROUP


def _dft_tables():
    n1 = np.arange(FFT_N1_LIVE, dtype=np.float64)
    k1 = np.arange(N_K1, dtype=np.float64)
    th = 2.0 * np.pi * np.outer(k1, n1) / FFT_N1
    eye = np.eye(CH_GROUP)
    def stage1(cos, sin):
        return np.concatenate([np.einsum("kn,cd->kcnd", cos, eye).reshape(RE_ROWS, -1),
                               np.einsum("kn,cd->kcnd", -sin[1:-1], eye).reshape(IM_ROWS, -1)])
    bd1 = stage1(np.cos(th), np.sin(th))
    thk = 2.0 * np.pi * np.outer(k1, np.arange(FFT_N1, dtype=np.float64)) / FFT_N1
    bd1k = stage1(np.cos(thk), np.sin(thk))
    wgt = np.where((k1 == 0) | (k1 == FFT_N1 // 2), 1.0, 2.0)[:, None] / FFT_N
    bd1inv = np.concatenate(
        [np.einsum("kn,cd->nckd", wgt * np.cos(th), eye).reshape(GROUP_ROWS, RE_ROWS),
         np.einsum("kn,cd->nckd", (-wgt * np.sin(th))[1:-1], eye).reshape(GROUP_ROWS, IM_ROWS)], axis=1)
    n2 = np.arange(FFT_N2, dtype=np.float64)
    ph = 2.0 * np.pi * np.outer(k1, n2) / FFT_N
    twr = np.repeat(np.cos(ph), CH_GROUP, axis=0)
    twi = np.repeat(-np.sin(ph), CH_GROUP, axis=0)
    ps = 2.0 * np.pi * np.outer(n2, n2) / FFT_N2
    cr, ci = np.cos(ps), -np.sin(ps)
    w3 = np.block([[cr, ci], [-ci, cr]])
    w3inv = np.block([[cr, -ci], [ci, cr]])
    f32 = lambda a: jnp.asarray(a, F32)
    return (f32(bd1).astype(BF16), f32(bd1k).astype(BF16), f32(bd1inv).astype(BF16), f32(twr), f32(twi),
            f32(w3).astype(BF16), f32(w3inv).astype(BF16))


def _hyena_kernel(zx0_ref, zx1_ref, zv_ref, px0_ref, px1_ref, pv_ref, kf_ref, kb_ref, bias_ref,
                  bd1_ref, bd1k_ref, bd1inv_ref, twr_ref, twi_ref, w3_ref, w3inv_ref,
                  w1_ref, w2_ref, w1n_ref, w2n_ref, wo_ref,
                  ya_ref, w1b_ref, w2b_ref, w1nb_ref, w2nb_ref, wob_ref):
    for src, dst in ((w1_ref, w1b_ref), (w2_ref, w2b_ref), (w1n_ref, w1nb_ref), (w2n_ref, w2nb_ref),
                     (wo_ref, wob_ref)):
        dst[...] = src[...].astype(BF16)
    twr, twi = twr_ref[...], twi_ref[...]
    zrow = jnp.zeros((CH_GROUP, FFT_N2), F32)

    def stage1(xq, bd_ref):
        xq = xq.reshape(xq.shape[0] * CH_GROUP, FFT_N2).astype(BF16)
        a = jnp.dot(bd_ref[...], xq, preferred_element_type=F32)
        ar = a[:RE_ROWS]
        ai = jnp.concatenate([zrow, a[RE_ROWS:], zrow], axis=0)
        br = ar * twr - ai * twi
        bi = ar * twi + ai * twr
        return jnp.concatenate([br, bi], axis=1)

    def pair(pi):
        gs = [2 * pi, 2 * pi + 1]
        rows = slice(pi * GATE_ROWS, (pi + 1) * GATE_ROWS)
        x1c = _conv3_bias_chunks(zx1_ref, px1_ref, rows)
        vc = _conv3_bias_chunks(zv_ref, pv_ref, rows)
        us = _to_q_groups([v * x1 for v, x1 in zip(vc, x1c)])
        x0s = _to_q_groups(_conv3_bias_chunks(zx0_ref, px0_ref, rows))
        kpair = jnp.concatenate([kf_ref[pi], kb_ref[pi]], axis=0).astype(F32)
        ks = [kpair[:, i * CH_GROUP:(i + 1) * CH_GROUP, :] for i in range(2)]
        b_all = jnp.concatenate(
            [blk for u, k in zip(us, ks) for blk in (stage1(u, bd1_ref), stage1(k, bd1k_ref))], axis=0)
        s_all = jnp.dot(b_all.astype(BF16), w3_ref[...], preferred_element_type=F32)
        ycats = []
        for i in range(2):
            xs = s_all[(2 * i) * RE_ROWS:(2 * i + 1) * RE_ROWS]
            ks = s_all[(2 * i + 1) * RE_ROWS:(2 * i + 2) * RE_ROWS]
            xr, xi = xs[:, :FFT_N2], xs[:, FFT_N2:]
            kr, ki = ks[:, :FFT_N2], ks[:, FFT_N2:]
            ycats.append(jnp.concatenate([xr * kr - xi * ki, xr * ki + xi * kr], axis=1))
        ap_all = jnp.dot(jnp.concatenate(ycats, axis=0).astype(BF16), w3inv_ref[...],
                         preferred_element_type=F32)
        yas = []
        for i, (g, u) in enumerate(zip(gs, us)):
            ap = ap_all[i * RE_ROWS:(i + 1) * RE_ROWS]
            apr, api = ap[:, :FFT_N2], ap[:, FFT_N2:]
            bpr = apr * twr + api * twi
            bpi = api * twr - apr * twi
            b2 = jnp.concatenate([bpr, bpi[CH_GROUP:RE_ROWS - CH_GROUP]], axis=0).astype(BF16)
            y = jnp.dot(bd1inv_ref[...], b2, preferred_element_type=F32)
            y3 = y.reshape(FFT_N1_LIVE, CH_GROUP, FFT_N2)
            bias = bias_ref[g * CH_GROUP:(g + 1) * CH_GROUP, :]
            yas.append(x0s[i] * (y3 + bias[None] * u))
        ya_ref[pi] = jnp.concatenate(yas, axis=1).astype(BF16)

    for pi in range(HY_CB // GATE_ROWS):
        pair(pi)


def _hyena(zt, hy_par, kf_q, kb_q, bias2, tables, mlp_w1, mlp_w2, layer, w_out):
    bd1, bd1k, bd1inv, twr, twi, w3, w3inv = tables
    steps = D_HY // HY_CB
    zspec = lambda s: pl.BlockSpec((HY_CB, SEQ), lambda j, s=s: (s * steps + j, 0))
    pspec = lambda s: pl.BlockSpec((HY_CB, 4), lambda j, s=s: (s * steps + j, 0))
    qspec = pl.BlockSpec((HY_CB // PAIR,) + Q16_SHAPE[1:], lambda j: (j, 0, 0, 0))
    full = lambda a: pl.BlockSpec(a.shape, lambda j: (0,) * a.ndim)
    w1_blk = (D_MODEL // steps, D_FF)
    w2_blk = (D_FF // steps, D_MODEL)
    wo_blk = (D_MODEL // steps, D_MODEL)
    return pl.pallas_call(
        _hyena_kernel,
        grid=(steps,),
        in_specs=[zspec(0), zspec(1), zspec(2), pspec(0), pspec(1), pspec(2), qspec, qspec,
                  pl.BlockSpec((HY_CB, FFT_N2), lambda j: (j, 0)),
                  full(bd1), full(bd1k), full(bd1inv), full(twr), full(twi), full(w3), full(w3inv),
                  pl.BlockSpec((None,) + w1_blk, lambda j: (layer, j, 0)),
                  pl.BlockSpec((None,) + w2_blk, lambda j: (layer, j, 0)),
                  pl.BlockSpec((None,) + w1_blk, lambda j: (layer + 1, j, 0)),
                  pl.BlockSpec((None,) + w2_blk, lambda j: (layer + 1, j, 0)),
                  pl.BlockSpec(wo_blk, lambda j: (j, 0))],
        out_specs=[qspec, pl.BlockSpec(w1_blk, lambda j: (j, 0)), pl.BlockSpec(w2_blk, lambda j: (j, 0)),
                   pl.BlockSpec(w1_blk, lambda j: (j, 0)), pl.BlockSpec(w2_blk, lambda j: (j, 0)),
                   pl.BlockSpec(wo_blk, lambda j: (j, 0))],
        out_shape=[jax.ShapeDtypeStruct(Q16_SHAPE, BF16),
                   jax.ShapeDtypeStruct((D_MODEL, D_FF), BF16),
                   jax.ShapeDtypeStruct((D_FF, D_MODEL), BF16),
                   jax.ShapeDtypeStruct((D_MODEL, D_FF), BF16),
                   jax.ShapeDtypeStruct((D_FF, D_MODEL), BF16),
                   jax.ShapeDtypeStruct((D_MODEL, D_MODEL), BF16)],
        compiler_params=_cparams(1),
        name="hyena_fftconv",
    )(zt, zt, zt, hy_par, hy_par, hy_par, kf_q, kb_q, bias2, bd1, bd1k, bd1inv, twr, twi, w3, w3inv,
      mlp_w1, mlp_w2, mlp_w1, mlp_w2, w_out)


OUT_TM = 512
OUT_CH = 256


def _out_proj_kernel(ya_ref, gb_ref, gcl_ref, gc_ref, gcr_ref, xvl_ref, xv_ref, xvr_ref, psc_ref,
                     wa_ref, wb_ref, x_ref, g_ref, o_ref):
    i = pl.program_id(0)
    tn = (((0,), (0,)), ((), ()))
    ya = jnp.concatenate([ya_ref[:, a].reshape(D_HY, FFT_N2) for a in range(OUT_TM // FFT_N2)], axis=1)
    cur = slice(LANES, LANES + OUT_TM)
    lane = lax.broadcasted_iota(jnp.int32, (OUT_CH, OUT_TM), 1)
    first = (lane == 0) & (i == 0)
    final = (lane == OUT_TM - 1) & (i == pl.num_programs(0) - 1)
    m = lax.dot_general(ya, wa_ref[...], tn, preferred_element_type=F32)
    for c in range(D_SC // OUT_CH):
        rows = slice(c * OUT_CH, (c + 1) * OUT_CH)
        window = lambda l, mid, r: jnp.concatenate([l[rows, :], mid[rows, :], r[rows, :]], axis=1).astype(F32)
        p = window(gcl_ref, gc_ref, gcr_ref) * window(xvl_ref, xv_ref, xvr_ref)
        pm = jnp.where(first, 0.0, pltpu.roll(p, 1, 1)[:, cur])
        pp = jnp.where(final, 0.0, pltpu.roll(p, p.shape[1] - 1, 1)[:, cur])
        w = psc_ref[rows, :]
        conv = pm * w[:, 0:1] + p[:, cur] * w[:, 1:2] + pp * w[:, 2:3]
        yb = (gb_ref[rows, :].astype(F32) * conv).astype(BF16)
        m = m + lax.dot_general(yb, wb_ref[rows, :], tn, preferred_element_type=F32)
    o_ref[...] = x_ref[...] + _rms_rows(m, g_ref[...])


def _out_proj(ya_q, zt, sc_par, wo, x2d, g):
    per = OUT_TM // LANES
    last = SEQ // LANES - 1
    main = lambda s: pl.BlockSpec((D_SC, OUT_TM), lambda i, s=s: (s, i))
    left = lambda s: pl.BlockSpec((D_SC, LANES), lambda i, s=s: (s, jnp.maximum(i * per - 1, 0)))
    right = lambda s: pl.BlockSpec((D_SC, LANES), lambda i, s=s: (s, jnp.minimum((i + 1) * per, last)))
    return pl.pallas_call(
        _out_proj_kernel,
        grid=(SEQ // OUT_TM,),
        in_specs=[
            pl.BlockSpec((Q16_SHAPE[0], OUT_TM // FFT_N2, PAIR, FFT_N2), lambda i: (0, i, 0, 0)),
            main(3), left(4), main(4), right(4), left(5), main(5), right(5),
            pl.BlockSpec((D_SC, 4), lambda i: (0, 0)),
            pl.BlockSpec((D_HY, D_MODEL), lambda i: (0, 0)),
            pl.BlockSpec((D_SC, D_MODEL), lambda i: (1, 0)),
            pl.BlockSpec((OUT_TM, D_MODEL), lambda i: (i, 0)),
            pl.BlockSpec((1, D_MODEL), lambda i: (0, 0)),
        ],
        out_specs=pl.BlockSpec((OUT_TM, D_MODEL), lambda i: (i, 0)),
        out_shape=jax.ShapeDtypeStruct((SEQ, D_MODEL), F32),
        compiler_params=_cparams(1),
        name="out_proj",
    )(ya_q, zt, zt, zt, zt, zt, zt, zt, sc_par, wo, wo, x2d, g)


MLP_TM = 1024
MLP_TK = 1024
MLP_TA = 512
MLP_TN = 512
MLP_ROWS = 128


def _mlp_kernel(x_ref, gin_ref, gout_ref, w1_ref, w2_ref, o_ref, h_ref):
    k = pl.program_id(1)

    def row_chunks(fn):
        def body(c, carry):
            fn(pl.ds(pl.multiple_of(c * MLP_ROWS, MLP_ROWS), MLP_ROWS))
            return carry
        lax.fori_loop(0, MLP_TM // MLP_ROWS, body, 0)

    @pl.when(k == 0)
    def _():
        def prologue(rows):
            h_ref[rows, :] = _rms_rows(x_ref[rows, :], gin_ref[...]).astype(BF16)
            o_ref[rows, :] = jnp.zeros((MLP_ROWS, D_MODEL), F32)
        row_chunks(prologue)

    for c in range(MLP_TK // MLP_TA):
        mid = slice(c * MLP_TA, (c + 1) * MLP_TA)
        a = jnp.dot(h_ref[...], w1_ref[:, mid], preferred_element_type=F32)
        a = jnp.square(jnp.maximum(a, 0.0)).astype(BF16)
        for n in range(D_MODEL // MLP_TN):
            cols = slice(n * MLP_TN, (n + 1) * MLP_TN)
            o_ref[:, cols] += jnp.dot(a, w2_ref[mid, cols], preferred_element_type=F32)

    @pl.when(k == pl.num_programs(1) - 1)
    def _():
        def epilogue(rows):
            o_ref[rows, :] = x_ref[rows, :] + _rms_rows(o_ref[rows, :], gout_ref[...])
        row_chunks(epilogue)


def _mlp(x2d, gin, gout, w1, w2):
    return pl.pallas_call(
        _mlp_kernel,
        grid=(SEQ // MLP_TM, D_FF // MLP_TK),
        in_specs=[
            pl.BlockSpec((MLP_TM, D_MODEL), lambda i, k: (i, 0)),
            pl.BlockSpec((1, D_MODEL), lambda i, k: (0, 0)),
            pl.BlockSpec((1, D_MODEL), lambda i, k: (0, 0)),
            pl.BlockSpec((D_MODEL, MLP_TK), lambda i, k: (0, k)),
            pl.BlockSpec((MLP_TK, D_MODEL), lambda i, k: (k, 0)),
        ],
        out_specs=pl.BlockSpec((MLP_TM, D_MODEL), lambda i, k: (i, 0)),
        out_shape=jax.ShapeDtypeStruct((SEQ, D_MODEL), F32),
        scratch_shapes=[pltpu.VMEM((MLP_TM, D_MODEL), BF16)],
        compiler_params=_cparams(2),
        name="mlp",
    )(x2d, gin, gout, w1, w2)


POOL_TM = 512
POOL_SUB = 128
HALO = 8


def _pool_bands():
    t = np.arange(POOL_SUB)[:, None]
    j = np.arange(POOL_SUB + 2 * HALO)[None, :]
    return np.stack([(np.abs(j - HALO - t) <= w // 2) for w in POOL_WINDOWS]).astype(np.float32)


def _pool_kernel(xp_ref, xm_ref, xn_ref, gin_ref, gout_ref, band_ref, pw_ref, ps_ref, o_ref,
                 hs_ref, hi_ref, lo_ref, m_ref):
    i = pl.program_id(0)
    gin = gin_ref[...]
    hs_ref[0:HALO, :] = jnp.where(i > 0, _rms_rows(xp_ref[...], gin), 0.0)
    hs_ref[HALO:HALO + POOL_TM, :] = _rms_rows(xm_ref[...], gin)
    hs_ref[HALO + POOL_TM:, :] = jnp.where(i < pl.num_programs(0) - 1, _rms_rows(xn_ref[...], gin), 0.0)
    hs = hs_ref[...]
    hi = hs.astype(BF16)
    hi_ref[...] = hi
    lo_ref[...] = (hs - hi.astype(F32)).astype(BF16)
    row = lax.broadcasted_iota(jnp.int32, (POOL_SUB, D_POOL_G), 0) + i * POOL_TM
    for gi, w in enumerate(POOL_WINDOWS):
        r = w // 2
        cols = slice(gi * D_POOL_G, (gi + 1) * D_POOL_G)
        band = band_ref[gi]
        parts = []
        for sb in range(POOL_TM // POOL_SUB):
            win = slice(sb * POOL_SUB, (sb + 1) * POOL_SUB + 2 * HALO)
            s = (jnp.dot(band, hi_ref[win, cols], preferred_element_type=F32)
                 + jnp.dot(band, lo_ref[win, cols], preferred_element_type=F32))
            t = row + sb * POOL_SUB
            cnt = jnp.minimum(t + r + 1, SEQ) - jnp.maximum(t - r, 0)
            u = hs_ref[sb * POOL_SUB + HALO:(sb + 1) * POOL_SUB + HALO, cols]
            parts.append((s / cnt.astype(F32) - u).astype(BF16))
        d = jnp.concatenate(parts, axis=0)
        mg = jnp.dot(d, pw_ref[gi], preferred_element_type=F32)
        m_ref[:, cols] = mg * ps_ref[:, cols]
    o_ref[...] = xm_ref[...] + _rms_rows(m_ref[...], gout_ref[...])


def _pool(x2d, gin, gout, pw, ps):
    nb8 = POOL_TM // HALO
    last8 = SEQ // HALO - 1
    band = jnp.asarray(_pool_bands(), F32).astype(BF16)
    ext = POOL_TM + 2 * HALO
    return pl.pallas_call(
        _pool_kernel,
        grid=(SEQ // POOL_TM,),
        in_specs=[
            pl.BlockSpec((HALO, D_MODEL), lambda i: (jnp.maximum(i * nb8 - 1, 0), 0)),
            pl.BlockSpec((POOL_TM, D_MODEL), lambda i: (i, 0)),
            pl.BlockSpec((HALO, D_MODEL), lambda i: (jnp.minimum((i + 1) * nb8, last8), 0)),
            pl.BlockSpec((1, D_MODEL), lambda i: (0, 0)),
            pl.BlockSpec((1, D_MODEL), lambda i: (0, 0)),
            pl.BlockSpec(band.shape, lambda i: (0, 0, 0)),
            pl.BlockSpec((len(POOL_WINDOWS), D_POOL_G, D_POOL_G), lambda i: (0, 0, 0)),
            pl.BlockSpec((1, D_MODEL), lambda i: (0, 0)),
        ],
        out_specs=pl.BlockSpec((POOL_TM, D_MODEL), lambda i: (i, 0)),
        out_shape=jax.ShapeDtypeStruct((SEQ, D_MODEL), F32),
        scratch_shapes=[pltpu.VMEM((ext, D_MODEL), F32), pltpu.VMEM((ext, D_MODEL), BF16),
                        pltpu.VMEM((ext, D_MODEL), BF16), pltpu.VMEM((POOL_TM, D_MODEL), F32)],
        compiler_params=_cparams(1),
        name="pool_mixer",
    )(x2d, x2d, x2d, gin, gout, band, pw, ps)


def _filter_constants():
    L = SEQ
    t = np.linspace(0.0, 1.0, L)
    w_pos = 2.0 * np.pi * np.arange(L) / L
    bands = np.linspace(1e-4, FILTER_BANDS - 1, FILTER_BANDS)
    ang = w_pos[:, None] * bands[None, :]
    emb = np.concatenate([t[:, None], np.cos(ang), -np.sin(ang)], axis=-1)
    emb_t = np.zeros((EMB_PAD, L + LANES))
    emb_t[:EMB_DIM, :L] = emb.T
    t_rev = t[(L - np.arange(L)) % L]
    max_decay = math.log(DECAY_TARGET) / FAST_DECAY_PCT
    min_decay = math.log(DECAY_TARGET) / SLOW_DECAY_PCT
    absdelta = np.abs(np.linspace(min_decay, max_decay, D_HY))[:, None]
    f32 = lambda a: jnp.asarray(a, F32)
    return f32(emb_t), f32(t[None, :]), f32(t_rev[None, :]), f32(absdelta)


def _mixer_layer0(x2d, g, w_in, hy_short_w, hy_short_b, f_w1, f_b1, f_w2, f_b2, f_w3, f_b3,
                  freq, hy_bias, sc_conv_w, w_out, mlp_w1, mlp_w2, layer):
    emb_t, t_fwd, t_rev, absdelta = _filter_constants()
    w1t = jnp.zeros((FILTER_HIDDEN, EMB_PAD), F32).at[:, :EMB_DIM].set(f_w1.T)
    col = lambda v: v[:, None].astype(F32)
    kf_q, kb_q, wt = _filters(emb_t, t_fwd, t_rev, w1t, col(f_b1), f_w2.T, col(f_b2), col(freq),
                              f_w3.T.astype(BF16), col(f_b3), absdelta, w_in)

    zt = _in_proj(x2d, g[0][None, :], wt)
    hy_par = jnp.concatenate([hy_short_w.T, hy_short_b[:, None]], axis=1)
    sc_par = jnp.concatenate([sc_conv_w.T, jnp.zeros((D_SC, 1), F32)], axis=1)

    bias2 = jnp.broadcast_to(hy_bias[:, None].astype(F32), (D_HY, FFT_N2))
    ya_q, w1_this, w2_this, w1_next, w2_next, wo = _hyena(
        zt, hy_par, kf_q, kb_q, bias2, _dft_tables(), mlp_w1, mlp_w2, layer, w_out)
    x2d = _out_proj(ya_q, zt, sc_par, wo, x2d, g[1][None, :])
    return x2d, (w1_this, w2_this), (w1_next, w2_next)


def kernel(x, norm_g, mix_w_in, hy_short_w, hy_short_b, hy_filt_w1, hy_filt_b1, hy_filt_w2,
           hy_filt_b2, hy_filt_w3, hy_filt_b3, hy_freq, hy_bias, sc_conv_w, mix_w_out,
           pool_w, pool_scale, mlp_w1, mlp_w2):
    x2d = x.reshape(SEQ, D_MODEL)
    depth = norm_g.shape[0]
    assert depth % 2 == 0, "an even layer's long-conv kernel narrows the MLP weights of itself and the next layer"
    for i in range(depth):
        g = norm_g[i]
        j = i // 2
        gin, gout = g[2][None, :], g[3][None, :]
        if i % 2 == 0:
            x2d, w_this, w_next = _mixer_layer0(
                x2d, g, mix_w_in[j], hy_short_w[j], hy_short_b[j], hy_filt_w1[j], hy_filt_b1[j],
                hy_filt_w2[j], hy_filt_b2[j], hy_filt_w3[j], hy_filt_b3[j], hy_freq[j], hy_bias[j],
                sc_conv_w[j], mix_w_out[j], mlp_w1, mlp_w2, i)
            x2d = _mlp(x2d, gin, gout, *w_this)
        else:
            x2d = _pool(x2d, g[0][None, :], g[1][None, :], pool_w[j].astype(BF16),
                        pool_scale[j][None, :])
            x2d = _mlp(x2d, gin, gout, *w_next)
    return x2d.reshape(x.shape)
```

```python
import math

import numpy as np
import jax
import jax.numpy as jnp
from jax import lax
from jax.experimental import pallas as pl
from jax.experimental.pallas import tpu as pltpu

F32 = jnp.float32
BF16 = jnp.bfloat16

D_MODEL = 2048
SEQ = 8192
D_HY = D_MODEL // 2
D_SC = D_MODEL // 2
D_IN = 3 * D_HY + 3 * D_SC
FILTER_BANDS = 16
EMB_DIM = 1 + 2 * FILTER_BANDS
EMB_PAD = 40
FILTER_HIDDEN = 64
DECAY_TARGET = 1e-2
FAST_DECAY_PCT = 0.3
SLOW_DECAY_PCT = 1.5
POOL_WINDOWS = (2, 4, 8, 16)
D_POOL_G = D_MODEL // len(POOL_WINDOWS)
D_FF = 4 * D_MODEL
NORM_EPS = 1e-6
LANES = 128

FFT_N = 2 * SEQ
FFT_N2 = 256
FFT_N1 = FFT_N // FFT_N2
FFT_N1_LIVE = SEQ // FFT_N2
CH_GROUP = 8
PAIR = 2 * CH_GROUP
Q16_SHAPE = (D_MODEL // 2 // PAIR, FFT_N1_LIVE, PAIR, FFT_N2)

VMEM_LIMIT = 56 * 1024 * 1024


def _cparams(n_axes):
    return pltpu.CompilerParams(
        dimension_semantics=("arbitrary",) * n_axes, vmem_limit_bytes=VMEM_LIMIT)


def _rms_rows(x, g):
    r = lax.rsqrt(jnp.mean(x * x, axis=-1, keepdims=True) + NORM_EPS)
    return x * r * g


IN_TM = 1024
IN_TN = 2048


def _in_proj_kernel(x_ref, g_ref, wt_ref, zt_ref, h_ref):
    @pl.when(pl.program_id(1) == 0)
    def _():
        h_ref[...] = _rms_rows(x_ref[...], g_ref[...]).astype(BF16)

    zt_ref[...] = lax.dot_general(
        wt_ref[...], h_ref[...], (((1,), (1,)), ((), ())), preferred_element_type=F32).astype(BF16)


def _in_proj(x2d, g, wt):
    return pl.pallas_call(
        _in_proj_kernel,
        grid=(SEQ // IN_TM, D_IN // IN_TN),
        in_specs=[
            pl.BlockSpec((IN_TM, D_MODEL), lambda i, j: (i, 0)),
            pl.BlockSpec((1, D_MODEL), lambda i, j: (0, 0)),
            pl.BlockSpec((IN_TN, D_MODEL), lambda i, j: (j, 0)),
        ],
        out_specs=pl.BlockSpec((IN_TN, IN_TM), lambda i, j: (j, i)),
        out_shape=jax.ShapeDtypeStruct((D_IN, SEQ), BF16),
        scratch_shapes=[pltpu.VMEM((IN_TM, D_MODEL), BF16)],
        compiler_params=_cparams(2),
        name="in_proj",
    )(x2d, g, wt)


GATE_ROWS = 16
GATE_CHUNK = 1024


def _conv3_chunk(zwin, w, off, first, last):
    width = zwin.shape[1]
    zm = pltpu.roll(zwin, 1, 1)[:, off:off + GATE_CHUNK]
    zp = pltpu.roll(zwin, width - 1, 1)[:, off:off + GATE_CHUNK]
    z = zwin[:, off:off + GATE_CHUNK]
    lane = lax.broadcasted_iota(jnp.int32, z.shape, 1)
    if first:
        zm = jnp.where(lane == 0, 0.0, zm)
    if last:
        zp = jnp.where(lane == GATE_CHUNK - 1, 0.0, zp)
    return zm * w[:, 0:1] + z * w[:, 1:2] + zp * w[:, 2:3]


def _conv3_bias_chunks(z_ref, p_ref, rows):
    n_chunks = SEQ // GATE_CHUNK
    p = p_ref[rows, :]
    out = []
    for c in range(n_chunks):
        lo = max(c * GATE_CHUNK - LANES, 0)
        hi = min((c + 1) * GATE_CHUNK + LANES, SEQ)
        zwin = z_ref[rows, lo:hi].astype(F32)
        out.append(_conv3_chunk(zwin, p, c * GATE_CHUNK - lo, c == 0, c == n_chunks - 1) + p[:, 3:4])
    return out


def _to_q_groups(chunks):
    groups = []
    for g in range(GATE_ROWS // CH_GROUP):
        slabs = [ch[g * CH_GROUP:(g + 1) * CH_GROUP, a * FFT_N2:(a + 1) * FFT_N2]
                 for ch in chunks for a in range(GATE_CHUNK // FFT_N2)]
        groups.append(jnp.stack(slabs, axis=0))
    return groups


FILT_TB = 1024


def _filter_kernel(emb_ref, embn_ref, tf_ref, tb_ref, w1_ref, b1_ref, w2_ref, b2_ref, fr_ref, w3_ref, b3_ref,
                   ad_ref, rev_ref, win_ref, wo_ref, kf_ref, kb_ref, wt_ref, wob_ref, kfs_ref, kbs_ref):
    wt_ref[...] = win_ref[...].T.astype(BF16)
    wob_ref[...] = wo_ref[...].astype(BF16)
    hi = lax.Precision.HIGHEST
    fr = fr_ref[...]
    emb = jnp.concatenate([emb_ref[...], embn_ref[...]], axis=1)
    a1 = jnp.dot(w1_ref[...], emb, precision=hi, preferred_element_type=F32) + b1_ref[...]
    h1 = jnp.sin(fr * a1)
    a2 = jnp.dot(w2_ref[...], h1, precision=hi, preferred_element_type=F32) + b2_ref[...]
    h2 = jnp.sin(fr * a2)
    ad = ad_ref[...]
    h2f = h2[:, :FILT_TB].astype(BF16)
    of = jnp.dot(w3_ref[:D_HY, :], h2f, preferred_element_type=F32) + b3_ref[:D_HY, :]
    kfs_ref[...] = of * jnp.exp(-(ad * tf_ref[...]))
    h2b = pltpu.roll(h2, FILT_TB + LANES - 1, 1)[:, :FILT_TB].astype(BF16)
    h2r = jnp.dot(h2b, rev_ref[...], preferred_element_type=F32).astype(BF16)
    ob = jnp.dot(w3_ref[D_HY:, :], h2r, preferred_element_type=F32) + b3_ref[D_HY:, :]
    lane = lax.broadcasted_iota(jnp.int32, (D_HY, FILT_TB), 1)
    is_sep = (lane == 0) & (pl.program_id(0) == pl.num_programs(0) - 1)
    kbs_ref[...] = jnp.where(is_sep, 0.0, ob * jnp.exp(-(ad * tb_ref[...])))

    def to_q(cp, carry):
        rows = pl.ds(pl.multiple_of(cp * PAIR, PAIR), PAIR)
        for n1 in range(FILT_TB // FFT_N2):
            lanes = slice(n1 * FFT_N2, (n1 + 1) * FFT_N2)
            kf_ref[cp, n1] = kfs_ref[rows, lanes].astype(BF16)
            kb_ref[cp, n1] = kbs_ref[rows, lanes].astype(BF16)
        return carry

    lax.fori_loop(0, D_HY // PAIR, to_q, 0)


def _filters(emb_t, t_fwd, t_rev, w1t, b1, w2t, b2, fr, w3t, b3, absdelta, w_in, w_out):
    full = lambda a: pl.BlockSpec(a.shape, lambda i: (0,) * a.ndim)
    steps = SEQ // FILT_TB
    per = FILT_TB // LANES
    rev = jnp.asarray(np.eye(FILT_TB)[::-1], F32).astype(BF16)
    out = jax.ShapeDtypeStruct(Q16_SHAPE, BF16)
    qblk = (Q16_SHAPE[0], FILT_TB // FFT_N2, PAIR, FFT_N2)
    return pl.pallas_call(
        _filter_kernel,
        grid=(steps,),
        in_specs=[pl.BlockSpec((EMB_PAD, FILT_TB), lambda i: (0, i)),
                  pl.BlockSpec((EMB_PAD, LANES), lambda i: (0, (i + 1) * per)),
                  pl.BlockSpec((1, FILT_TB), lambda i: (0, i)),
                  pl.BlockSpec((1, FILT_TB), lambda i: (0, steps - 1 - i)),
                  full(w1t), full(b1), full(w2t), full(b2), full(fr), full(w3t), full(b3),
                  full(absdelta), full(rev),
                  pl.BlockSpec((D_MODEL, D_IN // steps), lambda i: (0, i)),
                  pl.BlockSpec((D_MODEL // steps, D_MODEL), lambda i: (i, 0))],
        out_specs=[pl.BlockSpec(qblk, lambda i: (0, i, 0, 0)),
                   pl.BlockSpec(qblk, lambda i: (0, steps - 1 - i, 0, 0)),
                   pl.BlockSpec((D_IN // steps, D_MODEL), lambda i: (i, 0)),
                   pl.BlockSpec((D_MODEL // steps, D_MODEL), lambda i: (i, 0))],
        out_shape=[out, out, jax.ShapeDtypeStruct((D_IN, D_MODEL), BF16),
                   jax.ShapeDtypeStruct((D_MODEL, D_MODEL), BF16)],
        scratch_shapes=[pltpu.VMEM((D_HY, FILT_TB), F32)] * 2,
        compiler_params=_cparams(1),
        name="filters",
    )(emb_t, emb_t, t_fwd, t_rev, w1t, b1, w2t, b2, fr, w3t, b3, absdelta, rev, w_in, w_out)


HY_CB = 32
GROUP_ROWS = CH_GROUP * FFT_N1_LIVE
N_K1 = FFT_N1 // 2 + 1
RE_ROWS = N_K1 * CH_GROUP
IM_ROWS = (N_K1 - 2) * CH_GROUP


def _dft_tables():
    n1 = np.arange(FFT_N1_LIVE, dtype=np.float64)
    k1 = np.arange(N_K1, dtype=np.float64)
    th = 2.0 * np.pi * np.outer(k1, n1) / FFT_N1
    eye = np.eye(CH_GROUP)
    def stage1(cos, sin):
        return np.concatenate([np.einsum("kn,cd->kcnd", cos, eye).reshape(RE_ROWS, -1),
                               np.einsum("kn,cd->kcnd", -sin[1:-1], eye).reshape(IM_ROWS, -1)])
    bd1 = stage1(np.cos(th), np.sin(th))
    thk = 2.0 * np.pi * np.outer(k1, np.arange(FFT_N1, dtype=np.float64)) / FFT_N1
    bd1k = stage1(np.cos(thk), np.sin(thk))
    wgt = np.where((k1 == 0) | (k1 == FFT_N1 // 2), 1.0, 2.0)[:, None] / FFT_N
    bd1inv = np.concatenate(
        [np.einsum("kn,cd->nckd", wgt * np.cos(th), eye).reshape(GROUP_ROWS, RE_ROWS),
         np.einsum("kn,cd->nckd", (-wgt * np.sin(th))[1:-1], eye).reshape(GROUP_ROWS, IM_ROWS)], axis=1)
    n2 = np.arange(FFT_N2, dtype=np.float64)
    ph = 2.0 * np.pi * np.outer(k1, n2) / FFT_N
    twr = np.repeat(np.cos(ph), CH_GROUP, axis=0)
    twi = np.repeat(-np.sin(ph), CH_GROUP, axis=0)
    ps = 2.0 * np.pi * np.outer(n2, n2) / FFT_N2
    cr, ci = np.cos(ps), -np.sin(ps)
    w3 = np.block([[cr, ci], [-ci, cr]])
    w3inv = np.block([[cr, -ci], [ci, cr]])
    f32 = lambda a: jnp.asarray(a, F32)
    return (f32(bd1).astype(BF16), f32(bd1k).astype(BF16), f32(bd1inv).astype(BF16), f32(twr), f32(twi),
            f32(w3).astype(BF16), f32(w3inv).astype(BF16))


def _hyena_kernel(zx0_ref, zx1_ref, zv_ref, px0_ref, px1_ref, pv_ref, kf_ref, kb_ref, bias_ref,
                  bd1_ref, bd1k_ref, bd1inv_ref, twr_ref, twi_ref, w3_ref, w3inv_ref,
                  w1_ref, w2_ref, w1n_ref, w2n_ref, ya_ref, w1b_ref, w2b_ref, w1nb_ref, w2nb_ref):
    for src, dst in ((w1_ref, w1b_ref), (w2_ref, w2b_ref), (w1n_ref, w1nb_ref), (w2n_ref, w2nb_ref)):
        dst[...] = src[...].astype(BF16)
    twr, twi = twr_ref[...], twi_ref[...]
    zrow = jnp.zeros((CH_GROUP, FFT_N2), F32)

    def stage1(xq, bd_ref):
        xq = xq.reshape(xq.shape[0] * CH_GROUP, FFT_N2).astype(BF16)
        a = jnp.dot(bd_ref[...], xq, preferred_element_type=F32)
        ar = a[:RE_ROWS]
        ai = jnp.concatenate([zrow, a[RE_ROWS:], zrow], axis=0)
        br = ar * twr - ai * twi
        bi = ar * twi + ai * twr
        return jnp.concatenate([br, bi], axis=1)

    def pair(pi):
        gs = [2 * pi, 2 * pi + 1]
        rows = slice(pi * GATE_ROWS, (pi + 1) * GATE_ROWS)
        x1c = _conv3_bias_chunks(zx1_ref, px1_ref, rows)
        vc = _conv3_bias_chunks(zv_ref, pv_ref, rows)
        us = _to_q_groups([v * x1 for v, x1 in zip(vc, x1c)])
        x0s = _to_q_groups(_conv3_bias_chunks(zx0_ref, px0_ref, rows))
        kpair = jnp.concatenate([kf_ref[pi], kb_ref[pi]], axis=0).astype(F32)
        ks = [kpair[:, i * CH_GROUP:(i + 1) * CH_GROUP, :] for i in range(2)]
        b_all = jnp.concatenate(
            [blk for u, k in zip(us, ks) for blk in (stage1(u, bd1_ref), stage1(k, bd1k_ref))], axis=0)
        s_all = jnp.dot(b_all.astype(BF16), w3_ref[...], preferred_element_type=F32)
        ycats = []
        for i in range(2):
            xs = s_all[(2 * i) * RE_ROWS:(2 * i + 1) * RE_ROWS]
            ks = s_all[(2 * i + 1) * RE_ROWS:(2 * i + 2) * RE_ROWS]
            xr, xi = xs[:, :FFT_N2], xs[:, FFT_N2:]
            kr, ki = ks[:, :FFT_N2], ks[:, FFT_N2:]
            ycats.append(jnp.concatenate([xr * kr - xi * ki, xr * ki + xi * kr], axis=1))
        ap_all = jnp.dot(jnp.concatenate(ycats, axis=0).astype(BF16), w3inv_ref[...],
                         preferred_element_type=F32)
        yas = []
        for i, (g, u) in enumerate(zip(gs, us)):
            ap = ap_all[i * RE_ROWS:(i + 1) * RE_ROWS]
            apr, api = ap[:, :FFT_N2], ap[:, FFT_N2:]
            bpr = apr * twr + api * twi
            bpi = api * twr - apr * twi
            b2 = jnp.concatenate([bpr, bpi[CH_GROUP:RE_ROWS - CH_GROUP]], axis=0).astype(BF16)
            y = jnp.dot(bd1inv_ref[...], b2, preferred_element_type=F32)
            y3 = y.reshape(FFT_N1_LIVE, CH_GROUP, FFT_N2)
            bias = bias_ref[g * CH_GROUP:(g + 1) * CH_GROUP, :]
            yas.append(x0s[i] * (y3 + bias[None] * u))
        ya_ref[pi] = jnp.concatenate(yas, axis=1).astype(BF16)

    for pi in range(HY_CB // GATE_ROWS):
        pair(pi)


def _hyena(zt, hy_par, kf_q, kb_q, bias2, tables, mlp_w1, mlp_w2, layer):
    bd1, bd1k, bd1inv, twr, twi, w3, w3inv = tables
    steps = D_HY // HY_CB
    zspec = lambda s: pl.BlockSpec((HY_CB, SEQ), lambda j, s=s: (s * steps + j, 0))
    pspec = lambda s: pl.BlockSpec((HY_CB, 4), lambda j, s=s: (s * steps + j, 0))
    qspec = pl.BlockSpec((HY_CB // PAIR,) + Q16_SHAPE[1:], lambda j: (j, 0, 0, 0))
    full = lambda a: pl.BlockSpec(a.shape, lambda j: (0,) * a.ndim)
    w1_blk = (D_MODEL // steps, D_FF)
    w2_blk = (D_FF // steps, D_MODEL)
    return pl.pallas_call(
        _hyena_kernel,
        grid=(steps,),
        in_specs=[zspec(0), zspec(1), zspec(2), pspec(0), pspec(1), pspec(2), qspec, qspec,
                  pl.BlockSpec((HY_CB, FFT_N2), lambda j: (j, 0)),
                  full(bd1), full(bd1k), full(bd1inv), full(twr), full(twi), full(w3), full(w3inv),
                  pl.BlockSpec((None,) + w1_blk, lambda j: (layer, j, 0)),
                  pl.BlockSpec((None,) + w2_blk, lambda j: (layer, j, 0)),
                  pl.BlockSpec((None,) + w1_blk, lambda j: (layer + 1, j, 0)),
                  pl.BlockSpec((None,) + w2_blk, lambda j: (layer + 1, j, 0))],
        out_specs=[qspec, pl.BlockSpec(w1_blk, lambda j: (j, 0)), pl.BlockSpec(w2_blk, lambda j: (j, 0)),
                   pl.BlockSpec(w1_blk, lambda j: (j, 0)), pl.BlockSpec(w2_blk, lambda j: (j, 0))],
        out_shape=[jax.ShapeDtypeStruct(Q16_SHAPE, BF16),
                   jax.ShapeDtypeStruct((D_MODEL, D_FF), BF16),
                   jax.ShapeDtypeStruct((D_FF, D_MODEL), BF16),
                   jax.ShapeDtypeStruct((D_MODEL, D_FF), BF16),
                   jax.ShapeDtypeStruct((D_FF, D_MODEL), BF16)],
        compiler_params=_cparams(1),
        name="hyena_fftconv",
    )(zt, zt, zt, hy_par, hy_par, hy_par, kf_q, kb_q, bias2, bd1, bd1k, bd1inv, twr, twi, w3, w3inv,
      mlp_w1, mlp_w2, mlp_w1, mlp_w2)


OUT_TM = 512
OUT_CH = 256


def _out_proj_kernel(ya_ref, gb_ref, gcl_ref, gc_ref, gcr_ref, xvl_ref, xv_ref, xvr_ref, psc_ref,
                     wa_ref, wb_ref, x_ref, g_ref, o_ref):
    i = pl.program_id(0)
    tn = (((0,), (0,)), ((), ()))
    ya = jnp.concatenate([ya_ref[:, a].reshape(D_HY, FFT_N2) for a in range(OUT_TM // FFT_N2)], axis=1)
    cur = slice(LANES, LANES + OUT_TM)
    lane = lax.broadcasted_iota(jnp.int32, (OUT_CH, OUT_TM), 1)
    first = (lane == 0) & (i == 0)
    final = (lane == OUT_TM - 1) & (i == pl.num_programs(0) - 1)
    m = lax.dot_general(ya, wa_ref[...], tn, preferred_element_type=F32)
    for c in range(D_SC // OUT_CH):
        rows = slice(c * OUT_CH, (c + 1) * OUT_CH)
        window = lambda l, mid, r: jnp.concatenate([l[rows, :], mid[rows, :], r[rows, :]], axis=1).astype(F32)
        p = window(gcl_ref, gc_ref, gcr_ref) * window(xvl_ref, xv_ref, xvr_ref)
        pm = jnp.where(first, 0.0, pltpu.roll(p, 1, 1)[:, cur])
        pp = jnp.where(final, 0.0, pltpu.roll(p, p.shape[1] - 1, 1)[:, cur])
        w = psc_ref[rows, :]
        conv = pm * w[:, 0:1] + p[:, cur] * w[:, 1:2] + pp * w[:, 2:3]
        yb = (gb_ref[rows, :].astype(F32) * conv).astype(BF16)
        m = m + lax.dot_general(yb, wb_ref[rows, :], tn, preferred_element_type=F32)
    o_ref[...] = x_ref[...] + _rms_rows(m, g_ref[...])


def _out_proj(ya_q, zt, sc_par, wo, x2d, g):
    per = OUT_TM // LANES
    last = SEQ // LANES - 1
    main = lambda s: pl.BlockSpec((D_SC, OUT_TM), lambda i, s=s: (s, i))
    left = lambda s: pl.BlockSpec((D_SC, LANES), lambda i, s=s: (s, jnp.maximum(i * per - 1, 0)))
    right = lambda s: pl.BlockSpec((D_SC, LANES), lambda i, s=s: (s, jnp.minimum((i + 1) * per, last)))
    return pl.pallas_call(
        _out_proj_kernel,
        grid=(SEQ // OUT_TM,),
        in_specs=[
            pl.BlockSpec((Q16_SHAPE[0], OUT_TM // FFT_N2, PAIR, FFT_N2), lambda i: (0, i, 0, 0)),
            main(3), left(4), main(4), right(4), left(5), main(5), right(5),
            pl.BlockSpec((D_SC, 4), lambda i: (0, 0)),
            pl.BlockSpec((D_HY, D_MODEL), lambda i: (0, 0)),
            pl.BlockSpec((D_SC, D_MODEL), lambda i: (1, 0)),
            pl.BlockSpec((OUT_TM, D_MODEL), lambda i: (i, 0)),
            pl.BlockSpec((1, D_MODEL), lambda i: (0, 0)),
        ],
        out_specs=pl.BlockSpec((OUT_TM, D_MODEL), lambda i: (i, 0)),
        out_shape=jax.ShapeDtypeStruct((SEQ, D_MODEL), F32),
        compiler_params=_cparams(1),
        name="out_proj",
    )(ya_q, zt, zt, zt, zt, zt, zt, zt, sc_par, wo, wo, x2d, g)


MLP_TM = 1024
MLP_TK = 1024
MLP_TA = 512
MLP_TN = 512
MLP_ROWS = 128


def _mlp_kernel(x_ref, gin_ref, gout_ref, w1_ref, w2_ref, o_ref, h_ref):
    k = pl.program_id(1)

    def row_chunks(fn):
        def body(c, carry):
            fn(pl.ds(pl.multiple_of(c * MLP_ROWS, MLP_ROWS), MLP_ROWS))
            return carry
        lax.fori_loop(0, MLP_TM // MLP_ROWS, body, 0)

    @pl.when(k == 0)
    def _():
        def prologue(rows):
            h_ref[rows, :] = _rms_rows(x_ref[rows, :], gin_ref[...]).astype(BF16)
            o_ref[rows, :] = jnp.zeros((MLP_ROWS, D_MODEL), F32)
        row_chunks(prologue)

    for c in range(MLP_TK // MLP_TA):
        mid = slice(c * MLP_TA, (c + 1) * MLP_TA)
        a = jnp.dot(h_ref[...], w1_ref[:, mid], preferred_element_type=F32)
        a = jnp.square(jnp.maximum(a, 0.0)).astype(BF16)
        for n in range(D_MODEL // MLP_TN):
            cols = slice(n * MLP_TN, (n + 1) * MLP_TN)
            o_ref[:, cols] += jnp.dot(a, w2_ref[mid, cols], preferred_element_type=F32)

    @pl.when(k == pl.num_programs(1) - 1)
    def _():
        def epilogue(rows):
            o_ref[rows, :] = x_ref[rows, :] + _rms_rows(o_ref[rows, :], gout_ref[...])
        row_chunks(epilogue)


def _mlp(x2d, gin, gout, w1, w2):
    return pl.pallas_call(
        _mlp_kernel,
        grid=(SEQ // MLP_TM, D_FF // MLP_TK),
        in_specs=[
            pl.BlockSpec((MLP_TM, D_MODEL), lambda i, k: (i, 0)),
            pl.BlockSpec((1, D_MODEL), lambda i, k: (0, 0)),
            pl.BlockSpec((1, D_MODEL), lambda i, k: (0, 0)),
            pl.BlockSpec((D_MODEL, MLP_TK), lambda i, k: (0, k)),
            pl.BlockSpec((MLP_TK, D_MODEL), lambda i, k: (k, 0)),
        ],
        out_specs=pl.BlockSpec((MLP_TM, D_MODEL), lambda i, k: (i, 0)),
        out_shape=jax.ShapeDtypeStruct((SEQ, D_MODEL), F32),
        scratch_shapes=[pltpu.VMEM((MLP_TM, D_MODEL), BF16)],
        compiler_params=_cparams(2),
        name="mlp",
    )(x2d, gin, gout, w1, w2)


POOL_TM = 512
POOL_SUB = 128
HALO = 8


def _pool_bands():
    t = np.arange(POOL_SUB)[:, None]
    j = np.arange(POOL_SUB + 2 * HALO)[None, :]
    return np.stack([(np.abs(j - HALO - t) <= w // 2) for w in POOL_WINDOWS]).astype(np.float32)


def _pool_kernel(xp_ref, xm_ref, xn_ref, gin_ref, gout_ref, band_ref, pw_ref, ps_ref, o_ref,
                 hs_ref, hi_ref, lo_ref, m_ref):
    i = pl.program_id(0)
    gin = gin_ref[...]
    hs_ref[0:HALO, :] = jnp.where(i > 0, _rms_rows(xp_ref[...], gin), 0.0)
    hs_ref[HALO:HALO + POOL_TM, :] = _rms_rows(xm_ref[...], gin)
    hs_ref[HALO + POOL_TM:, :] = jnp.where(i < pl.num_programs(0) - 1, _rms_rows(xn_ref[...], gin), 0.0)
    hs = hs_ref[...]
    hi = hs.astype(BF16)
    hi_ref[...] = hi
    lo_ref[...] = (hs - hi.astype(F32)).astype(BF16)
    row = lax.broadcasted_iota(jnp.int32, (POOL_SUB, D_POOL_G), 0) + i * POOL_TM
    for gi, w in enumerate(POOL_WINDOWS):
        r = w // 2
        cols = slice(gi * D_POOL_G, (gi + 1) * D_POOL_G)
        band = band_ref[gi]
        parts = []
        for sb in range(POOL_TM // POOL_SUB):
            win = slice(sb * POOL_SUB, (sb + 1) * POOL_SUB + 2 * HALO)
            s = (jnp.dot(band, hi_ref[win, cols], preferred_element_type=F32)
                 + jnp.dot(band, lo_ref[win, cols], preferred_element_type=F32))
            t = row + sb * POOL_SUB
            cnt = jnp.minimum(t + r + 1, SEQ) - jnp.maximum(t - r, 0)
            u = hs_ref[sb * POOL_SUB + HALO:(sb + 1) * POOL_SUB + HALO, cols]
            parts.append((s / cnt.astype(F32) - u).astype(BF16))
        d = jnp.concatenate(parts, axis=0)
        mg = jnp.dot(d, pw_ref[gi], preferred_element_type=F32)
        m_ref[:, cols] = mg * ps_ref[:, cols]
    o_ref[...] = xm_ref[...] + _rms_rows(m_ref[...], gout_ref[...])


def _pool(x2d, gin, gout, pw, ps):
    nb8 = POOL_TM // HALO
    last8 = SEQ // HALO - 1
    band = jnp.asarray(_pool_bands(), F32).astype(BF16)
    ext = POOL_TM + 2 * HALO
    return pl.pallas_call(
        _pool_kernel,
        grid=(SEQ // POOL_TM,),
        in_specs=[
            pl.BlockSpec((HALO, D_MODEL), lambda i: (jnp.maximum(i * nb8 - 1, 0), 0)),
            pl.BlockSpec((POOL_TM, D_MODEL), lambda i: (i, 0)),
            pl.BlockSpec((HALO, D_MODEL), lambda i: (jnp.minimum((i + 1) * nb8, last8), 0)),
            pl.BlockSpec((1, D_MODEL), lambda i: (0, 0)),
            pl.BlockSpec((1, D_MODEL), lambda i: (0, 0)),
            pl.BlockSpec(band.shape, lambda i: (0, 0, 0)),
            pl.BlockSpec((len(POOL_WINDOWS), D_POOL_G, D_POOL_G), lambda i: (0, 0, 0)),
            pl.BlockSpec((1, D_MODEL), lambda i: (0, 0)),
        ],
        out_specs=pl.BlockSpec((POOL_TM, D_MODEL), lambda i: (i, 0)),
        out_shape=jax.ShapeDtypeStruct((SEQ, D_MODEL), F32),
        scratch_shapes=[pltpu.VMEM((ext, D_MODEL), F32), pltpu.VMEM((ext, D_MODEL), BF16),
                        pltpu.VMEM((ext, D_MODEL), BF16), pltpu.VMEM((POOL_TM, D_MODEL), F32)],
        compiler_params=_cparams(1),
        name="pool_mixer",
    )(x2d, x2d, x2d, gin, gout, band, pw, ps)


def _filter_constants():
    L = SEQ
    t = np.linspace(0.0, 1.0, L)
    w_pos = 2.0 * np.pi * np.arange(L) / L
    bands = np.linspace(1e-4, FILTER_BANDS - 1, FILTER_BANDS)
    ang = w_pos[:, None] * bands[None, :]
    emb = np.concatenate([t[:, None], np.cos(ang), -np.sin(ang)], axis=-1)
    emb_t = np.zeros((EMB_PAD, L + LANES))
    emb_t[:EMB_DIM, :L] = emb.T
    t_rev = t[(L - np.arange(L)) % L]
    max_decay = math.log(DECAY_TARGET) / FAST_DECAY_PCT
    min_decay = math.log(DECAY_TARGET) / SLOW_DECAY_PCT
    absdelta = np.abs(np.linspace(min_decay, max_decay, D_HY))[:, None]
    f32 = lambda a: jnp.asarray(a, F32)
    return f32(emb_t), f32(t[None, :]), f32(t_rev[None, :]), f32(absdelta)


def _mixer_layer0(x2d, g, w_in, hy_short_w, hy_short_b, f_w1, f_b1, f_w2, f_b2, f_w3, f_b3,
                  freq, hy_bias, sc_conv_w, w_out, mlp_w1, mlp_w2, layer):
    emb_t, t_fwd, t_rev, absdelta = _filter_constants()
    w1t = jnp.zeros((FILTER_HIDDEN, EMB_PAD), F32).at[:, :EMB_DIM].set(f_w1.T)
    col = lambda v: v[:, None].astype(F32)
    kf_q, kb_q, wt, wo = _filters(emb_t, t_fwd, t_rev, w1t, col(f_b1), f_w2.T, col(f_b2), col(freq),
                                  f_w3.T.astype(BF16), col(f_b3), absdelta, w_in, w_out)

    zt = _in_proj(x2d, g[0][None, :], wt)
    hy_par = jnp.concatenate([hy_short_w.T, hy_short_b[:, None]], axis=1)
    sc_par = jnp.concatenate([sc_conv_w.T, jnp.zeros((D_SC, 1), F32)], axis=1)

    bias2 = jnp.broadcast_to(hy_bias[:, None].astype(F32), (D_HY, FFT_N2))
    ya_q, w1_this, w2_this, w1_next, w2_next = _hyena(
        zt, hy_par, kf_q, kb_q, bias2, _dft_tables(), mlp_w1, mlp_w2, layer)
    x2d = _out_proj(ya_q, zt, sc_par, wo, x2d, g[1][None, :])
    return x2d, (w1_this, w2_this), (w1_next, w2_next)


def kernel(x, norm_g, mix_w_in, hy_short_w, hy_short_b, hy_filt_w1, hy_filt_b1, hy_filt_w2,
           hy_filt_b2, hy_filt_w3, hy_filt_b3, hy_freq, hy_bias, sc_conv_w, mix_w_out,
           pool_w, pool_scale, mlp_w1, mlp_w2):
    x2d = x.reshape(SEQ, D_MODEL)
    depth = norm_g.shape[0]
    assert depth % 2 == 0, "an even layer's long-conv kernel narrows the MLP weights of itself and the next layer"
    for i in range(depth):
        g = norm_g[i]
        j = i // 2
        gin, gout = g[2][None, :], g[3][None, :]
        if i % 2 == 0:
            x2d, w_this, w_next = _mixer_layer0(
                x2d, g, mix_w_in[j], hy_short_w[j], hy_short_b[j], hy_filt_w1[j], hy_filt_b1[j],
                hy_filt_w2[j], hy_filt_b2[j], hy_filt_w3[j], hy_filt_b3[j], hy_freq[j], hy_bias[j],
                sc_conv_w[j], mix_w_out[j], mlp_w1, mlp_w2, i)
            x2d = _mlp(x2d, gin, gout, *w_this)
        else:
            x2d = _pool(x2d, g[0][None, :], g[1][None, :], pool_w[j].astype(BF16),
                        pool_scale[j][None, :])
            x2d = _mlp(x2d, gin, gout, *w_next)
    return x2d.reshape(x.shape)
```

```python
import math

import numpy as np
import jax
import jax.numpy as jnp
from jax import lax
from jax.experimental import pallas as pl
from jax.experimental.pallas import tpu as pltpu

F32 = jnp.float32
BF16 = jnp.bfloat16

D_MODEL = 2048
SEQ = 8192
D_HY = D_MODEL // 2
D_SC = D_MODEL // 2
D_IN = 3 * D_HY + 3 * D_SC
FILTER_BANDS = 16
EMB_DIM = 1 + 2 * FILTER_BANDS
EMB_PAD = 40
FILTER_HIDDEN = 64
DECAY_TARGET = 1e-2
FAST_DECAY_PCT = 0.3
SLOW_DECAY_PCT = 1.5
POOL_WINDOWS = (2, 4, 8, 16)
D_POOL_G = D_MODEL // len(POOL_WINDOWS)
D_FF = 4 * D_MODEL
NORM_EPS = 1e-6
LANES = 128

FFT_N = 2 * SEQ
FFT_N2 = 256
FFT_N1 = FFT_N // FFT_N2
FFT_N1_LIVE = SEQ // FFT_N2
CH_GROUP = 8
PAIR = 2 * CH_GROUP
Q16_SHAPE = (D_MODEL // 2 // PAIR, FFT_N1_LIVE, PAIR, FFT_N2)

VMEM_LIMIT = 56 * 1024 * 1024


def _cparams(n_axes):
    return pltpu.CompilerParams(
        dimension_semantics=("arbitrary",) * n_axes, vmem_limit_bytes=VMEM_LIMIT)


def _rms_rows(x, g):
    r = lax.rsqrt(jnp.mean(x * x, axis=-1, keepdims=True) + NORM_EPS)
    return x * r * g


IN_TM = 1024
IN_TN = 1536
IN_TC = 256


def _in_proj_kernel(x_ref, g_ref, wt_ref, w1_ref, zt_ref, w1b_ref, h_ref):
    w1b_ref[...] = w1_ref[...].astype(BF16)

    @pl.when(pl.program_id(1) == 0)
    def _():
        h_ref[...] = _rms_rows(x_ref[...], g_ref[...]).astype(BF16)

    for c in range(IN_TM // IN_TC):
        t = slice(c * IN_TC, (c + 1) * IN_TC)
        zt_ref[:, t] = lax.dot_general(
            wt_ref[...], h_ref[t, :], (((1,), (1,)), ((), ())), preferred_element_type=F32).astype(BF16)


def _in_proj(x2d, g, wt, mlp_w1, layer):
    ni, nj = SEQ // IN_TM, D_IN // IN_TN
    w1_blk = (D_MODEL // ni, D_FF // nj)
    return pl.pallas_call(
        _in_proj_kernel,
        grid=(ni, nj),
        in_specs=[
            pl.BlockSpec((IN_TM, D_MODEL), lambda i, j: (i, 0)),
            pl.BlockSpec((1, D_MODEL), lambda i, j: (0, 0)),
            pl.BlockSpec((IN_TN, D_MODEL), lambda i, j: (j, 0)),
            pl.BlockSpec((None,) + w1_blk, lambda i, j: (layer, i, j)),
        ],
        out_specs=[pl.BlockSpec((IN_TN, IN_TM), lambda i, j: (j, i)),
                   pl.BlockSpec(w1_blk, lambda i, j: (i, j))],
        out_shape=[jax.ShapeDtypeStruct((D_IN, SEQ), BF16),
                   jax.ShapeDtypeStruct((D_MODEL, D_FF), BF16)],
        scratch_shapes=[pltpu.VMEM((IN_TM, D_MODEL), BF16)],
        compiler_params=_cparams(2),
        name="in_proj",
    )(x2d, g, wt, mlp_w1)


GATE_ROWS = 16
GATE_CHUNK = 1024


def _conv3_chunk(zwin, w, off, first, last):
    width = zwin.shape[1]
    zm = pltpu.roll(zwin, 1, 1)[:, off:off + GATE_CHUNK]
    zp = pltpu.roll(zwin, width - 1, 1)[:, off:off + GATE_CHUNK]
    z = zwin[:, off:off + GATE_CHUNK]
    lane = lax.broadcasted_iota(jnp.int32, z.shape, 1)
    if first:
        zm = jnp.where(lane == 0, 0.0, zm)
    if last:
        zp = jnp.where(lane == GATE_CHUNK - 1, 0.0, zp)
    return zm * w[:, 0:1] + z * w[:, 1:2] + zp * w[:, 2:3]


def _conv3_bias_chunks(z_ref, p_ref, rows):
    n_chunks = SEQ // GATE_CHUNK
    p = p_ref[rows, :]
    out = []
    for c in range(n_chunks):
        lo = max(c * GATE_CHUNK - LANES, 0)
        hi = min((c + 1) * GATE_CHUNK + LANES, SEQ)
        zwin = z_ref[rows, lo:hi].astype(F32)
        out.append(_conv3_chunk(zwin, p, c * GATE_CHUNK - lo, c == 0, c == n_chunks - 1) + p[:, 3:4])
    return out


def _to_q_groups(chunks):
    groups = []
    for g in range(GATE_ROWS // CH_GROUP):
        slabs = [ch[g * CH_GROUP:(g + 1) * CH_GROUP, a * FFT_N2:(a + 1) * FFT_N2]
                 for ch in chunks for a in range(GATE_CHUNK // FFT_N2)]
        groups.append(jnp.stack(slabs, axis=0))
    return groups


FILT_TB = 1024


def _filter_kernel(emb_ref, embn_ref, tf_ref, tb_ref, w1_ref, b1_ref, w2_ref, b2_ref, fr_ref, w3_ref, b3_ref,
                   ad_ref, rev_ref, win_ref, wo_ref, kf_ref, kb_ref, wt_ref, wob_ref, kfs_ref, kbs_ref):
    wt_ref[...] = win_ref[...].T.astype(BF16)
    wob_ref[...] = wo_ref[...].astype(BF16)
    hi = lax.Precision.HIGHEST
    fr = fr_ref[...]
    emb = jnp.concatenate([emb_ref[...], embn_ref[...]], axis=1)
    a1 = jnp.dot(w1_ref[...], emb, precision=hi, preferred_element_type=F32) + b1_ref[...]
    h1 = jnp.sin(fr * a1)
    a2 = jnp.dot(w2_ref[...], h1, precision=hi, preferred_element_type=F32) + b2_ref[...]
    h2 = jnp.sin(fr * a2)
    ad = ad_ref[...]
    h2f = h2[:, :FILT_TB].astype(BF16)
    of = jnp.dot(w3_ref[:D_HY, :], h2f, preferred_element_type=F32) + b3_ref[:D_HY, :]
    kfs_ref[...] = of * jnp.exp(-(ad * tf_ref[...]))
    h2b = pltpu.roll(h2, FILT_TB + LANES - 1, 1)[:, :FILT_TB].astype(BF16)
    h2r = jnp.dot(h2b, rev_ref[...], preferred_element_type=F32).astype(BF16)
    ob = jnp.dot(w3_ref[D_HY:, :], h2r, preferred_element_type=F32) + b3_ref[D_HY:, :]
    lane = lax.broadcasted_iota(jnp.int32, (D_HY, FILT_TB), 1)
    is_sep = (lane == 0) & (pl.program_id(0) == pl.num_programs(0) - 1)
    kbs_ref[...] = jnp.where(is_sep, 0.0, ob * jnp.exp(-(ad * tb_ref[...])))

    def to_q(cp, carry):
        rows = pl.ds(pl.multiple_of(cp * PAIR, PAIR), PAIR)
        for n1 in range(FILT_TB // FFT_N2):
            lanes = slice(n1 * FFT_N2, (n1 + 1) * FFT_N2)
            kf_ref[cp, n1] = kfs_ref[rows, lanes].astype(BF16)
            kb_ref[cp, n1] = kbs_ref[rows, lanes].astype(BF16)
        return carry

    lax.fori_loop(0, D_HY // PAIR, to_q, 0)


def _filters(emb_t, t_fwd, t_rev, w1t, b1, w2t, b2, fr, w3t, b3, absdelta, w_in, w_out):
    full = lambda a: pl.BlockSpec(a.shape, lambda i: (0,) * a.ndim)
    steps = SEQ // FILT_TB
    per = FILT_TB // LANES
    rev = jnp.asarray(np.eye(FILT_TB)[::-1], BF16)
    out = jax.ShapeDtypeStruct(Q16_SHAPE, BF16)
    qblk = (Q16_SHAPE[0], FILT_TB // FFT_N2, PAIR, FFT_N2)
    return pl.pallas_call(
        _filter_kernel,
        grid=(steps,),
        in_specs=[pl.BlockSpec((EMB_PAD, FILT_TB), lambda i: (0, i)),
                  pl.BlockSpec((EMB_PAD, LANES), lambda i: (0, (i + 1) * per)),
                  pl.BlockSpec((1, FILT_TB), lambda i: (0, i)),
                  pl.BlockSpec((1, FILT_TB), lambda i: (0, steps - 1 - i)),
                  full(w1t), full(b1), full(w2t), full(b2), full(fr), full(w3t), full(b3),
                  full(absdelta), full(rev),
                  pl.BlockSpec((D_MODEL, D_IN // steps), lambda i: (0, i)),
                  pl.BlockSpec((D_MODEL // steps, D_MODEL), lambda i: (i, 0))],
        out_specs=[pl.BlockSpec(qblk, lambda i: (0, i, 0, 0)),
                   pl.BlockSpec(qblk, lambda i: (0, steps - 1 - i, 0, 0)),
                   pl.BlockSpec((D_IN // steps, D_MODEL), lambda i: (i, 0)),
                   pl.BlockSpec((D_MODEL // steps, D_MODEL), lambda i: (i, 0))],
        out_shape=[out, out, jax.ShapeDtypeStruct((D_IN, D_MODEL), BF16),
                   jax.ShapeDtypeStruct((D_MODEL, D_MODEL), BF16)],
        scratch_shapes=[pltpu.VMEM((D_HY, FILT_TB), F32)] * 2,
        compiler_params=_cparams(1),
        name="filters",
    )(emb_t, emb_t, t_fwd, t_rev, w1t, b1, w2t, b2, fr, w3t, b3, absdelta, rev, w_in, w_out)


HY_CB = 32
GROUP_ROWS = CH_GROUP * FFT_N1_LIVE
N_K1 = FFT_N1 // 2 + 1
RE_ROWS = N_K1 * CH_GROUP
IM_ROWS = (N_K1 - 2) * CH_GROUP


def _dft_tables():
    n1 = np.arange(FFT_N1_LIVE, dtype=np.float64)
    k1 = np.arange(N_K1, dtype=np.float64)
    th = 2.0 * np.pi * np.outer(k1, n1) / FFT_N1
    eye = np.eye(CH_GROUP)
    def stage1(cos, sin):
        return np.concatenate([np.einsum("kn,cd->kcnd", cos, eye).reshape(RE_ROWS, -1),
                               np.einsum("kn,cd->kcnd", -sin[1:-1], eye).reshape(IM_ROWS, -1)])
    bd1 = stage1(np.cos(th), np.sin(th))
    thk = 2.0 * np.pi * np.outer(k1, np.arange(FFT_N1, dtype=np.float64)) / FFT_N1
    bd1k = stage1(np.cos(thk), np.sin(thk))
    wgt = np.where((k1 == 0) | (k1 == FFT_N1 // 2), 1.0, 2.0)[:, None] / FFT_N
    bd1inv = np.concatenate(
        [np.einsum("kn,cd->nckd", wgt * np.cos(th), eye).reshape(GROUP_ROWS, RE_ROWS),
         np.einsum("kn,cd->nckd", (-wgt * np.sin(th))[1:-1], eye).reshape(GROUP_ROWS, IM_ROWS)], axis=1)
    n2 = np.arange(FFT_N2, dtype=np.float64)
    ph = 2.0 * np.pi * np.outer(k1, n2) / FFT_N
    twr = np.repeat(np.cos(ph), CH_GROUP, axis=0)
    twi = np.repeat(-np.sin(ph), CH_GROUP, axis=0)
    ps = 2.0 * np.pi * np.outer(n2, n2) / FFT_N2
    cr, ci = np.cos(ps), -np.sin(ps)
    w3 = np.block([[cr, ci], [-ci, cr]])
    w3inv = np.block([[cr, -ci], [ci, cr]])
    f32 = lambda a: jnp.asarray(a, F32)
    return (f32(bd1).astype(BF16), f32(bd1k).astype(BF16), f32(bd1inv).astype(BF16), f32(twr), f32(twi),
            f32(w3).astype(BF16), f32(w3inv).astype(BF16))


def _hyena_kernel(zx0_ref, zx1_ref, zv_ref, px0_ref, px1_ref, pv_ref, kf_ref, kb_ref, bias_ref,
                  bd1_ref, bd1k_ref, bd1inv_ref, twr_ref, twi_ref, w3_ref, w3inv_ref,
                  w1_ref, w2_ref, w2n_ref, ya_ref, w1b_ref, w2b_ref, w2nb_ref):
    for src, dst in ((w1_ref, w1b_ref), (w2_ref, w2b_ref), (w2n_ref, w2nb_ref)):
        dst[...] = src[...].astype(BF16)
    twr, twi = twr_ref[...], twi_ref[...]
    zrow = jnp.zeros((CH_GROUP, FFT_N2), F32)

    def stage1(xq, bd_ref):
        xq = xq.reshape(xq.shape[0] * CH_GROUP, FFT_N2).astype(BF16)
        a = jnp.dot(bd_ref[...], xq, preferred_element_type=F32)
        ar = a[:RE_ROWS]
        ai = jnp.concatenate([zrow, a[RE_ROWS:], zrow], axis=0)
        br = ar * twr - ai * twi
        bi = ar * twi + ai * twr
        return jnp.concatenate([br, bi], axis=1)

    def pair(pi):
        gs = [2 * pi, 2 * pi + 1]
        rows = slice(pi * GATE_ROWS, (pi + 1) * GATE_ROWS)
        x1c = _conv3_bias_chunks(zx1_ref, px1_ref, rows)
        vc = _conv3_bias_chunks(zv_ref, pv_ref, rows)
        us = _to_q_groups([v * x1 for v, x1 in zip(vc, x1c)])
        x0s = _to_q_groups(_conv3_bias_chunks(zx0_ref, px0_ref, rows))
        kpair = jnp.concatenate([kf_ref[pi], kb_ref[pi]], axis=0).astype(F32)
        ks = [kpair[:, i * CH_GROUP:(i + 1) * CH_GROUP, :] for i in range(2)]
        b_all = jnp.concatenate(
            [blk for u, k in zip(us, ks) for blk in (stage1(u, bd1_ref), stage1(k, bd1k_ref))], axis=0)
        s_all = jnp.dot(b_all.astype(BF16), w3_ref[...], preferred_element_type=F32)
        ycats = []
        for i in range(2):
            xs = s_all[(2 * i) * RE_ROWS:(2 * i + 1) * RE_ROWS]
            ks = s_all[(2 * i + 1) * RE_ROWS:(2 * i + 2) * RE_ROWS]
            xr, xi = xs[:, :FFT_N2], xs[:, FFT_N2:]
            kr, ki = ks[:, :FFT_N2], ks[:, FFT_N2:]
            ycats.append(jnp.concatenate([xr * kr - xi * ki, xr * ki + xi * kr], axis=1))
        ap_all = jnp.dot(jnp.concatenate(ycats, axis=0).astype(BF16), w3inv_ref[...],
                         preferred_element_type=F32)
        yas = []
        for i, (g, u) in enumerate(zip(gs, us)):
            ap = ap_all[i * RE_ROWS:(i + 1) * RE_ROWS]
            apr, api = ap[:, :FFT_N2], ap[:, FFT_N2:]
            bpr = apr * twr + api * twi
            bpi = api * twr - apr * twi
            b2 = jnp.concatenate([bpr, bpi[CH_GROUP:RE_ROWS - CH_GROUP]], axis=0).astype(BF16)
            y = jnp.dot(bd1inv_ref[...], b2, preferred_element_type=F32)
            y3 = y.reshape(FFT_N1_LIVE, CH_GROUP, FFT_N2)
            bias = bias_ref[g * CH_GROUP:(g + 1) * CH_GROUP, :]
            yas.append(x0s[i] * (y3 + bias[None] * u))
        ya_ref[pi] = jnp.concatenate(yas, axis=1).astype(BF16)

    for pi in range(HY_CB // GATE_ROWS):
        pair(pi)


def _hyena(zt, hy_par, kf_q, kb_q, bias2, tables, mlp_w1, mlp_w2, layer):
    bd1, bd1k, bd1inv, twr, twi, w3, w3inv = tables
    steps = D_HY // HY_CB
    zspec = lambda s: pl.BlockSpec((HY_CB, SEQ), lambda j, s=s: (s * steps + j, 0))
    pspec = lambda s: pl.BlockSpec((HY_CB, 4), lambda j, s=s: (s * steps + j, 0))
    qspec = pl.BlockSpec((HY_CB // PAIR,) + Q16_SHAPE[1:], lambda j: (j, 0, 0, 0))
    full = lambda a: pl.BlockSpec(a.shape, lambda j: (0,) * a.ndim)
    w1_blk = (D_MODEL // steps, D_FF)
    w2_blk = (D_FF // steps, D_MODEL)
    return pl.pallas_call(
        _hyena_kernel,
        grid=(steps,),
        in_specs=[zspec(0), zspec(1), zspec(2), pspec(0), pspec(1), pspec(2), qspec, qspec,
                  pl.BlockSpec((HY_CB, FFT_N2), lambda j: (j, 0)),
                  full(bd1), full(bd1k), full(bd1inv), full(twr), full(twi), full(w3), full(w3inv),
                  pl.BlockSpec((None,) + w1_blk, lambda j: (layer, j, 0)),
                  pl.BlockSpec((None,) + w2_blk, lambda j: (layer, j, 0)),
                  pl.BlockSpec((None,) + w2_blk, lambda j: (layer + 1, j, 0))],
        out_specs=[qspec, pl.BlockSpec(w1_blk, lambda j: (j, 0)), pl.BlockSpec(w2_blk, lambda j: (j, 0)),
                   pl.BlockSpec(w2_blk, lambda j: (j, 0))],
        out_shape=[jax.ShapeDtypeStruct(Q16_SHAPE, BF16),
                   jax.ShapeDtypeStruct((D_MODEL, D_FF), BF16),
                   jax.ShapeDtypeStruct((D_FF, D_MODEL), BF16),
                   jax.ShapeDtypeStruct((D_FF, D_MODEL), BF16)],
        compiler_params=_cparams(1),
        name="hyena_fftconv",
    )(zt, zt, zt, hy_par, hy_par, hy_par, kf_q, kb_q, bias2, bd1, bd1k, bd1inv, twr, twi, w3, w3inv,
      mlp_w1, mlp_w2, mlp_w2)


OUT_TM = 512
OUT_CH = 256


def _out_proj_kernel(ya_ref, gb_ref, gcl_ref, gc_ref, gcr_ref, xvl_ref, xv_ref, xvr_ref, psc_ref,
                     wa_ref, wb_ref, x_ref, g_ref, o_ref):
    i = pl.program_id(0)
    tn = (((0,), (0,)), ((), ()))
    ya = jnp.concatenate([ya_ref[:, a].reshape(D_HY, FFT_N2) for a in range(OUT_TM // FFT_N2)], axis=1)
    cur = slice(LANES, LANES + OUT_TM)
    lane = lax.broadcasted_iota(jnp.int32, (OUT_CH, OUT_TM), 1)
    first = (lane == 0) & (i == 0)
    final = (lane == OUT_TM - 1) & (i == pl.num_programs(0) - 1)
    m = lax.dot_general(ya, wa_ref[...], tn, preferred_element_type=F32)
    for c in range(D_SC // OUT_CH):
        rows = slice(c * OUT_CH, (c + 1) * OUT_CH)
        window = lambda l, mid, r: jnp.concatenate([l[rows, :], mid[rows, :], r[rows, :]], axis=1).astype(F32)
        p = window(gcl_ref, gc_ref, gcr_ref) * window(xvl_ref, xv_ref, xvr_ref)
        pm = jnp.where(first, 0.0, pltpu.roll(p, 1, 1)[:, cur])
        pp = jnp.where(final, 0.0, pltpu.roll(p, p.shape[1] - 1, 1)[:, cur])
        w = psc_ref[rows, :]
        conv = pm * w[:, 0:1] + p[:, cur] * w[:, 1:2] + pp * w[:, 2:3]
        yb = (gb_ref[rows, :].astype(F32) * conv).astype(BF16)
        m = m + lax.dot_general(yb, wb_ref[rows, :], tn, preferred_element_type=F32)
    o_ref[...] = x_ref[...] + _rms_rows(m, g_ref[...])


def _out_proj(ya_q, zt, sc_par, wo, x2d, g):
    per = OUT_TM // LANES
    last = SEQ // LANES - 1
    main = lambda s: pl.BlockSpec((D_SC, OUT_TM), lambda i, s=s: (s, i))
    left = lambda s: pl.BlockSpec((D_SC, LANES), lambda i, s=s: (s, jnp.maximum(i * per - 1, 0)))
    right = lambda s: pl.BlockSpec((D_SC, LANES), lambda i, s=s: (s, jnp.minimum((i + 1) * per, last)))
    return pl.pallas_call(
        _out_proj_kernel,
        grid=(SEQ // OUT_TM,),
        in_specs=[
            pl.BlockSpec((Q16_SHAPE[0], OUT_TM // FFT_N2, PAIR, FFT_N2), lambda i: (0, i, 0, 0)),
            main(3), left(4), main(4), right(4), left(5), main(5), right(5),
            pl.BlockSpec((D_SC, 4), lambda i: (0, 0)),
            pl.BlockSpec((D_HY, D_MODEL), lambda i: (0, 0)),
            pl.BlockSpec((D_SC, D_MODEL), lambda i: (1, 0)),
            pl.BlockSpec((OUT_TM, D_MODEL), lambda i: (i, 0)),
            pl.BlockSpec((1, D_MODEL), lambda i: (0, 0)),
        ],
        out_specs=pl.BlockSpec((OUT_TM, D_MODEL), lambda i: (i, 0)),
        out_shape=jax.ShapeDtypeStruct((SEQ, D_MODEL), F32),
        compiler_params=_cparams(1),
        name="out_proj",
    )(ya_q, zt, zt, zt, zt, zt, zt, zt, sc_par, wo, wo, x2d, g)


MLP_TM = 1024
MLP_TK = 1024
MLP_TA = 512
MLP_TN = 512
MLP_ROWS = 128


def _mlp_kernel(x_ref, gin_ref, gout_ref, w1_ref, w2_ref, o_ref, h_ref):
    k = pl.program_id(1)

    def row_chunks(fn):
        def body(c, carry):
            fn(pl.ds(pl.multiple_of(c * MLP_ROWS, MLP_ROWS), MLP_ROWS))
            return carry
        lax.fori_loop(0, MLP_TM // MLP_ROWS, body, 0)

    @pl.when(k == 0)
    def _():
        def prologue(rows):
            h_ref[rows, :] = _rms_rows(x_ref[rows, :], gin_ref[...]).astype(BF16)
            o_ref[rows, :] = jnp.zeros((MLP_ROWS, D_MODEL), F32)
        row_chunks(prologue)

    for c in range(MLP_TK // MLP_TA):
        mid = slice(c * MLP_TA, (c + 1) * MLP_TA)
        a = jnp.dot(h_ref[...], w1_ref[:, mid], preferred_element_type=F32)
        a = jnp.square(jnp.maximum(a, 0.0)).astype(BF16)
        for n in range(D_MODEL // MLP_TN):
            cols = slice(n * MLP_TN, (n + 1) * MLP_TN)
            o_ref[:, cols] += jnp.dot(a, w2_ref[mid, cols], preferred_element_type=F32)

    @pl.when(k == pl.num_programs(1) - 1)
    def _():
        def epilogue(rows):
            o_ref[rows, :] = x_ref[rows, :] + _rms_rows(o_ref[rows, :], gout_ref[...])
        row_chunks(epilogue)


def _mlp(x2d, gin, gout, w1, w2):
    return pl.pallas_call(
        _mlp_kernel,
        grid=(SEQ // MLP_TM, D_FF // MLP_TK),
        in_specs=[
            pl.BlockSpec((MLP_TM, D_MODEL), lambda i, k: (i, 0)),
            pl.BlockSpec((1, D_MODEL), lambda i, k: (0, 0)),
            pl.BlockSpec((1, D_MODEL), lambda i, k: (0, 0)),
            pl.BlockSpec((D_MODEL, MLP_TK), lambda i, k: (0, k)),
            pl.BlockSpec((MLP_TK, D_MODEL), lambda i, k: (k, 0)),
        ],
        out_specs=pl.BlockSpec((MLP_TM, D_MODEL), lambda i, k: (i, 0)),
        out_shape=jax.ShapeDtypeStruct((SEQ, D_MODEL), F32),
        scratch_shapes=[pltpu.VMEM((MLP_TM, D_MODEL), BF16)],
        compiler_params=_cparams(2),
        name="mlp",
    )(x2d, gin, gout, w1, w2)


POOL_TM = 512
POOL_SUB = 128
HALO = 8


def _pool_bands():
    t = np.arange(POOL_SUB)[:, None]
    j = np.arange(POOL_SUB + 2 * HALO)[None, :]
    return np.stack([(np.abs(j - HALO - t) <= w // 2) for w in POOL_WINDOWS]).astype(np.float32)


def _pool_kernel(xp_ref, xm_ref, xn_ref, gin_ref, gout_ref, band_ref, pw_ref, ps_ref, o_ref,
                 hs_ref, hi_ref, lo_ref, m_ref):
    i = pl.program_id(0)
    gin = gin_ref[...]
    hs_ref[0:HALO, :] = jnp.where(i > 0, _rms_rows(xp_ref[...], gin), 0.0)
    hs_ref[HALO:HALO + POOL_TM, :] = _rms_rows(xm_ref[...], gin)
    hs_ref[HALO + POOL_TM:, :] = jnp.where(i < pl.num_programs(0) - 1, _rms_rows(xn_ref[...], gin), 0.0)
    hs = hs_ref[...]
    hi = hs.astype(BF16)
    hi_ref[...] = hi
    lo_ref[...] = (hs - hi.astype(F32)).astype(BF16)
    row = lax.broadcasted_iota(jnp.int32, (POOL_SUB, D_POOL_G), 0) + i * POOL_TM
    for gi, w in enumerate(POOL_WINDOWS):
        r = w // 2
        cols = slice(gi * D_POOL_G, (gi + 1) * D_POOL_G)
        band = band_ref[gi]
        parts = []
        for sb in range(POOL_TM // POOL_SUB):
            win = slice(sb * POOL_SUB, (sb + 1) * POOL_SUB + 2 * HALO)
            s = (jnp.dot(band, hi_ref[win, cols], preferred_element_type=F32)
                 + jnp.dot(band, lo_ref[win, cols], preferred_element_type=F32))
            t = row + sb * POOL_SUB
            cnt = jnp.minimum(t + r + 1, SEQ) - jnp.maximum(t - r, 0)
            u = hs_ref[sb * POOL_SUB + HALO:(sb + 1) * POOL_SUB + HALO, cols]
            parts.append((s / cnt.astype(F32) - u).astype(BF16))
        d = jnp.concatenate(parts, axis=0)
        mg = jnp.dot(d, pw_ref[gi], preferred_element_type=F32)
        m_ref[:, cols] = mg * ps_ref[:, cols]
    o_ref[...] = xm_ref[...] + _rms_rows(m_ref[...], gout_ref[...])


def _pool(x2d, gin, gout, pw, ps):
    nb8 = POOL_TM // HALO
    last8 = SEQ // HALO - 1
    band = jnp.asarray(_pool_bands(), BF16)
    ext = POOL_TM + 2 * HALO
    return pl.pallas_call(
        _pool_kernel,
        grid=(SEQ // POOL_TM,),
        in_specs=[
            pl.BlockSpec((HALO, D_MODEL), lambda i: (jnp.maximum(i * nb8 - 1, 0), 0)),
            pl.BlockSpec((POOL_TM, D_MODEL), lambda i: (i, 0)),
            pl.BlockSpec((HALO, D_MODEL), lambda i: (jnp.minimum((i + 1) * nb8, last8), 0)),
            pl.BlockSpec((1, D_MODEL), lambda i: (0, 0)),
            pl.BlockSpec((1, D_MODEL), lambda i: (0, 0)),
            pl.BlockSpec(band.shape, lambda i: (0, 0, 0)),
            pl.BlockSpec((len(POOL_WINDOWS), D_POOL_G, D_POOL_G), lambda i: (0, 0, 0)),
            pl.BlockSpec((1, D_MODEL), lambda i: (0, 0)),
        ],
        out_specs=pl.BlockSpec((POOL_TM, D_MODEL), lambda i: (i, 0)),
        out_shape=jax.ShapeDtypeStruct((SEQ, D_MODEL), F32),
        scratch_shapes=[pltpu.VMEM((ext, D_MODEL), F32), pltpu.VMEM((ext, D_MODEL), BF16),
                        pltpu.VMEM((ext, D_MODEL), BF16), pltpu.VMEM((POOL_TM, D_MODEL), F32)],
        compiler_params=_cparams(1),
        name="pool_mixer",
    )(x2d, x2d, x2d, gin, gout, band, pw, ps)


def _filter_constants():
    L = SEQ
    t = np.linspace(0.0, 1.0, L)
    w_pos = 2.0 * np.pi * np.arange(L) / L
    bands = np.linspace(1e-4, FILTER_BANDS - 1, FILTER_BANDS)
    ang = w_pos[:, None] * bands[None, :]
    emb = np.concatenate([t[:, None], np.cos(ang), -np.sin(ang)], axis=-1)
    emb_t = np.zeros((EMB_PAD, L + LANES))
    emb_t[:EMB_DIM, :L] = emb.T
    t_rev = t[(L - np.arange(L)) % L]
    max_decay = math.log(DECAY_TARGET) / FAST_DECAY_PCT
    min_decay = math.log(DECAY_TARGET) / SLOW_DECAY_PCT
    absdelta = np.abs(np.linspace(min_decay, max_decay, D_HY))[:, None]
    f32 = lambda a: jnp.asarray(a, F32)
    return f32(emb_t), f32(t[None, :]), f32(t_rev[None, :]), f32(absdelta)


def _mixer_layer0(x2d, g, w_in, hy_short_w, hy_short_b, f_w1, f_b1, f_w2, f_b2, f_w3, f_b3,
                  freq, hy_bias, sc_conv_w, w_out, mlp_w1, mlp_w2, layer):
    emb_t, t_fwd, t_rev, absdelta = _filter_constants()
    w1t = jnp.zeros((FILTER_HIDDEN, EMB_PAD), F32).at[:, :EMB_DIM].set(f_w1.T)
    col = lambda v: v[:, None].astype(F32)
    kf_q, kb_q, wt, wo = _filters(emb_t, t_fwd, t_rev, w1t, col(f_b1), f_w2.T, col(f_b2), col(freq),
                                  f_w3.T.astype(BF16), col(f_b3), absdelta, w_in, w_out)

    zt, w1_next = _in_proj(x2d, g[0][None, :], wt, mlp_w1, layer + 1)
    hy_par = jnp.concatenate([hy_short_w.T, hy_short_b[:, None]], axis=1)
    sc_par = jnp.concatenate([sc_conv_w.T, jnp.zeros((D_SC, 1), F32)], axis=1)

    bias2 = jnp.broadcast_to(hy_bias[:, None].astype(F32), (D_HY, FFT_N2))
    ya_q, w1_this, w2_this, w2_next = _hyena(
        zt, hy_par, kf_q, kb_q, bias2, _dft_tables(), mlp_w1, mlp_w2, layer)
    x2d = _out_proj(ya_q, zt, sc_par, wo, x2d, g[1][None, :])
    return x2d, (w1_this, w2_this), (w1_next, w2_next)


def kernel(x, norm_g, mix_w_in, hy_short_w, hy_short_b, hy_filt_w1, hy_filt_b1, hy_filt_w2,
           hy_filt_b2, hy_filt_w3, hy_filt_b3, hy_freq, hy_bias, sc_conv_w, mix_w_out,
           pool_w, pool_scale, mlp_w1, mlp_w2):
    x2d = x.reshape(SEQ, D_MODEL)
    depth = norm_g.shape[0]
    assert depth % 2 == 0, "an even layer's mixer kernels narrow the MLP weights of itself and the next layer"
    for i in range(depth):
        g = norm_g[i]
        j = i // 2
        gin, gout = g[2][None, :], g[3][None, :]
        if i % 2 == 0:
            x2d, w_this, w_next = _mixer_layer0(
                x2d, g, mix_w_in[j], hy_short_w[j], hy_short_b[j], hy_filt_w1[j], hy_filt_b1[j],
                hy_filt_w2[j], hy_filt_b2[j], hy_filt_w3[j], hy_filt_b3[j], hy_freq[j], hy_bias[j],
                sc_conv_w[j], mix_w_out[j], mlp_w1, mlp_w2, i)
            x2d = _mlp(x2d, gin, gout, *w_this)
        else:
            x2d = _pool(x2d, g[0][None, :], g[1][None, :], pool_w[j].astype(BF16),
                        pool_scale[j][None, :])
            x2d = _mlp(x2d, gin, gout, *w_next)
    return x2d.reshape(x.shape)
```

```python
import functools
import math

import numpy as np
import jax
import jax.numpy as jnp
from jax import lax
from jax.experimental import pallas as pl
from jax.experimental.pallas import tpu as pltpu

F32 = jnp.float32
BF16 = jnp.bfloat16

D_MODEL = 2048
SEQ = 8192
D_HY = D_MODEL // 2
D_SC = D_MODEL // 2
D_IN = 3 * D_HY + 3 * D_SC
FILTER_BANDS = 16
EMB_DIM = 1 + 2 * FILTER_BANDS
EMB_PAD = 40
FILTER_HIDDEN = 64
DECAY_TARGET = 1e-2
FAST_DECAY_PCT = 0.3
SLOW_DECAY_PCT = 1.5
POOL_WINDOWS = (2, 4, 8, 16)
D_POOL_G = D_MODEL // len(POOL_WINDOWS)
D_FF = 4 * D_MODEL
NORM_EPS = 1e-6
LANES = 128

FFT_N = 2 * SEQ
FFT_N2 = 256
FFT_N1 = FFT_N // FFT_N2
FFT_N1_LIVE = SEQ // FFT_N2
CH_GROUP = 8
PAIR = 2 * CH_GROUP
Q16_SHAPE = (D_MODEL // 2 // PAIR, FFT_N1_LIVE, PAIR, FFT_N2)

VMEM_LIMIT = 56 * 1024 * 1024


def _cparams(n_axes):
    return pltpu.CompilerParams(
        dimension_semantics=("arbitrary",) * n_axes, vmem_limit_bytes=VMEM_LIMIT)


def _rms_rows(x, g):
    r = lax.rsqrt(jnp.mean(x * x, axis=-1, keepdims=True) + NORM_EPS)
    return x * r * g


IN_TM = 1024
IN_TN = 1536
IN_TC = 256


def _in_proj_kernel(x_ref, g_ref, wt_ref, w1_ref, zt_ref, w1b_ref, h_ref):
    w1b_ref[...] = w1_ref[...].astype(BF16)

    @pl.when(pl.program_id(1) == 0)
    def _():
        h_ref[...] = _rms_rows(x_ref[...], g_ref[...]).astype(BF16)

    for c in range(IN_TM // IN_TC):
        t = slice(c * IN_TC, (c + 1) * IN_TC)
        zt_ref[:, t] = lax.dot_general(
            wt_ref[...], h_ref[t, :], (((1,), (1,)), ((), ())), preferred_element_type=F32).astype(BF16)


def _in_proj(x2d, g, wt, mlp_w1, layer):
    ni, nj = SEQ // IN_TM, D_IN // IN_TN
    w1_blk = (D_MODEL // ni, D_FF // nj)
    return pl.pallas_call(
        _in_proj_kernel,
        grid=(ni, nj),
        in_specs=[
            pl.BlockSpec((IN_TM, D_MODEL), lambda i, j: (i, 0)),
            pl.BlockSpec((1, D_MODEL), lambda i, j: (0, 0)),
            pl.BlockSpec((IN_TN, D_MODEL), lambda i, j: (j, 0)),
            pl.BlockSpec((None,) + w1_blk, lambda i, j: (layer, i, j)),
        ],
        out_specs=[pl.BlockSpec((IN_TN, IN_TM), lambda i, j: (j, i)),
                   pl.BlockSpec(w1_blk, lambda i, j: (i, j))],
        out_shape=[jax.ShapeDtypeStruct((D_IN, SEQ), BF16),
                   jax.ShapeDtypeStruct((D_MODEL, D_FF), BF16)],
        scratch_shapes=[pltpu.VMEM((IN_TM, D_MODEL), BF16)],
        compiler_params=_cparams(2),
        name="in_proj",
    )(x2d, g, wt, mlp_w1)


GATE_ROWS = 16
GATE_CHUNK = 1024


def _conv3_chunk(zwin, w, off, first, last):
    width = zwin.shape[1]
    zm = pltpu.roll(zwin, 1, 1)[:, off:off + GATE_CHUNK]
    zp = pltpu.roll(zwin, width - 1, 1)[:, off:off + GATE_CHUNK]
    z = zwin[:, off:off + GATE_CHUNK]
    lane = lax.broadcasted_iota(jnp.int32, z.shape, 1)
    if first:
        zm = jnp.where(lane == 0, 0.0, zm)
    if last:
        zp = jnp.where(lane == GATE_CHUNK - 1, 0.0, zp)
    return zm * w[:, 0:1] + z * w[:, 1:2] + zp * w[:, 2:3]


def _conv3_bias_chunks(z_ref, p_ref, rows):
    n_chunks = SEQ // GATE_CHUNK
    p = p_ref[rows, :]
    out = []
    for c in range(n_chunks):
        lo = max(c * GATE_CHUNK - LANES, 0)
        hi = min((c + 1) * GATE_CHUNK + LANES, SEQ)
        zwin = z_ref[rows, lo:hi].astype(F32)
        out.append(_conv3_chunk(zwin, p, c * GATE_CHUNK - lo, c == 0, c == n_chunks - 1) + p[:, 3:4])
    return out


def _to_q_groups(chunks):
    groups = []
    for g in range(GATE_ROWS // CH_GROUP):
        slabs = [ch[g * CH_GROUP:(g + 1) * CH_GROUP, a * FFT_N2:(a + 1) * FFT_N2]
                 for ch in chunks for a in range(GATE_CHUNK // FFT_N2)]
        groups.append(jnp.stack(slabs, axis=0))
    return groups


FILT_TB = 1024


def _filter_kernel(emb_ref, embn_ref, tf_ref, tb_ref, w1_ref, b1_ref, w2_ref, b2_ref, fr_ref, w3_ref, b3_ref,
                   ad_ref, rev_ref, win_ref, wo_ref, kf_ref, kb_ref, wt_ref, wob_ref, kfs_ref, kbs_ref):
    wt_ref[...] = win_ref[...].T.astype(BF16)
    wob_ref[...] = wo_ref[...].astype(BF16)
    hi = lax.Precision.HIGHEST
    fr = fr_ref[...]
    emb = jnp.concatenate([emb_ref[...], embn_ref[...]], axis=1)
    a1 = jnp.dot(w1_ref[...], emb, precision=hi, preferred_element_type=F32) + b1_ref[...]
    h1 = jnp.sin(fr * a1)
    a2 = jnp.dot(w2_ref[...], h1, precision=hi, preferred_element_type=F32) + b2_ref[...]
    h2 = jnp.sin(fr * a2)
    ad = ad_ref[...]
    h2f = h2[:, :FILT_TB].astype(BF16)
    of = jnp.dot(w3_ref[:D_HY, :], h2f, preferred_element_type=F32) + b3_ref[:D_HY, :]
    kfs_ref[...] = of * jnp.exp(-(ad * tf_ref[...]))
    h2b = pltpu.roll(h2, FILT_TB + LANES - 1, 1)[:, :FILT_TB].astype(BF16)
    h2r = jnp.dot(h2b, rev_ref[...], preferred_element_type=F32).astype(BF16)
    ob = jnp.dot(w3_ref[D_HY:, :], h2r, preferred_element_type=F32) + b3_ref[D_HY:, :]
    lane = lax.broadcasted_iota(jnp.int32, (D_HY, FILT_TB), 1)
    is_sep = (lane == 0) & (pl.program_id(0) == pl.num_programs(0) - 1)
    kbs_ref[...] = jnp.where(is_sep, 0.0, ob * jnp.exp(-(ad * tb_ref[...])))

    def to_q(cp, carry):
        rows = pl.ds(pl.multiple_of(cp * PAIR, PAIR), PAIR)
        for n1 in range(FILT_TB // FFT_N2):
            lanes = slice(n1 * FFT_N2, (n1 + 1) * FFT_N2)
            kf_ref[cp, n1] = kfs_ref[rows, lanes].astype(BF16)
            kb_ref[cp, n1] = kbs_ref[rows, lanes].astype(BF16)
        return carry

    lax.fori_loop(0, D_HY // PAIR, to_q, 0)


def _filters(emb_t, t_fwd, t_rev, w1t, b1, w2t, b2, fr, w3t, b3, absdelta, w_in, w_out):
    full = lambda a: pl.BlockSpec(a.shape, lambda i: (0,) * a.ndim)
    steps = SEQ // FILT_TB
    per = FILT_TB // LANES
    rev = jnp.asarray(np.eye(FILT_TB)[::-1], BF16)
    out = jax.ShapeDtypeStruct(Q16_SHAPE, BF16)
    qblk = (Q16_SHAPE[0], FILT_TB // FFT_N2, PAIR, FFT_N2)
    return pl.pallas_call(
        _filter_kernel,
        grid=(steps,),
        in_specs=[pl.BlockSpec((EMB_PAD, FILT_TB), lambda i: (0, i)),
                  pl.BlockSpec((EMB_PAD, LANES), lambda i: (0, (i + 1) * per)),
                  pl.BlockSpec((1, FILT_TB), lambda i: (0, i)),
                  pl.BlockSpec((1, FILT_TB), lambda i: (0, steps - 1 - i)),
                  full(w1t), full(b1), full(w2t), full(b2), full(fr), full(w3t), full(b3),
                  full(absdelta), full(rev),
                  pl.BlockSpec((D_MODEL, D_IN // steps), lambda i: (0, i)),
                  pl.BlockSpec((D_MODEL // steps, D_MODEL), lambda i: (i, 0))],
        out_specs=[pl.BlockSpec(qblk, lambda i: (0, i, 0, 0)),
                   pl.BlockSpec(qblk, lambda i: (0, steps - 1 - i, 0, 0)),
                   pl.BlockSpec((D_IN // steps, D_MODEL), lambda i: (i, 0)),
                   pl.BlockSpec((D_MODEL // steps, D_MODEL), lambda i: (i, 0))],
        out_shape=[out, out, jax.ShapeDtypeStruct((D_IN, D_MODEL), BF16),
                   jax.ShapeDtypeStruct((D_MODEL, D_MODEL), BF16)],
        scratch_shapes=[pltpu.VMEM((D_HY, FILT_TB), F32)] * 2,
        compiler_params=_cparams(1),
        name="filters",
    )(emb_t, emb_t, t_fwd, t_rev, w1t, b1, w2t, b2, fr, w3t, b3, absdelta, rev, w_in, w_out)


HY_CB = 32
RING = 3
GROUP_ROWS = CH_GROUP * FFT_N1_LIVE
N_K1 = FFT_N1 // 2 + 1
RE_ROWS = N_K1 * CH_GROUP
IM_ROWS = (N_K1 - 2) * CH_GROUP


def _dft_tables():
    n1 = np.arange(FFT_N1_LIVE, dtype=np.float64)
    k1 = np.arange(N_K1, dtype=np.float64)
    th = 2.0 * np.pi * np.outer(k1, n1) / FFT_N1
    eye = np.eye(CH_GROUP)
    def stage1(cos, sin):
        return np.concatenate([np.einsum("kn,cd->kcnd", cos, eye).reshape(RE_ROWS, -1),
                               np.einsum("kn,cd->kcnd", -sin[1:-1], eye).reshape(IM_ROWS, -1)])
    bd1 = stage1(np.cos(th), np.sin(th))
    thk = 2.0 * np.pi * np.outer(k1, np.arange(FFT_N1, dtype=np.float64)) / FFT_N1
    bd1k = stage1(np.cos(thk), np.sin(thk))
    wgt = np.where((k1 == 0) | (k1 == FFT_N1 // 2), 1.0, 2.0)[:, None] / FFT_N
    bd1inv = np.concatenate(
        [np.einsum("kn,cd->nckd", wgt * np.cos(th), eye).reshape(GROUP_ROWS, RE_ROWS),
         np.einsum("kn,cd->nckd", (-wgt * np.sin(th))[1:-1], eye).reshape(GROUP_ROWS, IM_ROWS)], axis=1)
    n2 = np.arange(FFT_N2, dtype=np.float64)
    ph = 2.0 * np.pi * np.outer(k1, n2) / FFT_N
    twr = np.repeat(np.cos(ph), CH_GROUP, axis=0)
    twi = np.repeat(-np.sin(ph), CH_GROUP, axis=0)
    ps = 2.0 * np.pi * np.outer(n2, n2) / FFT_N2
    cr, ci = np.cos(ps), -np.sin(ps)
    w3 = np.block([[cr, ci], [-ci, cr]])
    w3inv = np.block([[cr, -ci], [ci, cr]])
    f32 = lambda a: jnp.asarray(a, F32)
    return (f32(bd1).astype(BF16), f32(bd1k).astype(BF16), f32(bd1inv).astype(BF16), f32(twr), f32(twi),
            f32(w3).astype(BF16), f32(w3inv).astype(BF16))


def _hyena_kernel(zx0_ref, zx1_ref, zv_ref, px0_ref, px1_ref, pv_ref, kf_ref, kb_ref, bias_ref,
                  bd1_ref, bd1k_ref, bd1inv_ref, twr_ref, twi_ref, w3_ref, w3inv_ref,
                  w1_hbm, w2_hbm, ya_ref, w1b_ref, w2b_ref, w2nb_ref, ring1, ring2, ring3, sems, *, layer):
    j = pl.program_id(0)
    steps = pl.num_programs(0)

    def fetch(s):
        slot = lax.rem(s, RING)
        r1, r2 = w1b_ref.shape[0], w2b_ref.shape[0]
        return (
            pltpu.make_async_copy(w1_hbm.at[layer, pl.ds(s * r1, r1), :], ring1.at[slot], sems.at[0, slot]),
            pltpu.make_async_copy(w2_hbm.at[layer, pl.ds(s * r2, r2), :], ring2.at[slot], sems.at[1, slot]),
            pltpu.make_async_copy(w2_hbm.at[layer + 1, pl.ds(s * r2, r2), :], ring3.at[slot], sems.at[2, slot]),
        )

    @pl.when(j == 0)
    def _():
        for s in range(RING - 1):
            for cp in fetch(s):
                cp.start()

    @pl.when(j + RING - 1 < steps)
    def _():
        for cp in fetch(j + RING - 1):
            cp.start()

    for cp in fetch(j):
        cp.wait()
    slot = lax.rem(j, RING)
    w1b_ref[...] = ring1[slot].astype(BF16)
    w2b_ref[...] = ring2[slot].astype(BF16)
    w2nb_ref[...] = ring3[slot].astype(BF16)
    twr, twi = twr_ref[...], twi_ref[...]
    zrow = jnp.zeros((CH_GROUP, FFT_N2), F32)

    def stage1(xq, bd_ref):
        xq = xq.reshape(xq.shape[0] * CH_GROUP, FFT_N2).astype(BF16)
        a = jnp.dot(bd_ref[...], xq, preferred_element_type=F32)
        ar = a[:RE_ROWS]
        ai = jnp.concatenate([zrow, a[RE_ROWS:], zrow], axis=0)
        br = ar * twr - ai * twi
        bi = ar * twi + ai * twr
        return jnp.concatenate([br, bi], axis=1)

    def pair(pi):
        gs = [2 * pi, 2 * pi + 1]
        rows = slice(pi * GATE_ROWS, (pi + 1) * GATE_ROWS)
        x1c = _conv3_bias_chunks(zx1_ref, px1_ref, rows)
        vc = _conv3_bias_chunks(zv_ref, pv_ref, rows)
        us = _to_q_groups([v * x1 for v, x1 in zip(vc, x1c)])
        x0s = _to_q_groups(_conv3_bias_chunks(zx0_ref, px0_ref, rows))
        kpair = jnp.concatenate([kf_ref[pi], kb_ref[pi]], axis=0).astype(F32)
        ks = [kpair[:, i * CH_GROUP:(i + 1) * CH_GROUP, :] for i in range(2)]
        b_all = jnp.concatenate(
            [blk for u, k in zip(us, ks) for blk in (stage1(u, bd1_ref), stage1(k, bd1k_ref))], axis=0)
        s_all = jnp.dot(b_all.astype(BF16), w3_ref[...], preferred_element_type=F32)
        ycats = []
        for i in range(2):
            xs = s_all[(2 * i) * RE_ROWS:(2 * i + 1) * RE_ROWS]
            ks = s_all[(2 * i + 1) * RE_ROWS:(2 * i + 2) * RE_ROWS]
            xr, xi = xs[:, :FFT_N2], xs[:, FFT_N2:]
            kr, ki = ks[:, :FFT_N2], ks[:, FFT_N2:]
            ycats.append(jnp.concatenate([xr * kr - xi * ki, xr * ki + xi * kr], axis=1))
        ap_all = jnp.dot(jnp.concatenate(ycats, axis=0).astype(BF16), w3inv_ref[...],
                         preferred_element_type=F32)
        yas = []
        for i, (g, u) in enumerate(zip(gs, us)):
            ap = ap_all[i * RE_ROWS:(i + 1) * RE_ROWS]
            apr, api = ap[:, :FFT_N2], ap[:, FFT_N2:]
            bpr = apr * twr + api * twi
            bpi = api * twr - apr * twi
            b2 = jnp.concatenate([bpr, bpi[CH_GROUP:RE_ROWS - CH_GROUP]], axis=0).astype(BF16)
            y = jnp.dot(bd1inv_ref[...], b2, preferred_element_type=F32)
            y3 = y.reshape(FFT_N1_LIVE, CH_GROUP, FFT_N2)
            bias = bias_ref[g * CH_GROUP:(g + 1) * CH_GROUP, :]
            yas.append(x0s[i] * (y3 + bias[None] * u))
        ya_ref[pi] = jnp.concatenate(yas, axis=1).astype(BF16)

    for pi in range(HY_CB // GATE_ROWS):
        pair(pi)


def _hyena(zt, hy_par, kf_q, kb_q, bias2, tables, mlp_w1, mlp_w2, layer):
    bd1, bd1k, bd1inv, twr, twi, w3, w3inv = tables
    steps = D_HY // HY_CB
    zspec = lambda s: pl.BlockSpec((HY_CB, SEQ), lambda j, s=s: (s * steps + j, 0))
    pspec = lambda s: pl.BlockSpec((HY_CB, 4), lambda j, s=s: (s * steps + j, 0))
    qspec = pl.BlockSpec((HY_CB // PAIR,) + Q16_SHAPE[1:], lambda j: (j, 0, 0, 0))
    full = lambda a: pl.BlockSpec(a.shape, lambda j: (0,) * a.ndim)
    w1_blk = (D_MODEL // steps, D_FF)
    w2_blk = (D_FF // steps, D_MODEL)
    assert steps >= RING
    return pl.pallas_call(
        functools.partial(_hyena_kernel, layer=layer),
        grid=(steps,),
        in_specs=[zspec(0), zspec(1), zspec(2), pspec(0), pspec(1), pspec(2), qspec, qspec,
                  pl.BlockSpec((HY_CB, FFT_N2), lambda j: (j, 0)),
                  full(bd1), full(bd1k), full(bd1inv), full(twr), full(twi), full(w3), full(w3inv),
                  pl.BlockSpec(memory_space=pl.ANY), pl.BlockSpec(memory_space=pl.ANY)],
        out_specs=[qspec, pl.BlockSpec(w1_blk, lambda j: (j, 0)), pl.BlockSpec(w2_blk, lambda j: (j, 0)),
                   pl.BlockSpec(w2_blk, lambda j: (j, 0))],
        out_shape=[jax.ShapeDtypeStruct(Q16_SHAPE, BF16),
                   jax.ShapeDtypeStruct((D_MODEL, D_FF), BF16),
                   jax.ShapeDtypeStruct((D_FF, D_MODEL), BF16),
                   jax.ShapeDtypeStruct((D_FF, D_MODEL), BF16)],
        scratch_shapes=[pltpu.VMEM((RING,) + w1_blk, F32), pltpu.VMEM((RING,) + w2_blk, F32),
                        pltpu.VMEM((RING,) + w2_blk, F32), pltpu.SemaphoreType.DMA((3, RING))],
        compiler_params=_cparams(1),
        name="hyena_fftconv",
    )(zt, zt, zt, hy_par, hy_par, hy_par, kf_q, kb_q, bias2, bd1, bd1k, bd1inv, twr, twi, w3, w3inv,
      mlp_w1, mlp_w2)


OUT_TM = 512
OUT_CH = 256


def _out_proj_kernel(ya_ref, gb_ref, gcl_ref, gc_ref, gcr_ref, xvl_ref, xv_ref, xvr_ref, psc_ref,
                     wa_ref, wb_ref, x_ref, g_ref, o_ref):
    i = pl.program_id(0)
    tn = (((0,), (0,)), ((), ()))
    ya = jnp.concatenate([ya_ref[:, a].reshape(D_HY, FFT_N2) for a in range(OUT_TM // FFT_N2)], axis=1)
    cur = slice(LANES, LANES + OUT_TM)
    lane = lax.broadcasted_iota(jnp.int32, (OUT_CH, OUT_TM), 1)
    first = (lane == 0) & (i == 0)
    final = (lane == OUT_TM - 1) & (i == pl.num_programs(0) - 1)
    m = lax.dot_general(ya, wa_ref[...], tn, preferred_element_type=F32)
    for c in range(D_SC // OUT_CH):
        rows = slice(c * OUT_CH, (c + 1) * OUT_CH)
        window = lambda l, mid, r: jnp.concatenate([l[rows, :], mid[rows, :], r[rows, :]], axis=1).astype(F32)
        p = window(gcl_ref, gc_ref, gcr_ref) * window(xvl_ref, xv_ref, xvr_ref)
        pm = jnp.where(first, 0.0, pltpu.roll(p, 1, 1)[:, cur])
        pp = jnp.where(final, 0.0, pltpu.roll(p, p.shape[1] - 1, 1)[:, cur])
        w = psc_ref[rows, :]
        conv = pm * w[:, 0:1] + p[:, cur] * w[:, 1:2] + pp * w[:, 2:3]
        yb = (gb_ref[rows, :].astype(F32) * conv).astype(BF16)
        m = m + lax.dot_general(yb, wb_ref[rows, :], tn, preferred_element_type=F32)
    o_ref[...] = x_ref[...] + _rms_rows(m, g_ref[...])


def _out_proj(ya_q, zt, sc_par, wo, x2d, g):
    per = OUT_TM // LANES
    last = SEQ // LANES - 1
    main = lambda s: pl.BlockSpec((D_SC, OUT_TM), lambda i, s=s: (s, i))
    left = lambda s: pl.BlockSpec((D_SC, LANES), lambda i, s=s: (s, jnp.maximum(i * per - 1, 0)))
    right = lambda s: pl.BlockSpec((D_SC, LANES), lambda i, s=s: (s, jnp.minimum((i + 1) * per, last)))
    return pl.pallas_call(
        _out_proj_kernel,
        grid=(SEQ // OUT_TM,),
        in_specs=[
            pl.BlockSpec((Q16_SHAPE[0], OUT_TM // FFT_N2, PAIR, FFT_N2), lambda i: (0, i, 0, 0)),
            main(3), left(4), main(4), right(4), left(5), main(5), right(5),
            pl.BlockSpec((D_SC, 4), lambda i: (0, 0)),
            pl.BlockSpec((D_HY, D_MODEL), lambda i: (0, 0)),
            pl.BlockSpec((D_SC, D_MODEL), lambda i: (1, 0)),
            pl.BlockSpec((OUT_TM, D_MODEL), lambda i: (i, 0)),
            pl.BlockSpec((1, D_MODEL), lambda i: (0, 0)),
        ],
        out_specs=pl.BlockSpec((OUT_TM, D_MODEL), lambda i: (i, 0)),
        out_shape=jax.ShapeDtypeStruct((SEQ, D_MODEL), F32),
        compiler_params=_cparams(1),
        name="out_proj",
    )(ya_q, zt, zt, zt, zt, zt, zt, zt, sc_par, wo, wo, x2d, g)


MLP_TM = 1024
MLP_TK = 1024
MLP_TA = 512
MLP_TN = 512
MLP_ROWS = 128


def _mlp_kernel(x_ref, gin_ref, gout_ref, w1_ref, w2_ref, o_ref, h_ref):
    k = pl.program_id(1)

    def row_chunks(fn):
        def body(c, carry):
            fn(pl.ds(pl.multiple_of(c * MLP_ROWS, MLP_ROWS), MLP_ROWS))
            return carry
        lax.fori_loop(0, MLP_TM // MLP_ROWS, body, 0)

    @pl.when(k == 0)
    def _():
        def prologue(rows):
            h_ref[rows, :] = _rms_rows(x_ref[rows, :], gin_ref[...]).astype(BF16)
            o_ref[rows, :] = jnp.zeros((MLP_ROWS, D_MODEL), F32)
        row_chunks(prologue)

    for c in range(MLP_TK // MLP_TA):
        mid = slice(c * MLP_TA, (c + 1) * MLP_TA)
        a = jnp.dot(h_ref[...], w1_ref[:, mid], preferred_element_type=F32)
        a = jnp.square(jnp.maximum(a, 0.0)).astype(BF16)
        for n in range(D_MODEL // MLP_TN):
            cols = slice(n * MLP_TN, (n + 1) * MLP_TN)
            o_ref[:, cols] += jnp.dot(a, w2_ref[mid, cols], preferred_element_type=F32)

    @pl.when(k == pl.num_programs(1) - 1)
    def _():
        def epilogue(rows):
            o_ref[rows, :] = x_ref[rows, :] + _rms_rows(o_ref[rows, :], gout_ref[...])
        row_chunks(epilogue)


def _mlp(x2d, gin, gout, w1, w2):
    return pl.pallas_call(
        _mlp_kernel,
        grid=(SEQ // MLP_TM, D_FF // MLP_TK),
        in_specs=[
            pl.BlockSpec((MLP_TM, D_MODEL), lambda i, k: (i, 0)),
            pl.BlockSpec((1, D_MODEL), lambda i, k: (0, 0)),
            pl.BlockSpec((1, D_MODEL), lambda i, k: (0, 0)),
            pl.BlockSpec((D_MODEL, MLP_TK), lambda i, k: (0, k)),
            pl.BlockSpec((MLP_TK, D_MODEL), lambda i, k: (k, 0)),
        ],
        out_specs=pl.BlockSpec((MLP_TM, D_MODEL), lambda i, k: (i, 0)),
        out_shape=jax.ShapeDtypeStruct((SEQ, D_MODEL), F32),
        scratch_shapes=[pltpu.VMEM((MLP_TM, D_MODEL), BF16)],
        compiler_params=_cparams(2),
        name="mlp",
    )(x2d, gin, gout, w1, w2)


POOL_TM = 512
POOL_SUB = 128
HALO = 8


def _pool_bands():
    t = np.arange(POOL_SUB)[:, None]
    j = np.arange(POOL_SUB + 2 * HALO)[None, :]
    return np.stack([(np.abs(j - HALO - t) <= w // 2) for w in POOL_WINDOWS]).astype(np.float32)


def _pool_kernel(xp_ref, xm_ref, xn_ref, gin_ref, gout_ref, band_ref, pw_ref, ps_ref, o_ref,
                 hs_ref, hi_ref, lo_ref, m_ref):
    i = pl.program_id(0)
    gin = gin_ref[...]
    hs_ref[0:HALO, :] = jnp.where(i > 0, _rms_rows(xp_ref[...], gin), 0.0)
    hs_ref[HALO:HALO + POOL_TM, :] = _rms_rows(xm_ref[...], gin)
    hs_ref[HALO + POOL_TM:, :] = jnp.where(i < pl.num_programs(0) - 1, _rms_rows(xn_ref[...], gin), 0.0)
    hs = hs_ref[...]
    hi = hs.astype(BF16)
    hi_ref[...] = hi
    lo_ref[...] = (hs - hi.astype(F32)).astype(BF16)
    row = lax.broadcasted_iota(jnp.int32, (POOL_SUB, D_POOL_G), 0) + i * POOL_TM
    for gi, w in enumerate(POOL_WINDOWS):
        r = w // 2
        cols = slice(gi * D_POOL_G, (gi + 1) * D_POOL_G)
        band = band_ref[gi]
        parts = []
        for sb in range(POOL_TM // POOL_SUB):
            win = slice(sb * POOL_SUB, (sb + 1) * POOL_SUB + 2 * HALO)
            s = (jnp.dot(band, hi_ref[win, cols], preferred_element_type=F32)
                 + jnp.dot(band, lo_ref[win, cols], preferred_element_type=F32))
            t = row + sb * POOL_SUB
            cnt = jnp.minimum(t + r + 1, SEQ) - jnp.maximum(t - r, 0)
            u = hs_ref[sb * POOL_SUB + HALO:(sb + 1) * POOL_SUB + HALO, cols]
            parts.append((s / cnt.astype(F32) - u).astype(BF16))
        d = jnp.concatenate(parts, axis=0)
        mg = jnp.dot(d, pw_ref[gi], preferred_element_type=F32)
        m_ref[:, cols] = mg * ps_ref[:, cols]
    o_ref[...] = xm_ref[...] + _rms_rows(m_ref[...], gout_ref[...])


def _pool(x2d, gin, gout, pw, ps):
    nb8 = POOL_TM // HALO
    last8 = SEQ // HALO - 1
    band = jnp.asarray(_pool_bands(), BF16)
    ext = POOL_TM + 2 * HALO
    return pl.pallas_call(
        _pool_kernel,
        grid=(SEQ // POOL_TM,),
        in_specs=[
            pl.BlockSpec((HALO, D_MODEL), lambda i: (jnp.maximum(i * nb8 - 1, 0), 0)),
            pl.BlockSpec((POOL_TM, D_MODEL), lambda i: (i, 0)),
            pl.BlockSpec((HALO, D_MODEL), lambda i: (jnp.minimum((i + 1) * nb8, last8), 0)),
            pl.BlockSpec((1, D_MODEL), lambda i: (0, 0)),
            pl.BlockSpec((1, D_MODEL), lambda i: (0, 0)),
            pl.BlockSpec(band.shape, lambda i: (0, 0, 0)),
            pl.BlockSpec((len(POOL_WINDOWS), D_POOL_G, D_POOL_G), lambda i: (0, 0, 0)),
            pl.BlockSpec((1, D_MODEL), lambda i: (0, 0)),
        ],
        out_specs=pl.BlockSpec((POOL_TM, D_MODEL), lambda i: (i, 0)),
        out_shape=jax.ShapeDtypeStruct((SEQ, D_MODEL), F32),
        scratch_shapes=[pltpu.VMEM((ext, D_MODEL), F32), pltpu.VMEM((ext, D_MODEL), BF16),
                        pltpu.VMEM((ext, D_MODEL), BF16), pltpu.VMEM((POOL_TM, D_MODEL), F32)],
        compiler_params=_cparams(1),
        name="pool_mixer",
    )(x2d, x2d, x2d, gin, gout, band, pw, ps)


def _filter_constants():
    L = SEQ
    t = np.linspace(0.0, 1.0, L)
    w_pos = 2.0 * np.pi * np.arange(L) / L
    bands = np.linspace(1e-4, FILTER_BANDS - 1, FILTER_BANDS)
    ang = w_pos[:, None] * bands[None, :]
    emb = np.concatenate([t[:, None], np.cos(ang), -np.sin(ang)], axis=-1)
    emb_t = np.zeros((EMB_PAD, L + LANES))
    emb_t[:EMB_DIM, :L] = emb.T
    t_rev = t[(L - np.arange(L)) % L]
    max_decay = math.log(DECAY_TARGET) / FAST_DECAY_PCT
    min_decay = math.log(DECAY_TARGET) / SLOW_DECAY_PCT
    absdelta = np.abs(np.linspace(min_decay, max_decay, D_HY))[:, None]
    f32 = lambda a: jnp.asarray(a, F32)
    return f32(emb_t), f32(t[None, :]), f32(t_rev[None, :]), f32(absdelta)


def _mixer_layer0(x2d, g, w_in, hy_short_w, hy_short_b, f_w1, f_b1, f_w2, f_b2, f_w3, f_b3,
                  freq, hy_bias, sc_conv_w, w_out, mlp_w1, mlp_w2, layer):
    emb_t, t_fwd, t_rev, absdelta = _filter_constants()
    w1t = jnp.zeros((FILTER_HIDDEN, EMB_PAD), F32).at[:, :EMB_DIM].set(f_w1.T)
    col = lambda v: v[:, None].astype(F32)
    kf_q, kb_q, wt, wo = _filters(emb_t, t_fwd, t_rev, w1t, col(f_b1), f_w2.T, col(f_b2), col(freq),
                                  f_w3.T.astype(BF16), col(f_b3), absdelta, w_in, w_out)

    zt, w1_next = _in_proj(x2d, g[0][None, :], wt, mlp_w1, layer + 1)
    hy_par = jnp.concatenate([hy_short_w.T, hy_short_b[:, None]], axis=1)
    sc_par = jnp.concatenate([sc_conv_w.T, jnp.zeros((D_SC, 1), F32)], axis=1)

    bias2 = jnp.broadcast_to(hy_bias[:, None].astype(F32), (D_HY, FFT_N2))
    ya_q, w1_this, w2_this, w2_next = _hyena(
        zt, hy_par, kf_q, kb_q, bias2, _dft_tables(), mlp_w1, mlp_w2, layer)
    x2d = _out_proj(ya_q, zt, sc_par, wo, x2d, g[1][None, :])
    return x2d, (w1_this, w2_this), (w1_next, w2_next)


def kernel(x, norm_g, mix_w_in, hy_short_w, hy_short_b, hy_filt_w1, hy_filt_b1, hy_filt_w2,
           hy_filt_b2, hy_filt_w3, hy_filt_b3, hy_freq, hy_bias, sc_conv_w, mix_w_out,
           pool_w, pool_scale, mlp_w1, mlp_w2):
    x2d = x.reshape(SEQ, D_MODEL)
    depth = norm_g.shape[0]
    assert depth % 2 == 0, "an even layer's mixer kernels narrow the MLP weights of itself and the next layer"
    for i in range(depth):
        g = norm_g[i]
        j = i // 2
        gin, gout = g[2][None, :], g[3][None, :]
        if i % 2 == 0:
            x2d, w_this, w_next = _mixer_layer0(
                x2d, g, mix_w_in[j], hy_short_w[j], hy_short_b[j], hy_filt_w1[j], hy_filt_b1[j],
                hy_filt_w2[j], hy_filt_b2[j], hy_filt_w3[j], hy_filt_b3[j], hy_freq[j], hy_bias[j],
                sc_conv_w[j], mix_w_out[j], mlp_w1, mlp_w2, i)
            x2d = _mlp(x2d, gin, gout, *w_this)
        else:
            x2d = _pool(x2d, g[0][None, :], g[1][None, :], pool_w[j].astype(BF16),
                        pool_scale[j][None, :])
            x2d = _mlp(x2d, gin, gout, *w_next)
    return x2d.reshape(x.shape)
```

```python
import functools
import math

import numpy as np
import jax
import jax.numpy as jnp
from jax import lax
from jax.experimental import pallas as pl
from jax.experimental.pallas import tpu as pltpu

F32 = jnp.float32
BF16 = jnp.bfloat16

D_MODEL = 2048
SEQ = 8192
D_HY = D_MODEL // 2
D_SC = D_MODEL // 2
D_IN = 3 * D_HY + 3 * D_SC
FILTER_BANDS = 16
EMB_DIM = 1 + 2 * FILTER_BANDS
EMB_PAD = 40
FILTER_HIDDEN = 64
DECAY_TARGET = 1e-2
FAST_DECAY_PCT = 0.3
SLOW_DECAY_PCT = 1.5
POOL_WINDOWS = (2, 4, 8, 16)
D_POOL_G = D_MODEL // len(POOL_WINDOWS)
D_FF = 4 * D_MODEL
NORM_EPS = 1e-6
LANES = 128

FFT_N = 2 * SEQ
FFT_N2 = 256
FFT_N1 = FFT_N // FFT_N2
FFT_N1_LIVE = SEQ // FFT_N2
CH_GROUP = 8
PAIR = 2 * CH_GROUP
Q16_SHAPE = (D_MODEL // 2 // PAIR, FFT_N1_LIVE, PAIR, FFT_N2)

VMEM_LIMIT = 56 * 1024 * 1024


def _cparams(n_axes):
    return pltpu.CompilerParams(
        dimension_semantics=("arbitrary",) * n_axes, vmem_limit_bytes=VMEM_LIMIT)


def _rms_rows(x, g):
    r = lax.rsqrt(jnp.mean(x * x, axis=-1, keepdims=True) + NORM_EPS)
    return x * r * g


IN_TM = 1024
IN_TN = 1536
IN_TC = 256


def _in_proj_kernel(x_ref, g_ref, wt_ref, w1_ref, zt_ref, w1b_ref, h_ref):
    w1b_ref[...] = w1_ref[...].astype(BF16)

    @pl.when(pl.program_id(1) == 0)
    def _():
        h_ref[...] = _rms_rows(x_ref[...], g_ref[...]).astype(BF16)

    for c in range(IN_TM // IN_TC):
        t = slice(c * IN_TC, (c + 1) * IN_TC)
        zt_ref[:, t] = lax.dot_general(
            wt_ref[...], h_ref[t, :], (((1,), (1,)), ((), ())), preferred_element_type=F32).astype(BF16)


def _in_proj(x2d, g, wt, mlp_w1, layer):
    ni, nj = SEQ // IN_TM, D_IN // IN_TN
    w1_blk = (D_MODEL // ni, D_FF // nj)
    return pl.pallas_call(
        _in_proj_kernel,
        grid=(ni, nj),
        in_specs=[
            pl.BlockSpec((IN_TM, D_MODEL), lambda i, j: (i, 0)),
            pl.BlockSpec((1, D_MODEL), lambda i, j: (0, 0)),
            pl.BlockSpec((IN_TN, D_MODEL), lambda i, j: (j, 0)),
            pl.BlockSpec((None,) + w1_blk, lambda i, j: (layer, i, j)),
        ],
        out_specs=[pl.BlockSpec((IN_TN, IN_TM), lambda i, j: (j, i)),
                   pl.BlockSpec(w1_blk, lambda i, j: (i, j))],
        out_shape=[jax.ShapeDtypeStruct((D_IN, SEQ), BF16),
                   jax.ShapeDtypeStruct((D_MODEL, D_FF), BF16)],
        scratch_shapes=[pltpu.VMEM((IN_TM, D_MODEL), BF16)],
        compiler_params=_cparams(2),
        name="in_proj",
    )(x2d, g, wt, mlp_w1)


GATE_ROWS = 16
GATE_CHUNK = 1024


def _conv3_chunk(zwin, w, off, first, last):
    width = zwin.shape[1]
    zm = pltpu.roll(zwin, 1, 1)[:, off:off + GATE_CHUNK]
    zp = pltpu.roll(zwin, width - 1, 1)[:, off:off + GATE_CHUNK]
    z = zwin[:, off:off + GATE_CHUNK]
    lane = lax.broadcasted_iota(jnp.int32, z.shape, 1)
    if first:
        zm = jnp.where(lane == 0, 0.0, zm)
    if last:
        zp = jnp.where(lane == GATE_CHUNK - 1, 0.0, zp)
    return zm * w[:, 0:1] + z * w[:, 1:2] + zp * w[:, 2:3]


def _conv3_bias_chunks(z_ref, p_ref, rows):
    n_chunks = SEQ // GATE_CHUNK
    p = p_ref[rows, :]
    out = []
    for c in range(n_chunks):
        lo = max(c * GATE_CHUNK - LANES, 0)
        hi = min((c + 1) * GATE_CHUNK + LANES, SEQ)
        zwin = z_ref[rows, lo:hi].astype(F32)
        out.append(_conv3_chunk(zwin, p, c * GATE_CHUNK - lo, c == 0, c == n_chunks - 1) + p[:, 3:4])
    return out


def _to_q_groups(chunks):
    groups = []
    for g in range(GATE_ROWS // CH_GROUP):
        slabs = [ch[g * CH_GROUP:(g + 1) * CH_GROUP, a * FFT_N2:(a + 1) * FFT_N2]
                 for ch in chunks for a in range(GATE_CHUNK // FFT_N2)]
        groups.append(jnp.stack(slabs, axis=0))
    return groups


FILT_TB = 1024


def _filter_kernel(emb_ref, embn_ref, tf_ref, tb_ref, w1_ref, b1_ref, w2_ref, b2_ref, fr_ref, w3_ref, b3_ref,
                   ad_ref, rev_ref, win_ref, wo_ref, kf_ref, kb_ref, wt_ref, wob_ref, kfs_ref, kbs_ref):
    wt_ref[...] = win_ref[...].T.astype(BF16)
    wob_ref[...] = wo_ref[...].astype(BF16)
    hi = lax.Precision.HIGHEST
    fr = fr_ref[...]
    emb = jnp.concatenate([emb_ref[...], embn_ref[...]], axis=1)
    a1 = jnp.dot(w1_ref[...], emb, precision=hi, preferred_element_type=F32) + b1_ref[...]
    h1 = jnp.sin(fr * a1)
    a2 = jnp.dot(w2_ref[...], h1, precision=hi, preferred_element_type=F32) + b2_ref[...]
    h2 = jnp.sin(fr * a2)
    ad = ad_ref[...]
    h2f = h2[:, :FILT_TB].astype(BF16)
    of = jnp.dot(w3_ref[:D_HY, :], h2f, preferred_element_type=F32) + b3_ref[:D_HY, :]
    kfs_ref[...] = of * jnp.exp(-(ad * tf_ref[...]))
    h2b = pltpu.roll(h2, FILT_TB + LANES - 1, 1)[:, :FILT_TB].astype(BF16)
    h2r = jnp.dot(h2b, rev_ref[...], preferred_element_type=F32).astype(BF16)
    ob = jnp.dot(w3_ref[D_HY:, :], h2r, preferred_element_type=F32) + b3_ref[D_HY:, :]
    lane = lax.broadcasted_iota(jnp.int32, (D_HY, FILT_TB), 1)
    is_sep = (lane == 0) & (pl.program_id(0) == pl.num_programs(0) - 1)
    kbs_ref[...] = jnp.where(is_sep, 0.0, ob * jnp.exp(-(ad * tb_ref[...])))

    def to_q(cp, carry):
        rows = pl.ds(pl.multiple_of(cp * PAIR, PAIR), PAIR)
        for n1 in range(FILT_TB // FFT_N2):
            lanes = slice(n1 * FFT_N2, (n1 + 1) * FFT_N2)
            kf_ref[cp, n1] = kfs_ref[rows, lanes].astype(BF16)
            kb_ref[cp, n1] = kbs_ref[rows, lanes].astype(BF16)
        return carry

    lax.fori_loop(0, D_HY // PAIR, to_q, 0)


def _filters(emb_t, t_fwd, t_rev, w1t, b1, w2t, b2, fr, w3t, b3, absdelta, w_in, w_out):
    full = lambda a: pl.BlockSpec(a.shape, lambda i: (0,) * a.ndim)
    steps = SEQ // FILT_TB
    per = FILT_TB // LANES
    rev = jnp.asarray(np.eye(FILT_TB)[::-1], BF16)
    out = jax.ShapeDtypeStruct(Q16_SHAPE, BF16)
    qblk = (Q16_SHAPE[0], FILT_TB // FFT_N2, PAIR, FFT_N2)
    return pl.pallas_call(
        _filter_kernel,
        grid=(steps,),
        in_specs=[pl.BlockSpec((EMB_PAD, FILT_TB), lambda i: (0, i)),
                  pl.BlockSpec((EMB_PAD, LANES), lambda i: (0, (i + 1) * per)),
                  pl.BlockSpec((1, FILT_TB), lambda i: (0, i)),
                  pl.BlockSpec((1, FILT_TB), lambda i: (0, steps - 1 - i)),
                  full(w1t), full(b1), full(w2t), full(b2), full(fr), full(w3t), full(b3),
                  full(absdelta), full(rev),
                  pl.BlockSpec((D_MODEL, D_IN // steps), lambda i: (0, i)),
                  pl.BlockSpec((D_MODEL // steps, D_MODEL), lambda i: (i, 0))],
        out_specs=[pl.BlockSpec(qblk, lambda i: (0, i, 0, 0)),
                   pl.BlockSpec(qblk, lambda i: (0, steps - 1 - i, 0, 0)),
                   pl.BlockSpec((D_IN // steps, D_MODEL), lambda i: (i, 0)),
                   pl.BlockSpec((D_MODEL // steps, D_MODEL), lambda i: (i, 0))],
        out_shape=[out, out, jax.ShapeDtypeStruct((D_IN, D_MODEL), BF16),
                   jax.ShapeDtypeStruct((D_MODEL, D_MODEL), BF16)],
        scratch_shapes=[pltpu.VMEM((D_HY, FILT_TB), F32)] * 2,
        compiler_params=_cparams(1),
        name="filters",
    )(emb_t, emb_t, t_fwd, t_rev, w1t, b1, w2t, b2, fr, w3t, b3, absdelta, rev, w_in, w_out)


HY_CB = 32
RING = 3
GROUP_ROWS = CH_GROUP * FFT_N1_LIVE
N_K1 = FFT_N1 // 2 + 1
RE_ROWS = N_K1 * CH_GROUP
IM_ROWS = (N_K1 - 2) * CH_GROUP


def _dft_tables():
    n1 = np.arange(FFT_N1_LIVE, dtype=np.float64)
    k1 = np.arange(N_K1, dtype=np.float64)
    th = 2.0 * np.pi * np.outer(k1, n1) / FFT_N1
    eye = np.eye(CH_GROUP)
    def stage1(cos, sin):
        return np.concatenate([np.einsum("kn,cd->kcnd", cos, eye).reshape(RE_ROWS, -1),
                               np.einsum("kn,cd->kcnd", -sin[1:-1], eye).reshape(IM_ROWS, -1)])
    bd1 = stage1(np.cos(th), np.sin(th))
    thk = 2.0 * np.pi * np.outer(k1, np.arange(FFT_N1, dtype=np.float64)) / FFT_N1
    bd1k = stage1(np.cos(thk), np.sin(thk))
    wgt = np.where((k1 == 0) | (k1 == FFT_N1 // 2), 1.0, 2.0)[:, None] / FFT_N
    bd1inv = np.concatenate(
        [np.einsum("kn,cd->nckd", wgt * np.cos(th), eye).reshape(GROUP_ROWS, RE_ROWS),
         np.einsum("kn,cd->nckd", (-wgt * np.sin(th))[1:-1], eye).reshape(GROUP_ROWS, IM_ROWS)], axis=1)
    n2 = np.arange(FFT_N2, dtype=np.float64)
    ph = 2.0 * np.pi * np.outer(k1, n2) / FFT_N
    twr = np.repeat(np.cos(ph), CH_GROUP, axis=0)
    twi = np.repeat(-np.sin(ph), CH_GROUP, axis=0)
    ps = 2.0 * np.pi * np.outer(n2, n2) / FFT_N2
    cr, ci = np.cos(ps), -np.sin(ps)
    w3 = np.block([[cr, ci], [-ci, cr]])
    w3inv = np.block([[cr, -ci], [ci, cr]])
    f32 = lambda a: jnp.asarray(a, F32)
    return (f32(bd1).astype(BF16), f32(bd1k).astype(BF16), f32(bd1inv).astype(BF16), f32(twr), f32(twi),
            f32(w3).astype(BF16), f32(w3inv).astype(BF16))


def _hyena_kernel(zx0_ref, zx1_ref, zv_ref, px0_ref, px1_ref, pv_ref, kf_ref, kb_ref, bias_ref,
                  bd1_ref, bd1k_ref, bd1inv_ref, twr_ref, twi_ref, w3_ref, w3inv_ref,
                  w1_hbm, w2_hbm, ya_ref, w1b_ref, w2b_ref, w2nb_ref, ring1, ring2, ring3, sems, *, layer):
    j = pl.program_id(0)
    steps = pl.num_programs(0)

    def fetch(s):
        slot = lax.rem(s, RING)
        r1, r2 = w1b_ref.shape[0], w2b_ref.shape[0]
        return (
            pltpu.make_async_copy(w1_hbm.at[layer, pl.ds(s * r1, r1), :], ring1.at[slot], sems.at[0, slot]),
            pltpu.make_async_copy(w2_hbm.at[layer, pl.ds(s * r2, r2), :], ring2.at[slot], sems.at[1, slot]),
            pltpu.make_async_copy(w2_hbm.at[layer + 1, pl.ds(s * r2, r2), :], ring3.at[slot], sems.at[2, slot]),
        )

    @pl.when(j == 0)
    def _():
        for s in range(RING - 1):
            for cp in fetch(s):
                cp.start()

    @pl.when(j + RING - 1 < steps)
    def _():
        for cp in fetch(j + RING - 1):
            cp.start()

    for cp in fetch(j):
        cp.wait()
    slot = lax.rem(j, RING)
    w1b_ref[...] = ring1[slot].astype(BF16)
    w2b_ref[...] = ring2[slot].astype(BF16)
    w2nb_ref[...] = ring3[slot].astype(BF16)
    twr, twi = twr_ref[...], twi_ref[...]
    zrow = jnp.zeros((CH_GROUP, FFT_N2), F32)

    def stage1(xq, bd_ref):
        xq = xq.reshape(xq.shape[0] * CH_GROUP, FFT_N2).astype(BF16)
        a = jnp.dot(bd_ref[...], xq, preferred_element_type=F32)
        ar = a[:RE_ROWS]
        ai = jnp.concatenate([zrow, a[RE_ROWS:], zrow], axis=0)
        br = ar * twr - ai * twi
        bi = ar * twi + ai * twr
        return jnp.concatenate([br, bi], axis=1)

    def pair(pi):
        gs = [2 * pi, 2 * pi + 1]
        rows = slice(pi * GATE_ROWS, (pi + 1) * GATE_ROWS)
        x1c = _conv3_bias_chunks(zx1_ref, px1_ref, rows)
        vc = _conv3_bias_chunks(zv_ref, pv_ref, rows)
        us = _to_q_groups([v * x1 for v, x1 in zip(vc, x1c)])
        x0s = _to_q_groups(_conv3_bias_chunks(zx0_ref, px0_ref, rows))
        kpair = jnp.concatenate([kf_ref[pi], kb_ref[pi]], axis=0).astype(F32)
        ks = [kpair[:, i * CH_GROUP:(i + 1) * CH_GROUP, :] for i in range(2)]
        b_all = jnp.concatenate(
            [blk for u, k in zip(us, ks) for blk in (stage1(u, bd1_ref), stage1(k, bd1k_ref))], axis=0)
        s_all = jnp.dot(b_all.astype(BF16), w3_ref[...], preferred_element_type=F32)
        ycats = []
        for i in range(2):
            xs = s_all[(2 * i) * RE_ROWS:(2 * i + 1) * RE_ROWS]
            ks = s_all[(2 * i + 1) * RE_ROWS:(2 * i + 2) * RE_ROWS]
            xr, xi = xs[:, :FFT_N2], xs[:, FFT_N2:]
            kr, ki = ks[:, :FFT_N2], ks[:, FFT_N2:]
            ycats.append(jnp.concatenate([xr * kr - xi * ki, xr * ki + xi * kr], axis=1))
        ap_all = jnp.dot(jnp.concatenate(ycats, axis=0).astype(BF16), w3inv_ref[...],
                         preferred_element_type=F32)
        yas = []
        for i, (g, u) in enumerate(zip(gs, us)):
            ap = ap_all[i * RE_ROWS:(i + 1) * RE_ROWS]
            apr, api = ap[:, :FFT_N2], ap[:, FFT_N2:]
            bpr = apr * twr + api * twi
            bpi = api * twr - apr * twi
            b2 = jnp.concatenate([bpr, bpi[CH_GROUP:RE_ROWS - CH_GROUP]], axis=0).astype(BF16)
            y = jnp.dot(bd1inv_ref[...], b2, preferred_element_type=F32)
            y3 = y.reshape(FFT_N1_LIVE, CH_GROUP, FFT_N2)
            bias = bias_ref[g * CH_GROUP:(g + 1) * CH_GROUP, :]
            yas.append(x0s[i] * (y3 + bias[None] * u))
        ya_ref[pi] = jnp.concatenate(yas, axis=1).astype(BF16)

    for pi in range(HY_CB // GATE_ROWS):
        pair(pi)


def _hyena(zt, hy_par, kf_q, kb_q, bias2, tables, mlp_w1, mlp_w2, layer):
    bd1, bd1k, bd1inv, twr, twi, w3, w3inv = tables
    steps = D_HY // HY_CB
    zspec = lambda s: pl.BlockSpec((HY_CB, SEQ), lambda j, s=s: (s * steps + j, 0))
    pspec = lambda s: pl.BlockSpec((HY_CB, 4), lambda j, s=s: (s * steps + j, 0))
    qspec = pl.BlockSpec((HY_CB // PAIR,) + Q16_SHAPE[1:], lambda j: (j, 0, 0, 0))
    full = lambda a: pl.BlockSpec(a.shape, lambda j: (0,) * a.ndim)
    w1_blk = (D_MODEL // steps, D_FF)
    w2_blk = (D_FF // steps, D_MODEL)
    assert steps >= RING
    return pl.pallas_call(
        functools.partial(_hyena_kernel, layer=layer),
        grid=(steps,),
        in_specs=[zspec(0), zspec(1), zspec(2), pspec(0), pspec(1), pspec(2), qspec, qspec,
                  pl.BlockSpec((HY_CB, FFT_N2), lambda j: (j, 0)),
                  full(bd1), full(bd1k), full(bd1inv), full(twr), full(twi), full(w3), full(w3inv),
                  pl.BlockSpec(memory_space=pl.ANY), pl.BlockSpec(memory_space=pl.ANY)],
        out_specs=[qspec, pl.BlockSpec(w1_blk, lambda j: (j, 0)), pl.BlockSpec(w2_blk, lambda j: (j, 0)),
                   pl.BlockSpec(w2_blk, lambda j: (j, 0))],
        out_shape=[jax.ShapeDtypeStruct(Q16_SHAPE, BF16),
                   jax.ShapeDtypeStruct((D_MODEL, D_FF), BF16),
                   jax.ShapeDtypeStruct((D_FF, D_MODEL), BF16),
                   jax.ShapeDtypeStruct((D_FF, D_MODEL), BF16)],
        scratch_shapes=[pltpu.VMEM((RING,) + w1_blk, F32), pltpu.VMEM((RING,) + w2_blk, F32),
                        pltpu.VMEM((RING,) + w2_blk, F32), pltpu.SemaphoreType.DMA((3, RING))],
        compiler_params=_cparams(1),
        name="hyena_fftconv",
    )(zt, zt, zt, hy_par, hy_par, hy_par, kf_q, kb_q, bias2, bd1, bd1k, bd1inv, twr, twi, w3, w3inv,
      mlp_w1, mlp_w2)


OUT_TM = 512
OUT_CH = 256


def _out_proj_kernel(ya_ref, gb_ref, gcl_ref, gc_ref, gcr_ref, xvl_ref, xv_ref, xvr_ref, psc_ref,
                     wa_ref, wb_ref, x_ref, g_ref, o_ref):
    i = pl.program_id(0)
    tn = (((0,), (0,)), ((), ()))
    ya = jnp.concatenate([ya_ref[:, a].reshape(D_HY, FFT_N2) for a in range(OUT_TM // FFT_N2)], axis=1)
    cur = slice(LANES, LANES + OUT_TM)
    lane = lax.broadcasted_iota(jnp.int32, (OUT_CH, OUT_TM), 1)
    first = (lane == 0) & (i == 0)
    final = (lane == OUT_TM - 1) & (i == pl.num_programs(0) - 1)
    m = lax.dot_general(ya, wa_ref[...], tn, preferred_element_type=F32)
    for c in range(D_SC // OUT_CH):
        rows = slice(c * OUT_CH, (c + 1) * OUT_CH)
        window = lambda l, mid, r: jnp.concatenate([l[rows, :], mid[rows, :], r[rows, :]], axis=1).astype(F32)
        p = window(gcl_ref, gc_ref, gcr_ref) * window(xvl_ref, xv_ref, xvr_ref)
        pm = jnp.where(first, 0.0, pltpu.roll(p, 1, 1)[:, cur])
        pp = jnp.where(final, 0.0, pltpu.roll(p, p.shape[1] - 1, 1)[:, cur])
        w = psc_ref[rows, :]
        conv = pm * w[:, 0:1] + p[:, cur] * w[:, 1:2] + pp * w[:, 2:3]
        yb = (gb_ref[rows, :].astype(F32) * conv).astype(BF16)
        m = m + lax.dot_general(yb, wb_ref[rows, :], tn, preferred_element_type=F32)
    o_ref[...] = x_ref[...] + _rms_rows(m, g_ref[...])


def _out_proj(ya_q, zt, sc_par, wo, x2d, g):
    per = OUT_TM // LANES
    last = SEQ // LANES - 1
    main = lambda s: pl.BlockSpec((D_SC, OUT_TM), lambda i, s=s: (s, i))
    left = lambda s: pl.BlockSpec((D_SC, LANES), lambda i, s=s: (s, jnp.maximum(i * per - 1, 0)))
    right = lambda s: pl.BlockSpec((D_SC, LANES), lambda i, s=s: (s, jnp.minimum((i + 1) * per, last)))
    return pl.pallas_call(
        _out_proj_kernel,
        grid=(SEQ // OUT_TM,),
        in_specs=[
            pl.BlockSpec((Q16_SHAPE[0], OUT_TM // FFT_N2, PAIR, FFT_N2), lambda i: (0, i, 0, 0)),
            main(3), left(4), main(4), right(4), left(5), main(5), right(5),
            pl.BlockSpec((D_SC, 4), lambda i: (0, 0)),
            pl.BlockSpec((D_HY, D_MODEL), lambda i: (0, 0)),
            pl.BlockSpec((D_SC, D_MODEL), lambda i: (1, 0)),
            pl.BlockSpec((OUT_TM, D_MODEL), lambda i: (i, 0)),
            pl.BlockSpec((1, D_MODEL), lambda i: (0, 0)),
        ],
        out_specs=pl.BlockSpec((OUT_TM, D_MODEL), lambda i: (i, 0)),
        out_shape=jax.ShapeDtypeStruct((SEQ, D_MODEL), F32),
        compiler_params=_cparams(1),
        name="out_proj",
    )(ya_q, zt, zt, zt, zt, zt, zt, zt, sc_par, wo, wo, x2d, g)


MLP_TM = 1024
MLP_TK = 1024
MLP_TA = 512
MLP_TN = 512
MLP_ROWS = 128


def _mlp_kernel(x_ref, gin_ref, gout_ref, w1_ref, w2_ref, o_ref, h_ref):
    k = pl.program_id(1)

    def row_chunks(fn):
        def body(c, carry):
            fn(pl.ds(pl.multiple_of(c * MLP_ROWS, MLP_ROWS), MLP_ROWS))
            return carry
        lax.fori_loop(0, MLP_TM // MLP_ROWS, body, 0)

    @pl.when(k == 0)
    def _():
        def prologue(rows):
            h_ref[rows, :] = _rms_rows(x_ref[rows, :], gin_ref[...]).astype(BF16)
            o_ref[rows, :] = jnp.zeros((MLP_ROWS, D_MODEL), F32)
        row_chunks(prologue)

    for c in range(MLP_TK // MLP_TA):
        mid = slice(c * MLP_TA, (c + 1) * MLP_TA)
        a = jnp.dot(h_ref[...], w1_ref[:, mid], preferred_element_type=F32)
        a = jnp.square(jnp.maximum(a, 0.0)).astype(BF16)
        for n in range(D_MODEL // MLP_TN):
            cols = slice(n * MLP_TN, (n + 1) * MLP_TN)
            o_ref[:, cols] += jnp.dot(a, w2_ref[mid, cols], preferred_element_type=F32)

    @pl.when(k == pl.num_programs(1) - 1)
    def _():
        def epilogue(rows):
            o_ref[rows, :] = x_ref[rows, :] + _rms_rows(o_ref[rows, :], gout_ref[...])
        row_chunks(epilogue)


def _mlp(x2d, gin, gout, w1, w2):
    return pl.pallas_call(
        _mlp_kernel,
        grid=(SEQ // MLP_TM, D_FF // MLP_TK),
        in_specs=[
            pl.BlockSpec((MLP_TM, D_MODEL), lambda i, k: (i, 0)),
            pl.BlockSpec((1, D_MODEL), lambda i, k: (0, 0)),
            pl.BlockSpec((1, D_MODEL), lambda i, k: (0, 0)),
            pl.BlockSpec((D_MODEL, MLP_TK), lambda i, k: (0, k)),
            pl.BlockSpec((MLP_TK, D_MODEL), lambda i, k: (k, 0)),
        ],
        out_specs=pl.BlockSpec((MLP_TM, D_MODEL), lambda i, k: (i, 0)),
        out_shape=jax.ShapeDtypeStruct((SEQ, D_MODEL), F32),
        scratch_shapes=[pltpu.VMEM((MLP_TM, D_MODEL), BF16)],
        compiler_params=_cparams(2),
        name="mlp",
    )(x2d, gin, gout, w1, w2)


POOL_TM = 512
POOL_SUB = 128
HALO = 8


def _pool_bands():
    t = np.arange(POOL_SUB)[:, None]
    j = np.arange(POOL_SUB + 2 * HALO)[None, :]
    return np.stack([(np.abs(j - HALO - t) <= w // 2) for w in POOL_WINDOWS]).astype(np.float32)


def _pool_kernel(xp_ref, x_hbm, xn_ref, gin_ref, gout_ref, band_ref, pw_ref, ps_ref, o_ref,
                 hs_ref, hi_ref, lo_ref, m_ref, xring, sems):
    i = pl.program_id(0)
    steps = pl.num_programs(0)
    gin = gin_ref[...]

    def fetch(s):
        slot = lax.rem(s, RING)
        return pltpu.make_async_copy(x_hbm.at[pl.ds(s * POOL_TM, POOL_TM), :], xring.at[slot], sems.at[slot])

    @pl.when(i == 0)
    def _():
        for s in range(RING - 1):
            fetch(s).start()

    @pl.when(i + RING - 1 < steps)
    def _():
        fetch(i + RING - 1).start()

    fetch(i).wait()
    xm_ref = xring.at[lax.rem(i, RING)]
    hs_ref[0:HALO, :] = jnp.where(i > 0, _rms_rows(xp_ref[...], gin), 0.0)
    hs_ref[HALO:HALO + POOL_TM, :] = _rms_rows(xm_ref[...], gin)
    hs_ref[HALO + POOL_TM:, :] = jnp.where(i < pl.num_programs(0) - 1, _rms_rows(xn_ref[...], gin), 0.0)
    hs = hs_ref[...]
    hi = hs.astype(BF16)
    hi_ref[...] = hi
    lo_ref[...] = (hs - hi.astype(F32)).astype(BF16)
    row = lax.broadcasted_iota(jnp.int32, (POOL_SUB, D_POOL_G), 0) + i * POOL_TM
    for gi, w in enumerate(POOL_WINDOWS):
        r = w // 2
        cols = slice(gi * D_POOL_G, (gi + 1) * D_POOL_G)
        band = band_ref[gi]
        parts = []
        for sb in range(POOL_TM // POOL_SUB):
            win = slice(sb * POOL_SUB, (sb + 1) * POOL_SUB + 2 * HALO)
            s = (jnp.dot(band, hi_ref[win, cols], preferred_element_type=F32)
                 + jnp.dot(band, lo_ref[win, cols], preferred_element_type=F32))
            t = row + sb * POOL_SUB
            cnt = jnp.minimum(t + r + 1, SEQ) - jnp.maximum(t - r, 0)
            u = hs_ref[sb * POOL_SUB + HALO:(sb + 1) * POOL_SUB + HALO, cols]
            parts.append((s / cnt.astype(F32) - u).astype(BF16))
        d = jnp.concatenate(parts, axis=0)
        mg = jnp.dot(d, pw_ref[gi], preferred_element_type=F32)
        m_ref[:, cols] = mg * ps_ref[:, cols]
    o_ref[...] = xm_ref[...] + _rms_rows(m_ref[...], gout_ref[...])


def _pool(x2d, gin, gout, pw, ps):
    nb8 = POOL_TM // HALO
    last8 = SEQ // HALO - 1
    band = jnp.asarray(_pool_bands(), BF16)
    ext = POOL_TM + 2 * HALO
    return pl.pallas_call(
        _pool_kernel,
        grid=(SEQ // POOL_TM,),
        in_specs=[
            pl.BlockSpec((HALO, D_MODEL), lambda i: (jnp.maximum(i * nb8 - 1, 0), 0)),
            pl.BlockSpec(memory_space=pl.ANY),
            pl.BlockSpec((HALO, D_MODEL), lambda i: (jnp.minimum((i + 1) * nb8, last8), 0)),
            pl.BlockSpec((1, D_MODEL), lambda i: (0, 0)),
            pl.BlockSpec((1, D_MODEL), lambda i: (0, 0)),
            pl.BlockSpec(band.shape, lambda i: (0, 0, 0)),
            pl.BlockSpec((len(POOL_WINDOWS), D_POOL_G, D_POOL_G), lambda i: (0, 0, 0)),
            pl.BlockSpec((1, D_MODEL), lambda i: (0, 0)),
        ],
        out_specs=pl.BlockSpec((POOL_TM, D_MODEL), lambda i: (i, 0)),
        out_shape=jax.ShapeDtypeStruct((SEQ, D_MODEL), F32),
        scratch_shapes=[pltpu.VMEM((ext, D_MODEL), F32), pltpu.VMEM((ext, D_MODEL), BF16),
                        pltpu.VMEM((ext, D_MODEL), BF16), pltpu.VMEM((POOL_TM, D_MODEL), F32),
                        pltpu.VMEM((RING, POOL_TM, D_MODEL), F32), pltpu.SemaphoreType.DMA((RING,))],
        compiler_params=_cparams(1),
        name="pool_mixer",
    )(x2d, x2d, x2d, gin, gout, band, pw, ps)


def _filter_constants():
    L = SEQ
    t = np.linspace(0.0, 1.0, L)
    w_pos = 2.0 * np.pi * np.arange(L) / L
    bands = np.linspace(1e-4, FILTER_BANDS - 1, FILTER_BANDS)
    ang = w_pos[:, None] * bands[None, :]
    emb = np.concatenate([t[:, None], np.cos(ang), -np.sin(ang)], axis=-1)
    emb_t = np.zeros((EMB_PAD, L + LANES))
    emb_t[:EMB_DIM, :L] = emb.T
    t_rev = t[(L - np.arange(L)) % L]
    max_decay = math.log(DECAY_TARGET) / FAST_DECAY_PCT
    min_decay = math.log(DECAY_TARGET) / SLOW_DECAY_PCT
    absdelta = np.abs(np.linspace(min_decay, max_decay, D_HY))[:, None]
    f32 = lambda a: jnp.asarray(a, F32)
    return f32(emb_t), f32(t[None, :]), f32(t_rev[None, :]), f32(absdelta)


def _mixer_layer0(x2d, g, w_in, hy_short_w, hy_short_b, f_w1, f_b1, f_w2, f_b2, f_w3, f_b3,
                  freq, hy_bias, sc_conv_w, w_out, mlp_w1, mlp_w2, layer):
    emb_t, t_fwd, t_rev, absdelta = _filter_constants()
    w1t = jnp.zeros((FILTER_HIDDEN, EMB_PAD), F32).at[:, :EMB_DIM].set(f_w1.T)
    col = lambda v: v[:, None].astype(F32)
    kf_q, kb_q, wt, wo = _filters(emb_t, t_fwd, t_rev, w1t, col(f_b1), f_w2.T, col(f_b2), col(freq),
                                  f_w3.T.astype(BF16), col(f_b3), absdelta, w_in, w_out)

    zt, w1_next = _in_proj(x2d, g[0][None, :], wt, mlp_w1, layer + 1)
    hy_par = jnp.concatenate([hy_short_w.T, hy_short_b[:, None]], axis=1)
    sc_par = jnp.concatenate([sc_conv_w.T, jnp.zeros((D_SC, 1), F32)], axis=1)

    bias2 = jnp.broadcast_to(hy_bias[:, None].astype(F32), (D_HY, FFT_N2))
    ya_q, w1_this, w2_this, w2_next = _hyena(
        zt, hy_par, kf_q, kb_q, bias2, _dft_tables(), mlp_w1, mlp_w2, layer)
    x2d = _out_proj(ya_q, zt, sc_par, wo, x2d, g[1][None, :])
    return x2d, (w1_this, w2_this), (w1_next, w2_next)


def kernel(x, norm_g, mix_w_in, hy_short_w, hy_short_b, hy_filt_w1, hy_filt_b1, hy_filt_w2,
           hy_filt_b2, hy_filt_w3, hy_filt_b3, hy_freq, hy_bias, sc_conv_w, mix_w_out,
           pool_w, pool_scale, mlp_w1, mlp_w2):
    x2d = x.reshape(SEQ, D_MODEL)
    depth = norm_g.shape[0]
    assert depth % 2 == 0, "an even layer's mixer kernels narrow the MLP weights of itself and the next layer"
    for i in range(depth):
        g = norm_g[i]
        j = i // 2
        gin, gout = g[2][None, :], g[3][None, :]
        if i % 2 == 0:
            x2d, w_this, w_next = _mixer_layer0(
                x2d, g, mix_w_in[j], hy_short_w[j], hy_short_b[j], hy_filt_w1[j], hy_filt_b1[j],
                hy_filt_w2[j], hy_filt_b2[j], hy_filt_w3[j], hy_filt_b3[j], hy_freq[j], hy_bias[j],
                sc_conv_w[j], mix_w_out[j], mlp_w1, mlp_w2, i)
            x2d = _mlp(x2d, gin, gout, *w_this)
        else:
            x2d = _pool(x2d, g[0][None, :], g[1][None, :], pool_w[j].astype(BF16),
                        pool_scale[j][None, :])
            x2d = _mlp(x2d, gin, gout, *w_next)
    return x2d.reshape(x.shape)
```

```python
import functools
import math

import numpy as np
import jax
import jax.numpy as jnp
from jax import lax
from jax.experimental import pallas as pl
from jax.experimental.pallas import tpu as pltpu

F32 = jnp.float32
BF16 = jnp.bfloat16

D_MODEL = 2048
SEQ = 8192
D_HY = D_MODEL // 2
D_SC = D_MODEL // 2
D_IN = 3 * D_HY + 3 * D_SC
FILTER_BANDS = 16
EMB_DIM = 1 + 2 * FILTER_BANDS
EMB_PAD = 40
FILTER_HIDDEN = 64
DECAY_TARGET = 1e-2
FAST_DECAY_PCT = 0.3
SLOW_DECAY_PCT = 1.5
POOL_WINDOWS = (2, 4, 8, 16)
D_POOL_G = D_MODEL // len(POOL_WINDOWS)
D_FF = 4 * D_MODEL
NORM_EPS = 1e-6
LANES = 128

FFT_N = 2 * SEQ
FFT_N2 = 256
FFT_N1 = FFT_N // FFT_N2
FFT_N1_LIVE = SEQ // FFT_N2
CH_GROUP = 8
PAIR = 2 * CH_GROUP
Q16_SHAPE = (D_MODEL // 2 // PAIR, FFT_N1_LIVE, PAIR, FFT_N2)

VMEM_LIMIT = 56 * 1024 * 1024


def _cparams(n_axes):
    return pltpu.CompilerParams(
        dimension_semantics=("arbitrary",) * n_axes, vmem_limit_bytes=VMEM_LIMIT)


def _rms_rows(x, g):
    r = lax.rsqrt(jnp.mean(x * x, axis=-1, keepdims=True) + NORM_EPS)
    return x * r * g


IN_TM = 1024
IN_TN = 1536
IN_TC = 256


def _in_proj_kernel(x_ref, g_ref, wt_ref, w1_ref, zt_ref, w1b_ref, h_ref):
    w1b_ref[...] = w1_ref[...].astype(BF16)

    @pl.when(pl.program_id(1) == 0)
    def _():
        h_ref[...] = _rms_rows(x_ref[...], g_ref[...]).astype(BF16)

    for c in range(IN_TM // IN_TC):
        t = slice(c * IN_TC, (c + 1) * IN_TC)
        zt_ref[:, t] = lax.dot_general(
            wt_ref[...], h_ref[t, :], (((1,), (1,)), ((), ())), preferred_element_type=F32).astype(BF16)


def _in_proj(x2d, g, wt, mlp_w1, layer):
    ni, nj = SEQ // IN_TM, D_IN // IN_TN
    w1_blk = (D_MODEL // ni, D_FF // nj)
    return pl.pallas_call(
        _in_proj_kernel,
        grid=(ni, nj),
        in_specs=[
            pl.BlockSpec((IN_TM, D_MODEL), lambda i, j: (i, 0)),
            pl.BlockSpec((1, D_MODEL), lambda i, j: (0, 0)),
            pl.BlockSpec((IN_TN, D_MODEL), lambda i, j: (j, 0)),
            pl.BlockSpec((None,) + w1_blk, lambda i, j: (layer, i, j)),
        ],
        out_specs=[pl.BlockSpec((IN_TN, IN_TM), lambda i, j: (j, i)),
                   pl.BlockSpec(w1_blk, lambda i, j: (i, j))],
        out_shape=[jax.ShapeDtypeStruct((D_IN, SEQ), BF16),
                   jax.ShapeDtypeStruct((D_MODEL, D_FF), BF16)],
        scratch_shapes=[pltpu.VMEM((IN_TM, D_MODEL), BF16)],
        compiler_params=_cparams(2),
        name="in_proj",
    )(x2d, g, wt, mlp_w1)


GATE_ROWS = 16
GATE_CHUNK = 1024


def _conv3_chunk(zwin, w, off, first, last):
    width = zwin.shape[1]
    zm = pltpu.roll(zwin, 1, 1)[:, off:off + GATE_CHUNK]
    zp = pltpu.roll(zwin, width - 1, 1)[:, off:off + GATE_CHUNK]
    z = zwin[:, off:off + GATE_CHUNK]
    lane = lax.broadcasted_iota(jnp.int32, z.shape, 1)
    if first:
        zm = jnp.where(lane == 0, 0.0, zm)
    if last:
        zp = jnp.where(lane == GATE_CHUNK - 1, 0.0, zp)
    return zm * w[:, 0:1] + z * w[:, 1:2] + zp * w[:, 2:3]


def _conv3_bias_chunks(z_ref, p_ref, rows):
    n_chunks = SEQ // GATE_CHUNK
    p = p_ref[rows, :]
    out = []
    for c in range(n_chunks):
        lo = max(c * GATE_CHUNK - LANES, 0)
        hi = min((c + 1) * GATE_CHUNK + LANES, SEQ)
        zwin = z_ref[rows, lo:hi].astype(F32)
        out.append(_conv3_chunk(zwin, p, c * GATE_CHUNK - lo, c == 0, c == n_chunks - 1) + p[:, 3:4])
    return out


def _to_q_groups(chunks):
    groups = []
    for g in range(GATE_ROWS // CH_GROUP):
        slabs = [ch[g * CH_GROUP:(g + 1) * CH_GROUP, a * FFT_N2:(a + 1) * FFT_N2]
                 for ch in chunks for a in range(GATE_CHUNK // FFT_N2)]
        groups.append(jnp.stack(slabs, axis=0))
    return groups


FILT_TB = 1024


def _filter_kernel(emb_ref, embn_ref, tf_ref, tb_ref, w1_ref, b1_ref, w2_ref, b2_ref, fr_ref, w3_ref, b3_ref,
                   ad_ref, rev_ref, win_ref, wo_ref, kf_ref, kb_ref, wt_ref, wob_ref, kfs_ref, kbs_ref):
    wt_ref[...] = win_ref[...].T.astype(BF16)
    wob_ref[...] = wo_ref[...].astype(BF16)
    hi = lax.Precision.HIGHEST
    fr = fr_ref[...]
    emb = jnp.concatenate([emb_ref[...], embn_ref[...]], axis=1)
    a1 = jnp.dot(w1_ref[...], emb, precision=hi, preferred_element_type=F32) + b1_ref[...]
    h1 = jnp.sin(fr * a1)
    a2 = jnp.dot(w2_ref[...], h1, precision=hi, preferred_element_type=F32) + b2_ref[...]
    h2 = jnp.sin(fr * a2)
    ad = ad_ref[...]
    h2f = h2[:, :FILT_TB].astype(BF16)
    of = jnp.dot(w3_ref[:D_HY, :], h2f, preferred_element_type=F32) + b3_ref[:D_HY, :]
    kfs_ref[...] = of * jnp.exp(-(ad * tf_ref[...]))
    h2b = pltpu.roll(h2, FILT_TB + LANES - 1, 1)[:, :FILT_TB].astype(BF16)
    h2r = jnp.dot(h2b, rev_ref[...], preferred_element_type=F32).astype(BF16)
    ob = jnp.dot(w3_ref[D_HY:, :], h2r, preferred_element_type=F32) + b3_ref[D_HY:, :]
    lane = lax.broadcasted_iota(jnp.int32, (D_HY, FILT_TB), 1)
    is_sep = (lane == 0) & (pl.program_id(0) == pl.num_programs(0) - 1)
    kbs_ref[...] = jnp.where(is_sep, 0.0, ob * jnp.exp(-(ad * tb_ref[...])))

    def to_q(cp, carry):
        rows = pl.ds(pl.multiple_of(cp * PAIR, PAIR), PAIR)
        for n1 in range(FILT_TB // FFT_N2):
            lanes = slice(n1 * FFT_N2, (n1 + 1) * FFT_N2)
            kf_ref[cp, n1] = kfs_ref[rows, lanes].astype(BF16)
            kb_ref[cp, n1] = kbs_ref[rows, lanes].astype(BF16)
        return carry

    lax.fori_loop(0, D_HY // PAIR, to_q, 0)


def _filters(emb_t, t_fwd, t_rev, w1t, b1, w2t, b2, fr, w3t, b3, absdelta, w_in, w_out):
    full = lambda a: pl.BlockSpec(a.shape, lambda i: (0,) * a.ndim)
    steps = SEQ // FILT_TB
    per = FILT_TB // LANES
    rev = jnp.asarray(np.eye(FILT_TB)[::-1], BF16)
    out = jax.ShapeDtypeStruct(Q16_SHAPE, BF16)
    qblk = (Q16_SHAPE[0], FILT_TB // FFT_N2, PAIR, FFT_N2)
    return pl.pallas_call(
        _filter_kernel,
        grid=(steps,),
        in_specs=[pl.BlockSpec((EMB_PAD, FILT_TB), lambda i: (0, i)),
                  pl.BlockSpec((EMB_PAD, LANES), lambda i: (0, (i + 1) * per)),
                  pl.BlockSpec((1, FILT_TB), lambda i: (0, i)),
                  pl.BlockSpec((1, FILT_TB), lambda i: (0, steps - 1 - i)),
                  full(w1t), full(b1), full(w2t), full(b2), full(fr), full(w3t), full(b3),
                  full(absdelta), full(rev),
                  pl.BlockSpec((D_MODEL, D_IN // steps), lambda i: (0, i)),
                  pl.BlockSpec((D_MODEL // steps, D_MODEL), lambda i: (i, 0))],
        out_specs=[pl.BlockSpec(qblk, lambda i: (0, i, 0, 0)),
                   pl.BlockSpec(qblk, lambda i: (0, steps - 1 - i, 0, 0)),
                   pl.BlockSpec((D_IN // steps, D_MODEL), lambda i: (i, 0)),
                   pl.BlockSpec((D_MODEL // steps, D_MODEL), lambda i: (i, 0))],
        out_shape=[out, out, jax.ShapeDtypeStruct((D_IN, D_MODEL), BF16),
                   jax.ShapeDtypeStruct((D_MODEL, D_MODEL), BF16)],
        scratch_shapes=[pltpu.VMEM((D_HY, FILT_TB), F32)] * 2,
        compiler_params=_cparams(1),
        name="filters",
    )(emb_t, emb_t, t_fwd, t_rev, w1t, b1, w2t, b2, fr, w3t, b3, absdelta, rev, w_in, w_out)


HY_CB = 32
RING = 3
GROUP_ROWS = CH_GROUP * FFT_N1_LIVE
N_K1 = FFT_N1 // 2 + 1
RE_ROWS = N_K1 * CH_GROUP
IM_ROWS = (N_K1 - 2) * CH_GROUP


def _dft_tables():
    n1 = np.arange(FFT_N1_LIVE, dtype=np.float64)
    k1 = np.arange(N_K1, dtype=np.float64)
    th = 2.0 * np.pi * np.outer(k1, n1) / FFT_N1
    eye = np.eye(CH_GROUP)
    def stage1(cos, sin):
        return np.concatenate([np.einsum("kn,cd->kcnd", cos, eye).reshape(RE_ROWS, -1),
                               np.einsum("kn,cd->kcnd", -sin[1:-1], eye).reshape(IM_ROWS, -1)])
    bd1 = stage1(np.cos(th), np.sin(th))
    thk = 2.0 * np.pi * np.outer(k1, np.arange(FFT_N1, dtype=np.float64)) / FFT_N1
    bd1k = stage1(np.cos(thk), np.sin(thk))
    wgt = np.where((k1 == 0) | (k1 == FFT_N1 // 2), 1.0, 2.0)[:, None] / FFT_N
    bd1inv = np.concatenate(
        [np.einsum("kn,cd->nckd", wgt * np.cos(th), eye).reshape(GROUP_ROWS, RE_ROWS),
         np.einsum("kn,cd->nckd", (-wgt * np.sin(th))[1:-1], eye).reshape(GROUP_ROWS, IM_ROWS)], axis=1)
    n2 = np.arange(FFT_N2, dtype=np.float64)
    ph = 2.0 * np.pi * np.outer(k1, n2) / FFT_N
    twr = np.repeat(np.cos(ph), CH_GROUP, axis=0)
    twi = np.repeat(-np.sin(ph), CH_GROUP, axis=0)
    ps = 2.0 * np.pi * np.outer(n2, n2) / FFT_N2
    cr, ci = np.cos(ps), -np.sin(ps)
    w3 = np.block([[cr, ci], [-ci, cr]])
    w3inv = np.block([[cr, -ci], [ci, cr]])
    f32 = lambda a: jnp.asarray(a, F32)
    return (f32(bd1).astype(BF16), f32(bd1k).astype(BF16), f32(bd1inv).astype(BF16), f32(twr), f32(twi),
            f32(w3).astype(BF16), f32(w3inv).astype(BF16))


def _hyena_kernel(zx0_ref, zx1_ref, zv_ref, px0_ref, px1_ref, pv_ref, kf_ref, kb_ref, bias_ref,
                  bd1_ref, bd1k_ref, bd1inv_ref, twr_ref, twi_ref, w3_ref, w3inv_ref,
                  w1_hbm, w2_hbm, ya_ref, w1b_ref, w2b_ref, w2nb_ref, ring1, ring2, ring3, sems, *, layer):
    j = pl.program_id(0)
    steps = pl.num_programs(0)

    def fetch(s):
        slot = lax.rem(s, RING)
        r1, r2 = w1b_ref.shape[0], w2b_ref.shape[0]
        return (
            pltpu.make_async_copy(w1_hbm.at[layer, pl.ds(s * r1, r1), :], ring1.at[slot], sems.at[0, slot]),
            pltpu.make_async_copy(w2_hbm.at[layer, pl.ds(s * r2, r2), :], ring2.at[slot], sems.at[1, slot]),
            pltpu.make_async_copy(w2_hbm.at[layer + 1, pl.ds(s * r2, r2), :], ring3.at[slot], sems.at[2, slot]),
        )

    @pl.when(j == 0)
    def _():
        for s in range(RING - 1):
            for cp in fetch(s):
                cp.start()

    @pl.when(j + RING - 1 < steps)
    def _():
        for cp in fetch(j + RING - 1):
            cp.start()

    for cp in fetch(j):
        cp.wait()
    slot = lax.rem(j, RING)
    w1b_ref[...] = ring1[slot].astype(BF16)
    w2b_ref[...] = ring2[slot].astype(BF16)
    w2nb_ref[...] = ring3[slot].astype(BF16)
    twr, twi = twr_ref[...], twi_ref[...]
    zrow = jnp.zeros((CH_GROUP, FFT_N2), F32)

    def stage1(xq, bd_ref):
        xq = xq.reshape(xq.shape[0] * CH_GROUP, FFT_N2).astype(BF16)
        a = jnp.dot(bd_ref[...], xq, preferred_element_type=F32)
        ar = a[:RE_ROWS]
        ai = jnp.concatenate([zrow, a[RE_ROWS:], zrow], axis=0)
        br = ar * twr - ai * twi
        bi = ar * twi + ai * twr
        return jnp.concatenate([br, bi], axis=1)

    def pair(pi):
        gs = [2 * pi, 2 * pi + 1]
        rows = slice(pi * GATE_ROWS, (pi + 1) * GATE_ROWS)
        x1c = _conv3_bias_chunks(zx1_ref, px1_ref, rows)
        vc = _conv3_bias_chunks(zv_ref, pv_ref, rows)
        us = _to_q_groups([v * x1 for v, x1 in zip(vc, x1c)])
        x0s = _to_q_groups(_conv3_bias_chunks(zx0_ref, px0_ref, rows))
        kpair = jnp.concatenate([kf_ref[pi], kb_ref[pi]], axis=0).astype(F32)
        ks = [kpair[:, i * CH_GROUP:(i + 1) * CH_GROUP, :] for i in range(2)]
        b_all = jnp.concatenate(
            [blk for u, k in zip(us, ks) for blk in (stage1(u, bd1_ref), stage1(k, bd1k_ref))], axis=0)
        s_all = jnp.dot(b_all.astype(BF16), w3_ref[...], preferred_element_type=F32)
        ycats = []
        for i in range(2):
            xs = s_all[(2 * i) * RE_ROWS:(2 * i + 1) * RE_ROWS]
            ks = s_all[(2 * i + 1) * RE_ROWS:(2 * i + 2) * RE_ROWS]
            xr, xi = xs[:, :FFT_N2], xs[:, FFT_N2:]
            kr, ki = ks[:, :FFT_N2], ks[:, FFT_N2:]
            ycats.append(jnp.concatenate([xr * kr - xi * ki, xr * ki + xi * kr], axis=1))
        ap_all = jnp.dot(jnp.concatenate(ycats, axis=0).astype(BF16), w3inv_ref[...],
                         preferred_element_type=F32)
        yas = []
        for i, (g, u) in enumerate(zip(gs, us)):
            ap = ap_all[i * RE_ROWS:(i + 1) * RE_ROWS]
            apr, api = ap[:, :FFT_N2], ap[:, FFT_N2:]
            bpr = apr * twr + api * twi
            bpi = api * twr - apr * twi
            b2 = jnp.concatenate([bpr, bpi[CH_GROUP:RE_ROWS - CH_GROUP]], axis=0).astype(BF16)
            y = jnp.dot(bd1inv_ref[...], b2, preferred_element_type=F32)
            y3 = y.reshape(FFT_N1_LIVE, CH_GROUP, FFT_N2)
            bias = bias_ref[g * CH_GROUP:(g + 1) * CH_GROUP, :]
            yas.append(x0s[i] * (y3 + bias[None] * u))
        ya_ref[pi] = jnp.concatenate(yas, axis=1).astype(BF16)

    for pi in range(HY_CB // GATE_ROWS):
        pair(pi)


def _hyena(zt, hy_par, kf_q, kb_q, bias2, tables, mlp_w1, mlp_w2, layer):
    bd1, bd1k, bd1inv, twr, twi, w3, w3inv = tables
    steps = D_HY // HY_CB
    zspec = lambda s: pl.BlockSpec((HY_CB, SEQ), lambda j, s=s: (s * steps + j, 0))
    pspec = lambda s: pl.BlockSpec((HY_CB, 4), lambda j, s=s: (s * steps + j, 0))
    qspec = pl.BlockSpec((HY_CB // PAIR,) + Q16_SHAPE[1:], lambda j: (j, 0, 0, 0))
    full = lambda a: pl.BlockSpec(a.shape, lambda j: (0,) * a.ndim)
    w1_blk = (D_MODEL // steps, D_FF)
    w2_blk = (D_FF // steps, D_MODEL)
    assert steps >= RING
    return pl.pallas_call(
        functools.partial(_hyena_kernel, layer=layer),
        grid=(steps,),
        in_specs=[zspec(0), zspec(1), zspec(2), pspec(0), pspec(1), pspec(2), qspec, qspec,
                  pl.BlockSpec((HY_CB, FFT_N2), lambda j: (j, 0)),
                  full(bd1), full(bd1k), full(bd1inv), full(twr), full(twi), full(w3), full(w3inv),
                  pl.BlockSpec(memory_space=pl.ANY), pl.BlockSpec(memory_space=pl.ANY)],
        out_specs=[qspec, pl.BlockSpec(w1_blk, lambda j: (j, 0)), pl.BlockSpec(w2_blk, lambda j: (j, 0)),
                   pl.BlockSpec(w2_blk, lambda j: (j, 0))],
        out_shape=[jax.ShapeDtypeStruct(Q16_SHAPE, BF16),
                   jax.ShapeDtypeStruct((D_MODEL, D_FF), BF16),
                   jax.ShapeDtypeStruct((D_FF, D_MODEL), BF16),
                   jax.ShapeDtypeStruct((D_FF, D_MODEL), BF16)],
        scratch_shapes=[pltpu.VMEM((RING,) + w1_blk, F32), pltpu.VMEM((RING,) + w2_blk, F32),
                        pltpu.VMEM((RING,) + w2_blk, F32), pltpu.SemaphoreType.DMA((3, RING))],
        compiler_params=_cparams(1),
        name="hyena_fftconv",
    )(zt, zt, zt, hy_par, hy_par, hy_par, kf_q, kb_q, bias2, bd1, bd1k, bd1inv, twr, twi, w3, w3inv,
      mlp_w1, mlp_w2)


OUT_TM = 512
OUT_CH = 256


def _out_proj_kernel(ya_ref, gb_ref, gcl_ref, gc_ref, gcr_ref, xvl_ref, xv_ref, xvr_ref, psc_ref,
                     wa_ref, wb_ref, x_ref, g_ref, o_ref):
    i = pl.program_id(0)
    tn = (((0,), (0,)), ((), ()))
    ya = jnp.concatenate([ya_ref[:, a].reshape(D_HY, FFT_N2) for a in range(OUT_TM // FFT_N2)], axis=1)
    cur = slice(LANES, LANES + OUT_TM)
    lane = lax.broadcasted_iota(jnp.int32, (OUT_CH, OUT_TM), 1)
    first = (lane == 0) & (i == 0)
    final = (lane == OUT_TM - 1) & (i == pl.num_programs(0) - 1)
    o_ref[...] = lax.dot_general(ya, wa_ref[...], tn, preferred_element_type=F32)
    for c in range(D_SC // OUT_CH):
        rows = slice(c * OUT_CH, (c + 1) * OUT_CH)
        window = lambda l, mid, r: jnp.concatenate([l[rows, :], mid[rows, :], r[rows, :]], axis=1).astype(F32)
        p = window(gcl_ref, gc_ref, gcr_ref) * window(xvl_ref, xv_ref, xvr_ref)
        pm = jnp.where(first, 0.0, pltpu.roll(p, 1, 1)[:, cur])
        pp = jnp.where(final, 0.0, pltpu.roll(p, p.shape[1] - 1, 1)[:, cur])
        w = psc_ref[rows, :]
        conv = pm * w[:, 0:1] + p[:, cur] * w[:, 1:2] + pp * w[:, 2:3]
        yb = (gb_ref[rows, :].astype(F32) * conv).astype(BF16)
        o_ref[...] += lax.dot_general(yb, wb_ref[rows, :], tn, preferred_element_type=F32)
    o_ref[...] = x_ref[...] + _rms_rows(o_ref[...], g_ref[...])


def _out_proj(ya_q, zt, sc_par, wo, x2d, g):
    per = OUT_TM // LANES
    last = SEQ // LANES - 1
    main = lambda s: pl.BlockSpec((D_SC, OUT_TM), lambda i, s=s: (s, i))
    left = lambda s: pl.BlockSpec((D_SC, LANES), lambda i, s=s: (s, jnp.maximum(i * per - 1, 0)))
    right = lambda s: pl.BlockSpec((D_SC, LANES), lambda i, s=s: (s, jnp.minimum((i + 1) * per, last)))
    return pl.pallas_call(
        _out_proj_kernel,
        grid=(SEQ // OUT_TM,),
        in_specs=[
            pl.BlockSpec((Q16_SHAPE[0], OUT_TM // FFT_N2, PAIR, FFT_N2), lambda i: (0, i, 0, 0)),
            main(3), left(4), main(4), right(4), left(5), main(5), right(5),
            pl.BlockSpec((D_SC, 4), lambda i: (0, 0)),
            pl.BlockSpec((D_HY, D_MODEL), lambda i: (0, 0)),
            pl.BlockSpec((D_SC, D_MODEL), lambda i: (1, 0)),
            pl.BlockSpec((OUT_TM, D_MODEL), lambda i: (i, 0)),
            pl.BlockSpec((1, D_MODEL), lambda i: (0, 0)),
        ],
        out_specs=pl.BlockSpec((OUT_TM, D_MODEL), lambda i: (i, 0)),
        out_shape=jax.ShapeDtypeStruct((SEQ, D_MODEL), F32),
        compiler_params=_cparams(1),
        name="out_proj",
    )(ya_q, zt, zt, zt, zt, zt, zt, zt, sc_par, wo, wo, x2d, g)


MLP_TM = 1024
MLP_TK = 1024
MLP_TA = 512
MLP_TN = 512
MLP_ROWS = 128


def _mlp_kernel(x_ref, gin_ref, gout_ref, w1_ref, w2_ref, o_ref, h_ref):
    k = pl.program_id(1)

    def row_chunks(fn):
        def body(c, carry):
            fn(pl.ds(pl.multiple_of(c * MLP_ROWS, MLP_ROWS), MLP_ROWS))
            return carry
        lax.fori_loop(0, MLP_TM // MLP_ROWS, body, 0)

    @pl.when(k == 0)
    def _():
        def prologue(rows):
            h_ref[rows, :] = _rms_rows(x_ref[rows, :], gin_ref[...]).astype(BF16)
            o_ref[rows, :] = jnp.zeros((MLP_ROWS, D_MODEL), F32)
        row_chunks(prologue)

    for c in range(MLP_TK // MLP_TA):
        mid = slice(c * MLP_TA, (c + 1) * MLP_TA)
        a = jnp.dot(h_ref[...], w1_ref[:, mid], preferred_element_type=F32)
        a = jnp.square(jnp.maximum(a, 0.0)).astype(BF16)
        for n in range(D_MODEL // MLP_TN):
            cols = slice(n * MLP_TN, (n + 1) * MLP_TN)
            o_ref[:, cols] += jnp.dot(a, w2_ref[mid, cols], preferred_element_type=F32)

    @pl.when(k == pl.num_programs(1) - 1)
    def _():
        def epilogue(rows):
            o_ref[rows, :] = x_ref[rows, :] + _rms_rows(o_ref[rows, :], gout_ref[...])
        row_chunks(epilogue)


def _mlp(x2d, gin, gout, w1, w2):
    return pl.pallas_call(
        _mlp_kernel,
        grid=(SEQ // MLP_TM, D_FF // MLP_TK),
        in_specs=[
            pl.BlockSpec((MLP_TM, D_MODEL), lambda i, k: (i, 0)),
            pl.BlockSpec((1, D_MODEL), lambda i, k: (0, 0)),
            pl.BlockSpec((1, D_MODEL), lambda i, k: (0, 0)),
            pl.BlockSpec((D_MODEL, MLP_TK), lambda i, k: (0, k)),
            pl.BlockSpec((MLP_TK, D_MODEL), lambda i, k: (k, 0)),
        ],
        out_specs=pl.BlockSpec((MLP_TM, D_MODEL), lambda i, k: (i, 0)),
        out_shape=jax.ShapeDtypeStruct((SEQ, D_MODEL), F32),
        scratch_shapes=[pltpu.VMEM((MLP_TM, D_MODEL), BF16)],
        compiler_params=_cparams(2),
        name="mlp",
    )(x2d, gin, gout, w1, w2)


POOL_TM = 512
POOL_SUB = 128
HALO = 8


def _pool_bands():
    t = np.arange(POOL_SUB)[:, None]
    j = np.arange(POOL_SUB + 2 * HALO)[None, :]
    return np.stack([(np.abs(j - HALO - t) <= w // 2) for w in POOL_WINDOWS]).astype(np.float32)


def _pool_kernel(xp_ref, x_hbm, xn_ref, gin_ref, gout_ref, band_ref, pw_ref, ps_ref, o_ref,
                 hs_ref, hi_ref, lo_ref, m_ref, xring, sems):
    i = pl.program_id(0)
    steps = pl.num_programs(0)
    gin = gin_ref[...]

    def fetch(s):
        slot = lax.rem(s, RING)
        return pltpu.make_async_copy(x_hbm.at[pl.ds(s * POOL_TM, POOL_TM), :], xring.at[slot], sems.at[slot])

    @pl.when(i == 0)
    def _():
        for s in range(RING - 1):
            fetch(s).start()

    @pl.when(i + RING - 1 < steps)
    def _():
        fetch(i + RING - 1).start()

    fetch(i).wait()
    xm_ref = xring.at[lax.rem(i, RING)]
    hs_ref[0:HALO, :] = jnp.where(i > 0, _rms_rows(xp_ref[...], gin), 0.0)
    hs_ref[HALO:HALO + POOL_TM, :] = _rms_rows(xm_ref[...], gin)
    hs_ref[HALO + POOL_TM:, :] = jnp.where(i < pl.num_programs(0) - 1, _rms_rows(xn_ref[...], gin), 0.0)
    hs = hs_ref[...]
    hi = hs.astype(BF16)
    hi_ref[...] = hi
    lo_ref[...] = (hs - hi.astype(F32)).astype(BF16)
    row = lax.broadcasted_iota(jnp.int32, (POOL_SUB, D_POOL_G), 0) + i * POOL_TM
    for gi, w in enumerate(POOL_WINDOWS):
        r = w // 2
        cols = slice(gi * D_POOL_G, (gi + 1) * D_POOL_G)
        band = band_ref[gi]
        parts = []
        for sb in range(POOL_TM // POOL_SUB):
            win = slice(sb * POOL_SUB, (sb + 1) * POOL_SUB + 2 * HALO)
            s = (jnp.dot(band, hi_ref[win, cols], preferred_element_type=F32)
                 + jnp.dot(band, lo_ref[win, cols], preferred_element_type=F32))
            t = row + sb * POOL_SUB
            cnt = jnp.minimum(t + r + 1, SEQ) - jnp.maximum(t - r, 0)
            u = hs_ref[sb * POOL_SUB + HALO:(sb + 1) * POOL_SUB + HALO, cols]
            parts.append((s / cnt.astype(F32) - u).astype(BF16))
        d = jnp.concatenate(parts, axis=0)
        mg = jnp.dot(d, pw_ref[gi], preferred_element_type=F32)
        m_ref[:, cols] = mg * ps_ref[:, cols]
    o_ref[...] = xm_ref[...] + _rms_rows(m_ref[...], gout_ref[...])


def _pool(x2d, gin, gout, pw, ps):
    nb8 = POOL_TM // HALO
    last8 = SEQ // HALO - 1
    band = jnp.asarray(_pool_bands(), BF16)
    ext = POOL_TM + 2 * HALO
    return pl.pallas_call(
        _pool_kernel,
        grid=(SEQ // POOL_TM,),
        in_specs=[
            pl.BlockSpec((HALO, D_MODEL), lambda i: (jnp.maximum(i * nb8 - 1, 0), 0)),
            pl.BlockSpec(memory_space=pl.ANY),
            pl.BlockSpec((HALO, D_MODEL), lambda i: (jnp.minimum((i + 1) * nb8, last8), 0)),
            pl.BlockSpec((1, D_MODEL), lambda i: (0, 0)),
            pl.BlockSpec((1, D_MODEL), lambda i: (0, 0)),
            pl.BlockSpec(band.shape, lambda i: (0, 0, 0)),
            pl.BlockSpec((len(POOL_WINDOWS), D_POOL_G, D_POOL_G), lambda i: (0, 0, 0)),
            pl.BlockSpec((1, D_MODEL), lambda i: (0, 0)),
        ],
        out_specs=pl.BlockSpec((POOL_TM, D_MODEL), lambda i: (i, 0)),
        out_shape=jax.ShapeDtypeStruct((SEQ, D_MODEL), F32),
        scratch_shapes=[pltpu.VMEM((ext, D_MODEL), F32), pltpu.VMEM((ext, D_MODEL), BF16),
                        pltpu.VMEM((ext, D_MODEL), BF16), pltpu.VMEM((POOL_TM, D_MODEL), F32),
                        pltpu.VMEM((RING, POOL_TM, D_MODEL), F32), pltpu.SemaphoreType.DMA((RING,))],
        compiler_params=_cparams(1),
        name="pool_mixer",
    )(x2d, x2d, x2d, gin, gout, band, pw, ps)


def _filter_constants():
    L = SEQ
    t = np.linspace(0.0, 1.0, L)
    w_pos = 2.0 * np.pi * np.arange(L) / L
    bands = np.linspace(1e-4, FILTER_BANDS - 1, FILTER_BANDS)
    ang = w_pos[:, None] * bands[None, :]
    emb = np.concatenate([t[:, None], np.cos(ang), -np.sin(ang)], axis=-1)
    emb_t = np.zeros((EMB_PAD, L + LANES))
    emb_t[:EMB_DIM, :L] = emb.T
    t_rev = t[(L - np.arange(L)) % L]
    max_decay = math.log(DECAY_TARGET) / FAST_DECAY_PCT
    min_decay = math.log(DECAY_TARGET) / SLOW_DECAY_PCT
    absdelta = np.abs(np.linspace(min_decay, max_decay, D_HY))[:, None]
    f32 = lambda a: jnp.asarray(a, F32)
    return f32(emb_t), f32(t[None, :]), f32(t_rev[None, :]), f32(absdelta)


def _mixer_layer0(x2d, g, w_in, hy_short_w, hy_short_b, f_w1, f_b1, f_w2, f_b2, f_w3, f_b3,
                  freq, hy_bias, sc_conv_w, w_out, mlp_w1, mlp_w2, layer):
    emb_t, t_fwd, t_rev, absdelta = _filter_constants()
    w1t = jnp.zeros((FILTER_HIDDEN, EMB_PAD), F32).at[:, :EMB_DIM].set(f_w1.T)
    col = lambda v: v[:, None].astype(F32)
    kf_q, kb_q, wt, wo = _filters(emb_t, t_fwd, t_rev, w1t, col(f_b1), f_w2.T, col(f_b2), col(freq),
                                  f_w3.T.astype(BF16), col(f_b3), absdelta, w_in, w_out)

    zt, w1_next = _in_proj(x2d, g[0][None, :], wt, mlp_w1, layer + 1)
    hy_par = jnp.concatenate([hy_short_w.T, hy_short_b[:, None]], axis=1)
    sc_par = jnp.concatenate([sc_conv_w.T, jnp.zeros((D_SC, 1), F32)], axis=1)

    bias2 = jnp.broadcast_to(hy_bias[:, None].astype(F32), (D_HY, FFT_N2))
    ya_q, w1_this, w2_this, w2_next = _hyena(
        zt, hy_par, kf_q, kb_q, bias2, _dft_tables(), mlp_w1, mlp_w2, layer)
    x2d = _out_proj(ya_q, zt, sc_par, wo, x2d, g[1][None, :])
    return x2d, (w1_this, w2_this), (w1_next, w2_next)


def kernel(x, norm_g, mix_w_in, hy_short_w, hy_short_b, hy_filt_w1, hy_filt_b1, hy_filt_w2,
           hy_filt_b2, hy_filt_w3, hy_filt_b3, hy_freq, hy_bias, sc_conv_w, mix_w_out,
           pool_w, pool_scale, mlp_w1, mlp_w2):
    x2d = x.reshape(SEQ, D_MODEL)
    depth = norm_g.shape[0]
    assert depth % 2 == 0, "an even layer's mixer kernels narrow the MLP weights of itself and the next layer"
    for i in range(depth):
        g = norm_g[i]
        j = i // 2
        gin, gout = g[2][None, :], g[3][None, :]
        if i % 2 == 0:
            x2d, w_this, w_next = _mixer_layer0(
                x2d, g, mix_w_in[j], hy_short_w[j], hy_short_b[j], hy_filt_w1[j], hy_filt_b1[j],
                hy_filt_w2[j], hy_filt_b2[j], hy_filt_w3[j], hy_filt_b3[j], hy_freq[j], hy_bias[j],
                sc_conv_w[j], mix_w_out[j], mlp_w1, mlp_w2, i)
            x2d = _mlp(x2d, gin, gout, *w_this)
        else:
            x2d = _pool(x2d, g[0][None, :], g[1][None, :], pool_w[j].astype(BF16),
                        pool_scale[j][None, :])
            x2d = _mlp(x2d, gin, gout, *w_next)
    return x2d.reshape(x.shape)
```
